```python
import math
import jax, jax.numpy as jnp
from jax import lax
import numpy as np

D_MODEL = 1024
BATCH = 2
SEQ = 8192
DEPTH = 1
DEC_BATCH = 128
DEC_SEQ = 1
PAST_LEN = 16384
PAGE_SIZE = 128

ATTN_HEADS = 8
KV_HEADS = 2
HEAD_DIM = 64
GQA_GROUP = ATTN_HEADS // KV_HEADS
ATTN_WIDTH = ATTN_HEADS * HEAD_DIM
KV_WIDTH = KV_HEADS * HEAD_DIM
WINDOW = 128
BLOCK = WINDOW
GDN_HEADS = 4
GDN_DK = 128
GDN_DV = 128
GDN_KEY_WIDTH = GDN_HEADS * GDN_DK
GDN_VAL_WIDTH = GDN_HEADS * GDN_DV
GDN_CONV_DIM = 2 * GDN_KEY_WIDTH + GDN_VAL_WIDTH
CONV_W = 4
CHUNK = 64
MIX_WIDTH = ATTN_WIDTH + GDN_VAL_WIDTH
IN_WIDTH = ATTN_WIDTH + 2 * KV_WIDTH + GDN_CONV_DIM + GDN_VAL_WIDTH + 2 * GDN_HEADS
D_FF = 4 * D_MODEL
EPS = 1e-6

kernel_name = 'hymba_swa_sink_alibi_gated_deltanet_step'


def _split_points():
    sizes = (ATTN_WIDTH, KV_WIDTH, KV_WIDTH, GDN_CONV_DIM, GDN_VAL_WIDTH, GDN_HEADS, GDN_HEADS)
    pts, acc = [], 0
    for s in sizes[:-1]:
        acc += s
        pts.append(acc)
    return pts


def rms_norm(x, w):
    xf = x.astype(jnp.float32)
    y = xf * lax.rsqrt(jnp.mean(xf * xf, axis=-1, keepdims=True) + EPS)
    return (y * w.astype(jnp.float32)).astype(x.dtype)


def l2_norm(x):
    return x * lax.rsqrt(jnp.sum(x * x, axis=-1, keepdims=True) + EPS)


def alibi_slopes():
    return jnp.exp2(-8.0 * jnp.arange(1, ATTN_HEADS + 1, dtype=jnp.float32) / ATTN_HEADS)


def window_attend(q, k, v, dist, valid, sinks):
    scores = jnp.einsum('...qkgd,...skd->...kgqs', q, k).astype(jnp.float32) * (HEAD_DIM ** -0.5)
    slopes = alibi_slopes().reshape(KV_HEADS, GQA_GROUP, 1, 1)
    scores = scores - slopes * dist.astype(jnp.float32)
    scores = jnp.where(valid, scores, -jnp.inf)
    sink = jnp.broadcast_to(sinks.astype(jnp.float32).reshape(KV_HEADS, GQA_GROUP, 1, 1),
                            scores.shape[:-1] + (1,))
    probs = jax.nn.softmax(jnp.concatenate([scores, sink], axis=-1), axis=-1)[..., :-1]
    return jnp.einsum('...kgqs,...skd->...qkgd', probs.astype(v.dtype), v)


def attn_prompt(q, k, v, sinks):
    Bn, L = q.shape[:2]
    nb = L // BLOCK
    qb = q.reshape(Bn, nb, BLOCK, KV_HEADS, GQA_GROUP, HEAD_DIM)

    def with_prev(x):
        xb = x.reshape(Bn, nb, BLOCK, KV_HEADS, HEAD_DIM)
        prev = jnp.concatenate([jnp.zeros_like(xb[:, :1]), xb[:, :-1]], axis=1)
        return jnp.concatenate([prev, xb], axis=2)

    qi = jnp.arange(BLOCK)[:, None]
    si = jnp.arange(2 * BLOCK)[None, :]
    dist = BLOCK + qi - si
    band = (dist >= 0) & (dist <= WINDOW)
    in_seq = (jnp.arange(nb)[:, None, None] > 0) | (si >= BLOCK)[None]
    valid = (band[None] & in_seq).reshape(nb, 1, 1, BLOCK, 2 * BLOCK)
    out = window_attend(qb, with_prev(k), with_prev(v), dist, valid, sinks)
    return out.reshape(Bn, L, ATTN_WIDTH)


def attn_sample(q, k_new, v_new, k_buf, v_buf, sinks):
    DB, T = q.shape[:2]
    keys = jnp.concatenate([k_buf.astype(k_new.dtype), k_new], axis=1)
    vals = jnp.concatenate([v_buf.astype(v_new.dtype), v_new], axis=1)
    qi = jnp.arange(T)[:, None]
    sj = jnp.arange(WINDOW + T)[None, :]
    dist = WINDOW + qi - sj
    valid = (dist >= 0) & (dist <= WINDOW)
    out = window_attend(q.reshape(DB, T, KV_HEADS, GQA_GROUP, HEAD_DIM), keys, vals, dist, valid, sinks)
    return out.reshape(DB, T, ATTN_WIDTH), keys[:, -WINDOW:], vals[:, -WINDOW:]


def causal_conv(ext, w):
    L = ext.shape[1] - (CONV_W - 1)
    w = w.astype(ext.dtype)
    y = ext[:, 0:L] * w[0]
    for i in range(1, CONV_W):
        y = y + ext[:, i:i + L] * w[i]
    return jax.nn.silu(y)


def gdn_prep(qkv, a, b, a_log, dt_bias):
    Bn, L = qkv.shape[:2]
    qkv = qkv.astype(jnp.float32)
    q, k, v = jnp.split(qkv, [GDN_KEY_WIDTH, 2 * GDN_KEY_WIDTH], axis=-1)
    q = l2_norm(q.reshape(Bn, L, GDN_HEADS, GDN_DK)) * (GDN_DK ** -0.5)
    k = l2_norm(k.reshape(Bn, L, GDN_HEADS, GDN_DK))
    v = v.reshape(Bn, L, GDN_HEADS, GDN_DV)
    g = -jnp.exp(a_log.astype(jnp.float32)) * jax.nn.softplus(a.astype(jnp.float32) + dt_bias.astype(jnp.float32))
    beta = jax.nn.sigmoid(b.astype(jnp.float32))
    return q, k, v, g, beta


def gdn_chunked(q, k, v, g, beta):
    Bn, L, H, dk = q.shape
    dv = v.shape[-1]
    N = L // CHUNK

    def to_chunks(x):
        x = x.reshape((Bn, N, CHUNK, H) + x.shape[3:])
        return jnp.moveaxis(x, (1, 3), (0, 2))

    qc, kc, vc, gc, bc = (to_chunks(t) for t in (q, k, v, g, beta))
    G = jnp.cumsum(gc, axis=-1)
    idx = jnp.arange(CHUNK)
    lower_incl = idx[:, None] >= idx[None, :]
    lower_strict = idx[:, None] > idx[None, :]
    diff = G[..., :, None] - G[..., None, :]
    decay = jnp.where(lower_incl, jnp.exp(jnp.where(lower_incl, diff, 0.0)), 0.0)
    kk = jnp.einsum('nbhid,nbhjd->nbhij', kc, kc)
    A = jnp.where(lower_strict, bc[..., :, None] * kk * decay, 0.0)
    eye = jnp.eye(CHUNK, dtype=jnp.float32)
    Tinv = lax.linalg.triangular_solve(eye + A, jnp.broadcast_to(eye, A.shape), left_side=True, lower=True)
    u_base = jnp.einsum('nbhij,nbhjd->nbhid', Tinv, vc * bc[..., None])
    w = jnp.einsum('nbhij,nbhjd->nbhid', Tinv, kc * (bc * jnp.exp(G))[..., None])
    qk = jnp.where(lower_incl, jnp.einsum('nbhid,nbhjd->nbhij', qc, kc) * decay, 0.0)
    q_dec = qc * jnp.exp(G)[..., None]
    k_dec = kc * jnp.exp(G[..., -1:] - G)[..., None]
    last_decay = jnp.exp(G[..., -1])

    def step(S, xs):
        u_b, w_c, qk_c, q_d, k_d, ld = xs
        u = u_b - jnp.einsum('bhcd,bhde->bhce', w_c, S)
        o = jnp.einsum('bhcd,bhde->bhce', q_d, S) + jnp.einsum('bhij,bhje->bhie', qk_c, u)
        S = S * ld[..., None, None] + jnp.einsum('bhcd,bhce->bhde', k_d, u)
        return S, o

    S0 = jnp.zeros((Bn, H, dk, dv), jnp.float32)
    S_fin, o = lax.scan(step, S0, (u_base, w, qk, q_dec, k_dec, last_decay))
    o = jnp.moveaxis(o, (0, 2), (1, 3)).reshape(Bn, L, H, dv)
    return o, S_fin


def gdn_recurrent(q, k, v, g, beta, S0):
    def step(S, xs):
        q_t, k_t, v_t, g_t, b_t = xs
        S = S * jnp.exp(g_t)[..., None, None]
        kv = jnp.einsum('bhd,bhde->bhe', k_t, S)
        u = (v_t - kv) * b_t[..., None]
        S = S + jnp.einsum('bhd,bhe->bhde', k_t, u)
        return S, jnp.einsum('bhd,bhde->bhe', q_t, S)

    xs = tuple(jnp.moveaxis(t, 1, 0) for t in (q, k, v, g, beta))
    S, o = lax.scan(step, S0.astype(jnp.float32), xs)
    return jnp.moveaxis(o, 0, 1), S


def mixer_in(x, lp):
    Bn, L = x.shape[:2]
    h = rms_norm(x, lp['norm_mix_pre'])
    proj = jnp.einsum('bld,de->ble', h, lp['w_in'].astype(h.dtype))
    q_a, k_a, v_a, qkv_g, z_g, a_g, b_g = jnp.split(proj, _split_points(), axis=-1)
    q_a = q_a.reshape(Bn, L, ATTN_HEADS, HEAD_DIM)
    k_a = k_a.reshape(Bn, L, KV_HEADS, HEAD_DIM)
    v_a = v_a.reshape(Bn, L, KV_HEADS, HEAD_DIM)
    return q_a, k_a, v_a, qkv_g, z_g, a_g, b_g


def mixer_out(x, attn_o, gdn_o, z, lp):
    Bn, L = x.shape[:2]
    gate = jax.nn.silu(z.astype(jnp.float32).reshape(Bn, L, GDN_HEADS, GDN_DV))
    gdn_y = rms_norm(gdn_o, lp['gdn_norm']) * gate
    mix = jnp.concatenate([attn_o.astype(x.dtype), gdn_y.reshape(Bn, L, GDN_VAL_WIDTH).astype(x.dtype)], axis=-1)
    x = x + rms_norm(jnp.einsum('blm,md->bld', mix, lp['w_out'].astype(x.dtype)), lp['norm_mix_post'])
    h = rms_norm(x, lp['norm_ffn_pre'])
    u = jax.nn.relu(jnp.einsum('bld,df->blf', h, lp['w_up'].astype(h.dtype)))
    f = jnp.einsum('blf,fd->bld', u * u, lp['w_down'].astype(h.dtype))
    return x + rms_norm(f, lp['norm_ffn_post'])


def prompt_layer(x, lp):
    q_a, k_a, v_a, qkv, z, a, b = mixer_in(x, lp)
    attn_o = attn_prompt(q_a, k_a, v_a, lp['attn_sinks'])
    ext = jnp.concatenate([jnp.zeros_like(qkv[:, :CONV_W - 1]), qkv], axis=1)
    q, k, v, g, beta = gdn_prep(causal_conv(ext, lp['conv_w']), a, b, lp['gdn_a_log'], lp['gdn_dt_bias'])
    gdn_o, S = gdn_chunked(q, k, v, g, beta)
    y = mixer_out(x, attn_o, gdn_o, z, lp)
    return y, ext[:, -(CONV_W - 1):], k_a[:, -WINDOW:], v_a[:, -WINDOW:], S


def sample_layer(x, conv_state, k_buf, v_buf, S0, lp):
    q_a, k_a, v_a, qkv, z, a, b = mixer_in(x, lp)
    attn_o, new_k, new_v = attn_sample(q_a, k_a, v_a, k_buf, v_buf, lp['attn_sinks'])
    ext = jnp.concatenate([conv_state.astype(qkv.dtype), qkv], axis=1)
    q, k, v, g, beta = gdn_prep(causal_conv(ext, lp['conv_w']), a, b, lp['gdn_a_log'], lp['gdn_dt_bias'])
    gdn_o, S = gdn_recurrent(q, k, v, g, beta, S0)
    y = mixer_out(x, attn_o, gdn_o, z, lp)
    return y, ext[:, -(CONV_W - 1):], new_k, new_v, S


def setup_inputs(seed: int = 0) -> dict:
    key = jax.random.key(seed)
    ks = jax.random.split(key, 24)
    f32 = jnp.float32

    def nrm(k, shape, scale):
        return jax.random.normal(k, shape, f32) * scale

    def gain(k, n):
        return 1.0 + 0.01 * jax.random.normal(k, (DEPTH, n), f32)

    dt = jnp.exp(jax.random.uniform(ks[10], (DEPTH, GDN_HEADS), f32, math.log(1e-3), math.log(0.1)))
    return {
        'x_prompt': nrm(ks[0], (BATCH, SEQ, D_MODEL), 1.0),
        'x_sample': nrm(ks[1], (DEC_BATCH, DEC_SEQ, D_MODEL), 1.0),
        'state_conv': nrm(ks[2], (DEPTH, DEC_BATCH, CONV_W - 1, GDN_CONV_DIM), 1.0),
        'cache_win_k': nrm(ks[3], (DEPTH, DEC_BATCH, WINDOW, KV_HEADS, HEAD_DIM), 1.0),
        'cache_win_v': nrm(ks[4], (DEPTH, DEC_BATCH, WINDOW, KV_HEADS, HEAD_DIM), 1.0),
        'state_gdn': nrm(ks[5], (DEPTH, DEC_BATCH, GDN_HEADS, GDN_DK, GDN_DV), GDN_DK ** -0.5),
        'norm_mix_pre': gain(ks[6], D_MODEL),
        'w_in': nrm(ks[7], (DEPTH, D_MODEL, IN_WIDTH), D_MODEL ** -0.5),
        'attn_sinks': nrm(ks[8], (DEPTH, ATTN_HEADS), 1.0),
        'conv_w': nrm(ks[9], (DEPTH, CONV_W, GDN_CONV_DIM), CONV_W ** -0.5),
        'gdn_a_log': jnp.log(jax.random.uniform(ks[11], (DEPTH, GDN_HEADS), f32, 1.0, 16.0)),
        'gdn_dt_bias': dt + jnp.log(-jnp.expm1(-dt)),
        'gdn_norm': gain(ks[12], GDN_DV),
        'w_out': nrm(ks[13], (DEPTH, MIX_WIDTH, D_MODEL), MIX_WIDTH ** -0.5),
        'norm_mix_post': gain(ks[14], D_MODEL),
        'norm_ffn_pre': gain(ks[15], D_MODEL),
        'w_up': nrm(ks[16], (DEPTH, D_MODEL, D_FF), D_MODEL ** -0.5),
        'w_down': nrm(ks[17], (DEPTH, D_FF, D_MODEL), D_FF ** -0.5),
        'norm_ffn_post': gain(ks[18], D_MODEL),
    }


def reference(x_prompt, x_sample, state_conv, cache_win_k, cache_win_v, state_gdn,
              norm_mix_pre, w_in, attn_sinks, conv_w, gdn_a_log, gdn_dt_bias, gdn_norm,
              w_out, norm_mix_post, norm_ffn_pre, w_up, w_down, norm_ffn_post):
    xp, xs = x_prompt, x_sample
    pc, pk, pv, ps = [], [], [], []
    sc, sk, sv, ss = [], [], [], []
    for l in range(DEPTH):
        lp = dict(norm_mix_pre=norm_mix_pre[l], w_in=w_in[l], attn_sinks=attn_sinks[l],
                  conv_w=conv_w[l], gdn_a_log=gdn_a_log[l], gdn_dt_bias=gdn_dt_bias[l],
                  gdn_norm=gdn_norm[l], w_out=w_out[l], norm_mix_post=norm_mix_post[l],
                  norm_ffn_pre=norm_ffn_pre[l], w_up=w_up[l], w_down=w_down[l],
                  norm_ffn_post=norm_ffn_post[l])
        xp, c1, k1, v1, s1 = prompt_layer(xp, lp)
        xs, c2, k2, v2, s2 = sample_layer(xs, state_conv[l], cache_win_k[l], cache_win_v[l], state_gdn[l], lp)
        pc.append(c1.astype(state_conv.dtype)); pk.append(k1.astype(cache_win_k.dtype))
        pv.append(v1.astype(cache_win_v.dtype)); ps.append(s1.astype(state_gdn.dtype))
        sc.append(c2.astype(state_conv.dtype)); sk.append(k2.astype(cache_win_k.dtype))
        sv.append(v2.astype(cache_win_v.dtype)); ss.append(s2.astype(state_gdn.dtype))
    p_state_conv = jnp.stack(pc)
    p_cache_win_k = jnp.stack(pk)
    p_cache_win_v = jnp.stack(pv)
    p_state_gdn = jnp.stack(ps)
    s_state_conv = jnp.stack(sc)
    s_cache_win_k = jnp.stack(sk)
    s_cache_win_v = jnp.stack(sv)
    s_state_gdn = jnp.stack(ss)
    return (xp, xs, p_state_conv, p_cache_win_k, p_cache_win_v, p_state_gdn,
            s_state_conv, s_cache_win_k, s_cache_win_v, s_state_gdn)
```

```python
import functools

import jax
import jax.numpy as jnp
from jax import lax
from jax.experimental import pallas as pl
from jax.experimental.pallas import tpu as pltpu

F32 = jnp.float32
BF16 = jnp.bfloat16

D_MODEL = 1024
ATTN_HEADS = 8
KV_HEADS = 2
HEAD_DIM = 64
GQA_GROUP = ATTN_HEADS // KV_HEADS
ATTN_WIDTH = ATTN_HEADS * HEAD_DIM
KV_WIDTH = KV_HEADS * HEAD_DIM
WINDOW = 128
GDN_HEADS = 4
GDN_DK = 128
GDN_DV = 128
GDN_KEY_WIDTH = GDN_HEADS * GDN_DK
GDN_VAL_WIDTH = GDN_HEADS * GDN_DV
GDN_CONV_DIM = 2 * GDN_KEY_WIDTH + GDN_VAL_WIDTH
CONV_W = 4
D_FF = 4 * D_MODEL
EPS = 1e-6

LANES = 128
SUBLANES = 8
ATTN_COLS = ATTN_WIDTH + 2 * KV_WIDTH
AB_COLS = LANES
IN_COLS = ATTN_COLS + GDN_CONV_DIM + GDN_VAL_WIDTH + AB_COLS
OFF_GDN = ATTN_COLS
OFF_Z = OFF_GDN + GDN_CONV_DIM
OFF_AB = OFF_Z + GDN_VAL_WIDTH
CHUNK = 128
VMEM_LIMIT = 56 * 1024 * 1024


def _rms(x, w):
    return x * lax.rsqrt(jnp.mean(x * x, axis=-1, keepdims=True) + EPS) * w


def _sigmoid(x):
    return 1.0 / (1.0 + jnp.exp(-x))


def _softplus(x):
    return jnp.maximum(x, 0.0) + jnp.log(1.0 + jnp.exp(-jnp.abs(x)))


def _dot(a, b):
    return jnp.dot(a.astype(BF16), b.astype(BF16), preferred_element_type=F32)


def _dot_nt(a, b):
    return lax.dot_general(a.astype(BF16), b.astype(BF16), (((1,), (1,)), ((), ())),
                           preferred_element_type=F32)


def _inproj_kernel(x_ref, nw_ref, w_ref, oa_ref, og_ref, oz_ref, oab_ref):
    h = _rms(x_ref[...], nw_ref[...]).astype(BF16)
    oa_ref[...] = jnp.dot(h, w_ref[:, 0:OFF_GDN], preferred_element_type=F32)
    og_ref[...] = jnp.dot(h, w_ref[:, OFF_GDN:OFF_Z], preferred_element_type=F32)
    oz_ref[...] = jnp.dot(h, w_ref[:, OFF_Z:OFF_AB], preferred_element_type=F32)
    oab_ref[...] = jnp.dot(h, w_ref[:, OFF_AB:IN_COLS], preferred_element_type=F32)


def _inproj(x, nw, w, tm):
    m = x.shape[0]
    row = lambda i: (i, 0)
    fixed = lambda i: (0, 0)
    return pl.pallas_call(
        _inproj_kernel,
        grid=(m // tm,),
        in_specs=[pl.BlockSpec((tm, D_MODEL), row),
                  pl.BlockSpec((1, D_MODEL), fixed),
                  pl.BlockSpec((D_MODEL, IN_COLS), fixed)],
        out_specs=[pl.BlockSpec((tm, ATTN_COLS), row),
                   pl.BlockSpec((tm, GDN_CONV_DIM), row),
                   pl.BlockSpec((tm, GDN_VAL_WIDTH), row),
                   pl.BlockSpec((tm, AB_COLS), row)],
        out_shape=[jax.ShapeDtypeStruct((m, ATTN_COLS), F32),
                   jax.ShapeDtypeStruct((m, GDN_CONV_DIM), F32),
                   jax.ShapeDtypeStruct((m, GDN_VAL_WIDTH), F32),
                   jax.ShapeDtypeStruct((m, AB_COLS), F32)],
        compiler_params=pltpu.CompilerParams(dimension_semantics=("arbitrary",),
                                             vmem_limit_bytes=VMEM_LIMIT),
        name="inproj",
    )(x, nw, w)


FF_CHUNK = 512


def _out_ffn_kernel(x_ref, ao_ref, gy_ref, wout_ref, npost_ref, npre_ref, wup_ref, wdown_ref,
                    nffn_ref, y_ref):
    m = (jnp.dot(ao_ref[...].astype(BF16), wout_ref[0:ATTN_WIDTH, :], preferred_element_type=F32)
         + jnp.dot(gy_ref[...].astype(BF16), wout_ref[ATTN_WIDTH:, :], preferred_element_type=F32))
    x1 = x_ref[...] + _rms(m, npost_ref[...])
    h = _rms(x1, npre_ref[...]).astype(BF16)
    acc = None
    for j in range(0, D_FF, FF_CHUNK):
        u = jnp.maximum(jnp.dot(h, wup_ref[:, j:j + FF_CHUNK], preferred_element_type=F32), 0.0)
        d = jnp.dot((u * u).astype(BF16), wdown_ref[j:j + FF_CHUNK, :], preferred_element_type=F32)
        acc = d if acc is None else acc + d
    y_ref[...] = x1 + _rms(acc, nffn_ref[...])


def _out_ffn(x, ao, gy, wout, npost, npre, wup, wdown, nffn, tm):
    m = x.shape[0]
    row = lambda i: (i, 0)
    fixed = lambda i: (0, 0)
    return pl.pallas_call(
        _out_ffn_kernel,
        grid=(m // tm,),
        in_specs=[pl.BlockSpec((tm, D_MODEL), row),
                  pl.BlockSpec((tm, ATTN_WIDTH), row),
                  pl.BlockSpec((tm, GDN_VAL_WIDTH), row),
                  pl.BlockSpec((D_MODEL, D_MODEL), fixed, pipeline_mode=pl.Buffered(1)),
                  pl.BlockSpec((1, D_MODEL), fixed),
                  pl.BlockSpec((1, D_MODEL), fixed),
                  pl.BlockSpec((D_MODEL, D_FF), fixed, pipeline_mode=pl.Buffered(1)),
                  pl.BlockSpec((D_FF, D_MODEL), fixed, pipeline_mode=pl.Buffered(1)),
                  pl.BlockSpec((1, D_MODEL), fixed)],
        out_specs=pl.BlockSpec((tm, D_MODEL), row),
        out_shape=jax.ShapeDtypeStruct((m, D_MODEL), F32),
        compiler_params=pltpu.CompilerParams(dimension_semantics=("arbitrary",),
                                             vmem_limit_bytes=VMEM_LIMIT),
        name="out_ffn",
    )(x, ao, gy, wout, npost, npre, wup, wdown, nffn)


ATTN_TQ = 256


def _alibi_slope(h):
    return 2.0 ** (-8.0 * (h + 1) / ATTN_HEADS)


def _dup_half(x, lo):
    xr = pltpu.roll(x, HEAD_DIM, axis=1)
    return jnp.where(lo, x, xr), jnp.where(lo, xr, x)


def _attn_prompt_kernel(sink_ref, cur_ref, prev_ref, o_ref):
    i = pl.program_id(1)
    cur = cur_ref[...]
    q = cur[:, 0:ATTN_WIDTH]
    kall = jnp.concatenate([prev_ref[:, 0:KV_WIDTH], cur[:, ATTN_WIDTH:ATTN_WIDTH + KV_WIDTH]], axis=0)
    vall = jnp.concatenate([prev_ref[:, KV_WIDTH:], cur[:, ATTN_WIDTH + KV_WIDTH:]], axis=0)
    lo = lax.broadcasted_iota(jnp.int32, (1, LANES), 1) < HEAD_DIM
    k2 = _dup_half(kall, lo)
    v2 = _dup_half(vall, lo)
    qi = lax.broadcasted_iota(jnp.int32, (WINDOW, 2 * WINDOW), 0)
    si = lax.broadcasted_iota(jnp.int32, (WINDOW, 2 * WINDOW), 1)
    dist = WINDOW + qi - si
    band = (dist >= 0) & (dist <= WINDOW)
    distf = dist.astype(F32)
    for r in range(ATTN_TQ // WINDOW):
        qb = q[r * WINDOW:(r + 1) * WINDOW]
        valid = band
        if r == 0:
            valid = band & ((i * ATTN_TQ - WINDOW + si) >= 0)
        for j in range(KV_HEADS):
            kw = k2[j][r * WINDOW:(r + 2) * WINDOW]
            vw = v2[j][r * WINDOW:(r + 2) * WINDOW]
            t0 = qb[:, (2 * j) * LANES:(2 * j + 1) * LANES]
            t1 = qb[:, (2 * j + 1) * LANES:(2 * j + 2) * LANES]
            lhs = jnp.concatenate([jnp.where(lo, t0, 0.0), jnp.where(lo, 0.0, t0),
                                   jnp.where(lo, t1, 0.0), jnp.where(lo, 0.0, t1)], axis=0)
            s = _dot_nt(lhs, kw) * (HEAD_DIM ** -0.5)
            probs = []
            for g in range(GQA_GROUP):
                h = j * GQA_GROUP + g
                sg = s[g * WINDOW:(g + 1) * WINDOW] - _alibi_slope(h) * distf
                sg = jnp.where(valid, sg, -jnp.inf)
                sink = sink_ref[0, h]
                mx = jnp.maximum(jnp.max(sg, axis=-1, keepdims=True), sink)
                p = jnp.exp(sg - mx)
                den = jnp.sum(p, axis=-1, keepdims=True) + jnp.exp(sink - mx)
                probs.append(p / den)
            o = _dot(jnp.concatenate(probs, axis=0), vw)
            o01 = jnp.where(lo, o[0:WINDOW], o[WINDOW:2 * WINDOW])
            o23 = jnp.where(lo, o[2 * WINDOW:3 * WINDOW], o[3 * WINDOW:4 * WINDOW])
            rows = slice(r * WINDOW, (r + 1) * WINDOW)
            o_ref[rows, (2 * j) * LANES:(2 * j + 1) * LANES] = o01.astype(o_ref.dtype)
            o_ref[rows, (2 * j + 1) * LANES:(2 * j + 2) * LANES] = o23.astype(o_ref.dtype)


def _attn_prompt(pa, sinks):
    b, l, _ = pa.shape
    blocks_per_step = ATTN_TQ // WINDOW
    return pl.pallas_call(
        _attn_prompt_kernel,
        grid=(b, l // ATTN_TQ),
        in_specs=[pl.BlockSpec(memory_space=pltpu.SMEM),
                  pl.BlockSpec((None, ATTN_TQ, ATTN_COLS), lambda bb, i: (bb, i, 0)),
                  pl.BlockSpec((None, WINDOW, 2 * KV_WIDTH),
                               lambda bb, i: (bb, jnp.maximum(i * blocks_per_step - 1, 0),
                                              ATTN_WIDTH // (2 * KV_WIDTH)))],
        out_specs=pl.BlockSpec((None, ATTN_TQ, ATTN_WIDTH), lambda bb, i: (bb, i, 0)),
        out_shape=jax.ShapeDtypeStruct((b, l, ATTN_WIDTH), BF16),
        compiler_params=pltpu.CompilerParams(dimension_semantics=("arbitrary", "arbitrary"),
                                             vmem_limit_bytes=VMEM_LIMIT),
        name="attn_prompt",
    )(sinks, pa, pa)


GDN_T = 256


def _conv_silu(x, carry, w):
    row8 = lax.broadcasted_iota(jnp.int32, (SUBLANES, 1), 0)
    y = None
    for shift in range(CONV_W - 1, -1, -1):
        if shift == 0:
            xs = x
        else:
            xr = pltpu.roll(x, shift, axis=0)
            cr = pltpu.roll(carry, shift, axis=0)
            head = jnp.where(row8 < shift, cr, xr[0:SUBLANES])
            xs = jnp.concatenate([head, xr[SUBLANES:]], axis=0)
        term = xs * w[CONV_W - 1 - shift:CONV_W - shift]
        y = term if y is None else y + term
    return y * _sigmoid(y)


def _l2n(x):
    return x * lax.rsqrt(jnp.sum(x * x, axis=-1, keepdims=True) + EPS)


def _unit_lower_inverse_minus_eye(a, row, col):
    n = a.shape[0]
    first = ((row >> 1) == (col >> 1)) & ((row & 1) == 1) & ((col & 1) == 0)
    xo = jnp.where(first, -a, 0.0)
    size = 2
    while size < n:
        shift = size.bit_length()
        m = ((row >> shift) == (col >> shift)) & ((row & size) != 0) & ((col & size) == 0)
        aoff = jnp.where(m, a, 0.0)
        y = aoff + _dot(aoff, xo)
        xo = xo - y - _dot(xo, y)
        size *= 2
    return xo


def _gdn_prompt_kernel(pg_ref, pab_ref, pz_ref, cw_ref, alog_ref, dtb_ref, gnw_ref,
                       gy_ref, s_out_ref, conv_out_ref, s_scr, carry_scr):
    i = pl.program_id(1)

    @pl.when(i == 0)
    def _():
        s_scr[...] = jnp.zeros_like(s_scr)
        carry_scr[...] = jnp.zeros_like(carry_scr)

    x = pg_ref[...]
    act = _conv_silu(x, carry_scr[...], cw_ref[...])
    tail = x[GDN_T - SUBLANES:GDN_T]
    carry_scr[...] = tail
    conv_out_ref[...] = tail

    ab = pab_ref[...]
    g = -jnp.exp(alog_ref[...]) * _softplus(ab + dtb_ref[...])
    beta = _sigmoid(ab)
    rin = lax.broadcasted_iota(jnp.int32, (GDN_T, 1), 0) & (CHUNK - 1)
    step = 1
    while step < CHUNK:
        g = g + jnp.where(rin >= step, pltpu.roll(g, step, axis=0), 0.0)
        step *= 2
    gt = g.T

    row = lax.broadcasted_iota(jnp.int32, (CHUNK, CHUNK), 0)
    col = lax.broadcasted_iota(jnp.int32, (CHUNK, CHUNK), 1)
    lower_incl = row >= col
    lower_strict = row > col
    gnw = gnw_ref[...]

    for h in range(GDN_HEADS):
        lanes = slice(h * LANES, (h + 1) * LANES)
        qn = _l2n(act[:, lanes]) * (GDN_DK ** -0.5)
        kn = _l2n(act[:, GDN_KEY_WIDTH + h * LANES:GDN_KEY_WIDTH + (h + 1) * LANES])
        vv = act[:, 2 * GDN_KEY_WIDTH + h * LANES:2 * GDN_KEY_WIDTH + (h + 1) * LANES]
        gcol = g[:, h:h + 1]
        bcol = beta[:, GDN_HEADS + h:GDN_HEADS + h + 1]
        s = s_scr[h]
        for c in range(GDN_T // CHUNK):
            rows = slice(c * CHUNK, (c + 1) * CHUNK)
            kc, qc, vc = kn[rows], qn[rows], vv[rows]
            gc = gcol[rows]
            bc = bcol[rows]
            gr = gt[h:h + 1, c * CHUNK:(c + 1) * CHUNK]
            decay = jnp.where(lower_incl, jnp.exp(jnp.where(lower_incl, gc - gr, 0.0)), 0.0)
            kq = _dot_nt(jnp.concatenate([kc, qc], axis=0), kc)
            a = jnp.where(lower_strict, bc * kq[0:CHUNK] * decay, 0.0)
            xo = _unit_lower_inverse_minus_eye(a, row, col)
            eg = jnp.exp(gc)
            glast = gc[CHUNK - 1:CHUNK]
            rhs = jnp.concatenate([vc * bc, kc * (bc * eg)], axis=1)
            tr = rhs + _dot(xo, rhs)
            u_base, w = tr[:, 0:GDN_DV], tr[:, GDN_DV:]
            qk = jnp.where(lower_incl, kq[CHUNK:] * decay, 0.0)
            q_dec = qc * eg
            k_dec = kc * jnp.exp(glast - gc)
            ws = _dot(jnp.concatenate([w, q_dec], axis=0), s)
            u = u_base - ws[0:CHUNK]
            ou = _dot(jnp.concatenate([qk, k_dec.T], axis=0), u)
            o = ws[CHUNK:] + ou[0:CHUNK]
            s = s * jnp.exp(glast) + ou[CHUNK:]
            z = pz_ref[rows, lanes]
            gy_ref[rows, lanes] = (_rms(o, gnw) * (z * _sigmoid(z))).astype(gy_ref.dtype)
        s_scr[h] = s
        s_out_ref[h] = s


def _gdn_prompt(pg, pab, pz, cw, alog, dtb, gnw):
    b, l, _ = pg.shape
    tok = lambda bb, i: (bb, i, 0)
    fixed = lambda bb, i: (0, 0)
    return pl.pallas_call(
        _gdn_prompt_kernel,
        grid=(b, l // GDN_T),
        in_specs=[pl.BlockSpec((None, GDN_T, GDN_CONV_DIM), tok),
                  pl.BlockSpec((None, GDN_T, AB_COLS), tok),
                  pl.BlockSpec((None, GDN_T, GDN_VAL_WIDTH), tok),
                  pl.BlockSpec((CONV_W, GDN_CONV_DIM), fixed),
                  pl.BlockSpec((1, LANES), fixed),
                  pl.BlockSpec((1, LANES), fixed),
                  pl.BlockSpec((1, GDN_DV), fixed)],
        out_specs=[pl.BlockSpec((None, GDN_T, GDN_VAL_WIDTH), tok),
                   pl.BlockSpec((None, GDN_HEADS, GDN_DK, GDN_DV), lambda bb, i: (bb, 0, 0, 0)),
                   pl.BlockSpec((None, SUBLANES, GDN_CONV_DIM), lambda bb, i: (bb, 0, 0))],
        out_shape=[jax.ShapeDtypeStruct((b, l, GDN_VAL_WIDTH), BF16),
                   jax.ShapeDtypeStruct((b, GDN_HEADS, GDN_DK, GDN_DV), F32),
                   jax.ShapeDtypeStruct((b, SUBLANES, GDN_CONV_DIM), F32)],
        scratch_shapes=[pltpu.VMEM((GDN_HEADS, GDN_DK, GDN_DV), F32),
                        pltpu.VMEM((SUBLANES, GDN_CONV_DIM), F32)],
        compiler_params=pltpu.CompilerParams(dimension_semantics=("arbitrary", "arbitrary"),
                                             vmem_limit_bytes=VMEM_LIMIT),
        name="gdn_prompt",
    )(pg, pab, pz, cw, alog, dtb, gnw)


def _sample_prep_kernel(x_ref, nw_ref, w_ref, cs_ref, cw_ref, alog_ref, dtb_ref,
                        qexp_ref, knew_ref, vnew_ref, qt_ref, kt_ref, vg_ref, gate_ref, z_ref,
                        cs_out_ref):
    h = _rms(x_ref[...], nw_ref[...]).astype(BF16)
    pa = jnp.dot(h, w_ref[:, 0:OFF_GDN], preferred_element_type=F32)
    pg = jnp.dot(h, w_ref[:, OFF_GDN:OFF_Z], preferred_element_type=F32)
    z_ref[...] = jnp.dot(h, w_ref[:, OFF_Z:OFF_AB], preferred_element_type=F32)
    ab = jnp.dot(h, w_ref[:, OFF_AB:IN_COLS], preferred_element_type=F32)

    lo = lax.broadcasted_iota(jnp.int32, (1, LANES), 1) < HEAD_DIM
    for t in range(ATTN_WIDTH // LANES):
        tile = pa[:, t * LANES:(t + 1) * LANES]
        rolled = pltpu.roll(tile, HEAD_DIM, axis=1)
        if t < ATTN_WIDTH // LANES // KV_HEADS:
            even, odd = jnp.where(lo, tile, 0.0), jnp.where(lo, rolled, 0.0)
        else:
            even, odd = jnp.where(lo, 0.0, rolled), jnp.where(lo, 0.0, tile)
        qexp_ref[2 * t] = even
        qexp_ref[2 * t + 1] = odd
    knew_ref[...] = pa[:, ATTN_WIDTH:ATTN_WIDTH + KV_WIDTH]
    vnew_ref[...] = pa[:, ATTN_WIDTH + KV_WIDTH:ATTN_COLS]

    cw = cw_ref[...]
    y = None
    for r in range(CONV_W - 1):
        term = cs_ref[:, r * GDN_CONV_DIM:(r + 1) * GDN_CONV_DIM] * cw[r:r + 1]
        y = term if y is None else y + term
    y = y + pg * cw[CONV_W - 1:CONV_W]
    act = y * _sigmoid(y)
    cs_out_ref[:, 0:(CONV_W - 2) * GDN_CONV_DIM] = cs_ref[:, GDN_CONV_DIM:(CONV_W - 1) * GDN_CONV_DIM]
    cs_out_ref[:, (CONV_W - 2) * GDN_CONV_DIM:] = pg

    for hd in range(GDN_HEADS):
        qn = _l2n(act[:, hd * LANES:(hd + 1) * LANES]) * (GDN_DK ** -0.5)
        kn = _l2n(act[:, GDN_KEY_WIDTH + hd * LANES:GDN_KEY_WIDTH + (hd + 1) * LANES])
        qt_ref[hd] = qn.T
        kt_ref[hd] = kn.T
    vg_ref[...] = act[:, 2 * GDN_KEY_WIDTH:]
    g = -jnp.exp(alog_ref[...]) * _softplus(ab + dtb_ref[...])
    lane = lax.broadcasted_iota(jnp.int32, (1, LANES), 1)
    gate_ref[...] = jnp.where(lane < GDN_HEADS, jnp.exp(g), _sigmoid(ab))


def _sample_prep(x, nw, w, cs, cw, alog, dtb):
    n = x.shape[0]
    return pl.pallas_call(
        _sample_prep_kernel,
        out_shape=[jax.ShapeDtypeStruct((ATTN_HEADS, n, LANES), F32),
                   jax.ShapeDtypeStruct((n, KV_WIDTH), F32),
                   jax.ShapeDtypeStruct((n, KV_WIDTH), F32),
                   jax.ShapeDtypeStruct((GDN_HEADS, GDN_DK, n), F32),
                   jax.ShapeDtypeStruct((GDN_HEADS, GDN_DK, n), F32),
                   jax.ShapeDtypeStruct((n, GDN_VAL_WIDTH), F32),
                   jax.ShapeDtypeStruct((n, LANES), F32),
                   jax.ShapeDtypeStruct((n, GDN_VAL_WIDTH), F32),
                   jax.ShapeDtypeStruct((n, (CONV_W - 1) * GDN_CONV_DIM), F32)],
        compiler_params=pltpu.CompilerParams(vmem_limit_bytes=VMEM_LIMIT),
        name="sample_prep",
    )(x, nw, w, cs, cw, alog, dtb)


ATTN_BB = 16


def _attn_sample_kernel(sink_ref, qexp_ref, knew_ref, vnew_ref, kbuf_ref, vbuf_ref,
                        o_ref, kout_ref, vout_ref, q_scr, o_scr):
    for h in range(ATTN_HEADS):
        q_scr[pl.ds(h, ATTN_BB, stride=ATTN_HEADS), :] = qexp_ref[h]
    hrow = lax.broadcasted_iota(jnp.int32, (ATTN_HEADS, 1), 0)
    sink = jnp.zeros((ATTN_HEADS, 1), F32)
    slope = jnp.zeros((ATTN_HEADS, 1), F32)
    for h in range(ATTN_HEADS):
        sink = jnp.where(hrow == h, sink_ref[0, h], sink)
        slope = jnp.where(hrow == h, _alibi_slope(h), slope)
    pos = lax.broadcasted_iota(jnp.int32, (1, WINDOW), 1)
    bias = slope * (WINDOW - pos).astype(F32)
    rowid = lax.broadcasted_iota(jnp.int32, (WINDOW, 1), 0)
    last = rowid == WINDOW - 1
    for b in range(ATTN_BB):
        qe = q_scr[b * ATTN_HEADS:(b + 1) * ATTN_HEADS, :]
        kb, vb = kbuf_ref[b], vbuf_ref[b]
        kn, vn = knew_ref[b:b + 1, :], vnew_ref[b:b + 1, :]
        s_old = _dot_nt(qe, kb) * (HEAD_DIM ** -0.5) - bias
        s_new = jnp.sum(qe * kn, axis=-1, keepdims=True) * (HEAD_DIM ** -0.5)
        mx = jnp.maximum(jnp.maximum(jnp.max(s_old, axis=-1, keepdims=True), s_new), sink)
        p_old = jnp.exp(s_old - mx)
        p_new = jnp.exp(s_new - mx)
        den = jnp.sum(p_old, axis=-1, keepdims=True) + p_new + jnp.exp(sink - mx)
        o = _dot(p_old / den, vb) + (p_new / den) * vn
        o_scr[b * ATTN_HEADS:(b + 1) * ATTN_HEADS, :] = o
        kout_ref[b] = jnp.where(last, kn, pltpu.roll(kb, WINDOW - 1, axis=0))
        vout_ref[b] = jnp.where(last, vn, pltpu.roll(vb, WINDOW - 1, axis=0))
    lo = lax.broadcasted_iota(jnp.int32, (1, LANES), 1) < HEAD_DIM
    for t in range(ATTN_WIDTH // LANES):
        even = o_scr[pl.ds(2 * t, ATTN_BB, stride=ATTN_HEADS), :]
        odd = o_scr[pl.ds(2 * t + 1, ATTN_BB, stride=ATTN_HEADS), :]
        if t < ATTN_WIDTH // LANES // KV_HEADS:
            tile = jnp.where(lo, even, pltpu.roll(odd, HEAD_DIM, axis=1))
        else:
            tile = jnp.where(lo, pltpu.roll(even, HEAD_DIM, axis=1), odd)
        o_ref[:, t * LANES:(t + 1) * LANES] = tile.astype(o_ref.dtype)


def _attn_sample(sinks, qexp, knew, vnew, kbuf, vbuf):
    n = knew.shape[0]
    tok2 = lambda i: (i, 0)
    tok3 = lambda i: (i, 0, 0)
    return pl.pallas_call(
        _attn_sample_kernel,
        grid=(n // ATTN_BB,),
        in_specs=[pl.BlockSpec(memory_space=pltpu.SMEM),
                  pl.BlockSpec((ATTN_HEADS, ATTN_BB, LANES), lambda i: (0, i, 0)),
                  pl.BlockSpec((ATTN_BB, KV_WIDTH), tok2),
                  pl.BlockSpec((ATTN_BB, KV_WIDTH), tok2),
                  pl.BlockSpec((ATTN_BB, WINDOW, KV_WIDTH), tok3),
                  pl.BlockSpec((ATTN_BB, WINDOW, KV_WIDTH), tok3)],
        out_specs=[pl.BlockSpec((ATTN_BB, ATTN_WIDTH), tok2),
                   pl.BlockSpec((ATTN_BB, WINDOW, KV_WIDTH), tok3),
                   pl.BlockSpec((ATTN_BB, WINDOW, KV_WIDTH), tok3)],
        out_shape=[jax.ShapeDtypeStruct((n, ATTN_WIDTH), BF16),
                   jax.ShapeDtypeStruct((n, WINDOW, KV_WIDTH), F32),
                   jax.ShapeDtypeStruct((n, WINDOW, KV_WIDTH), F32)],
        scratch_shapes=[pltpu.VMEM((ATTN_BB * ATTN_HEADS, LANES), F32),
                        pltpu.VMEM((ATTN_BB * ATTN_HEADS, LANES), F32)],
        compiler_params=pltpu.CompilerParams(dimension_semantics=("arbitrary",),
                                             vmem_limit_bytes=VMEM_LIMIT),
        name="attn_sample",
    )(sinks, qexp, knew, vnew, kbuf, vbuf)


GDN_BB = 8


def _gdn_sample_kernel(gate_ref, qt_ref, kt_ref, vg_ref, z_ref, gnw_ref, s_ref, o_ref, s_out_ref):
    i = pl.program_id(0)
    n = qt_ref.shape[-1]
    shift = (n - i * GDN_BB) % n
    gnw = gnw_ref[...]
    for h in range(GDN_HEADS):
        qt = pltpu.roll(qt_ref[h], shift, axis=1)
        kt = pltpu.roll(kt_ref[h], shift, axis=1)
        lanes = slice(h * LANES, (h + 1) * LANES)
        for j in range(GDN_BB):
            tokn = i * GDN_BB + j
            decay = gate_ref[tokn * SUBLANES + h]
            beta = gate_ref[tokn * SUBLANES + GDN_HEADS + h]
            kcol = kt[:, j:j + 1]
            qcol = qt[:, j:j + 1]
            s = s_ref[j, h] * decay
            kv = jnp.sum(s * kcol, axis=0, keepdims=True)
            u = (vg_ref[j:j + 1, lanes] - kv) * beta
            s = s + kcol * u
            s_out_ref[j, h] = s
            o = jnp.sum(s * qcol, axis=0, keepdims=True)
            z = z_ref[j:j + 1, lanes]
            o_ref[j:j + 1, lanes] = (_rms(o, gnw) * (z * _sigmoid(z))).astype(o_ref.dtype)


def _gdn_sample(gate, qt, kt, vg, z, gnw, s0):
    n = vg.shape[0]
    return pl.pallas_call(
        _gdn_sample_kernel,
        grid=(n // GDN_BB,),
        in_specs=[pl.BlockSpec(memory_space=pltpu.SMEM),
                  pl.BlockSpec((GDN_HEADS, GDN_DK, n), lambda i: (0, 0, 0)),
                  pl.BlockSpec((GDN_HEADS, GDN_DK, n), lambda i: (0, 0, 0)),
                  pl.BlockSpec((GDN_BB, GDN_VAL_WIDTH), lambda i: (i, 0)),
                  pl.BlockSpec((GDN_BB, GDN_VAL_WIDTH), lambda i: (i, 0)),
                  pl.BlockSpec((1, GDN_DV), lambda i: (0, 0)),
                  pl.BlockSpec((GDN_BB, GDN_HEADS, GDN_DK, GDN_DV), lambda i: (i, 0, 0, 0))],
        out_specs=[pl.BlockSpec((GDN_BB, GDN_VAL_WIDTH), lambda i: (i, 0)),
                   pl.BlockSpec((GDN_BB, GDN_HEADS, GDN_DK, GDN_DV), lambda i: (i, 0, 0, 0))],
        out_shape=[jax.ShapeDtypeStruct((n, GDN_VAL_WIDTH), F32),
                   jax.ShapeDtypeStruct((n, GDN_HEADS, GDN_DK, GDN_DV), F32)],
        compiler_params=pltpu.CompilerParams(dimension_semantics=("arbitrary",),
                                             vmem_limit_bytes=VMEM_LIMIT),
        name="gdn_sample",
    )(gate, qt, kt, vg, z, gnw, s0)


def _pad_lanes(v):
    return jnp.pad(v.astype(F32), (0, LANES - v.shape[0])).reshape(1, LANES)


def kernel(x_prompt, x_sample, state_conv, cache_win_k, cache_win_v, state_gdn, norm_mix_pre, w_in,
           attn_sinks, conv_w, gdn_a_log, gdn_dt_bias, gdn_norm, w_out, norm_mix_post, norm_ffn_pre,
           w_up, w_down, norm_ffn_post):
    assert w_in.shape[0] == 1, "single-layer trunk"
    bsz, seq, _ = x_prompt.shape
    nsamp = x_sample.shape[0]

    w_in_p = jnp.pad(w_in[0], ((0, 0), (0, IN_COLS - w_in.shape[2]))).astype(BF16)
    w_out_b = w_out[0].astype(BF16)
    w_up_b = w_up[0].astype(BF16)
    w_down_b = w_down[0].astype(BF16)
    n_pre = norm_mix_pre[0].reshape(1, D_MODEL)
    n_post = norm_mix_post[0].reshape(1, D_MODEL)
    n_fpre = norm_ffn_pre[0].reshape(1, D_MODEL)
    n_fpost = norm_ffn_post[0].reshape(1, D_MODEL)
    sinks = attn_sinks[0].reshape(1, ATTN_HEADS)
    cw = conv_w[0]
    alog = _pad_lanes(gdn_a_log[0])
    dtb = _pad_lanes(gdn_dt_bias[0])
    gnw = gdn_norm[0].reshape(1, GDN_DV)

    xp = x_prompt.reshape(bsz * seq, D_MODEL)
    pa, pg, pz, pab = _inproj(xp, n_pre, w_in_p, 512)
    pa = pa.reshape(bsz, seq, ATTN_COLS)
    ao = _attn_prompt(pa, sinks)
    gy, p_s, p_conv = _gdn_prompt(pg.reshape(bsz, seq, GDN_CONV_DIM), pab.reshape(bsz, seq, AB_COLS),
                                  pz.reshape(bsz, seq, GDN_VAL_WIDTH), cw, alog, dtb, gnw)
    yp = _out_ffn(xp, ao.reshape(bsz * seq, ATTN_WIDTH), gy.reshape(bsz * seq, GDN_VAL_WIDTH),
                  w_out_b, n_post, n_fpre, w_up_b, w_down_b, n_fpost, 512)

    xs = x_sample.reshape(nsamp, D_MODEL)
    cs = state_conv[0].reshape(nsamp, (CONV_W - 1) * GDN_CONV_DIM)
    qexp, knew, vnew, qt, kt, vg, gate, zs, cs_new = _sample_prep(xs, n_pre, w_in_p, cs, cw, alog, dtb)
    kbuf = cache_win_k[0].reshape(nsamp, WINDOW, KV_WIDTH)
    vbuf = cache_win_v[0].reshape(nsamp, WINDOW, KV_WIDTH)
    ao_s, k_new_cache, v_new_cache = _attn_sample(sinks, qexp, knew, vnew, kbuf, vbuf)
    gy_s, s_new = _gdn_sample(gate[:, :SUBLANES].reshape(nsamp * SUBLANES), qt, kt, vg, zs, gnw,
                              state_gdn[0])
    ys = _out_ffn(xs, ao_s, gy_s, w_out_b, n_post, n_fpre, w_up_b, w_down_b, n_fpost, nsamp)

    kv_shape = (1, -1, WINDOW, KV_HEADS, HEAD_DIM)
    return (yp.reshape(bsz, seq, D_MODEL),
            ys.reshape(nsamp, 1, D_MODEL),
            p_conv[:, SUBLANES - (CONV_W - 1):, :][None],
            pa[:, seq - WINDOW:, ATTN_WIDTH:ATTN_WIDTH + KV_WIDTH].reshape(kv_shape),
            pa[:, seq - WINDOW:, ATTN_WIDTH + KV_WIDTH:].reshape(kv_shape),
            p_s[None],
            cs_new.reshape(1, nsamp, CONV_W - 1, GDN_CONV_DIM),
            k_new_cache.reshape(kv_shape),
            v_new_cache.reshape(kv_shape),
            s_new[None])
```

```python
import functools

import jax
import jax.numpy as jnp
from jax import lax
from jax.experimental import pallas as pl
from jax.experimental.pallas import tpu as pltpu

F32 = jnp.float32
BF16 = jnp.bfloat16

D_MODEL = 1024
ATTN_HEADS = 8
KV_HEADS = 2
HEAD_DIM = 64
GQA_GROUP = ATTN_HEADS // KV_HEADS
ATTN_WIDTH = ATTN_HEADS * HEAD_DIM
KV_WIDTH = KV_HEADS * HEAD_DIM
WINDOW = 128
GDN_HEADS = 4
GDN_DK = 128
GDN_DV = 128
GDN_KEY_WIDTH = GDN_HEADS * GDN_DK
GDN_VAL_WIDTH = GDN_HEADS * GDN_DV
GDN_CONV_DIM = 2 * GDN_KEY_WIDTH + GDN_VAL_WIDTH
CONV_W = 4
D_FF = 4 * D_MODEL
EPS = 1e-6

LANES = 128
SUBLANES = 8
ATTN_COLS = ATTN_WIDTH + 2 * KV_WIDTH
AB_COLS = LANES
IN_COLS = ATTN_COLS + GDN_CONV_DIM + GDN_VAL_WIDTH + AB_COLS
OFF_GDN = ATTN_COLS
OFF_Z = OFF_GDN + GDN_CONV_DIM
OFF_AB = OFF_Z + GDN_VAL_WIDTH
CHUNK = 128
VMEM_LIMIT = 56 * 1024 * 1024


def _rms(x, w):
    return x * lax.rsqrt(jnp.mean(x * x, axis=-1, keepdims=True) + EPS) * w


def _sigmoid(x):
    return 1.0 / (1.0 + jnp.exp(-x))


def _softplus(x):
    return jnp.maximum(x, 0.0) + jnp.log(1.0 + jnp.exp(-jnp.abs(x)))


def _dot(a, b):
    return jnp.dot(a.astype(BF16), b.astype(BF16), preferred_element_type=F32)


def _dot_nt(a, b):
    return lax.dot_general(a.astype(BF16), b.astype(BF16), (((1,), (1,)), ((), ())),
                           preferred_element_type=F32)


def _inproj_kernel(x_ref, nw_ref, w_ref, oa_ref, og_ref, oz_ref, oab_ref):
    h = _rms(x_ref[...], nw_ref[...]).astype(BF16)
    oa_ref[...] = jnp.dot(h, w_ref[:, 0:OFF_GDN], preferred_element_type=F32)
    og_ref[...] = jnp.dot(h, w_ref[:, OFF_GDN:OFF_Z], preferred_element_type=F32)
    oz_ref[...] = jnp.dot(h, w_ref[:, OFF_Z:OFF_AB], preferred_element_type=F32)
    oab_ref[...] = jnp.dot(h, w_ref[:, OFF_AB:IN_COLS], preferred_element_type=F32)


def _inproj(x, nw, w, tm):
    m = x.shape[0]
    row = lambda i: (i, 0)
    fixed = lambda i: (0, 0)
    return pl.pallas_call(
        _inproj_kernel,
        grid=(m // tm,),
        in_specs=[pl.BlockSpec((tm, D_MODEL), row),
                  pl.BlockSpec((1, D_MODEL), fixed),
                  pl.BlockSpec((D_MODEL, IN_COLS), fixed)],
        out_specs=[pl.BlockSpec((tm, ATTN_COLS), row),
                   pl.BlockSpec((tm, GDN_CONV_DIM), row),
                   pl.BlockSpec((tm, GDN_VAL_WIDTH), row),
                   pl.BlockSpec((tm, AB_COLS), row)],
        out_shape=[jax.ShapeDtypeStruct((m, ATTN_COLS), F32),
                   jax.ShapeDtypeStruct((m, GDN_CONV_DIM), F32),
                   jax.ShapeDtypeStruct((m, GDN_VAL_WIDTH), F32),
                   jax.ShapeDtypeStruct((m, AB_COLS), F32)],
        compiler_params=pltpu.CompilerParams(dimension_semantics=("arbitrary",),
                                             vmem_limit_bytes=VMEM_LIMIT),
        name="inproj",
    )(x, nw, w)


FF_CHUNK = 512


def _out_ffn_kernel(x_ref, ao_ref, gy_ref, wout_ref, npost_ref, npre_ref, wup_ref, wdown_ref,
                    nffn_ref, y_ref):
    m = (jnp.dot(ao_ref[...].astype(BF16), wout_ref[0:ATTN_WIDTH, :], preferred_element_type=F32)
         + jnp.dot(gy_ref[...].astype(BF16), wout_ref[ATTN_WIDTH:, :], preferred_element_type=F32))
    x1 = x_ref[...] + _rms(m, npost_ref[...])
    h = _rms(x1, npre_ref[...]).astype(BF16)
    acc = None
    for j in range(0, D_FF, FF_CHUNK):
        u = jnp.maximum(jnp.dot(h, wup_ref[:, j:j + FF_CHUNK], preferred_element_type=F32), 0.0)
        d = jnp.dot((u * u).astype(BF16), wdown_ref[j:j + FF_CHUNK, :], preferred_element_type=F32)
        acc = d if acc is None else acc + d
    y_ref[...] = x1 + _rms(acc, nffn_ref[...])


def _out_ffn(x, ao, gy, wout, npost, npre, wup, wdown, nffn, tm):
    m = x.shape[0]
    row = lambda i: (i, 0)
    fixed = lambda i: (0, 0)
    return pl.pallas_call(
        _out_ffn_kernel,
        grid=(m // tm,),
        in_specs=[pl.BlockSpec((tm, D_MODEL), row),
                  pl.BlockSpec((tm, ATTN_WIDTH), row),
                  pl.BlockSpec((tm, GDN_VAL_WIDTH), row),
                  pl.BlockSpec((D_MODEL, D_MODEL), fixed, pipeline_mode=pl.Buffered(1)),
                  pl.BlockSpec((1, D_MODEL), fixed),
                  pl.BlockSpec((1, D_MODEL), fixed),
                  pl.BlockSpec((D_MODEL, D_FF), fixed, pipeline_mode=pl.Buffered(1)),
                  pl.BlockSpec((D_FF, D_MODEL), fixed, pipeline_mode=pl.Buffered(1)),
                  pl.BlockSpec((1, D_MODEL), fixed)],
        out_specs=pl.BlockSpec((tm, D_MODEL), row),
        out_shape=jax.ShapeDtypeStruct((m, D_MODEL), F32),
        compiler_params=pltpu.CompilerParams(dimension_semantics=("arbitrary",),
                                             vmem_limit_bytes=VMEM_LIMIT),
        name="out_ffn",
    )(x, ao, gy, wout, npost, npre, wup, wdown, nffn)


ATTN_TQ = 256


def _alibi_slope(h):
    return 2.0 ** (-8.0 * (h + 1) / ATTN_HEADS)


def _dup_half(x, lo):
    xr = pltpu.roll(x, HEAD_DIM, axis=1)
    return jnp.where(lo, x, xr), jnp.where(lo, xr, x)


def _attn_prompt_kernel(sink_ref, cur_ref, prev_ref, o_ref):
    i = pl.program_id(1)
    cur = cur_ref[...]
    q = cur[:, 0:ATTN_WIDTH]
    kall = jnp.concatenate([prev_ref[:, 0:KV_WIDTH], cur[:, ATTN_WIDTH:ATTN_WIDTH + KV_WIDTH]], axis=0)
    vall = jnp.concatenate([prev_ref[:, KV_WIDTH:], cur[:, ATTN_WIDTH + KV_WIDTH:]], axis=0)
    lo = lax.broadcasted_iota(jnp.int32, (1, LANES), 1) < HEAD_DIM
    k2 = _dup_half(kall, lo)
    v2 = _dup_half(vall, lo)
    qi = lax.broadcasted_iota(jnp.int32, (WINDOW, 2 * WINDOW), 0)
    si = lax.broadcasted_iota(jnp.int32, (WINDOW, 2 * WINDOW), 1)
    dist = WINDOW + qi - si
    band = (dist >= 0) & (dist <= WINDOW)
    distf = dist.astype(F32)
    for r in range(ATTN_TQ // WINDOW):
        qb = q[r * WINDOW:(r + 1) * WINDOW]
        valid = band
        if r == 0:
            valid = band & ((i * ATTN_TQ - WINDOW + si) >= 0)
        for j in range(KV_HEADS):
            kw = k2[j][r * WINDOW:(r + 2) * WINDOW]
            vw = v2[j][r * WINDOW:(r + 2) * WINDOW]
            t0 = qb[:, (2 * j) * LANES:(2 * j + 1) * LANES]
            t1 = qb[:, (2 * j + 1) * LANES:(2 * j + 2) * LANES]
            lhs = jnp.concatenate([jnp.where(lo, t0, 0.0), jnp.where(lo, 0.0, t0),
                                   jnp.where(lo, t1, 0.0), jnp.where(lo, 0.0, t1)], axis=0)
            s = _dot_nt(lhs, kw) * (HEAD_DIM ** -0.5)
            probs = []
            for g in range(GQA_GROUP):
                h = j * GQA_GROUP + g
                sg = s[g * WINDOW:(g + 1) * WINDOW] - _alibi_slope(h) * distf
                sg = jnp.where(valid, sg, -jnp.inf)
                sink = sink_ref[0, h]
                mx = jnp.maximum(jnp.max(sg, axis=-1, keepdims=True), sink)
                p = jnp.exp(sg - mx)
                den = jnp.sum(p, axis=-1, keepdims=True) + jnp.exp(sink - mx)
                probs.append(p / den)
            o = _dot(jnp.concatenate(probs, axis=0), vw)
            o01 = jnp.where(lo, o[0:WINDOW], o[WINDOW:2 * WINDOW])
            o23 = jnp.where(lo, o[2 * WINDOW:3 * WINDOW], o[3 * WINDOW:4 * WINDOW])
            rows = slice(r * WINDOW, (r + 1) * WINDOW)
            o_ref[rows, (2 * j) * LANES:(2 * j + 1) * LANES] = o01.astype(o_ref.dtype)
            o_ref[rows, (2 * j + 1) * LANES:(2 * j + 2) * LANES] = o23.astype(o_ref.dtype)


def _attn_prompt(pa, sinks):
    b, l, _ = pa.shape
    blocks_per_step = ATTN_TQ // WINDOW
    return pl.pallas_call(
        _attn_prompt_kernel,
        grid=(b, l // ATTN_TQ),
        in_specs=[pl.BlockSpec(memory_space=pltpu.SMEM),
                  pl.BlockSpec((None, ATTN_TQ, ATTN_COLS), lambda bb, i: (bb, i, 0)),
                  pl.BlockSpec((None, WINDOW, 2 * KV_WIDTH),
                               lambda bb, i: (bb, jnp.maximum(i * blocks_per_step - 1, 0),
                                              ATTN_WIDTH // (2 * KV_WIDTH)))],
        out_specs=pl.BlockSpec((None, ATTN_TQ, ATTN_WIDTH), lambda bb, i: (bb, i, 0)),
        out_shape=jax.ShapeDtypeStruct((b, l, ATTN_WIDTH), BF16),
        compiler_params=pltpu.CompilerParams(dimension_semantics=("arbitrary", "arbitrary"),
                                             vmem_limit_bytes=VMEM_LIMIT),
        name="attn_prompt",
    )(sinks, pa, pa)


GDN_T = 256


def _conv_silu(x, carry, w):
    row8 = lax.broadcasted_iota(jnp.int32, (SUBLANES, 1), 0)
    y = None
    for shift in range(CONV_W - 1, -1, -1):
        if shift == 0:
            xs = x
        else:
            xr = pltpu.roll(x, shift, axis=0)
            cr = pltpu.roll(carry, shift, axis=0)
            head = jnp.where(row8 < shift, cr, xr[0:SUBLANES])
            xs = jnp.concatenate([head, xr[SUBLANES:]], axis=0)
        term = xs * w[CONV_W - 1 - shift:CONV_W - shift]
        y = term if y is None else y + term
    return y * _sigmoid(y)


def _l2n(x):
    return x * lax.rsqrt(jnp.sum(x * x, axis=-1, keepdims=True) + EPS)


def _unit_lower_inverse_minus_eye(mats, row, col):
    n = mats[0].shape[0]
    first = ((row >> 1) == (col >> 1)) & ((row & 1) == 1) & ((col & 1) == 0)
    xo = [jnp.where(first, -a, 0.0) for a in mats]
    size = 2
    while size < n:
        shift = size.bit_length()
        m = ((row >> shift) == (col >> shift)) & ((row & size) != 0) & ((col & size) == 0)
        aoff = [jnp.where(m, a, 0.0) for a in mats]
        y = [ao + _dot(ao, x) for ao, x in zip(aoff, xo)]
        xo = [x - yy - _dot(x, yy) for x, yy in zip(xo, y)]
        size *= 2
    return xo


def _gdn_prompt_kernel(pg_ref, pab_ref, pz_ref, cw_ref, alog_ref, dtb_ref, gnw_ref,
                       gy_ref, s_out_ref, conv_out_ref, s_scr, carry_scr):
    i = pl.program_id(1)

    @pl.when(i == 0)
    def _():
        s_scr[...] = jnp.zeros_like(s_scr)
        carry_scr[...] = jnp.zeros_like(carry_scr)

    x = pg_ref[...]
    act = _conv_silu(x, carry_scr[...], cw_ref[...])
    tail = x[GDN_T - SUBLANES:GDN_T]
    carry_scr[...] = tail
    conv_out_ref[...] = tail

    ab = pab_ref[...]
    g = -jnp.exp(alog_ref[...]) * _softplus(ab + dtb_ref[...])
    beta = _sigmoid(ab)
    rin = lax.broadcasted_iota(jnp.int32, (GDN_T, 1), 0) & (CHUNK - 1)
    step = 1
    while step < CHUNK:
        g = g + jnp.where(rin >= step, pltpu.roll(g, step, axis=0), 0.0)
        step *= 2
    gt = g.T

    row = lax.broadcasted_iota(jnp.int32, (CHUNK, CHUNK), 0)
    col = lax.broadcasted_iota(jnp.int32, (CHUNK, CHUNK), 1)
    lower_incl = row >= col
    lower_strict = row > col
    gnw = gnw_ref[...]

    heads = range(GDN_HEADS)
    units = [(c, h) for c in range(GDN_T // CHUNK) for h in heads]
    qn = [_l2n(act[:, h * LANES:(h + 1) * LANES]) * (GDN_DK ** -0.5) for h in heads]
    kn = [_l2n(act[:, GDN_KEY_WIDTH + h * LANES:GDN_KEY_WIDTH + (h + 1) * LANES]) for h in heads]
    pre = []
    for c, h in units:
        rows = slice(c * CHUNK, (c + 1) * CHUNK)
        kc, qc = kn[h][rows], qn[h][rows]
        vc = act[rows, 2 * GDN_KEY_WIDTH + h * LANES:2 * GDN_KEY_WIDTH + (h + 1) * LANES]
        gc = g[rows, h:h + 1]
        bc = beta[rows, GDN_HEADS + h:GDN_HEADS + h + 1]
        gr = gt[h:h + 1, c * CHUNK:(c + 1) * CHUNK]
        decay = jnp.where(lower_incl, jnp.exp(jnp.where(lower_incl, gc - gr, 0.0)), 0.0)
        kq = _dot_nt(jnp.concatenate([kc, qc], axis=0), kc)
        a = jnp.where(lower_strict, bc * kq[0:CHUNK] * decay, 0.0)
        qk = jnp.where(lower_incl, kq[CHUNK:] * decay, 0.0)
        eg = jnp.exp(gc)
        glast = gc[CHUNK - 1:CHUNK]
        rhs = jnp.concatenate([vc * bc, kc * (bc * eg)], axis=1)
        k_dec_t = (kc * jnp.exp(glast - gc)).T
        pre.append(dict(a=a, rhs=rhs, lhs_u=jnp.concatenate([qk, k_dec_t], axis=0), q_dec=qc * eg,
                        ld=jnp.exp(glast)))
    xos = _unit_lower_inverse_minus_eye([p["a"] for p in pre], row, col)
    trs = [p["rhs"] + _dot(xo, p["rhs"]) for p, xo in zip(pre, xos)]

    state = [s_scr[h] for h in heads]
    for c in range(GDN_T // CHUNK):
        rows = slice(c * CHUNK, (c + 1) * CHUNK)
        idx = [c * GDN_HEADS + h for h in heads]
        ws = [_dot(jnp.concatenate([trs[u][:, GDN_DV:], pre[u]["q_dec"]], axis=0), state[h])
              for h, u in zip(heads, idx)]
        ou = [_dot(pre[u]["lhs_u"], trs[u][:, 0:GDN_DV] - ws[h][0:CHUNK]) for h, u in zip(heads, idx)]
        for h, u in zip(heads, idx):
            lanes = slice(h * LANES, (h + 1) * LANES)
            o = ws[h][CHUNK:] + ou[h][0:CHUNK]
            state[h] = state[h] * pre[u]["ld"] + ou[h][CHUNK:]
            z = pz_ref[rows, lanes]
            gy_ref[rows, lanes] = (_rms(o, gnw) * (z * _sigmoid(z))).astype(gy_ref.dtype)
    for h in heads:
        s_scr[h] = state[h]
        s_out_ref[h] = state[h]


def _gdn_prompt(pg, pab, pz, cw, alog, dtb, gnw):
    b, l, _ = pg.shape
    tok = lambda bb, i: (bb, i, 0)
    fixed = lambda bb, i: (0, 0)
    return pl.pallas_call(
        _gdn_prompt_kernel,
        grid=(b, l // GDN_T),
        in_specs=[pl.BlockSpec((None, GDN_T, GDN_CONV_DIM), tok),
                  pl.BlockSpec((None, GDN_T, AB_COLS), tok),
                  pl.BlockSpec((None, GDN_T, GDN_VAL_WIDTH), tok),
                  pl.BlockSpec((CONV_W, GDN_CONV_DIM), fixed),
                  pl.BlockSpec((1, LANES), fixed),
                  pl.BlockSpec((1, LANES), fixed),
                  pl.BlockSpec((1, GDN_DV), fixed)],
        out_specs=[pl.BlockSpec((None, GDN_T, GDN_VAL_WIDTH), tok),
                   pl.BlockSpec((None, GDN_HEADS, GDN_DK, GDN_DV), lambda bb, i: (bb, 0, 0, 0)),
                   pl.BlockSpec((None, SUBLANES, GDN_CONV_DIM), lambda bb, i: (bb, 0, 0))],
        out_shape=[jax.ShapeDtypeStruct((b, l, GDN_VAL_WIDTH), BF16),
                   jax.ShapeDtypeStruct((b, GDN_HEADS, GDN_DK, GDN_DV), F32),
                   jax.ShapeDtypeStruct((b, SUBLANES, GDN_CONV_DIM), F32)],
        scratch_shapes=[pltpu.VMEM((GDN_HEADS, GDN_DK, GDN_DV), F32),
                        pltpu.VMEM((SUBLANES, GDN_CONV_DIM), F32)],
        compiler_params=pltpu.CompilerParams(dimension_semantics=("arbitrary", "arbitrary"),
                                             vmem_limit_bytes=VMEM_LIMIT),
        name="gdn_prompt",
    )(pg, pab, pz, cw, alog, dtb, gnw)


def _sample_prep_kernel(x_ref, nw_ref, w_ref, cs_ref, cw_ref, alog_ref, dtb_ref,
                        qexp_ref, knew_ref, vnew_ref, qt_ref, kt_ref, vg_ref, gate_ref, z_ref,
                        cs_out_ref):
    h = _rms(x_ref[...], nw_ref[...]).astype(BF16)
    pa = jnp.dot(h, w_ref[:, 0:OFF_GDN], preferred_element_type=F32)
    pg = jnp.dot(h, w_ref[:, OFF_GDN:OFF_Z], preferred_element_type=F32)
    z_ref[...] = jnp.dot(h, w_ref[:, OFF_Z:OFF_AB], preferred_element_type=F32)
    ab = jnp.dot(h, w_ref[:, OFF_AB:IN_COLS], preferred_element_type=F32)

    lo = lax.broadcasted_iota(jnp.int32, (1, LANES), 1) < HEAD_DIM
    for t in range(ATTN_WIDTH // LANES):
        tile = pa[:, t * LANES:(t + 1) * LANES]
        rolled = pltpu.roll(tile, HEAD_DIM, axis=1)
        if t < ATTN_WIDTH // LANES // KV_HEADS:
            even, odd = jnp.where(lo, tile, 0.0), jnp.where(lo, rolled, 0.0)
        else:
            even, odd = jnp.where(lo, 0.0, rolled), jnp.where(lo, 0.0, tile)
        qexp_ref[2 * t] = even
        qexp_ref[2 * t + 1] = odd
    knew_ref[...] = pa[:, ATTN_WIDTH:ATTN_WIDTH + KV_WIDTH]
    vnew_ref[...] = pa[:, ATTN_WIDTH + KV_WIDTH:ATTN_COLS]

    cw = cw_ref[...]
    y = None
    for r in range(CONV_W - 1):
        term = cs_ref[:, r * GDN_CONV_DIM:(r + 1) * GDN_CONV_DIM] * cw[r:r + 1]
        y = term if y is None else y + term
    y = y + pg * cw[CONV_W - 1:CONV_W]
    act = y * _sigmoid(y)
    cs_out_ref[:, 0:(CONV_W - 2) * GDN_CONV_DIM] = cs_ref[:, GDN_CONV_DIM:(CONV_W - 1) * GDN_CONV_DIM]
    cs_out_ref[:, (CONV_W - 2) * GDN_CONV_DIM:] = pg

    for hd in range(GDN_HEADS):
        qn = _l2n(act[:, hd * LANES:(hd + 1) * LANES]) * (GDN_DK ** -0.5)
        kn = _l2n(act[:, GDN_KEY_WIDTH + hd * LANES:GDN_KEY_WIDTH + (hd + 1) * LANES])
        qt_ref[hd] = qn.T
        kt_ref[hd] = kn.T
    vg_ref[...] = act[:, 2 * GDN_KEY_WIDTH:]
    g = -jnp.exp(alog_ref[...]) * _softplus(ab + dtb_ref[...])
    lane = lax.broadcasted_iota(jnp.int32, (1, LANES), 1)
    gate_ref[...] = jnp.where(lane < GDN_HEADS, jnp.exp(g), _sigmoid(ab))


def _sample_prep(x, nw, w, cs, cw, alog, dtb):
    n = x.shape[0]
    return pl.pallas_call(
        _sample_prep_kernel,
        out_shape=[jax.ShapeDtypeStruct((ATTN_HEADS, n, LANES), F32),
                   jax.ShapeDtypeStruct((n, KV_WIDTH), F32),
                   jax.ShapeDtypeStruct((n, KV_WIDTH), F32),
                   jax.ShapeDtypeStruct((GDN_HEADS, GDN_DK, n), F32),
                   jax.ShapeDtypeStruct((GDN_HEADS, GDN_DK, n), F32),
                   jax.ShapeDtypeStruct((n, GDN_VAL_WIDTH), F32),
                   jax.ShapeDtypeStruct((n, LANES), F32),
                   jax.ShapeDtypeStruct((n, GDN_VAL_WIDTH), F32),
                   jax.ShapeDtypeStruct((n, (CONV_W - 1) * GDN_CONV_DIM), F32)],
        compiler_params=pltpu.CompilerParams(vmem_limit_bytes=VMEM_LIMIT),
        name="sample_prep",
    )(x, nw, w, cs, cw, alog, dtb)


ATTN_BB = 16


def _attn_sample_kernel(sink_ref, qexp_ref, knew_ref, vnew_ref, kbuf_ref, vbuf_ref,
                        o_ref, kout_ref, vout_ref, q_scr, o_scr):
    for h in range(ATTN_HEADS):
        q_scr[pl.ds(h, ATTN_BB, stride=ATTN_HEADS), :] = qexp_ref[h]
    hrow = lax.broadcasted_iota(jnp.int32, (ATTN_HEADS, 1), 0)
    sink = jnp.zeros((ATTN_HEADS, 1), F32)
    slope = jnp.zeros((ATTN_HEADS, 1), F32)
    for h in range(ATTN_HEADS):
        sink = jnp.where(hrow == h, sink_ref[0, h], sink)
        slope = jnp.where(hrow == h, _alibi_slope(h), slope)
    pos = lax.broadcasted_iota(jnp.int32, (1, WINDOW), 1)
    bias = slope * (WINDOW - pos).astype(F32)
    rowid = lax.broadcasted_iota(jnp.int32, (WINDOW, 1), 0)
    last = rowid == WINDOW - 1
    for b in range(ATTN_BB):
        qe = q_scr[b * ATTN_HEADS:(b + 1) * ATTN_HEADS, :]
        kb, vb = kbuf_ref[b], vbuf_ref[b]
        kn, vn = knew_ref[b:b + 1, :], vnew_ref[b:b + 1, :]
        s_old = _dot_nt(qe, kb) * (HEAD_DIM ** -0.5) - bias
        s_new = jnp.sum(qe * kn, axis=-1, keepdims=True) * (HEAD_DIM ** -0.5)
        mx = jnp.maximum(jnp.maximum(jnp.max(s_old, axis=-1, keepdims=True), s_new), sink)
        p_old = jnp.exp(s_old - mx)
        p_new = jnp.exp(s_new - mx)
        den = jnp.sum(p_old, axis=-1, keepdims=True) + p_new + jnp.exp(sink - mx)
        o = _dot(p_old / den, vb) + (p_new / den) * vn
        o_scr[b * ATTN_HEADS:(b + 1) * ATTN_HEADS, :] = o
        kout_ref[b] = jnp.where(last, kn, pltpu.roll(kb, WINDOW - 1, axis=0))
        vout_ref[b] = jnp.where(last, vn, pltpu.roll(vb, WINDOW - 1, axis=0))
    lo = lax.broadcasted_iota(jnp.int32, (1, LANES), 1) < HEAD_DIM
    for t in range(ATTN_WIDTH // LANES):
        even = o_scr[pl.ds(2 * t, ATTN_BB, stride=ATTN_HEADS), :]
        odd = o_scr[pl.ds(2 * t + 1, ATTN_BB, stride=ATTN_HEADS), :]
        if t < ATTN_WIDTH // LANES // KV_HEADS:
            tile = jnp.where(lo, even, pltpu.roll(odd, HEAD_DIM, axis=1))
        else:
            tile = jnp.where(lo, pltpu.roll(even, HEAD_DIM, axis=1), odd)
        o_ref[:, t * LANES:(t + 1) * LANES] = tile.astype(o_ref.dtype)


def _attn_sample(sinks, qexp, knew, vnew, kbuf, vbuf):
    n = knew.shape[0]
    tok2 = lambda i: (i, 0)
    tok3 = lambda i: (i, 0, 0)
    return pl.pallas_call(
        _attn_sample_kernel,
        grid=(n // ATTN_BB,),
        in_specs=[pl.BlockSpec(memory_space=pltpu.SMEM),
                  pl.BlockSpec((ATTN_HEADS, ATTN_BB, LANES), lambda i: (0, i, 0)),
                  pl.BlockSpec((ATTN_BB, KV_WIDTH), tok2),
                  pl.BlockSpec((ATTN_BB, KV_WIDTH), tok2),
                  pl.BlockSpec((ATTN_BB, WINDOW, KV_WIDTH), tok3),
                  pl.BlockSpec((ATTN_BB, WINDOW, KV_WIDTH), tok3)],
        out_specs=[pl.BlockSpec((ATTN_BB, ATTN_WIDTH), tok2),
                   pl.BlockSpec((ATTN_BB, WINDOW, KV_WIDTH), tok3),
                   pl.BlockSpec((ATTN_BB, WINDOW, KV_WIDTH), tok3)],
        out_shape=[jax.ShapeDtypeStruct((n, ATTN_WIDTH), BF16),
                   jax.ShapeDtypeStruct((n, WINDOW, KV_WIDTH), F32),
                   jax.ShapeDtypeStruct((n, WINDOW, KV_WIDTH), F32)],
        scratch_shapes=[pltpu.VMEM((ATTN_BB * ATTN_HEADS, LANES), F32),
                        pltpu.VMEM((ATTN_BB * ATTN_HEADS, LANES), F32)],
        compiler_params=pltpu.CompilerParams(dimension_semantics=("arbitrary",),
                                             vmem_limit_bytes=VMEM_LIMIT),
        name="attn_sample",
    )(sinks, qexp, knew, vnew, kbuf, vbuf)


GDN_BB = 8


def _gdn_sample_kernel(gate_ref, qt_ref, kt_ref, vg_ref, z_ref, gnw_ref, s_ref, o_ref, s_out_ref):
    i = pl.program_id(0)
    n = qt_ref.shape[-1]
    shift = (n - i * GDN_BB) % n
    gnw = gnw_ref[...]
    for h in range(GDN_HEADS):
        qt = pltpu.roll(qt_ref[h], shift, axis=1)
        kt = pltpu.roll(kt_ref[h], shift, axis=1)
        lanes = slice(h * LANES, (h + 1) * LANES)
        for j in range(GDN_BB):
            tokn = i * GDN_BB + j
            decay = gate_ref[tokn * SUBLANES + h]
            beta = gate_ref[tokn * SUBLANES + GDN_HEADS + h]
            kcol = kt[:, j:j + 1]
            qcol = qt[:, j:j + 1]
            s = s_ref[j, h] * decay
            kv = jnp.sum(s * kcol, axis=0, keepdims=True)
            u = (vg_ref[j:j + 1, lanes] - kv) * beta
            s = s + kcol * u
            s_out_ref[j, h] = s
            o = jnp.sum(s * qcol, axis=0, keepdims=True)
            z = z_ref[j:j + 1, lanes]
            o_ref[j:j + 1, lanes] = (_rms(o, gnw) * (z * _sigmoid(z))).astype(o_ref.dtype)


def _gdn_sample(gate, qt, kt, vg, z, gnw, s0):
    n = vg.shape[0]
    return pl.pallas_call(
        _gdn_sample_kernel,
        grid=(n // GDN_BB,),
        in_specs=[pl.BlockSpec(memory_space=pltpu.SMEM),
                  pl.BlockSpec((GDN_HEADS, GDN_DK, n), lambda i: (0, 0, 0)),
                  pl.BlockSpec((GDN_HEADS, GDN_DK, n), lambda i: (0, 0, 0)),
                  pl.BlockSpec((GDN_BB, GDN_VAL_WIDTH), lambda i: (i, 0)),
                  pl.BlockSpec((GDN_BB, GDN_VAL_WIDTH), lambda i: (i, 0)),
                  pl.BlockSpec((1, GDN_DV), lambda i: (0, 0)),
                  pl.BlockSpec((GDN_BB, GDN_HEADS, GDN_DK, GDN_DV), lambda i: (i, 0, 0, 0))],
        out_specs=[pl.BlockSpec((GDN_BB, GDN_VAL_WIDTH), lambda i: (i, 0)),
                   pl.BlockSpec((GDN_BB, GDN_HEADS, GDN_DK, GDN_DV), lambda i: (i, 0, 0, 0))],
        out_shape=[jax.ShapeDtypeStruct((n, GDN_VAL_WIDTH), F32),
                   jax.ShapeDtypeStruct((n, GDN_HEADS, GDN_DK, GDN_DV), F32)],
        compiler_params=pltpu.CompilerParams(dimension_semantics=("arbitrary",),
                                             vmem_limit_bytes=VMEM_LIMIT),
        name="gdn_sample",
    )(gate, qt, kt, vg, z, gnw, s0)


def _pad_lanes(v):
    return jnp.pad(v.astype(F32), (0, LANES - v.shape[0])).reshape(1, LANES)


def kernel(x_prompt, x_sample, state_conv, cache_win_k, cache_win_v, state_gdn, norm_mix_pre, w_in,
           attn_sinks, conv_w, gdn_a_log, gdn_dt_bias, gdn_norm, w_out, norm_mix_post, norm_ffn_pre,
           w_up, w_down, norm_ffn_post):
    assert w_in.shape[0] == 1, "single-layer trunk"
    bsz, seq, _ = x_prompt.shape
    nsamp = x_sample.shape[0]

    w_in_p = jnp.pad(w_in[0], ((0, 0), (0, IN_COLS - w_in.shape[2]))).astype(BF16)
    w_out_b = w_out[0].astype(BF16)
    w_up_b = w_up[0].astype(BF16)
    w_down_b = w_down[0].astype(BF16)
    n_pre = norm_mix_pre[0].reshape(1, D_MODEL)
    n_post = norm_mix_post[0].reshape(1, D_MODEL)
    n_fpre = norm_ffn_pre[0].reshape(1, D_MODEL)
    n_fpost = norm_ffn_post[0].reshape(1, D_MODEL)
    sinks = attn_sinks[0].reshape(1, ATTN_HEADS)
    cw = conv_w[0]
    alog = _pad_lanes(gdn_a_log[0])
    dtb = _pad_lanes(gdn_dt_bias[0])
    gnw = gdn_norm[0].reshape(1, GDN_DV)

    xp = x_prompt.reshape(bsz * seq, D_MODEL)
    pa, pg, pz, pab = _inproj(xp, n_pre, w_in_p, 512)
    pa = pa.reshape(bsz, seq, ATTN_COLS)
    ao = _attn_prompt(pa, sinks)
    gy, p_s, p_conv = _gdn_prompt(pg.reshape(bsz, seq, GDN_CONV_DIM), pab.reshape(bsz, seq, AB_COLS),
                                  pz.reshape(bsz, seq, GDN_VAL_WIDTH), cw, alog, dtb, gnw)
    yp = _out_ffn(xp, ao.reshape(bsz * seq, ATTN_WIDTH), gy.reshape(bsz * seq, GDN_VAL_WIDTH),
                  w_out_b, n_post, n_fpre, w_up_b, w_down_b, n_fpost, 512)

    xs = x_sample.reshape(nsamp, D_MODEL)
    cs = state_conv[0].reshape(nsamp, (CONV_W - 1) * GDN_CONV_DIM)
    qexp, knew, vnew, qt, kt, vg, gate, zs, cs_new = _sample_prep(xs, n_pre, w_in_p, cs, cw, alog, dtb)
    kbuf = cache_win_k[0].reshape(nsamp, WINDOW, KV_WIDTH)
    vbuf = cache_win_v[0].reshape(nsamp, WINDOW, KV_WIDTH)
    ao_s, k_new_cache, v_new_cache = _attn_sample(sinks, qexp, knew, vnew, kbuf, vbuf)
    gy_s, s_new = _gdn_sample(gate[:, :SUBLANES].reshape(nsamp * SUBLANES), qt, kt, vg, zs, gnw,
                              state_gdn[0])
    ys = _out_ffn(xs, ao_s, gy_s, w_out_b, n_post, n_fpre, w_up_b, w_down_b, n_fpost, nsamp)

    kv_shape = (1, -1, WINDOW, KV_HEADS, HEAD_DIM)
    return (yp.reshape(bsz, seq, D_MODEL),
            ys.reshape(nsamp, 1, D_MODEL),
            p_conv[:, SUBLANES - (CONV_W - 1):, :][None],
            pa[:, seq - WINDOW:, ATTN_WIDTH:ATTN_WIDTH + KV_WIDTH].reshape(kv_shape),
            pa[:, seq - WINDOW:, ATTN_WIDTH + KV_WIDTH:].reshape(kv_shape),
            p_s[None],
            cs_new.reshape(1, nsamp, CONV_W - 1, GDN_CONV_DIM),
            k_new_cache.reshape(kv_shape),
            v_new_cache.reshape(kv_shape),
            s_new[None])
```

```python
import functools

import jax
import jax.numpy as jnp
from jax import lax
from jax.experimental import pallas as pl
from jax.experimental.pallas import tpu as pltpu

F32 = jnp.float32
BF16 = jnp.bfloat16

D_MODEL = 1024
ATTN_HEADS = 8
KV_HEADS = 2
HEAD_DIM = 64
GQA_GROUP = ATTN_HEADS // KV_HEADS
ATTN_WIDTH = ATTN_HEADS * HEAD_DIM
KV_WIDTH = KV_HEADS * HEAD_DIM
WINDOW = 128
GDN_HEADS = 4
GDN_DK = 128
GDN_DV = 128
GDN_KEY_WIDTH = GDN_HEADS * GDN_DK
GDN_VAL_WIDTH = GDN_HEADS * GDN_DV
GDN_CONV_DIM = 2 * GDN_KEY_WIDTH + GDN_VAL_WIDTH
CONV_W = 4
D_FF = 4 * D_MODEL
EPS = 1e-6

LANES = 128
SUBLANES = 8
ATTN_COLS = ATTN_WIDTH + 2 * KV_WIDTH
AB_COLS = LANES
IN_COLS = ATTN_COLS + GDN_CONV_DIM + GDN_VAL_WIDTH + AB_COLS
OFF_GDN = ATTN_COLS
OFF_Z = OFF_GDN + GDN_CONV_DIM
OFF_AB = OFF_Z + GDN_VAL_WIDTH
CHUNK = 128
VMEM_LIMIT = 56 * 1024 * 1024


def _rms(x, w):
    return x * lax.rsqrt(jnp.mean(x * x, axis=-1, keepdims=True) + EPS) * w


def _sigmoid(x):
    return 1.0 / (1.0 + jnp.exp(-x))


def _softplus(x):
    return jnp.maximum(x, 0.0) + jnp.log(1.0 + jnp.exp(-jnp.abs(x)))


def _dot(a, b):
    return jnp.dot(a.astype(BF16), b.astype(BF16), preferred_element_type=F32)


def _dot_nt(a, b):
    return lax.dot_general(a.astype(BF16), b.astype(BF16), (((1,), (1,)), ((), ())),
                           preferred_element_type=F32)


def _conv_silu(x, carry, w):
    t, c = x.shape
    x3 = x.reshape(t // SUBLANES, SUBLANES, c)
    row8 = lax.broadcasted_iota(jnp.int32, (1, SUBLANES, 1), 1)
    y = None
    for shift in range(CONV_W - 1, -1, -1):
        if shift == 0:
            xs = x3
        else:
            xr = pltpu.roll(x3, shift, axis=1)
            cr = pltpu.roll(carry, shift, axis=0)
            prev = jnp.concatenate([cr[None], xr[:-1]], axis=0)
            xs = jnp.where(row8 < shift, prev, xr)
        term = xs * w[CONV_W - 1 - shift:CONV_W - shift][None]
        y = term if y is None else y + term
    return (y * _sigmoid(y)).reshape(t, c)


def _l2n(x):
    return x * lax.rsqrt(jnp.sum(x * x, axis=-1, keepdims=True) + EPS)


def _gates(ab, alog, dtb):
    g = -jnp.exp(alog) * _softplus(ab + dtb)
    lane = lax.broadcasted_iota(jnp.int32, (1, LANES), 1)
    return jnp.where(lane < GDN_HEADS, g, _sigmoid(ab))


def _inproj_kernel(x_ref, nw_ref, w_ref, cw_ref, alog_ref, dtb_ref,
                   oa_ref, og_ref, oz_ref, ogb_ref, conv_out_ref, carry_scr, *, tiles_per_seq):
    tm = x_ref.shape[0]

    @pl.when(lax.rem(pl.program_id(0), tiles_per_seq) == 0)
    def _():
        carry_scr[...] = jnp.zeros_like(carry_scr)

    h = _rms(x_ref[...], nw_ref[...]).astype(BF16)
    pg = jnp.dot(h, w_ref[:, OFF_GDN:OFF_Z], preferred_element_type=F32)
    oa_ref[...] = jnp.dot(h, w_ref[:, 0:OFF_GDN], preferred_element_type=F32)
    oz_ref[...] = jnp.dot(h, w_ref[:, OFF_Z:OFF_AB], preferred_element_type=F32)
    ab = jnp.dot(h, w_ref[:, OFF_AB:IN_COLS], preferred_element_type=F32)
    ogb_ref[...] = _gates(ab, alog_ref[...], dtb_ref[...])
    act = _conv_silu(pg, carry_scr[...], cw_ref[...])
    tail = pg[tm - SUBLANES:tm]
    carry_scr[...] = tail
    conv_out_ref[...] = tail
    for hd in range(GDN_HEADS):
        ql = slice(hd * LANES, (hd + 1) * LANES)
        kl = slice(GDN_KEY_WIDTH + hd * LANES, GDN_KEY_WIDTH + (hd + 1) * LANES)
        og_ref[:, ql] = _l2n(act[:, ql]) * (GDN_DK ** -0.5)
        og_ref[:, kl] = _l2n(act[:, kl])
    og_ref[:, 2 * GDN_KEY_WIDTH:] = act[:, 2 * GDN_KEY_WIDTH:]


def _inproj(x, nw, w, cw, alog, dtb, tm, seq):
    m = x.shape[0]
    tiles_per_seq = seq // tm
    row = lambda i: (i, 0)
    fixed = lambda i: (0, 0)
    return pl.pallas_call(
        functools.partial(_inproj_kernel, tiles_per_seq=tiles_per_seq),
        grid=(m // tm,),
        in_specs=[pl.BlockSpec((tm, D_MODEL), row),
                  pl.BlockSpec((1, D_MODEL), fixed),
                  pl.BlockSpec((D_MODEL, IN_COLS), fixed),
                  pl.BlockSpec((CONV_W, GDN_CONV_DIM), fixed),
                  pl.BlockSpec((1, LANES), fixed),
                  pl.BlockSpec((1, LANES), fixed)],
        out_specs=[pl.BlockSpec((tm, ATTN_COLS), row),
                   pl.BlockSpec((tm, GDN_CONV_DIM), row),
                   pl.BlockSpec((tm, GDN_VAL_WIDTH), row),
                   pl.BlockSpec((tm, AB_COLS), row),
                   pl.BlockSpec((None, SUBLANES, GDN_CONV_DIM), lambda i: (i // tiles_per_seq, 0, 0))],
        out_shape=[jax.ShapeDtypeStruct((m, ATTN_COLS), F32),
                   jax.ShapeDtypeStruct((m, GDN_CONV_DIM), F32),
                   jax.ShapeDtypeStruct((m, GDN_VAL_WIDTH), F32),
                   jax.ShapeDtypeStruct((m, AB_COLS), F32),
                   jax.ShapeDtypeStruct((m // seq, SUBLANES, GDN_CONV_DIM), F32)],
        scratch_shapes=[pltpu.VMEM((SUBLANES, GDN_CONV_DIM), F32)],
        compiler_params=pltpu.CompilerParams(dimension_semantics=("arbitrary",),
                                             vmem_limit_bytes=VMEM_LIMIT),
        name="inproj",
    )(x, nw, w, cw, alog, dtb)


FF_CHUNK = 512


def _out_ffn_kernel(x_ref, ao_ref, go_ref, z_ref, gnw_ref, wout_ref, npost_ref, npre_ref, wup_ref,
                    wdown_ref, nffn_ref, y_ref):
    gnw = gnw_ref[...]
    gy = []
    for hd in range(GDN_HEADS):
        lanes = slice(hd * LANES, (hd + 1) * LANES)
        z = z_ref[:, lanes]
        gy.append((_rms(go_ref[:, lanes], gnw) * (z * _sigmoid(z))).astype(BF16))
    m = (jnp.dot(ao_ref[...], wout_ref[0:ATTN_WIDTH, :], preferred_element_type=F32)
         + jnp.dot(jnp.concatenate(gy, axis=1), wout_ref[ATTN_WIDTH:, :], preferred_element_type=F32))
    x1 = x_ref[...] + _rms(m, npost_ref[...])
    h = _rms(x1, npre_ref[...]).astype(BF16)
    acc = None
    for j in range(0, D_FF, FF_CHUNK):
        u = jnp.maximum(jnp.dot(h, wup_ref[:, j:j + FF_CHUNK], preferred_element_type=F32), 0.0)
        d = jnp.dot((u * u).astype(BF16), wdown_ref[j:j + FF_CHUNK, :], preferred_element_type=F32)
        acc = d if acc is None else acc + d
    y_ref[...] = x1 + _rms(acc, nffn_ref[...])


def _out_ffn(x, ao, go, z, gnw, wout, npost, npre, wup, wdown, nffn, tm):
    m = x.shape[0]
    row = lambda i: (i, 0)
    fixed = lambda i: (0, 0)
    return pl.pallas_call(
        _out_ffn_kernel,
        grid=(m // tm,),
        in_specs=[pl.BlockSpec((tm, D_MODEL), row),
                  pl.BlockSpec((tm, ATTN_WIDTH), row),
                  pl.BlockSpec((tm, GDN_VAL_WIDTH), row),
                  pl.BlockSpec((tm, GDN_VAL_WIDTH), row),
                  pl.BlockSpec((1, GDN_DV), fixed),
                  pl.BlockSpec((D_MODEL, D_MODEL), fixed, pipeline_mode=pl.Buffered(1)),
                  pl.BlockSpec((1, D_MODEL), fixed),
                  pl.BlockSpec((1, D_MODEL), fixed),
                  pl.BlockSpec((D_MODEL, D_FF), fixed, pipeline_mode=pl.Buffered(1)),
                  pl.BlockSpec((D_FF, D_MODEL), fixed, pipeline_mode=pl.Buffered(1)),
                  pl.BlockSpec((1, D_MODEL), fixed)],
        out_specs=pl.BlockSpec((tm, D_MODEL), row),
        out_shape=jax.ShapeDtypeStruct((m, D_MODEL), F32),
        compiler_params=pltpu.CompilerParams(dimension_semantics=("arbitrary",),
                                             vmem_limit_bytes=VMEM_LIMIT),
        name="out_ffn",
    )(x, ao, go, z, gnw, wout, npost, npre, wup, wdown, nffn)


ATTN_TQ = 256


def _alibi_slope(h):
    return 2.0 ** (-8.0 * (h + 1) / ATTN_HEADS)


def _dup_half(x, lo):
    xr = pltpu.roll(x, HEAD_DIM, axis=1)
    return jnp.where(lo, x, xr), jnp.where(lo, xr, x)


def _attn_prompt_kernel(sink_ref, cur_ref, prev_ref, o_ref):
    i = pl.program_id(1)
    cur = cur_ref[...]
    q = cur[:, 0:ATTN_WIDTH]
    kall = jnp.concatenate([prev_ref[:, 0:KV_WIDTH], cur[:, ATTN_WIDTH:ATTN_WIDTH + KV_WIDTH]], axis=0)
    vall = jnp.concatenate([prev_ref[:, KV_WIDTH:], cur[:, ATTN_WIDTH + KV_WIDTH:]], axis=0)
    lo = lax.broadcasted_iota(jnp.int32, (1, LANES), 1) < HEAD_DIM
    k2 = _dup_half(kall, lo)
    v2 = _dup_half(vall, lo)
    qi = lax.broadcasted_iota(jnp.int32, (WINDOW, 2 * WINDOW), 0)
    si = lax.broadcasted_iota(jnp.int32, (WINDOW, 2 * WINDOW), 1)
    dist = WINDOW + qi - si
    band = (dist >= 0) & (dist <= WINDOW)
    distf = dist.astype(F32)
    for r in range(ATTN_TQ // WINDOW):
        qb = q[r * WINDOW:(r + 1) * WINDOW]
        valid = band
        if r == 0:
            valid = band & ((i * ATTN_TQ - WINDOW + si) >= 0)
        for j in range(KV_HEADS):
            kw = k2[j][r * WINDOW:(r + 2) * WINDOW]
            vw = v2[j][r * WINDOW:(r + 2) * WINDOW]
            t0 = qb[:, (2 * j) * LANES:(2 * j + 1) * LANES]
            t1 = qb[:, (2 * j + 1) * LANES:(2 * j + 2) * LANES]
            lhs = jnp.concatenate([jnp.where(lo, t0, 0.0), jnp.where(lo, 0.0, t0),
                                   jnp.where(lo, t1, 0.0), jnp.where(lo, 0.0, t1)], axis=0)
            s = _dot_nt(lhs, kw) * (HEAD_DIM ** -0.5)
            probs = []
            for g in range(GQA_GROUP):
                h = j * GQA_GROUP + g
                sg = s[g * WINDOW:(g + 1) * WINDOW] - _alibi_slope(h) * distf
                sg = jnp.where(valid, sg, -jnp.inf)
                sink = sink_ref[0, h]
                mx = jnp.maximum(jnp.max(sg, axis=-1, keepdims=True), sink)
                p = jnp.exp(sg - mx)
                den = jnp.sum(p, axis=-1, keepdims=True) + jnp.exp(sink - mx)
                probs.append(p / den)
            o = _dot(jnp.concatenate(probs, axis=0), vw)
            o01 = jnp.where(lo, o[0:WINDOW], o[WINDOW:2 * WINDOW])
            o23 = jnp.where(lo, o[2 * WINDOW:3 * WINDOW], o[3 * WINDOW:4 * WINDOW])
            rows = slice(r * WINDOW, (r + 1) * WINDOW)
            o_ref[rows, (2 * j) * LANES:(2 * j + 1) * LANES] = o01.astype(o_ref.dtype)
            o_ref[rows, (2 * j + 1) * LANES:(2 * j + 2) * LANES] = o23.astype(o_ref.dtype)


def _attn_prompt(pa, sinks):
    b, l, _ = pa.shape
    blocks_per_step = ATTN_TQ // WINDOW
    return pl.pallas_call(
        _attn_prompt_kernel,
        grid=(b, l // ATTN_TQ),
        in_specs=[pl.BlockSpec(memory_space=pltpu.SMEM),
                  pl.BlockSpec((None, ATTN_TQ, ATTN_COLS), lambda bb, i: (bb, i, 0)),
                  pl.BlockSpec((None, WINDOW, 2 * KV_WIDTH),
                               lambda bb, i: (bb, jnp.maximum(i * blocks_per_step - 1, 0),
                                              ATTN_WIDTH // (2 * KV_WIDTH)))],
        out_specs=pl.BlockSpec((None, ATTN_TQ, ATTN_WIDTH), lambda bb, i: (bb, i, 0)),
        out_shape=jax.ShapeDtypeStruct((b, l, ATTN_WIDTH), BF16),
        compiler_params=pltpu.CompilerParams(dimension_semantics=("arbitrary", "arbitrary"),
                                             vmem_limit_bytes=VMEM_LIMIT),
        name="attn_prompt",
    )(sinks, pa, pa)


GDN_T = 256


def _unit_lower_inverse_minus_eye(mats, row, col):
    n = mats[0].shape[0]
    first = ((row >> 1) == (col >> 1)) & ((row & 1) == 1) & ((col & 1) == 0)
    xo = [jnp.where(first, -a, 0.0) for a in mats]
    size = 2
    while size < n:
        shift = size.bit_length()
        m = ((row >> shift) == (col >> shift)) & ((row & size) != 0) & ((col & size) == 0)
        aoff = [jnp.where(m, a, 0.0) for a in mats]
        y = [ao + _dot(ao, x) for ao, x in zip(aoff, xo)]
        xo = [x - yy - _dot(x, yy) for x, yy in zip(xo, y)]
        size *= 2
    return xo


def _gdn_prompt_kernel(act_ref, gb_ref, o_ref, s_out_ref, s_scr):
    i = pl.program_id(1)

    @pl.when(i == 0)
    def _():
        s_scr[...] = jnp.zeros_like(s_scr)

    beta = gb_ref[...]
    g = beta
    rin = lax.broadcasted_iota(jnp.int32, (GDN_T, 1), 0) & (CHUNK - 1)
    step = 1
    while step < CHUNK:
        g = g + jnp.where(rin >= step, pltpu.roll(g, step, axis=0), 0.0)
        step *= 2
    gt = g.T

    row = lax.broadcasted_iota(jnp.int32, (CHUNK, CHUNK), 0)
    col = lax.broadcasted_iota(jnp.int32, (CHUNK, CHUNK), 1)
    lower_incl = row >= col
    lower_strict = row > col

    heads = range(GDN_HEADS)
    units = [(c, h) for c in range(GDN_T // CHUNK) for h in heads]
    pre = []
    for c, h in units:
        rows = slice(c * CHUNK, (c + 1) * CHUNK)
        qc = act_ref[rows, h * LANES:(h + 1) * LANES]
        kc = act_ref[rows, GDN_KEY_WIDTH + h * LANES:GDN_KEY_WIDTH + (h + 1) * LANES]
        vc = act_ref[rows, 2 * GDN_KEY_WIDTH + h * LANES:2 * GDN_KEY_WIDTH + (h + 1) * LANES]
        gc = g[rows, h:h + 1]
        bc = beta[rows, GDN_HEADS + h:GDN_HEADS + h + 1]
        gr = gt[h:h + 1, c * CHUNK:(c + 1) * CHUNK]
        decay = jnp.where(lower_incl, jnp.exp(jnp.where(lower_incl, gc - gr, 0.0)), 0.0)
        kq = _dot_nt(jnp.concatenate([kc, qc], axis=0), kc)
        a = jnp.where(lower_strict, bc * kq[0:CHUNK] * decay, 0.0)
        qk = jnp.where(lower_incl, kq[CHUNK:] * decay, 0.0)
        eg = jnp.exp(gc)
        glast = gc[CHUNK - 1:CHUNK]
        rhs = jnp.concatenate([vc * bc, kc * (bc * eg)], axis=1)
        k_dec_t = (kc * jnp.exp(glast - gc)).T
        pre.append(dict(a=a, rhs=rhs, lhs_u=jnp.concatenate([qk, k_dec_t], axis=0), q_dec=qc * eg,
                        ld=jnp.exp(glast)))
    xos = _unit_lower_inverse_minus_eye([p["a"] for p in pre], row, col)
    trs = [p["rhs"] + _dot(xo, p["rhs"]) for p, xo in zip(pre, xos)]

    state = [s_scr[h] for h in heads]
    for c in range(GDN_T // CHUNK):
        rows = slice(c * CHUNK, (c + 1) * CHUNK)
        idx = [c * GDN_HEADS + h for h in heads]
        ws = [_dot(jnp.concatenate([trs[u][:, GDN_DV:], pre[u]["q_dec"]], axis=0), state[h])
              for h, u in zip(heads, idx)]
        ou = [_dot(pre[u]["lhs_u"], trs[u][:, 0:GDN_DV] - ws[h][0:CHUNK]) for h, u in zip(heads, idx)]
        for h, u in zip(heads, idx):
            lanes = slice(h * LANES, (h + 1) * LANES)
            o_ref[rows, lanes] = ws[h][CHUNK:] + ou[h][0:CHUNK]
            state[h] = state[h] * pre[u]["ld"] + ou[h][CHUNK:]
    for h in heads:
        s_scr[h] = state[h]
        s_out_ref[h] = state[h]


def _gdn_prompt(act, gb):
    b, l, _ = act.shape
    tok = lambda bb, i: (bb, i, 0)
    return pl.pallas_call(
        _gdn_prompt_kernel,
        grid=(b, l // GDN_T),
        in_specs=[pl.BlockSpec((None, GDN_T, GDN_CONV_DIM), tok),
                  pl.BlockSpec((None, GDN_T, AB_COLS), tok)],
        out_specs=[pl.BlockSpec((None, GDN_T, GDN_VAL_WIDTH), tok),
                   pl.BlockSpec((None, GDN_HEADS, GDN_DK, GDN_DV), lambda bb, i: (bb, 0, 0, 0))],
        out_shape=[jax.ShapeDtypeStruct((b, l, GDN_VAL_WIDTH), F32),
                   jax.ShapeDtypeStruct((b, GDN_HEADS, GDN_DK, GDN_DV), F32)],
        scratch_shapes=[pltpu.VMEM((GDN_HEADS, GDN_DK, GDN_DV), F32)],
        compiler_params=pltpu.CompilerParams(dimension_semantics=("arbitrary", "arbitrary"),
                                             vmem_limit_bytes=VMEM_LIMIT),
        name="gdn_prompt",
    )(act, gb)


def _sample_prep_kernel(x_ref, nw_ref, w_ref, cs_ref, cw_ref, alog_ref, dtb_ref,
                        qexp_ref, knew_ref, vnew_ref, qt_ref, kt_ref, vg_ref, gate_ref, z_ref,
                        cs_out_ref):
    h = _rms(x_ref[...], nw_ref[...]).astype(BF16)
    pa = jnp.dot(h, w_ref[:, 0:OFF_GDN], preferred_element_type=F32)
    pg = jnp.dot(h, w_ref[:, OFF_GDN:OFF_Z], preferred_element_type=F32)
    z_ref[...] = jnp.dot(h, w_ref[:, OFF_Z:OFF_AB], preferred_element_type=F32)
    ab = jnp.dot(h, w_ref[:, OFF_AB:IN_COLS], preferred_element_type=F32)

    lo = lax.broadcasted_iota(jnp.int32, (1, LANES), 1) < HEAD_DIM
    for t in range(ATTN_WIDTH // LANES):
        tile = pa[:, t * LANES:(t + 1) * LANES]
        rolled = pltpu.roll(tile, HEAD_DIM, axis=1)
        if t < ATTN_WIDTH // LANES // KV_HEADS:
            even, odd = jnp.where(lo, tile, 0.0), jnp.where(lo, rolled, 0.0)
        else:
            even, odd = jnp.where(lo, 0.0, rolled), jnp.where(lo, 0.0, tile)
        qexp_ref[2 * t] = even
        qexp_ref[2 * t + 1] = odd
    knew_ref[...] = pa[:, ATTN_WIDTH:ATTN_WIDTH + KV_WIDTH]
    vnew_ref[...] = pa[:, ATTN_WIDTH + KV_WIDTH:ATTN_COLS]

    cw = cw_ref[...]
    y = None
    for r in range(CONV_W - 1):
        term = cs_ref[:, r * GDN_CONV_DIM:(r + 1) * GDN_CONV_DIM] * cw[r:r + 1]
        y = term if y is None else y + term
    y = y + pg * cw[CONV_W - 1:CONV_W]
    act = y * _sigmoid(y)
    cs_out_ref[:, 0:(CONV_W - 2) * GDN_CONV_DIM] = cs_ref[:, GDN_CONV_DIM:(CONV_W - 1) * GDN_CONV_DIM]
    cs_out_ref[:, (CONV_W - 2) * GDN_CONV_DIM:] = pg

    for hd in range(GDN_HEADS):
        qn = _l2n(act[:, hd * LANES:(hd + 1) * LANES]) * (GDN_DK ** -0.5)
        kn = _l2n(act[:, GDN_KEY_WIDTH + hd * LANES:GDN_KEY_WIDTH + (hd + 1) * LANES])
        qt_ref[hd] = qn.T
        kt_ref[hd] = kn.T
    vg_ref[...] = act[:, 2 * GDN_KEY_WIDTH:]
    gb = _gates(ab, alog_ref[...], dtb_ref[...])
    lane = lax.broadcasted_iota(jnp.int32, (1, LANES), 1)
    gate_ref[...] = jnp.where(lane < GDN_HEADS, jnp.exp(gb), gb)


def _sample_prep(x, nw, w, cs, cw, alog, dtb):
    n = x.shape[0]
    return pl.pallas_call(
        _sample_prep_kernel,
        out_shape=[jax.ShapeDtypeStruct((ATTN_HEADS, n, LANES), F32),
                   jax.ShapeDtypeStruct((n, KV_WIDTH), F32),
                   jax.ShapeDtypeStruct((n, KV_WIDTH), F32),
                   jax.ShapeDtypeStruct((GDN_HEADS, GDN_DK, n), F32),
                   jax.ShapeDtypeStruct((GDN_HEADS, GDN_DK, n), F32),
                   jax.ShapeDtypeStruct((n, GDN_VAL_WIDTH), F32),
                   jax.ShapeDtypeStruct((n, LANES), F32),
                   jax.ShapeDtypeStruct((n, GDN_VAL_WIDTH), F32),
                   jax.ShapeDtypeStruct((n, (CONV_W - 1) * GDN_CONV_DIM), F32)],
        compiler_params=pltpu.CompilerParams(vmem_limit_bytes=VMEM_LIMIT),
        name="sample_prep",
    )(x, nw, w, cs, cw, alog, dtb)


ATTN_BB = 16


def _attn_sample_kernel(sink_ref, qexp_ref, knew_ref, vnew_ref, kbuf_ref, vbuf_ref,
                        o_ref, kout_ref, vout_ref, q_scr, o_scr):
    for h in range(ATTN_HEADS):
        q_scr[pl.ds(h, ATTN_BB, stride=ATTN_HEADS), :] = qexp_ref[h]
    hrow = lax.broadcasted_iota(jnp.int32, (ATTN_HEADS, 1), 0)
    sink = jnp.zeros((ATTN_HEADS, 1), F32)
    slope = jnp.zeros((ATTN_HEADS, 1), F32)
    for h in range(ATTN_HEADS):
        sink = jnp.where(hrow == h, sink_ref[0, h], sink)
        slope = jnp.where(hrow == h, _alibi_slope(h), slope)
    pos = lax.broadcasted_iota(jnp.int32, (1, WINDOW), 1)
    bias = slope * (WINDOW - pos).astype(F32)
    rowid = lax.broadcasted_iota(jnp.int32, (WINDOW, 1), 0)
    last = rowid == WINDOW - 1
    for b in range(ATTN_BB):
        qe = q_scr[b * ATTN_HEADS:(b + 1) * ATTN_HEADS, :]
        kb, vb = kbuf_ref[b], vbuf_ref[b]
        kn, vn = knew_ref[b:b + 1, :], vnew_ref[b:b + 1, :]
        s_old = _dot_nt(qe, kb) * (HEAD_DIM ** -0.5) - bias
        s_new = jnp.sum(qe * kn, axis=-1, keepdims=True) * (HEAD_DIM ** -0.5)
        mx = jnp.maximum(jnp.maximum(jnp.max(s_old, axis=-1, keepdims=True), s_new), sink)
        p_old = jnp.exp(s_old - mx)
        p_new = jnp.exp(s_new - mx)
        den = jnp.sum(p_old, axis=-1, keepdims=True) + p_new + jnp.exp(sink - mx)
        o = _dot(p_old / den, vb) + (p_new / den) * vn
        o_scr[b * ATTN_HEADS:(b + 1) * ATTN_HEADS, :] = o
        kout_ref[b] = jnp.where(last, kn, pltpu.roll(kb, WINDOW - 1, axis=0))
        vout_ref[b] = jnp.where(last, vn, pltpu.roll(vb, WINDOW - 1, axis=0))
    lo = lax.broadcasted_iota(jnp.int32, (1, LANES), 1) < HEAD_DIM
    for t in range(ATTN_WIDTH // LANES):
        even = o_scr[pl.ds(2 * t, ATTN_BB, stride=ATTN_HEADS), :]
        odd = o_scr[pl.ds(2 * t + 1, ATTN_BB, stride=ATTN_HEADS), :]
        if t < ATTN_WIDTH // LANES // KV_HEADS:
            tile = jnp.where(lo, even, pltpu.roll(odd, HEAD_DIM, axis=1))
        else:
            tile = jnp.where(lo, pltpu.roll(even, HEAD_DIM, axis=1), odd)
        o_ref[:, t * LANES:(t + 1) * LANES] = tile.astype(o_ref.dtype)


def _attn_sample(sinks, qexp, knew, vnew, kbuf, vbuf):
    n = knew.shape[0]
    tok2 = lambda i: (i, 0)
    tok3 = lambda i: (i, 0, 0)
    return pl.pallas_call(
        _attn_sample_kernel,
        grid=(n // ATTN_BB,),
        in_specs=[pl.BlockSpec(memory_space=pltpu.SMEM),
                  pl.BlockSpec((ATTN_HEADS, ATTN_BB, LANES), lambda i: (0, i, 0)),
                  pl.BlockSpec((ATTN_BB, KV_WIDTH), tok2),
                  pl.BlockSpec((ATTN_BB, KV_WIDTH), tok2),
                  pl.BlockSpec((ATTN_BB, WINDOW, KV_WIDTH), tok3),
                  pl.BlockSpec((ATTN_BB, WINDOW, KV_WIDTH), tok3)],
        out_specs=[pl.BlockSpec((ATTN_BB, ATTN_WIDTH), tok2),
                   pl.BlockSpec((ATTN_BB, WINDOW, KV_WIDTH), tok3),
                   pl.BlockSpec((ATTN_BB, WINDOW, KV_WIDTH), tok3)],
        out_shape=[jax.ShapeDtypeStruct((n, ATTN_WIDTH), BF16),
                   jax.ShapeDtypeStruct((n, WINDOW, KV_WIDTH), F32),
                   jax.ShapeDtypeStruct((n, WINDOW, KV_WIDTH), F32)],
        scratch_shapes=[pltpu.VMEM((ATTN_BB * ATTN_HEADS, LANES), F32),
                        pltpu.VMEM((ATTN_BB * ATTN_HEADS, LANES), F32)],
        compiler_params=pltpu.CompilerParams(dimension_semantics=("arbitrary",),
                                             vmem_limit_bytes=VMEM_LIMIT),
        name="attn_sample",
    )(sinks, qexp, knew, vnew, kbuf, vbuf)


GDN_BB = 8


def _gdn_sample_kernel(gate_ref, qt_ref, kt_ref, vg_ref, s_ref, o_ref, s_out_ref):
    i = pl.program_id(0)
    n = qt_ref.shape[-1]
    shift = (n - i * GDN_BB) % n
    for h in range(GDN_HEADS):
        qt = pltpu.roll(qt_ref[h], shift, axis=1)
        kt = pltpu.roll(kt_ref[h], shift, axis=1)
        lanes = slice(h * LANES, (h + 1) * LANES)
        for j in range(GDN_BB):
            tokn = i * GDN_BB + j
            decay = gate_ref[tokn * SUBLANES + h]
            beta = gate_ref[tokn * SUBLANES + GDN_HEADS + h]
            kcol = kt[:, j:j + 1]
            qcol = qt[:, j:j + 1]
            s = s_ref[j, h] * decay
            kv = jnp.sum(s * kcol, axis=0, keepdims=True)
            u = (vg_ref[j:j + 1, lanes] - kv) * beta
            s = s + kcol * u
            s_out_ref[j, h] = s
            o_ref[j:j + 1, lanes] = jnp.sum(s * qcol, axis=0, keepdims=True)


def _gdn_sample(gate, qt, kt, vg, s0):
    n = vg.shape[0]
    return pl.pallas_call(
        _gdn_sample_kernel,
        grid=(n // GDN_BB,),
        in_specs=[pl.BlockSpec(memory_space=pltpu.SMEM),
                  pl.BlockSpec((GDN_HEADS, GDN_DK, n), lambda i: (0, 0, 0)),
                  pl.BlockSpec((GDN_HEADS, GDN_DK, n), lambda i: (0, 0, 0)),
                  pl.BlockSpec((GDN_BB, GDN_VAL_WIDTH), lambda i: (i, 0)),
                  pl.BlockSpec((GDN_BB, GDN_HEADS, GDN_DK, GDN_DV), lambda i: (i, 0, 0, 0))],
        out_specs=[pl.BlockSpec((GDN_BB, GDN_VAL_WIDTH), lambda i: (i, 0)),
                   pl.BlockSpec((GDN_BB, GDN_HEADS, GDN_DK, GDN_DV), lambda i: (i, 0, 0, 0))],
        out_shape=[jax.ShapeDtypeStruct((n, GDN_VAL_WIDTH), F32),
                   jax.ShapeDtypeStruct((n, GDN_HEADS, GDN_DK, GDN_DV), F32)],
        compiler_params=pltpu.CompilerParams(dimension_semantics=("arbitrary",),
                                             vmem_limit_bytes=VMEM_LIMIT),
        name="gdn_sample",
    )(gate, qt, kt, vg, s0)


def _pad_lanes(v):
    return jnp.pad(v.astype(F32), (0, LANES - v.shape[0])).reshape(1, LANES)


def kernel(x_prompt, x_sample, state_conv, cache_win_k, cache_win_v, state_gdn, norm_mix_pre, w_in,
           attn_sinks, conv_w, gdn_a_log, gdn_dt_bias, gdn_norm, w_out, norm_mix_post, norm_ffn_pre,
           w_up, w_down, norm_ffn_post):
    assert w_in.shape[0] == 1, "single-layer trunk"
    bsz, seq, _ = x_prompt.shape
    nsamp = x_sample.shape[0]

    w_in_p = jnp.pad(w_in[0], ((0, 0), (0, IN_COLS - w_in.shape[2]))).astype(BF16)
    w_out_b = w_out[0].astype(BF16)
    w_up_b = w_up[0].astype(BF16)
    w_down_b = w_down[0].astype(BF16)
    n_pre = norm_mix_pre[0].reshape(1, D_MODEL)
    n_post = norm_mix_post[0].reshape(1, D_MODEL)
    n_fpre = norm_ffn_pre[0].reshape(1, D_MODEL)
    n_fpost = norm_ffn_post[0].reshape(1, D_MODEL)
    sinks = attn_sinks[0].reshape(1, ATTN_HEADS)
    cw = conv_w[0]
    alog = _pad_lanes(gdn_a_log[0])
    dtb = _pad_lanes(gdn_dt_bias[0])
    gnw = gdn_norm[0].reshape(1, GDN_DV)

    xp = x_prompt.reshape(bsz * seq, D_MODEL)
    pa, act, pz, gb, p_conv = _inproj(xp, n_pre, w_in_p, cw, alog, dtb, 512, seq)
    pa = pa.reshape(bsz, seq, ATTN_COLS)
    ao = _attn_prompt(pa, sinks)
    go, p_s = _gdn_prompt(act.reshape(bsz, seq, GDN_CONV_DIM), gb.reshape(bsz, seq, AB_COLS))
    yp = _out_ffn(xp, ao.reshape(bsz * seq, ATTN_WIDTH), go.reshape(bsz * seq, GDN_VAL_WIDTH), pz, gnw,
                  w_out_b, n_post, n_fpre, w_up_b, w_down_b, n_fpost, 512)

    xs = x_sample.reshape(nsamp, D_MODEL)
    cs = state_conv[0].reshape(nsamp, (CONV_W - 1) * GDN_CONV_DIM)
    qexp, knew, vnew, qt, kt, vg, gate, zs, cs_new = _sample_prep(xs, n_pre, w_in_p, cs, cw, alog, dtb)
    kbuf = cache_win_k[0].reshape(nsamp, WINDOW, KV_WIDTH)
    vbuf = cache_win_v[0].reshape(nsamp, WINDOW, KV_WIDTH)
    ao_s, k_new_cache, v_new_cache = _attn_sample(sinks, qexp, knew, vnew, kbuf, vbuf)
    go_s, s_new = _gdn_sample(gate[:, :SUBLANES].reshape(nsamp * SUBLANES), qt, kt, vg, state_gdn[0])
    ys = _out_ffn(xs, ao_s, go_s, zs, gnw, w_out_b, n_post, n_fpre, w_up_b, w_down_b, n_fpost, nsamp)

    kv_shape = (1, -1, WINDOW, KV_HEADS, HEAD_DIM)
    return (yp.reshape(bsz, seq, D_MODEL),
            ys.reshape(nsamp, 1, D_MODEL),
            p_conv[:, SUBLANES - (CONV_W - 1):, :][None],
            pa[:, seq - WINDOW:, ATTN_WIDTH:ATTN_WIDTH + KV_WIDTH].reshape(kv_shape),
            pa[:, seq - WINDOW:, ATTN_WIDTH + KV_WIDTH:].reshape(kv_shape),
            p_s[None],
            cs_new.reshape(1, nsamp, CONV_W - 1, GDN_CONV_DIM),
            k_new_cache.reshape(kv_shape),
            v_new_cache.reshape(kv_shape),
            s_new[None])
```

```python
import functools

import jax
import jax.numpy as jnp
from jax import lax
from jax.experimental import pallas as pl
from jax.experimental.pallas import tpu as pltpu

F32 = jnp.float32
BF16 = jnp.bfloat16

D_MODEL = 1024
ATTN_HEADS = 8
KV_HEADS = 2
HEAD_DIM = 64
GQA_GROUP = ATTN_HEADS // KV_HEADS
ATTN_WIDTH = ATTN_HEADS * HEAD_DIM
KV_WIDTH = KV_HEADS * HEAD_DIM
WINDOW = 128
GDN_HEADS = 4
GDN_DK = 128
GDN_DV = 128
GDN_KEY_WIDTH = GDN_HEADS * GDN_DK
GDN_VAL_WIDTH = GDN_HEADS * GDN_DV
GDN_CONV_DIM = 2 * GDN_KEY_WIDTH + GDN_VAL_WIDTH
CONV_W = 4
D_FF = 4 * D_MODEL
EPS = 1e-6

LANES = 128
SUBLANES = 8
ATTN_COLS = ATTN_WIDTH + 2 * KV_WIDTH
AB_COLS = LANES
IN_COLS = ATTN_COLS + GDN_CONV_DIM + GDN_VAL_WIDTH + AB_COLS
OFF_GDN = ATTN_COLS
OFF_Z = OFF_GDN + GDN_CONV_DIM
OFF_AB = OFF_Z + GDN_VAL_WIDTH
CHUNK = 128
VMEM_LIMIT = 56 * 1024 * 1024
VMEM_LIMIT_MIX = 60 * 1024 * 1024


def _rms(x, w):
    return x * lax.rsqrt(jnp.mean(x * x, axis=-1, keepdims=True) + EPS) * w


def _sigmoid(x):
    return 1.0 / (1.0 + jnp.exp(-x))


def _softplus(x):
    return jnp.maximum(x, 0.0) + jnp.log(1.0 + jnp.exp(-jnp.abs(x)))


def _dot(a, b):
    return jnp.dot(a.astype(BF16), b.astype(BF16), preferred_element_type=F32)


def _dot_nt(a, b):
    return lax.dot_general(a.astype(BF16), b.astype(BF16), (((1,), (1,)), ((), ())),
                           preferred_element_type=F32)


def _alibi_slope(h):
    return 2.0 ** (-8.0 * (h + 1) / ATTN_HEADS)


def _conv_silu(x, carry, w):
    t, c = x.shape
    x3 = x.reshape(t // SUBLANES, SUBLANES, c)
    row8 = lax.broadcasted_iota(jnp.int32, (1, SUBLANES, 1), 1)
    y = None
    for shift in range(CONV_W - 1, -1, -1):
        if shift == 0:
            xs = x3
        else:
            xr = pltpu.roll(x3, shift, axis=1)
            cr = pltpu.roll(carry, shift, axis=0)
            prev = jnp.concatenate([cr[None], xr[:-1]], axis=0)
            xs = jnp.where(row8 < shift, prev, xr)
        term = xs * w[CONV_W - 1 - shift:CONV_W - shift][None]
        y = term if y is None else y + term
    return (y * _sigmoid(y)).reshape(t, c)


def _l2n(x):
    return x * lax.rsqrt(jnp.sum(x * x, axis=-1, keepdims=True) + EPS)


def _gates(ab, alog, dtb):
    g = -jnp.exp(alog) * _softplus(ab + dtb)
    lane = lax.broadcasted_iota(jnp.int32, (1, LANES), 1)
    return jnp.where(lane < GDN_HEADS, g, _sigmoid(ab))


INPROJ_SUB = 128


def _inproj_kernel(x_ref, nw_ref, w_ref, cw_ref, alog_ref, dtb_ref,
                   oa_ref, og_ref, oz_ref, ogb_ref, conv_out_ref, carry_scr, *, tiles_per_seq):
    tm = x_ref.shape[0]

    @pl.when(lax.rem(pl.program_id(0), tiles_per_seq) == 0)
    def _():
        carry_scr[...] = jnp.zeros_like(carry_scr)

    carry = carry_scr[...]
    for r0 in range(0, tm, INPROJ_SUB):
        rows = slice(r0, r0 + INPROJ_SUB)
        h = _rms(x_ref[rows, :], nw_ref[...]).astype(BF16)
        pg = jnp.dot(h, w_ref[:, OFF_GDN:OFF_Z], preferred_element_type=F32)
        oa_ref[rows, :] = jnp.dot(h, w_ref[:, 0:OFF_GDN], preferred_element_type=F32)
        oz_ref[rows, :] = jnp.dot(h, w_ref[:, OFF_Z:OFF_AB], preferred_element_type=F32)
        ab = jnp.dot(h, w_ref[:, OFF_AB:IN_COLS], preferred_element_type=F32)
        ogb_ref[rows, :] = _gates(ab, alog_ref[...], dtb_ref[...])
        act = _conv_silu(pg, carry, cw_ref[...])
        carry = pg[INPROJ_SUB - SUBLANES:INPROJ_SUB]
        for hd in range(GDN_HEADS):
            ql = slice(hd * LANES, (hd + 1) * LANES)
            kl = slice(GDN_KEY_WIDTH + hd * LANES, GDN_KEY_WIDTH + (hd + 1) * LANES)
            og_ref[rows, ql] = _l2n(act[:, ql]) * (GDN_DK ** -0.5)
            og_ref[rows, kl] = _l2n(act[:, kl])
        og_ref[rows, 2 * GDN_KEY_WIDTH:] = act[:, 2 * GDN_KEY_WIDTH:]
    carry_scr[...] = carry
    conv_out_ref[...] = carry


def _inproj(x, nw, w, cw, alog, dtb, tm, seq):
    m = x.shape[0]
    tiles_per_seq = seq // tm
    row = lambda i: (i, 0)
    fixed = lambda i: (0, 0)
    return pl.pallas_call(
        functools.partial(_inproj_kernel, tiles_per_seq=tiles_per_seq),
        grid=(m // tm,),
        in_specs=[pl.BlockSpec((tm, D_MODEL), row),
                  pl.BlockSpec((1, D_MODEL), fixed),
                  pl.BlockSpec((D_MODEL, IN_COLS), fixed),
                  pl.BlockSpec((CONV_W, GDN_CONV_DIM), fixed),
                  pl.BlockSpec((1, LANES), fixed),
                  pl.BlockSpec((1, LANES), fixed)],
        out_specs=[pl.BlockSpec((tm, ATTN_COLS), row),
                   pl.BlockSpec((tm, GDN_CONV_DIM), row),
                   pl.BlockSpec((tm, GDN_VAL_WIDTH), row),
                   pl.BlockSpec((tm, AB_COLS), row),
                   pl.BlockSpec((None, SUBLANES, GDN_CONV_DIM), lambda i: (i // tiles_per_seq, 0, 0))],
        out_shape=[jax.ShapeDtypeStruct((m, ATTN_COLS), F32),
                   jax.ShapeDtypeStruct((m, GDN_CONV_DIM), F32),
                   jax.ShapeDtypeStruct((m, GDN_VAL_WIDTH), F32),
                   jax.ShapeDtypeStruct((m, AB_COLS), F32),
                   jax.ShapeDtypeStruct((m // seq, SUBLANES, GDN_CONV_DIM), F32)],
        scratch_shapes=[pltpu.VMEM((SUBLANES, GDN_CONV_DIM), F32)],
        compiler_params=pltpu.CompilerParams(dimension_semantics=("arbitrary",),
                                             vmem_limit_bytes=VMEM_LIMIT),
        name="inproj",
    )(x, nw, w, cw, alog, dtb)


FF_CHUNK = 512


def _out_ffn_stages(x_ref, ao_ref, go_ref, z_ref, gnw_ref, wout_ref, npost_ref, npre_ref, wup_ref,
                    wdown_ref, nffn_ref, y_ref):
    gnw = gnw_ref[...]
    gy = []
    for hd in range(GDN_HEADS):
        lanes = slice(hd * LANES, (hd + 1) * LANES)
        z = z_ref[:, lanes]
        gy.append((_rms(go_ref[:, lanes], gnw) * (z * _sigmoid(z))).astype(BF16))
    m = (jnp.dot(ao_ref[...], wout_ref[0:ATTN_WIDTH, :], preferred_element_type=F32)
         + jnp.dot(jnp.concatenate(gy, axis=1), wout_ref[ATTN_WIDTH:, :], preferred_element_type=F32))
    x1 = x_ref[...] + _rms(m, npost_ref[...])
    h = _rms(x1, npre_ref[...]).astype(BF16)
    yield
    acc = None
    for j in range(0, D_FF, FF_CHUNK):
        u = jnp.maximum(jnp.dot(h, wup_ref[:, j:j + FF_CHUNK], preferred_element_type=F32), 0.0)
        d = jnp.dot((u * u).astype(BF16), wdown_ref[j:j + FF_CHUNK, :], preferred_element_type=F32)
        acc = d if acc is None else acc + d
        yield
    y_ref[...] = x1 + _rms(acc, nffn_ref[...])


def _dup_half(x, lo):
    xr = pltpu.roll(x, HEAD_DIM, axis=1)
    return jnp.where(lo, x, xr), jnp.where(lo, xr, x)


def _attn_prompt_stages(sink_ref, cur_ref, prev_ref, seq_row0, o_ref):
    tq = cur_ref.shape[0]
    kall = jnp.concatenate([prev_ref[:, 0:KV_WIDTH], cur_ref[:, ATTN_WIDTH:ATTN_WIDTH + KV_WIDTH]], axis=0)
    vall = jnp.concatenate([prev_ref[:, KV_WIDTH:], cur_ref[:, ATTN_WIDTH + KV_WIDTH:]], axis=0)
    lo = lax.broadcasted_iota(jnp.int32, (1, LANES), 1) < HEAD_DIM
    k2 = _dup_half(kall, lo)
    v2 = _dup_half(vall, lo)
    qi = lax.broadcasted_iota(jnp.int32, (WINDOW, 2 * WINDOW), 0)
    si = lax.broadcasted_iota(jnp.int32, (WINDOW, 2 * WINDOW), 1)
    dist = WINDOW + qi - si
    band = (dist >= 0) & (dist <= WINDOW)
    distf = dist.astype(F32)
    for r in range(tq // WINDOW):
        rows = slice(r * WINDOW, (r + 1) * WINDOW)
        valid = band
        if r == 0:
            valid = band & ((seq_row0 - WINDOW + si) >= 0)
        for j in range(KV_HEADS):
            kw = k2[j][r * WINDOW:(r + 2) * WINDOW]
            vw = v2[j][r * WINDOW:(r + 2) * WINDOW]
            t0 = cur_ref[rows, (2 * j) * LANES:(2 * j + 1) * LANES]
            t1 = cur_ref[rows, (2 * j + 1) * LANES:(2 * j + 2) * LANES]
            lhs = jnp.concatenate([jnp.where(lo, t0, 0.0), jnp.where(lo, 0.0, t0),
                                   jnp.where(lo, t1, 0.0), jnp.where(lo, 0.0, t1)], axis=0)
            s = _dot_nt(lhs, kw) * (HEAD_DIM ** -0.5)
            probs = []
            for g in range(GQA_GROUP):
                h = j * GQA_GROUP + g
                sg = s[g * WINDOW:(g + 1) * WINDOW] - _alibi_slope(h) * distf
                sg = jnp.where(valid, sg, -jnp.inf)
                sink = sink_ref[0, h]
                mx = jnp.maximum(jnp.max(sg, axis=-1, keepdims=True), sink)
                p = jnp.exp(sg - mx)
                den = jnp.sum(p, axis=-1, keepdims=True) + jnp.exp(sink - mx)
                probs.append(p / den)
            o = _dot(jnp.concatenate(probs, axis=0), vw)
            o01 = jnp.where(lo, o[0:WINDOW], o[WINDOW:2 * WINDOW])
            o23 = jnp.where(lo, o[2 * WINDOW:3 * WINDOW], o[3 * WINDOW:4 * WINDOW])
            o_ref[rows, (2 * j) * LANES:(2 * j + 1) * LANES] = o01.astype(o_ref.dtype)
            o_ref[rows, (2 * j + 1) * LANES:(2 * j + 2) * LANES] = o23.astype(o_ref.dtype)
            yield


def _gdn_prompt_stages(act_ref, gb_ref, state, o_ref):
    t = act_ref.shape[0]
    beta = gb_ref[...]
    g = beta
    rin = lax.broadcasted_iota(jnp.int32, (t, 1), 0) & (CHUNK - 1)
    step = 1
    while step < CHUNK:
        g = g + jnp.where(rin >= step, pltpu.roll(g, step, axis=0), 0.0)
        step *= 2
    gt = g.T

    row = lax.broadcasted_iota(jnp.int32, (CHUNK, CHUNK), 0)
    col = lax.broadcasted_iota(jnp.int32, (CHUNK, CHUNK), 1)
    lower_incl = row >= col
    lower_strict = row > col

    heads = range(GDN_HEADS)
    units = [(c, h) for c in range(t // CHUNK) for h in heads]
    pre = []
    for c, h in units:
        rows = slice(c * CHUNK, (c + 1) * CHUNK)
        qc = act_ref[rows, h * LANES:(h + 1) * LANES]
        kc = act_ref[rows, GDN_KEY_WIDTH + h * LANES:GDN_KEY_WIDTH + (h + 1) * LANES]
        vc = act_ref[rows, 2 * GDN_KEY_WIDTH + h * LANES:2 * GDN_KEY_WIDTH + (h + 1) * LANES]
        gc = g[rows, h:h + 1]
        bc = beta[rows, GDN_HEADS + h:GDN_HEADS + h + 1]
        gr = gt[h:h + 1, c * CHUNK:(c + 1) * CHUNK]
        decay = jnp.where(lower_incl, jnp.exp(jnp.where(lower_incl, gc - gr, 0.0)), 0.0)
        kq = _dot_nt(jnp.concatenate([kc, qc], axis=0), kc)
        a = jnp.where(lower_strict, bc * kq[0:CHUNK] * decay, 0.0)
        qk = jnp.where(lower_incl, kq[CHUNK:] * decay, 0.0)
        eg = jnp.exp(gc)
        glast = gc[CHUNK - 1:CHUNK]
        rhs = jnp.concatenate([vc * bc, kc * (bc * eg)], axis=1)
        k_dec_t = (kc * jnp.exp(glast - gc)).T
        pre.append(dict(a=a, rhs=rhs, lhs_u=jnp.concatenate([qk, k_dec_t], axis=0), q_dec=qc * eg,
                        ld=jnp.exp(glast)))
    yield

    mats = [p["a"] for p in pre]
    first = ((row >> 1) == (col >> 1)) & ((row & 1) == 1) & ((col & 1) == 0)
    xo = [jnp.where(first, -a, 0.0) for a in mats]
    size = 2
    while size < CHUNK:
        shift = size.bit_length()
        m = ((row >> shift) == (col >> shift)) & ((row & size) != 0) & ((col & size) == 0)
        aoff = [jnp.where(m, a, 0.0) for a in mats]
        y = [ao + _dot(ao, x) for ao, x in zip(aoff, xo)]
        xo = [x - yy - _dot(x, yy) for x, yy in zip(xo, y)]
        size *= 2
        yield
    trs = [p["rhs"] + _dot(x, p["rhs"]) for p, x in zip(pre, xo)]
    yield

    for c in range(t // CHUNK):
        rows = slice(c * CHUNK, (c + 1) * CHUNK)
        idx = [c * GDN_HEADS + h for h in heads]
        ws = [_dot(jnp.concatenate([trs[u][:, GDN_DV:], pre[u]["q_dec"]], axis=0), state[h])
              for h, u in zip(heads, idx)]
        ou = [_dot(pre[u]["lhs_u"], trs[u][:, 0:GDN_DV] - ws[h][0:CHUNK]) for h, u in zip(heads, idx)]
        for h, u in zip(heads, idx):
            o_ref[rows, h * LANES:(h + 1) * LANES] = ws[h][CHUNK:] + ou[h][0:CHUNK]
            state[h] = state[h] * pre[u]["ld"] + ou[h][CHUNK:]
        yield


def _run_interleaved(primary, secondary, per_primary):
    done = object()
    while True:
        p = next(primary, done)
        s = [next(secondary, done) for _ in range(per_primary)]
        if p is done and all(v is done for v in s):
            return


def _roundrobin(*gens):
    done = object()
    live = list(gens)
    while live:
        for gen in list(live):
            if next(gen, done) is done:
                live.remove(gen)
            else:
                yield


MIX_T = 512


def _mix_ffn_kernel(sink_ref, pa_ref, prev_ref, act_ref, gb_ref, x_ref, z_ref, gnw_ref, wout_ref,
                    npost_ref, npre_ref, wup_ref, wdown_ref, nffn_ref,
                    y_ref, s_out_ref, s_scr, ao_scr, go_scr, *, n_tiles, tiles_per_seq):
    i = pl.program_id(0)

    @pl.when(i == 0)
    def _():
        s_scr[...] = jnp.zeros_like(s_scr)
        ao_scr[...] = jnp.zeros_like(ao_scr)
        go_scr[...] = jnp.zeros_like(go_scr)

    front_tile = jnp.minimum(i, n_tiles - 1)
    tile_in_seq = lax.rem(front_tile, tiles_per_seq)
    seq_start = tile_in_seq == 0
    front_valid = i < n_tiles

    ffn = _out_ffn_stages(x_ref, ao_scr, go_scr, z_ref, gnw_ref, wout_ref, npost_ref, npre_ref,
                          wup_ref, wdown_ref, nffn_ref, y_ref)
    next(ffn)

    old = [s_scr[h] for h in range(GDN_HEADS)]
    state = [jnp.where(seq_start, 0.0, s) for s in old]
    front = _roundrobin(_gdn_prompt_stages(act_ref, gb_ref, state, go_scr),
                        _attn_prompt_stages(sink_ref, pa_ref, prev_ref, tile_in_seq * MIX_T, ao_scr))
    _run_interleaved(ffn, front, 2)

    for h in range(GDN_HEADS):
        s_new = jnp.where(front_valid, state[h], old[h])
        s_scr[h] = s_new
        s_out_ref[h] = s_new


def _mix_ffn(sinks, pa, act, gb, x, z, gnw, wout, npost, npre, wup, wdown, nffn, seq):
    m = x.shape[0]
    n_tiles = m // MIX_T
    tiles_per_seq = seq // MIX_T
    blocks_per_tile = MIX_T // WINDOW
    front = lambda i: (jnp.minimum(i, n_tiles - 1), 0)
    back = lambda i: (jnp.maximum(i - 1, 0), 0)
    fixed = lambda i: (0, 0)
    resident = dict(pipeline_mode=pl.Buffered(1))
    kern = functools.partial(_mix_ffn_kernel, n_tiles=n_tiles, tiles_per_seq=tiles_per_seq)
    return pl.pallas_call(
        kern,
        grid=(n_tiles + 1,),
        in_specs=[pl.BlockSpec(memory_space=pltpu.SMEM),
                  pl.BlockSpec((MIX_T, ATTN_COLS), front),
                  pl.BlockSpec((WINDOW, 2 * KV_WIDTH),
                               lambda i: (jnp.maximum(jnp.minimum(i, n_tiles - 1) * blocks_per_tile - 1, 0),
                                          ATTN_WIDTH // (2 * KV_WIDTH))),
                  pl.BlockSpec((MIX_T, GDN_CONV_DIM), front),
                  pl.BlockSpec((MIX_T, AB_COLS), front),
                  pl.BlockSpec((MIX_T, D_MODEL), back),
                  pl.BlockSpec((MIX_T, GDN_VAL_WIDTH), back),
                  pl.BlockSpec((1, GDN_DV), fixed),
                  pl.BlockSpec((D_MODEL, D_MODEL), fixed, **resident),
                  pl.BlockSpec((1, D_MODEL), fixed),
                  pl.BlockSpec((1, D_MODEL), fixed),
                  pl.BlockSpec((D_MODEL, D_FF), fixed, **resident),
                  pl.BlockSpec((D_FF, D_MODEL), fixed, **resident),
                  pl.BlockSpec((1, D_MODEL), fixed)],
        out_specs=[pl.BlockSpec((MIX_T, D_MODEL), back),
                   pl.BlockSpec((None, GDN_HEADS, GDN_DK, GDN_DV),
                                lambda i: (jnp.minimum(i, n_tiles - 1) // tiles_per_seq, 0, 0, 0))],
        out_shape=[jax.ShapeDtypeStruct((m, D_MODEL), F32),
                   jax.ShapeDtypeStruct((m // seq, GDN_HEADS, GDN_DK, GDN_DV), F32)],
        scratch_shapes=[pltpu.VMEM((GDN_HEADS, GDN_DK, GDN_DV), F32),
                        pltpu.VMEM((MIX_T, ATTN_WIDTH), BF16),
                        pltpu.VMEM((MIX_T, GDN_VAL_WIDTH), F32)],
        compiler_params=pltpu.CompilerParams(dimension_semantics=("arbitrary",),
                                             vmem_limit_bytes=VMEM_LIMIT_MIX),
        name="mix_ffn",
    )(sinks, pa, pa, act, gb, x, z, gnw, wout, npost, npre, wup, wdown, nffn)


def _out_ffn_kernel(*refs):
    for _ in _out_ffn_stages(*refs):
        pass


def _out_ffn(x, ao, go, z, gnw, wout, npost, npre, wup, wdown, nffn, tm):
    m = x.shape[0]
    row = lambda i: (i, 0)
    fixed = lambda i: (0, 0)
    return pl.pallas_call(
        _out_ffn_kernel,
        grid=(m // tm,),
        in_specs=[pl.BlockSpec((tm, D_MODEL), row),
                  pl.BlockSpec((tm, ATTN_WIDTH), row),
                  pl.BlockSpec((tm, GDN_VAL_WIDTH), row),
                  pl.BlockSpec((tm, GDN_VAL_WIDTH), row),
                  pl.BlockSpec((1, GDN_DV), fixed),
                  pl.BlockSpec((D_MODEL, D_MODEL), fixed, pipeline_mode=pl.Buffered(1)),
                  pl.BlockSpec((1, D_MODEL), fixed),
                  pl.BlockSpec((1, D_MODEL), fixed),
                  pl.BlockSpec((D_MODEL, D_FF), fixed, pipeline_mode=pl.Buffered(1)),
                  pl.BlockSpec((D_FF, D_MODEL), fixed, pipeline_mode=pl.Buffered(1)),
                  pl.BlockSpec((1, D_MODEL), fixed)],
        out_specs=pl.BlockSpec((tm, D_MODEL), row),
        out_shape=jax.ShapeDtypeStruct((m, D_MODEL), F32),
        compiler_params=pltpu.CompilerParams(dimension_semantics=("arbitrary",),
                                             vmem_limit_bytes=VMEM_LIMIT),
        name="out_ffn",
    )(x, ao, go, z, gnw, wout, npost, npre, wup, wdown, nffn)


def _sample_prep_kernel(x_ref, nw_ref, w_ref, cs_ref, cw_ref, alog_ref, dtb_ref,
                        qexp_ref, knew_ref, vnew_ref, qt_ref, kt_ref, vg_ref, gate_ref, z_ref,
                        cs_out_ref):
    h = _rms(x_ref[...], nw_ref[...]).astype(BF16)
    pa = jnp.dot(h, w_ref[:, 0:OFF_GDN], preferred_element_type=F32)
    pg = jnp.dot(h, w_ref[:, OFF_GDN:OFF_Z], preferred_element_type=F32)
    z_ref[...] = jnp.dot(h, w_ref[:, OFF_Z:OFF_AB], preferred_element_type=F32)
    ab = jnp.dot(h, w_ref[:, OFF_AB:IN_COLS], preferred_element_type=F32)

    lo = lax.broadcasted_iota(jnp.int32, (1, LANES), 1) < HEAD_DIM
    for t in range(ATTN_WIDTH // LANES):
        tile = pa[:, t * LANES:(t + 1) * LANES]
        rolled = pltpu.roll(tile, HEAD_DIM, axis=1)
        if t < ATTN_WIDTH // LANES // KV_HEADS:
            even, odd = jnp.where(lo, tile, 0.0), jnp.where(lo, rolled, 0.0)
        else:
            even, odd = jnp.where(lo, 0.0, rolled), jnp.where(lo, 0.0, tile)
        qexp_ref[2 * t] = even
        qexp_ref[2 * t + 1] = odd
    knew_ref[...] = pa[:, ATTN_WIDTH:ATTN_WIDTH + KV_WIDTH]
    vnew_ref[...] = pa[:, ATTN_WIDTH + KV_WIDTH:ATTN_COLS]

    cw = cw_ref[...]
    y = None
    for r in range(CONV_W - 1):
        term = cs_ref[:, r * GDN_CONV_DIM:(r + 1) * GDN_CONV_DIM] * cw[r:r + 1]
        y = term if y is None else y + term
    y = y + pg * cw[CONV_W - 1:CONV_W]
    act = y * _sigmoid(y)
    cs_out_ref[:, 0:(CONV_W - 2) * GDN_CONV_DIM] = cs_ref[:, GDN_CONV_DIM:(CONV_W - 1) * GDN_CONV_DIM]
    cs_out_ref[:, (CONV_W - 2) * GDN_CONV_DIM:] = pg

    for hd in range(GDN_HEADS):
        qn = _l2n(act[:, hd * LANES:(hd + 1) * LANES]) * (GDN_DK ** -0.5)
        kn = _l2n(act[:, GDN_KEY_WIDTH + hd * LANES:GDN_KEY_WIDTH + (hd + 1) * LANES])
        qt_ref[hd] = qn.T
        kt_ref[hd] = kn.T
    vg_ref[...] = act[:, 2 * GDN_KEY_WIDTH:]
    gb = _gates(ab, alog_ref[...], dtb_ref[...])
    lane = lax.broadcasted_iota(jnp.int32, (1, LANES), 1)
    gate_ref[...] = jnp.where(lane < GDN_HEADS, jnp.exp(gb), gb)


def _sample_prep(x, nw, w, cs, cw, alog, dtb):
    n = x.shape[0]
    return pl.pallas_call(
        _sample_prep_kernel,
        out_shape=[jax.ShapeDtypeStruct((ATTN_HEADS, n, LANES), F32),
                   jax.ShapeDtypeStruct((n, KV_WIDTH), F32),
                   jax.ShapeDtypeStruct((n, KV_WIDTH), F32),
                   jax.ShapeDtypeStruct((GDN_HEADS, GDN_DK, n), F32),
                   jax.ShapeDtypeStruct((GDN_HEADS, GDN_DK, n), F32),
                   jax.ShapeDtypeStruct((n, GDN_VAL_WIDTH), F32),
                   jax.ShapeDtypeStruct((n, LANES), F32),
                   jax.ShapeDtypeStruct((n, GDN_VAL_WIDTH), F32),
                   jax.ShapeDtypeStruct((n, (CONV_W - 1) * GDN_CONV_DIM), F32)],
        compiler_params=pltpu.CompilerParams(vmem_limit_bytes=VMEM_LIMIT),
        name="sample_prep",
    )(x, nw, w, cs, cw, alog, dtb)


ATTN_BB = 16


def _attn_sample_kernel(sink_ref, qexp_ref, knew_ref, vnew_ref, kbuf_ref, vbuf_ref,
                        o_ref, kout_ref, vout_ref, q_scr, o_scr):
    for h in range(ATTN_HEADS):
        q_scr[pl.ds(h, ATTN_BB, stride=ATTN_HEADS), :] = qexp_ref[h]
    hrow = lax.broadcasted_iota(jnp.int32, (ATTN_HEADS, 1), 0)
    sink = jnp.zeros((ATTN_HEADS, 1), F32)
    slope = jnp.zeros((ATTN_HEADS, 1), F32)
    for h in range(ATTN_HEADS):
        sink = jnp.where(hrow == h, sink_ref[0, h], sink)
        slope = jnp.where(hrow == h, _alibi_slope(h), slope)
    pos = lax.broadcasted_iota(jnp.int32, (1, WINDOW), 1)
    bias = slope * (WINDOW - pos).astype(F32)
    rowid = lax.broadcasted_iota(jnp.int32, (WINDOW, 1), 0)
    last = rowid == WINDOW - 1
    toks = range(ATTN_BB)
    qe = [q_scr[b * ATTN_HEADS:(b + 1) * ATTN_HEADS, :] for b in toks]
    s_old = [_dot_nt(qe[b], kbuf_ref[b]) * (HEAD_DIM ** -0.5) - bias for b in toks]
    probs, p_new = [], []
    for b in toks:
        s_new = jnp.sum(qe[b] * knew_ref[b:b + 1, :], axis=-1, keepdims=True) * (HEAD_DIM ** -0.5)
        mx = jnp.maximum(jnp.maximum(jnp.max(s_old[b], axis=-1, keepdims=True), s_new), sink)
        p_old = jnp.exp(s_old[b] - mx)
        pn = jnp.exp(s_new - mx)
        den = jnp.sum(p_old, axis=-1, keepdims=True) + pn + jnp.exp(sink - mx)
        probs.append(p_old / den)
        p_new.append(pn / den)
    for b in toks:
        o_scr[b * ATTN_HEADS:(b + 1) * ATTN_HEADS, :] = (_dot(probs[b], vbuf_ref[b])
                                                         + p_new[b] * vnew_ref[b:b + 1, :])
    for b in toks:
        kout_ref[b] = jnp.where(last, knew_ref[b:b + 1, :], pltpu.roll(kbuf_ref[b], WINDOW - 1, axis=0))
        vout_ref[b] = jnp.where(last, vnew_ref[b:b + 1, :], pltpu.roll(vbuf_ref[b], WINDOW - 1, axis=0))
    lo = lax.broadcasted_iota(jnp.int32, (1, LANES), 1) < HEAD_DIM
    for t in range(ATTN_WIDTH // LANES):
        even = o_scr[pl.ds(2 * t, ATTN_BB, stride=ATTN_HEADS), :]
        odd = o_scr[pl.ds(2 * t + 1, ATTN_BB, stride=ATTN_HEADS), :]
        if t < ATTN_WIDTH // LANES // KV_HEADS:
            tile = jnp.where(lo, even, pltpu.roll(odd, HEAD_DIM, axis=1))
        else:
            tile = jnp.where(lo, pltpu.roll(even, HEAD_DIM, axis=1), odd)
        o_ref[:, t * LANES:(t + 1) * LANES] = tile.astype(o_ref.dtype)


def _attn_sample(sinks, qexp, knew, vnew, kbuf, vbuf):
    n = knew.shape[0]
    tok2 = lambda i: (i, 0)
    tok3 = lambda i: (i, 0, 0)
    return pl.pallas_call(
        _attn_sample_kernel,
        grid=(n // ATTN_BB,),
        in_specs=[pl.BlockSpec(memory_space=pltpu.SMEM),
                  pl.BlockSpec((ATTN_HEADS, ATTN_BB, LANES), lambda i: (0, i, 0)),
                  pl.BlockSpec((ATTN_BB, KV_WIDTH), tok2),
                  pl.BlockSpec((ATTN_BB, KV_WIDTH), tok2),
                  pl.BlockSpec((ATTN_BB, WINDOW, KV_WIDTH), tok3),
                  pl.BlockSpec((ATTN_BB, WINDOW, KV_WIDTH), tok3)],
        out_specs=[pl.BlockSpec((ATTN_BB, ATTN_WIDTH), tok2),
                   pl.BlockSpec((ATTN_BB, WINDOW, KV_WIDTH), tok3),
                   pl.BlockSpec((ATTN_BB, WINDOW, KV_WIDTH), tok3)],
        out_shape=[jax.ShapeDtypeStruct((n, ATTN_WIDTH), BF16),
                   jax.ShapeDtypeStruct((n, WINDOW, KV_WIDTH), F32),
                   jax.ShapeDtypeStruct((n, WINDOW, KV_WIDTH), F32)],
        scratch_shapes=[pltpu.VMEM((ATTN_BB * ATTN_HEADS, LANES), F32),
                        pltpu.VMEM((ATTN_BB * ATTN_HEADS, LANES), F32)],
        compiler_params=pltpu.CompilerParams(dimension_semantics=("arbitrary",),
                                             vmem_limit_bytes=VMEM_LIMIT),
        name="attn_sample",
    )(sinks, qexp, knew, vnew, kbuf, vbuf)


GDN_BB = 8


def _gdn_sample_kernel(gate_ref, qt_ref, kt_ref, vg_ref, s_ref, o_ref, s_out_ref):
    i = pl.program_id(0)
    n = qt_ref.shape[-1]
    shift = (n - i * GDN_BB) % n
    for h in range(GDN_HEADS):
        qt = pltpu.roll(qt_ref[h], shift, axis=1)
        kt = pltpu.roll(kt_ref[h], shift, axis=1)
        lanes = slice(h * LANES, (h + 1) * LANES)
        for j in range(GDN_BB):
            tokn = i * GDN_BB + j
            decay = gate_ref[tokn * SUBLANES + h]
            beta = gate_ref[tokn * SUBLANES + GDN_HEADS + h]
            kcol = kt[:, j:j + 1]
            qcol = qt[:, j:j + 1]
            s = s_ref[j, h] * decay
            kv = jnp.sum(s * kcol, axis=0, keepdims=True)
            u = (vg_ref[j:j + 1, lanes] - kv) * beta
            s = s + kcol * u
            s_out_ref[j, h] = s
            o_ref[j:j + 1, lanes] = jnp.sum(s * qcol, axis=0, keepdims=True)


def _gdn_sample(gate, qt, kt, vg, s0):
    n = vg.shape[0]
    return pl.pallas_call(
        _gdn_sample_kernel,
        grid=(n // GDN_BB,),
        in_specs=[pl.BlockSpec(memory_space=pltpu.SMEM),
                  pl.BlockSpec((GDN_HEADS, GDN_DK, n), lambda i: (0, 0, 0)),
                  pl.BlockSpec((GDN_HEADS, GDN_DK, n), lambda i: (0, 0, 0)),
                  pl.BlockSpec((GDN_BB, GDN_VAL_WIDTH), lambda i: (i, 0)),
                  pl.BlockSpec((GDN_BB, GDN_HEADS, GDN_DK, GDN_DV), lambda i: (i, 0, 0, 0))],
        out_specs=[pl.BlockSpec((GDN_BB, GDN_VAL_WIDTH), lambda i: (i, 0)),
                   pl.BlockSpec((GDN_BB, GDN_HEADS, GDN_DK, GDN_DV), lambda i: (i, 0, 0, 0))],
        out_shape=[jax.ShapeDtypeStruct((n, GDN_VAL_WIDTH), F32),
                   jax.ShapeDtypeStruct((n, GDN_HEADS, GDN_DK, GDN_DV), F32)],
        compiler_params=pltpu.CompilerParams(dimension_semantics=("arbitrary",),
                                             vmem_limit_bytes=VMEM_LIMIT),
        name="gdn_sample",
    )(gate, qt, kt, vg, s0)


def _pad_lanes(v):
    return jnp.pad(v.astype(F32), (0, LANES - v.shape[0])).reshape(1, LANES)


def kernel(x_prompt, x_sample, state_conv, cache_win_k, cache_win_v, state_gdn, norm_mix_pre, w_in,
           attn_sinks, conv_w, gdn_a_log, gdn_dt_bias, gdn_norm, w_out, norm_mix_post, norm_ffn_pre,
           w_up, w_down, norm_ffn_post):
    assert w_in.shape[0] == 1, "single-layer trunk"
    bsz, seq, _ = x_prompt.shape
    nsamp = x_sample.shape[0]

    w_in_p = jnp.pad(w_in[0], ((0, 0), (0, IN_COLS - w_in.shape[2]))).astype(BF16)
    w_out_b = w_out[0].astype(BF16)
    w_up_b = w_up[0].astype(BF16)
    w_down_b = w_down[0].astype(BF16)
    n_pre = norm_mix_pre[0].reshape(1, D_MODEL)
    n_post = norm_mix_post[0].reshape(1, D_MODEL)
    n_fpre = norm_ffn_pre[0].reshape(1, D_MODEL)
    n_fpost = norm_ffn_post[0].reshape(1, D_MODEL)
    sinks = attn_sinks[0].reshape(1, ATTN_HEADS)
    cw = conv_w[0]
    alog = _pad_lanes(gdn_a_log[0])
    dtb = _pad_lanes(gdn_dt_bias[0])
    gnw = gdn_norm[0].reshape(1, GDN_DV)

    xp = x_prompt.reshape(bsz * seq, D_MODEL)
    pa, act, pz, gb, p_conv = _inproj(xp, n_pre, w_in_p, cw, alog, dtb, MIX_T, seq)
    yp, p_s = _mix_ffn(sinks, pa, act, gb, xp, pz, gnw, w_out_b, n_post, n_fpre, w_up_b, w_down_b,
                       n_fpost, seq)
    pa = pa.reshape(bsz, seq, ATTN_COLS)

    xs = x_sample.reshape(nsamp, D_MODEL)
    cs = state_conv[0].reshape(nsamp, (CONV_W - 1) * GDN_CONV_DIM)
    qexp, knew, vnew, qt, kt, vg, gate, zs, cs_new = _sample_prep(xs, n_pre, w_in_p, cs, cw, alog, dtb)
    kbuf = cache_win_k[0].reshape(nsamp, WINDOW, KV_WIDTH)
    vbuf = cache_win_v[0].reshape(nsamp, WINDOW, KV_WIDTH)
    ao_s, k_new_cache, v_new_cache = _attn_sample(sinks, qexp, knew, vnew, kbuf, vbuf)
    go_s, s_new = _gdn_sample(gate[:, :SUBLANES].reshape(nsamp * SUBLANES), qt, kt, vg, state_gdn[0])
    ys = _out_ffn(xs, ao_s, go_s, zs, gnw, w_out_b, n_post, n_fpre, w_up_b, w_down_b, n_fpost, nsamp)

    kv_shape = (1, -1, WINDOW, KV_HEADS, HEAD_DIM)
    return (yp.reshape(bsz, seq, D_MODEL),
            ys.reshape(nsamp, 1, D_MODEL),
            p_conv[:, SUBLANES - (CONV_W - 1):, :][None],
            pa[:, seq - WINDOW:, ATTN_WIDTH:ATTN_WIDTH + KV_WIDTH].reshape(kv_shape),
            pa[:, seq - WINDOW:, ATTN_WIDTH + KV_WIDTH:].reshape(kv_shape),
            p_s[None],
            cs_new.reshape(1, nsamp, CONV_W - 1, GDN_CONV_DIM),
            k_new_cache.reshape(kv_shape),
            v_new_cache.reshape(kv_shape),
            s_new[None])
```

```python
import functools

import jax
import jax.numpy as jnp
from jax import lax
from jax.experimental import pallas as pl
from jax.experimental.pallas import tpu as pltpu

F32 = jnp.float32
BF16 = jnp.bfloat16

D_MODEL = 1024
ATTN_HEADS = 8
KV_HEADS = 2
HEAD_DIM = 64
GQA_GROUP = ATTN_HEADS // KV_HEADS
ATTN_WIDTH = ATTN_HEADS * HEAD_DIM
KV_WIDTH = KV_HEADS * HEAD_DIM
WINDOW = 128
GDN_HEADS = 4
GDN_DK = 128
GDN_DV = 128
GDN_KEY_WIDTH = GDN_HEADS * GDN_DK
GDN_VAL_WIDTH = GDN_HEADS * GDN_DV
GDN_CONV_DIM = 2 * GDN_KEY_WIDTH + GDN_VAL_WIDTH
CONV_W = 4
D_FF = 4 * D_MODEL
EPS = 1e-6

LANES = 128
SUBLANES = 8
ATTN_COLS = ATTN_WIDTH + 2 * KV_WIDTH
AB_COLS = LANES
IN_COLS = ATTN_COLS + GDN_CONV_DIM + GDN_VAL_WIDTH + AB_COLS
OFF_GDN = ATTN_COLS
OFF_Z = OFF_GDN + GDN_CONV_DIM
OFF_AB = OFF_Z + GDN_VAL_WIDTH
CHUNK = 128
VMEM_LIMIT = 56 * 1024 * 1024
VMEM_LIMIT_MIX = 60 * 1024 * 1024


def _rms(x, w):
    return x * lax.rsqrt(jnp.mean(x * x, axis=-1, keepdims=True) + EPS) * w


def _sigmoid(x):
    return 1.0 / (1.0 + jnp.exp(-x))


def _softplus(x):
    return jnp.maximum(x, 0.0) + jnp.log(1.0 + jnp.exp(-jnp.abs(x)))


def _dot(a, b):
    return jnp.dot(a.astype(BF16), b.astype(BF16), preferred_element_type=F32)


def _dot_nt(a, b):
    return lax.dot_general(a.astype(BF16), b.astype(BF16), (((1,), (1,)), ((), ())),
                           preferred_element_type=F32)


def _alibi_slope(h):
    return 2.0 ** (-8.0 * (h + 1) / ATTN_HEADS)


def _causal_conv(x, carry, w):
    t, c = x.shape
    x3 = x.reshape(t // SUBLANES, SUBLANES, c)
    row8 = lax.broadcasted_iota(jnp.int32, (1, SUBLANES, 1), 1)
    y = None
    for shift in range(CONV_W - 1, -1, -1):
        if shift == 0:
            xs = x3
        else:
            xr = pltpu.roll(x3, shift, axis=1)
            cr = pltpu.roll(carry, shift, axis=0)
            prev = jnp.concatenate([cr[None], xr[:-1]], axis=0)
            xs = jnp.where(row8 < shift, prev, xr)
        term = xs * w[CONV_W - 1 - shift:CONV_W - shift][None]
        y = term if y is None else y + term
    return y.reshape(t, c)


def _silu(y):
    return y * _sigmoid(y)


def _l2n(x):
    return x * lax.rsqrt(jnp.sum(x * x, axis=-1, keepdims=True) + EPS)


def _gates(ab, alog, dtb):
    g = -jnp.exp(alog) * _softplus(ab + dtb)
    lane = lax.broadcasted_iota(jnp.int32, (1, LANES), 1)
    return jnp.where(lane < GDN_HEADS, g, _sigmoid(ab))


INPROJ_SUB = 128


def _inproj_kernel(x_ref, nw_ref, w_ref, cw_ref, alog_ref, dtb_ref,
                   oa_ref, og_ref, oz_ref, ogb_ref, conv_out_ref, carry_scr, *, tiles_per_seq):
    tm = x_ref.shape[0]

    @pl.when(lax.rem(pl.program_id(0), tiles_per_seq) == 0)
    def _():
        carry_scr[...] = jnp.zeros_like(carry_scr)

    carry = carry_scr[...]
    for r0 in range(0, tm, INPROJ_SUB):
        rows = slice(r0, r0 + INPROJ_SUB)
        h = _rms(x_ref[rows, :], nw_ref[...]).astype(BF16)
        pg = jnp.dot(h, w_ref[:, OFF_GDN:OFF_Z], preferred_element_type=F32)
        oa_ref[rows, :] = jnp.dot(h, w_ref[:, 0:OFF_GDN], preferred_element_type=F32)
        oz_ref[rows, :] = jnp.dot(h, w_ref[:, OFF_Z:OFF_AB], preferred_element_type=F32)
        ab = jnp.dot(h, w_ref[:, OFF_AB:IN_COLS], preferred_element_type=F32)
        ogb_ref[rows, :] = _gates(ab, alog_ref[...], dtb_ref[...])
        og_ref[rows, :] = _causal_conv(pg, carry, cw_ref[...])
        carry = pg[INPROJ_SUB - SUBLANES:INPROJ_SUB]
    carry_scr[...] = carry
    conv_out_ref[...] = carry


def _inproj(x, nw, w, cw, alog, dtb, tm, seq):
    m = x.shape[0]
    tiles_per_seq = seq // tm
    row = lambda i: (i, 0)
    fixed = lambda i: (0, 0)
    return pl.pallas_call(
        functools.partial(_inproj_kernel, tiles_per_seq=tiles_per_seq),
        grid=(m // tm,),
        in_specs=[pl.BlockSpec((tm, D_MODEL), row),
                  pl.BlockSpec((1, D_MODEL), fixed),
                  pl.BlockSpec((D_MODEL, IN_COLS), fixed),
                  pl.BlockSpec((CONV_W, GDN_CONV_DIM), fixed),
                  pl.BlockSpec((1, LANES), fixed),
                  pl.BlockSpec((1, LANES), fixed)],
        out_specs=[pl.BlockSpec((tm, ATTN_COLS), row),
                   pl.BlockSpec((tm, GDN_CONV_DIM), row),
                   pl.BlockSpec((tm, GDN_VAL_WIDTH), row),
                   pl.BlockSpec((tm, AB_COLS), row),
                   pl.BlockSpec((None, SUBLANES, GDN_CONV_DIM), lambda i: (i // tiles_per_seq, 0, 0))],
        out_shape=[jax.ShapeDtypeStruct((m, ATTN_COLS), F32),
                   jax.ShapeDtypeStruct((m, GDN_CONV_DIM), F32),
                   jax.ShapeDtypeStruct((m, GDN_VAL_WIDTH), F32),
                   jax.ShapeDtypeStruct((m, AB_COLS), F32),
                   jax.ShapeDtypeStruct((m // seq, SUBLANES, GDN_CONV_DIM), F32)],
        scratch_shapes=[pltpu.VMEM((SUBLANES, GDN_CONV_DIM), F32)],
        compiler_params=pltpu.CompilerParams(dimension_semantics=("arbitrary",),
                                             vmem_limit_bytes=VMEM_LIMIT),
        name="inproj",
    )(x, nw, w, cw, alog, dtb)


FF_CHUNK = 512


def _out_ffn_stages(x_ref, ao_ref, go_ref, z_ref, gnw_ref, wout_ref, npost_ref, npre_ref, wup_ref,
                    wdown_ref, nffn_ref, y_ref):
    gnw = gnw_ref[...]
    gy = []
    for hd in range(GDN_HEADS):
        lanes = slice(hd * LANES, (hd + 1) * LANES)
        z = z_ref[:, lanes]
        gy.append((_rms(go_ref[:, lanes], gnw) * (z * _sigmoid(z))).astype(BF16))
    m = (jnp.dot(ao_ref[...], wout_ref[0:ATTN_WIDTH, :], preferred_element_type=F32)
         + jnp.dot(jnp.concatenate(gy, axis=1), wout_ref[ATTN_WIDTH:, :], preferred_element_type=F32))
    x1 = x_ref[...] + _rms(m, npost_ref[...])
    h = _rms(x1, npre_ref[...]).astype(BF16)
    yield
    acc = None
    for j in range(0, D_FF, FF_CHUNK):
        u = jnp.maximum(jnp.dot(h, wup_ref[:, j:j + FF_CHUNK], preferred_element_type=F32), 0.0)
        d = jnp.dot((u * u).astype(BF16), wdown_ref[j:j + FF_CHUNK, :], preferred_element_type=F32)
        acc = d if acc is None else acc + d
        yield
    y_ref[...] = x1 + _rms(acc, nffn_ref[...])


def _dup_half(x, lo):
    xr = pltpu.roll(x, HEAD_DIM, axis=1)
    return jnp.where(lo, x, xr), jnp.where(lo, xr, x)


def _attn_prompt_stages(sink_ref, cur_ref, prev_ref, seq_row0, o_ref):
    tq = cur_ref.shape[0]
    kall = jnp.concatenate([prev_ref[:, 0:KV_WIDTH], cur_ref[:, ATTN_WIDTH:ATTN_WIDTH + KV_WIDTH]], axis=0)
    vall = jnp.concatenate([prev_ref[:, KV_WIDTH:], cur_ref[:, ATTN_WIDTH + KV_WIDTH:]], axis=0)
    lo = lax.broadcasted_iota(jnp.int32, (1, LANES), 1) < HEAD_DIM
    k2 = _dup_half(kall, lo)
    v2 = _dup_half(vall, lo)
    qi = lax.broadcasted_iota(jnp.int32, (WINDOW, 2 * WINDOW), 0)
    si = lax.broadcasted_iota(jnp.int32, (WINDOW, 2 * WINDOW), 1)
    dist = WINDOW + qi - si
    band = (dist >= 0) & (dist <= WINDOW)
    distf = dist.astype(F32)
    for r in range(tq // WINDOW):
        rows = slice(r * WINDOW, (r + 1) * WINDOW)
        valid = band
        if r == 0:
            valid = band & ((seq_row0 - WINDOW + si) >= 0)
        for j in range(KV_HEADS):
            kw = k2[j][r * WINDOW:(r + 2) * WINDOW]
            vw = v2[j][r * WINDOW:(r + 2) * WINDOW]
            t0 = cur_ref[rows, (2 * j) * LANES:(2 * j + 1) * LANES]
            t1 = cur_ref[rows, (2 * j + 1) * LANES:(2 * j + 2) * LANES]
            lhs = jnp.concatenate([jnp.where(lo, t0, 0.0), jnp.where(lo, 0.0, t0),
                                   jnp.where(lo, t1, 0.0), jnp.where(lo, 0.0, t1)], axis=0)
            s = _dot_nt(lhs, kw) * (HEAD_DIM ** -0.5)
            probs = []
            for g in range(GQA_GROUP):
                h = j * GQA_GROUP + g
                sg = s[g * WINDOW:(g + 1) * WINDOW] - _alibi_slope(h) * distf
                sg = jnp.where(valid, sg, -jnp.inf)
                sink = sink_ref[0, h]
                mx = jnp.maximum(jnp.max(sg, axis=-1, keepdims=True), sink)
                p = jnp.exp(sg - mx)
                den = jnp.sum(p, axis=-1, keepdims=True) + jnp.exp(sink - mx)
                probs.append(p / den)
            o = _dot(jnp.concatenate(probs, axis=0), vw)
            o01 = jnp.where(lo, o[0:WINDOW], o[WINDOW:2 * WINDOW])
            o23 = jnp.where(lo, o[2 * WINDOW:3 * WINDOW], o[3 * WINDOW:4 * WINDOW])
            o_ref[rows, (2 * j) * LANES:(2 * j + 1) * LANES] = o01.astype(o_ref.dtype)
            o_ref[rows, (2 * j + 1) * LANES:(2 * j + 2) * LANES] = o23.astype(o_ref.dtype)
            yield


def _gdn_prompt_stages(conv_ref, gb_ref, state, o_ref):
    t = conv_ref.shape[0]
    beta = gb_ref[...]
    g = beta
    rin = lax.broadcasted_iota(jnp.int32, (t, 1), 0) & (CHUNK - 1)
    step = 1
    while step < CHUNK:
        g = g + jnp.where(rin >= step, pltpu.roll(g, step, axis=0), 0.0)
        step *= 2
    gt = g.T

    row = lax.broadcasted_iota(jnp.int32, (CHUNK, CHUNK), 0)
    col = lax.broadcasted_iota(jnp.int32, (CHUNK, CHUNK), 1)
    lower_incl = row >= col
    lower_strict = row > col

    heads = range(GDN_HEADS)
    units = [(c, h) for c in range(t // CHUNK) for h in heads]
    pre = []
    for c, h in units:
        rows = slice(c * CHUNK, (c + 1) * CHUNK)
        qc = _l2n(_silu(conv_ref[rows, h * LANES:(h + 1) * LANES])) * (GDN_DK ** -0.5)
        kc = _l2n(_silu(conv_ref[rows, GDN_KEY_WIDTH + h * LANES:GDN_KEY_WIDTH + (h + 1) * LANES]))
        vc = _silu(conv_ref[rows, 2 * GDN_KEY_WIDTH + h * LANES:2 * GDN_KEY_WIDTH + (h + 1) * LANES])
        gc = g[rows, h:h + 1]
        bc = beta[rows, GDN_HEADS + h:GDN_HEADS + h + 1]
        gr = gt[h:h + 1, c * CHUNK:(c + 1) * CHUNK]
        decay = jnp.where(lower_incl, jnp.exp(jnp.where(lower_incl, gc - gr, 0.0)), 0.0)
        kq = _dot_nt(jnp.concatenate([kc, qc], axis=0), kc)
        a = jnp.where(lower_strict, bc * kq[0:CHUNK] * decay, 0.0)
        qk = jnp.where(lower_incl, kq[CHUNK:] * decay, 0.0)
        eg = jnp.exp(gc)
        glast = gc[CHUNK - 1:CHUNK]
        rhs = jnp.concatenate([vc * bc, kc * (bc * eg)], axis=1)
        k_dec_t = (kc * jnp.exp(glast - gc)).T
        pre.append(dict(a=a, rhs=rhs, lhs_u=jnp.concatenate([qk, k_dec_t], axis=0), q_dec=qc * eg,
                        ld=jnp.exp(glast)))
    yield

    mats = [p["a"] for p in pre]
    first = ((row >> 1) == (col >> 1)) & ((row & 1) == 1) & ((col & 1) == 0)
    xo = [jnp.where(first, -a, 0.0) for a in mats]
    size = 2
    while size < CHUNK:
        shift = size.bit_length()
        m = ((row >> shift) == (col >> shift)) & ((row & size) != 0) & ((col & size) == 0)
        aoff = [jnp.where(m, a, 0.0) for a in mats]
        y = [ao + _dot(ao, x) for ao, x in zip(aoff, xo)]
        xo = [x - yy - _dot(x, yy) for x, yy in zip(xo, y)]
        size *= 2
        yield
    trs = [p["rhs"] + _dot(x, p["rhs"]) for p, x in zip(pre, xo)]
    yield

    for c in range(t // CHUNK):
        rows = slice(c * CHUNK, (c + 1) * CHUNK)
        idx = [c * GDN_HEADS + h for h in heads]
        ws = [_dot(jnp.concatenate([trs[u][:, GDN_DV:], pre[u]["q_dec"]], axis=0), state[h])
              for h, u in zip(heads, idx)]
        ou = [_dot(pre[u]["lhs_u"], trs[u][:, 0:GDN_DV] - ws[h][0:CHUNK]) for h, u in zip(heads, idx)]
        for h, u in zip(heads, idx):
            o_ref[rows, h * LANES:(h + 1) * LANES] = ws[h][CHUNK:] + ou[h][0:CHUNK]
            state[h] = state[h] * pre[u]["ld"] + ou[h][CHUNK:]
        yield


def _run_interleaved(primary, secondary, per_primary):
    done = object()
    while True:
        p = next(primary, done)
        s = [next(secondary, done) for _ in range(per_primary)]
        if p is done and all(v is done for v in s):
            return


def _roundrobin(*gens):
    done = object()
    live = list(gens)
    while live:
        for gen in list(live):
            if next(gen, done) is done:
                live.remove(gen)
            else:
                yield


MIX_T = 512


def _mix_ffn_kernel(sink_ref, pa_ref, prev_ref, conv_ref, gb_ref, x_ref, z_ref, gnw_ref, wout_ref,
                    npost_ref, npre_ref, wup_ref, wdown_ref, nffn_ref,
                    y_ref, s_out_ref, s_scr, ao_scr, go_scr, *, n_tiles, tiles_per_seq):
    i = pl.program_id(0)

    @pl.when(i == 0)
    def _():
        s_scr[...] = jnp.zeros_like(s_scr)
        ao_scr[...] = jnp.zeros_like(ao_scr)
        go_scr[...] = jnp.zeros_like(go_scr)

    front_tile = jnp.minimum(i, n_tiles - 1)
    tile_in_seq = lax.rem(front_tile, tiles_per_seq)
    seq_start = tile_in_seq == 0
    front_valid = i < n_tiles

    ffn = _out_ffn_stages(x_ref, ao_scr, go_scr, z_ref, gnw_ref, wout_ref, npost_ref, npre_ref,
                          wup_ref, wdown_ref, nffn_ref, y_ref)
    next(ffn)

    old = [s_scr[h] for h in range(GDN_HEADS)]
    state = [jnp.where(seq_start, 0.0, s) for s in old]
    front = _roundrobin(_gdn_prompt_stages(conv_ref, gb_ref, state, go_scr),
                        _attn_prompt_stages(sink_ref, pa_ref, prev_ref, tile_in_seq * MIX_T, ao_scr))
    _run_interleaved(ffn, front, 2)

    for h in range(GDN_HEADS):
        s_new = jnp.where(front_valid, state[h], old[h])
        s_scr[h] = s_new
        s_out_ref[h] = s_new


def _mix_ffn(sinks, pa, act, gb, x, z, gnw, wout, npost, npre, wup, wdown, nffn, seq):
    m = x.shape[0]
    n_tiles = m // MIX_T
    tiles_per_seq = seq // MIX_T
    blocks_per_tile = MIX_T // WINDOW
    front = lambda i: (jnp.minimum(i, n_tiles - 1), 0)
    back = lambda i: (jnp.maximum(i - 1, 0), 0)
    fixed = lambda i: (0, 0)
    resident = dict(pipeline_mode=pl.Buffered(1))
    kern = functools.partial(_mix_ffn_kernel, n_tiles=n_tiles, tiles_per_seq=tiles_per_seq)
    return pl.pallas_call(
        kern,
        grid=(n_tiles + 1,),
        in_specs=[pl.BlockSpec(memory_space=pltpu.SMEM),
                  pl.BlockSpec((MIX_T, ATTN_COLS), front),
                  pl.BlockSpec((WINDOW, 2 * KV_WIDTH),
                               lambda i: (jnp.maximum(jnp.minimum(i, n_tiles - 1) * blocks_per_tile - 1, 0),
                                          ATTN_WIDTH // (2 * KV_WIDTH))),
                  pl.BlockSpec((MIX_T, GDN_CONV_DIM), front),
                  pl.BlockSpec((MIX_T, AB_COLS), front),
                  pl.BlockSpec((MIX_T, D_MODEL), back),
                  pl.BlockSpec((MIX_T, GDN_VAL_WIDTH), back),
                  pl.BlockSpec((1, GDN_DV), fixed),
                  pl.BlockSpec((D_MODEL, D_MODEL), fixed, **resident),
                  pl.BlockSpec((1, D_MODEL), fixed),
                  pl.BlockSpec((1, D_MODEL), fixed),
                  pl.BlockSpec((D_MODEL, D_FF), fixed, **resident),
                  pl.BlockSpec((D_FF, D_MODEL), fixed, **resident),
                  pl.BlockSpec((1, D_MODEL), fixed)],
        out_specs=[pl.BlockSpec((MIX_T, D_MODEL), back),
                   pl.BlockSpec((None, GDN_HEADS, GDN_DK, GDN_DV),
                                lambda i: (jnp.minimum(i, n_tiles - 1) // tiles_per_seq, 0, 0, 0))],
        out_shape=[jax.ShapeDtypeStruct((m, D_MODEL), F32),
                   jax.ShapeDtypeStruct((m // seq, GDN_HEADS, GDN_DK, GDN_DV), F32)],
        scratch_shapes=[pltpu.VMEM((GDN_HEADS, GDN_DK, GDN_DV), F32),
                        pltpu.VMEM((MIX_T, ATTN_WIDTH), BF16),
                        pltpu.VMEM((MIX_T, GDN_VAL_WIDTH), F32)],
        compiler_params=pltpu.CompilerParams(dimension_semantics=("arbitrary",),
                                             vmem_limit_bytes=VMEM_LIMIT_MIX),
        name="mix_ffn",
    )(sinks, pa, pa, act, gb, x, z, gnw, wout, npost, npre, wup, wdown, nffn)


def _out_ffn_kernel(*refs):
    for _ in _out_ffn_stages(*refs):
        pass


def _out_ffn(x, ao, go, z, gnw, wout, npost, npre, wup, wdown, nffn, tm):
    m = x.shape[0]
    row = lambda i: (i, 0)
    fixed = lambda i: (0, 0)
    return pl.pallas_call(
        _out_ffn_kernel,
        grid=(m // tm,),
        in_specs=[pl.BlockSpec((tm, D_MODEL), row),
                  pl.BlockSpec((tm, ATTN_WIDTH), row),
                  pl.BlockSpec((tm, GDN_VAL_WIDTH), row),
                  pl.BlockSpec((tm, GDN_VAL_WIDTH), row),
                  pl.BlockSpec((1, GDN_DV), fixed),
                  pl.BlockSpec((D_MODEL, D_MODEL), fixed, pipeline_mode=pl.Buffered(1)),
                  pl.BlockSpec((1, D_MODEL), fixed),
                  pl.BlockSpec((1, D_MODEL), fixed),
                  pl.BlockSpec((D_MODEL, D_FF), fixed, pipeline_mode=pl.Buffered(1)),
                  pl.BlockSpec((D_FF, D_MODEL), fixed, pipeline_mode=pl.Buffered(1)),
                  pl.BlockSpec((1, D_MODEL), fixed)],
        out_specs=pl.BlockSpec((tm, D_MODEL), row),
        out_shape=jax.ShapeDtypeStruct((m, D_MODEL), F32),
        compiler_params=pltpu.CompilerParams(dimension_semantics=("arbitrary",),
                                             vmem_limit_bytes=VMEM_LIMIT),
        name="out_ffn",
    )(x, ao, go, z, gnw, wout, npost, npre, wup, wdown, nffn)


def _sample_prep_kernel(x_ref, nw_ref, w_ref, cs_ref, cw_ref, alog_ref, dtb_ref,
                        qexp_ref, knew_ref, vnew_ref, qt_ref, kt_ref, vg_ref, gate_ref, z_ref,
                        cs_out_ref):
    h = _rms(x_ref[...], nw_ref[...]).astype(BF16)
    pa = jnp.dot(h, w_ref[:, 0:OFF_GDN], preferred_element_type=F32)
    pg = jnp.dot(h, w_ref[:, OFF_GDN:OFF_Z], preferred_element_type=F32)
    z_ref[...] = jnp.dot(h, w_ref[:, OFF_Z:OFF_AB], preferred_element_type=F32)
    ab = jnp.dot(h, w_ref[:, OFF_AB:IN_COLS], preferred_element_type=F32)

    lo = lax.broadcasted_iota(jnp.int32, (1, LANES), 1) < HEAD_DIM
    for t in range(ATTN_WIDTH // LANES):
        tile = pa[:, t * LANES:(t + 1) * LANES]
        rolled = pltpu.roll(tile, HEAD_DIM, axis=1)
        if t < ATTN_WIDTH // LANES // KV_HEADS:
            even, odd = jnp.where(lo, tile, 0.0), jnp.where(lo, rolled, 0.0)
        else:
            even, odd = jnp.where(lo, 0.0, rolled), jnp.where(lo, 0.0, tile)
        qexp_ref[2 * t] = even
        qexp_ref[2 * t + 1] = odd
    knew_ref[...] = pa[:, ATTN_WIDTH:ATTN_WIDTH + KV_WIDTH]
    vnew_ref[...] = pa[:, ATTN_WIDTH + KV_WIDTH:ATTN_COLS]

    cw = cw_ref[...]
    y = None
    for r in range(CONV_W - 1):
        term = cs_ref[:, r * GDN_CONV_DIM:(r + 1) * GDN_CONV_DIM] * cw[r:r + 1]
        y = term if y is None else y + term
    y = y + pg * cw[CONV_W - 1:CONV_W]
    act = y * _sigmoid(y)
    cs_out_ref[:, 0:(CONV_W - 2) * GDN_CONV_DIM] = cs_ref[:, GDN_CONV_DIM:(CONV_W - 1) * GDN_CONV_DIM]
    cs_out_ref[:, (CONV_W - 2) * GDN_CONV_DIM:] = pg

    for hd in range(GDN_HEADS):
        qn = _l2n(act[:, hd * LANES:(hd + 1) * LANES]) * (GDN_DK ** -0.5)
        kn = _l2n(act[:, GDN_KEY_WIDTH + hd * LANES:GDN_KEY_WIDTH + (hd + 1) * LANES])
        qt_ref[hd] = qn.T
        kt_ref[hd] = kn.T
    vg_ref[...] = act[:, 2 * GDN_KEY_WIDTH:]
    gb = _gates(ab, alog_ref[...], dtb_ref[...])
    lane = lax.broadcasted_iota(jnp.int32, (1, LANES), 1)
    gate_ref[...] = jnp.where(lane < GDN_HEADS, jnp.exp(gb), gb)


def _sample_prep(x, nw, w, cs, cw, alog, dtb):
    n = x.shape[0]
    return pl.pallas_call(
        _sample_prep_kernel,
        out_shape=[jax.ShapeDtypeStruct((ATTN_HEADS, n, LANES), F32),
                   jax.ShapeDtypeStruct((n, KV_WIDTH), F32),
                   jax.ShapeDtypeStruct((n, KV_WIDTH), F32),
                   jax.ShapeDtypeStruct((GDN_HEADS, GDN_DK, n), F32),
                   jax.ShapeDtypeStruct((GDN_HEADS, GDN_DK, n), F32),
                   jax.ShapeDtypeStruct((n, GDN_VAL_WIDTH), F32),
                   jax.ShapeDtypeStruct((n, LANES), F32),
                   jax.ShapeDtypeStruct((n, GDN_VAL_WIDTH), F32),
                   jax.ShapeDtypeStruct((n, (CONV_W - 1) * GDN_CONV_DIM), F32)],
        compiler_params=pltpu.CompilerParams(vmem_limit_bytes=VMEM_LIMIT),
        name="sample_prep",
    )(x, nw, w, cs, cw, alog, dtb)


ATTN_BB = 16


def _attn_sample_kernel(sink_ref, qexp_ref, knew_ref, vnew_ref, kbuf_ref, vbuf_ref,
                        o_ref, kout_ref, vout_ref, q_scr, o_scr):
    for h in range(ATTN_HEADS):
        q_scr[pl.ds(h, ATTN_BB, stride=ATTN_HEADS), :] = qexp_ref[h]
    hrow = lax.broadcasted_iota(jnp.int32, (ATTN_HEADS, 1), 0)
    sink = jnp.zeros((ATTN_HEADS, 1), F32)
    slope = jnp.zeros((ATTN_HEADS, 1), F32)
    for h in range(ATTN_HEADS):
        sink = jnp.where(hrow == h, sink_ref[0, h], sink)
        slope = jnp.where(hrow == h, _alibi_slope(h), slope)
    pos = lax.broadcasted_iota(jnp.int32, (1, WINDOW), 1)
    bias = slope * (WINDOW - pos).astype(F32)
    rowid = lax.broadcasted_iota(jnp.int32, (WINDOW, 1), 0)
    last = rowid == WINDOW - 1
    toks = range(ATTN_BB)
    qe = [q_scr[b * ATTN_HEADS:(b + 1) * ATTN_HEADS, :] for b in toks]
    s_old = [_dot_nt(qe[b], kbuf_ref[b]) * (HEAD_DIM ** -0.5) - bias for b in toks]
    probs, p_new = [], []
    for b in toks:
        s_new = jnp.sum(qe[b] * knew_ref[b:b + 1, :], axis=-1, keepdims=True) * (HEAD_DIM ** -0.5)
        mx = jnp.maximum(jnp.maximum(jnp.max(s_old[b], axis=-1, keepdims=True), s_new), sink)
        p_old = jnp.exp(s_old[b] - mx)
        pn = jnp.exp(s_new - mx)
        den = jnp.sum(p_old, axis=-1, keepdims=True) + pn + jnp.exp(sink - mx)
        probs.append(p_old / den)
        p_new.append(pn / den)
    for b in toks:
        o_scr[b * ATTN_HEADS:(b + 1) * ATTN_HEADS, :] = (_dot(probs[b], vbuf_ref[b])
                                                         + p_new[b] * vnew_ref[b:b + 1, :])
    for b in toks:
        kout_ref[b] = jnp.where(last, knew_ref[b:b + 1, :], pltpu.roll(kbuf_ref[b], WINDOW - 1, axis=0))
        vout_ref[b] = jnp.where(last, vnew_ref[b:b + 1, :], pltpu.roll(vbuf_ref[b], WINDOW - 1, axis=0))
    lo = lax.broadcasted_iota(jnp.int32, (1, LANES), 1) < HEAD_DIM
    for t in range(ATTN_WIDTH // LANES):
        even = o_scr[pl.ds(2 * t, ATTN_BB, stride=ATTN_HEADS), :]
        odd = o_scr[pl.ds(2 * t + 1, ATTN_BB, stride=ATTN_HEADS), :]
        if t < ATTN_WIDTH // LANES // KV_HEADS:
            tile = jnp.where(lo, even, pltpu.roll(odd, HEAD_DIM, axis=1))
        else:
            tile = jnp.where(lo, pltpu.roll(even, HEAD_DIM, axis=1), odd)
        o_ref[:, t * LANES:(t + 1) * LANES] = tile.astype(o_ref.dtype)


def _attn_sample(sinks, qexp, knew, vnew, kbuf, vbuf):
    n = knew.shape[0]
    tok2 = lambda i: (i, 0)
    tok3 = lambda i: (i, 0, 0)
    return pl.pallas_call(
        _attn_sample_kernel,
        grid=(n // ATTN_BB,),
        in_specs=[pl.BlockSpec(memory_space=pltpu.SMEM),
                  pl.BlockSpec((ATTN_HEADS, ATTN_BB, LANES), lambda i: (0, i, 0)),
                  pl.BlockSpec((ATTN_BB, KV_WIDTH), tok2),
                  pl.BlockSpec((ATTN_BB, KV_WIDTH), tok2),
                  pl.BlockSpec((ATTN_BB, WINDOW, KV_WIDTH), tok3),
                  pl.BlockSpec((ATTN_BB, WINDOW, KV_WIDTH), tok3)],
        out_specs=[pl.BlockSpec((ATTN_BB, ATTN_WIDTH), tok2),
                   pl.BlockSpec((ATTN_BB, WINDOW, KV_WIDTH), tok3),
                   pl.BlockSpec((ATTN_BB, WINDOW, KV_WIDTH), tok3)],
        out_shape=[jax.ShapeDtypeStruct((n, ATTN_WIDTH), BF16),
                   jax.ShapeDtypeStruct((n, WINDOW, KV_WIDTH), F32),
                   jax.ShapeDtypeStruct((n, WINDOW, KV_WIDTH), F32)],
        scratch_shapes=[pltpu.VMEM((ATTN_BB * ATTN_HEADS, LANES), F32),
                        pltpu.VMEM((ATTN_BB * ATTN_HEADS, LANES), F32)],
        compiler_params=pltpu.CompilerParams(dimension_semantics=("arbitrary",),
                                             vmem_limit_bytes=VMEM_LIMIT),
        name="attn_sample",
    )(sinks, qexp, knew, vnew, kbuf, vbuf)


GDN_BB = 8


def _gdn_sample_kernel(gate_ref, qt_ref, kt_ref, vg_ref, s_ref, o_ref, s_out_ref):
    i = pl.program_id(0)
    n = qt_ref.shape[-1]
    shift = (n - i * GDN_BB) % n
    units = [(j, h) for h in range(GDN_HEADS) for j in range(GDN_BB)]
    qt = [pltpu.roll(qt_ref[h], shift, axis=1) for h in range(GDN_HEADS)]
    kt = [pltpu.roll(kt_ref[h], shift, axis=1) for h in range(GDN_HEADS)]
    kb = [jnp.broadcast_to(kt[h][:, j:j + 1], (GDN_DK, GDN_DV)) for j, h in units]
    qb = [jnp.broadcast_to(qt[h][:, j:j + 1], (GDN_DK, GDN_DV)) for j, h in units]
    s = [s_ref[j, h] * gate_ref[(i * GDN_BB + j) * SUBLANES + h] for j, h in units]
    kv = [jnp.sum(su * k, axis=0, keepdims=True) for su, k in zip(s, kb)]
    u = [(vg_ref[j:j + 1, h * LANES:(h + 1) * LANES] - kvu)
         * gate_ref[(i * GDN_BB + j) * SUBLANES + GDN_HEADS + h] for (j, h), kvu in zip(units, kv)]
    s = [su + k * uu for su, k, uu in zip(s, kb, u)]
    for (j, h), su in zip(units, s):
        s_out_ref[j, h] = su
    for (j, h), su, q in zip(units, s, qb):
        o_ref[j:j + 1, h * LANES:(h + 1) * LANES] = jnp.sum(su * q, axis=0, keepdims=True)


def _gdn_sample(gate, qt, kt, vg, s0):
    n = vg.shape[0]
    return pl.pallas_call(
        _gdn_sample_kernel,
        grid=(n // GDN_BB,),
        in_specs=[pl.BlockSpec(memory_space=pltpu.SMEM),
                  pl.BlockSpec((GDN_HEADS, GDN_DK, n), lambda i: (0, 0, 0)),
                  pl.BlockSpec((GDN_HEADS, GDN_DK, n), lambda i: (0, 0, 0)),
                  pl.BlockSpec((GDN_BB, GDN_VAL_WIDTH), lambda i: (i, 0)),
                  pl.BlockSpec((GDN_BB, GDN_HEADS, GDN_DK, GDN_DV), lambda i: (i, 0, 0, 0))],
        out_specs=[pl.BlockSpec((GDN_BB, GDN_VAL_WIDTH), lambda i: (i, 0)),
                   pl.BlockSpec((GDN_BB, GDN_HEADS, GDN_DK, GDN_DV), lambda i: (i, 0, 0, 0))],
        out_shape=[jax.ShapeDtypeStruct((n, GDN_VAL_WIDTH), F32),
                   jax.ShapeDtypeStruct((n, GDN_HEADS, GDN_DK, GDN_DV), F32)],
        compiler_params=pltpu.CompilerParams(dimension_semantics=("arbitrary",),
                                             vmem_limit_bytes=VMEM_LIMIT),
        name="gdn_sample",
    )(gate, qt, kt, vg, s0)


def _pad_lanes(v):
    return jnp.pad(v.astype(F32), (0, LANES - v.shape[0])).reshape(1, LANES)


def kernel(x_prompt, x_sample, state_conv, cache_win_k, cache_win_v, state_gdn, norm_mix_pre, w_in,
           attn_sinks, conv_w, gdn_a_log, gdn_dt_bias, gdn_norm, w_out, norm_mix_post, norm_ffn_pre,
           w_up, w_down, norm_ffn_post):
    assert w_in.shape[0] == 1, "single-layer trunk"
    bsz, seq, _ = x_prompt.shape
    nsamp = x_sample.shape[0]

    w_in_p = jnp.pad(w_in[0], ((0, 0), (0, IN_COLS - w_in.shape[2]))).astype(BF16)
    w_out_b = w_out[0].astype(BF16)
    w_up_b = w_up[0].astype(BF16)
    w_down_b = w_down[0].astype(BF16)
    n_pre = norm_mix_pre[0].reshape(1, D_MODEL)
    n_post = norm_mix_post[0].reshape(1, D_MODEL)
    n_fpre = norm_ffn_pre[0].reshape(1, D_MODEL)
    n_fpost = norm_ffn_post[0].reshape(1, D_MODEL)
    sinks = attn_sinks[0].reshape(1, ATTN_HEADS)
    cw = conv_w[0]
    alog = _pad_lanes(gdn_a_log[0])
    dtb = _pad_lanes(gdn_dt_bias[0])
    gnw = gdn_norm[0].reshape(1, GDN_DV)

    xp = x_prompt.reshape(bsz * seq, D_MODEL)
    pa, act, pz, gb, p_conv = _inproj(xp, n_pre, w_in_p, cw, alog, dtb, MIX_T, seq)
    yp, p_s = _mix_ffn(sinks, pa, act, gb, xp, pz, gnw, w_out_b, n_post, n_fpre, w_up_b, w_down_b,
                       n_fpost, seq)
    pa = pa.reshape(bsz, seq, ATTN_COLS)

    xs = x_sample.reshape(nsamp, D_MODEL)
    cs = state_conv[0].reshape(nsamp, (CONV_W - 1) * GDN_CONV_DIM)
    qexp, knew, vnew, qt, kt, vg, gate, zs, cs_new = _sample_prep(xs, n_pre, w_in_p, cs, cw, alog, dtb)
    kbuf = cache_win_k[0].reshape(nsamp, WINDOW, KV_WIDTH)
    vbuf = cache_win_v[0].reshape(nsamp, WINDOW, KV_WIDTH)
    ao_s, k_new_cache, v_new_cache = _attn_sample(sinks, qexp, knew, vnew, kbuf, vbuf)
    go_s, s_new = _gdn_sample(gate[:, :SUBLANES].reshape(nsamp * SUBLANES), qt, kt, vg, state_gdn[0])
    ys = _out_ffn(xs, ao_s, go_s, zs, gnw, w_out_b, n_post, n_fpre, w_up_b, w_down_b, n_fpost, nsamp)

    kv_shape = (1, -1, WINDOW, KV_HEADS, HEAD_DIM)
    return (yp.reshape(bsz, seq, D_MODEL),
            ys.reshape(nsamp, 1, D_MODEL),
            p_conv[:, SUBLANES - (CONV_W - 1):, :][None],
            pa[:, seq - WINDOW:, ATTN_WIDTH:ATTN_WIDTH + KV_WIDTH].reshape(kv_shape),
            pa[:, seq - WINDOW:, ATTN_WIDTH + KV_WIDTH:].reshape(kv_shape),
            p_s[None],
            cs_new.reshape(1, nsamp, CONV_W - 1, GDN_CONV_DIM),
            k_new_cache.reshape(kv_shape),
            v_new_cache.reshape(kv_shape),
            s_new[None])
```

```python
import functools

import jax
import jax.numpy as jnp
from jax import lax
from jax.experimental import pallas as pl
from jax.experimental.pallas import tpu as pltpu

F32 = jnp.float32
BF16 = jnp.bfloat16

D_MODEL = 1024
ATTN_HEADS = 8
KV_HEADS = 2
HEAD_DIM = 64
GQA_GROUP = ATTN_HEADS // KV_HEADS
ATTN_WIDTH = ATTN_HEADS * HEAD_DIM
KV_WIDTH = KV_HEADS * HEAD_DIM
WINDOW = 128
GDN_HEADS = 4
GDN_DK = 128
GDN_DV = 128
GDN_KEY_WIDTH = GDN_HEADS * GDN_DK
GDN_VAL_WIDTH = GDN_HEADS * GDN_DV
GDN_CONV_DIM = 2 * GDN_KEY_WIDTH + GDN_VAL_WIDTH
CONV_W = 4
D_FF = 4 * D_MODEL
EPS = 1e-6

LANES = 128
SUBLANES = 8
ATTN_COLS = ATTN_WIDTH + 2 * KV_WIDTH
AB_COLS = LANES
IN_COLS = ATTN_COLS + GDN_CONV_DIM + GDN_VAL_WIDTH + AB_COLS
OFF_GDN = ATTN_COLS
OFF_Z = OFF_GDN + GDN_CONV_DIM
OFF_AB = OFF_Z + GDN_VAL_WIDTH
CHUNK = 128
VMEM_LIMIT = 56 * 1024 * 1024
VMEM_LIMIT_MIX = 60 * 1024 * 1024


def _rms(x, w):
    return x * lax.rsqrt(jnp.mean(x * x, axis=-1, keepdims=True) + EPS) * w


def _sigmoid(x):
    return 1.0 / (1.0 + jnp.exp(-x))


def _softplus(x):
    return jnp.maximum(x, 0.0) + jnp.log(1.0 + jnp.exp(-jnp.abs(x)))


def _dot(a, b):
    return jnp.dot(a.astype(BF16), b.astype(BF16), preferred_element_type=F32)


def _dot_nt(a, b):
    return lax.dot_general(a.astype(BF16), b.astype(BF16), (((1,), (1,)), ((), ())),
                           preferred_element_type=F32)


def _alibi_slope(h):
    return 2.0 ** (-8.0 * (h + 1) / ATTN_HEADS)


def _causal_conv(x, carry, w):
    t, c = x.shape
    x3 = x.reshape(t // SUBLANES, SUBLANES, c)
    row8 = lax.broadcasted_iota(jnp.int32, (1, SUBLANES, 1), 1)
    y = None
    for shift in range(CONV_W - 1, -1, -1):
        if shift == 0:
            xs = x3
        else:
            xr = pltpu.roll(x3, shift, axis=1)
            cr = pltpu.roll(carry, shift, axis=0)
            prev = jnp.concatenate([cr[None], xr[:-1]], axis=0)
            xs = jnp.where(row8 < shift, prev, xr)
        term = xs * w[CONV_W - 1 - shift:CONV_W - shift][None]
        y = term if y is None else y + term
    return y.reshape(t, c)


def _silu(y):
    return y * _sigmoid(y)


def _l2n(x):
    return x * lax.rsqrt(jnp.sum(x * x, axis=-1, keepdims=True) + EPS)


def _gates(ab, alog, dtb):
    g = -jnp.exp(alog) * _softplus(ab + dtb)
    lane = lax.broadcasted_iota(jnp.int32, (1, LANES), 1)
    return jnp.where(lane < GDN_HEADS, g, _sigmoid(ab))


INPROJ_SUB = 128


def _inproj_kernel(x_ref, nw_ref, w_ref, cw_ref, alog_ref, dtb_ref,
                   oa_ref, og_ref, oz_ref, ogb_ref, conv_out_ref, carry_scr, *, tiles_per_seq):
    tm = x_ref.shape[0]

    @pl.when(lax.rem(pl.program_id(0), tiles_per_seq) == 0)
    def _():
        carry_scr[...] = jnp.zeros_like(carry_scr)

    carry = carry_scr[...]
    for r0 in range(0, tm, INPROJ_SUB):
        rows = slice(r0, r0 + INPROJ_SUB)
        h = _rms(x_ref[rows, :], nw_ref[...]).astype(BF16)
        pg = jnp.dot(h, w_ref[:, OFF_GDN:OFF_Z], preferred_element_type=F32)
        oa_ref[rows, :] = jnp.dot(h, w_ref[:, 0:OFF_GDN], preferred_element_type=F32)
        oz_ref[rows, :] = jnp.dot(h, w_ref[:, OFF_Z:OFF_AB], preferred_element_type=F32)
        ab = jnp.dot(h, w_ref[:, OFF_AB:IN_COLS], preferred_element_type=F32)
        ogb_ref[rows, :] = _gates(ab, alog_ref[...], dtb_ref[...])
        og_ref[rows, :] = _causal_conv(pg, carry, cw_ref[...])
        carry = pg[INPROJ_SUB - SUBLANES:INPROJ_SUB]
    carry_scr[...] = carry
    conv_out_ref[...] = carry


def _inproj(x, nw, w, cw, alog, dtb, tm, seq):
    m = x.shape[0]
    tiles_per_seq = seq // tm
    row = lambda i: (i, 0)
    fixed = lambda i: (0, 0)
    return pl.pallas_call(
        functools.partial(_inproj_kernel, tiles_per_seq=tiles_per_seq),
        grid=(m // tm,),
        in_specs=[pl.BlockSpec((tm, D_MODEL), row),
                  pl.BlockSpec((1, D_MODEL), fixed),
                  pl.BlockSpec((D_MODEL, IN_COLS), fixed),
                  pl.BlockSpec((CONV_W, GDN_CONV_DIM), fixed),
                  pl.BlockSpec((1, LANES), fixed),
                  pl.BlockSpec((1, LANES), fixed)],
        out_specs=[pl.BlockSpec((tm, ATTN_COLS), row),
                   pl.BlockSpec((tm, GDN_CONV_DIM), row),
                   pl.BlockSpec((tm, GDN_VAL_WIDTH), row),
                   pl.BlockSpec((tm, AB_COLS), row),
                   pl.BlockSpec((None, SUBLANES, GDN_CONV_DIM), lambda i: (i // tiles_per_seq, 0, 0))],
        out_shape=[jax.ShapeDtypeStruct((m, ATTN_COLS), F32),
                   jax.ShapeDtypeStruct((m, GDN_CONV_DIM), F32),
                   jax.ShapeDtypeStruct((m, GDN_VAL_WIDTH), F32),
                   jax.ShapeDtypeStruct((m, AB_COLS), F32),
                   jax.ShapeDtypeStruct((m // seq, SUBLANES, GDN_CONV_DIM), F32)],
        scratch_shapes=[pltpu.VMEM((SUBLANES, GDN_CONV_DIM), F32)],
        compiler_params=pltpu.CompilerParams(dimension_semantics=("arbitrary",),
                                             vmem_limit_bytes=VMEM_LIMIT),
        name="inproj",
    )(x, nw, w, cw, alog, dtb)


FF_UP_CHUNK = 512
FF_DOWN_CHUNK = 256
FF_STAGES = D_FF // FF_UP_CHUNK + D_MODEL // FF_DOWN_CHUNK


def _out_ffn_stages(x_ref, ao_ref, go_ref, z_ref, gnw_ref, wout_ref, npost_ref, npre_ref, wup_ref,
                    wdown_ref, nffn_ref, y_ref):
    gnw = gnw_ref[...]
    gy = []
    for hd in range(GDN_HEADS):
        lanes = slice(hd * LANES, (hd + 1) * LANES)
        z = z_ref[:, lanes]
        gy.append((_rms(go_ref[:, lanes], gnw) * (z * _sigmoid(z))).astype(BF16))
    m = (jnp.dot(ao_ref[...], wout_ref[0:ATTN_WIDTH, :], preferred_element_type=F32)
         + jnp.dot(jnp.concatenate(gy, axis=1), wout_ref[ATTN_WIDTH:, :], preferred_element_type=F32))
    x1 = x_ref[...] + _rms(m, npost_ref[...])
    h = _rms(x1, npre_ref[...]).astype(BF16)
    yield
    hidden = []
    for j in range(0, D_FF, FF_UP_CHUNK):
        u = jnp.maximum(jnp.dot(h, wup_ref[:, j:j + FF_UP_CHUNK], preferred_element_type=F32), 0.0)
        hidden.append((u * u).astype(BF16))
        yield
    hidden = jnp.concatenate(hidden, axis=1)
    f = []
    for j in range(0, D_MODEL, FF_DOWN_CHUNK):
        f.append(jnp.dot(hidden, wdown_ref[:, j:j + FF_DOWN_CHUNK], preferred_element_type=F32))
        yield
    y_ref[...] = x1 + _rms(jnp.concatenate(f, axis=1), nffn_ref[...])


def _dup_half(x, lo):
    xr = pltpu.roll(x, HEAD_DIM, axis=1)
    return jnp.where(lo, x, xr), jnp.where(lo, xr, x)


def _attn_prompt_stages(sink_ref, cur_ref, prev_ref, seq_row0, o_ref):
    tq = cur_ref.shape[0]
    kall = jnp.concatenate([prev_ref[:, 0:KV_WIDTH], cur_ref[:, ATTN_WIDTH:ATTN_WIDTH + KV_WIDTH]], axis=0)
    vall = jnp.concatenate([prev_ref[:, KV_WIDTH:], cur_ref[:, ATTN_WIDTH + KV_WIDTH:]], axis=0)
    lo = lax.broadcasted_iota(jnp.int32, (1, LANES), 1) < HEAD_DIM
    k2 = _dup_half(kall, lo)
    v2 = _dup_half(vall, lo)
    qi = lax.broadcasted_iota(jnp.int32, (WINDOW, 2 * WINDOW), 0)
    si = lax.broadcasted_iota(jnp.int32, (WINDOW, 2 * WINDOW), 1)
    dist = WINDOW + qi - si
    band = (dist >= 0) & (dist <= WINDOW)
    distf = dist.astype(F32)
    for r in range(tq // WINDOW):
        rows = slice(r * WINDOW, (r + 1) * WINDOW)
        valid = band
        if r == 0:
            valid = band & ((seq_row0 - WINDOW + si) >= 0)
        for j in range(KV_HEADS):
            kw = k2[j][r * WINDOW:(r + 2) * WINDOW]
            vw = v2[j][r * WINDOW:(r + 2) * WINDOW]
            t0 = cur_ref[rows, (2 * j) * LANES:(2 * j + 1) * LANES]
            t1 = cur_ref[rows, (2 * j + 1) * LANES:(2 * j + 2) * LANES]
            lhs = jnp.concatenate([jnp.where(lo, t0, 0.0), jnp.where(lo, 0.0, t0),
                                   jnp.where(lo, t1, 0.0), jnp.where(lo, 0.0, t1)], axis=0)
            s = _dot_nt(lhs, kw) * (HEAD_DIM ** -0.5)
            probs = []
            for g in range(GQA_GROUP):
                h = j * GQA_GROUP + g
                sg = s[g * WINDOW:(g + 1) * WINDOW] - _alibi_slope(h) * distf
                sg = jnp.where(valid, sg, -jnp.inf)
                sink = sink_ref[0, h]
                mx = jnp.maximum(jnp.max(sg, axis=-1, keepdims=True), sink)
                p = jnp.exp(sg - mx)
                den = jnp.sum(p, axis=-1, keepdims=True) + jnp.exp(sink - mx)
                probs.append(p / den)
            yield
            o = _dot(jnp.concatenate(probs, axis=0), vw)
            o01 = jnp.where(lo, o[0:WINDOW], o[WINDOW:2 * WINDOW])
            o23 = jnp.where(lo, o[2 * WINDOW:3 * WINDOW], o[3 * WINDOW:4 * WINDOW])
            o_ref[rows, (2 * j) * LANES:(2 * j + 1) * LANES] = o01.astype(o_ref.dtype)
            o_ref[rows, (2 * j + 1) * LANES:(2 * j + 2) * LANES] = o23.astype(o_ref.dtype)
            yield


def _gdn_prompt_stages(conv_ref, gb_ref, state, o_ref):
    t = conv_ref.shape[0]
    beta = gb_ref[...]
    g = beta
    rin = lax.broadcasted_iota(jnp.int32, (t, 1), 0) & (CHUNK - 1)
    step = 1
    while step < CHUNK:
        g = g + jnp.where(rin >= step, pltpu.roll(g, step, axis=0), 0.0)
        step *= 2
    gt = g.T

    row = lax.broadcasted_iota(jnp.int32, (CHUNK, CHUNK), 0)
    col = lax.broadcasted_iota(jnp.int32, (CHUNK, CHUNK), 1)
    lower_incl = row >= col
    lower_strict = row > col

    heads = range(GDN_HEADS)
    units = [(c, h) for c in range(t // CHUNK) for h in heads]
    pre = []
    for c, h in units:
        rows = slice(c * CHUNK, (c + 1) * CHUNK)
        qc = _l2n(_silu(conv_ref[rows, h * LANES:(h + 1) * LANES])) * (GDN_DK ** -0.5)
        kc = _l2n(_silu(conv_ref[rows, GDN_KEY_WIDTH + h * LANES:GDN_KEY_WIDTH + (h + 1) * LANES]))
        vc = _silu(conv_ref[rows, 2 * GDN_KEY_WIDTH + h * LANES:2 * GDN_KEY_WIDTH + (h + 1) * LANES])
        gc = g[rows, h:h + 1]
        bc = beta[rows, GDN_HEADS + h:GDN_HEADS + h + 1]
        gr = gt[h:h + 1, c * CHUNK:(c + 1) * CHUNK]
        decay = jnp.where(lower_incl, jnp.exp(jnp.where(lower_incl, gc - gr, 0.0)), 0.0)
        eg = jnp.exp(gc)
        glast = gc[CHUNK - 1:CHUNK]
        rhs = jnp.concatenate([vc * bc, kc * (bc * eg)], axis=1)
        k_dec_t = (kc * jnp.exp(glast - gc)).T
        kq = _dot_nt(jnp.concatenate([kc, qc], axis=0), kc)
        a = jnp.where(lower_strict, bc * kq[0:CHUNK] * decay, 0.0)
        qk = jnp.where(lower_incl, kq[CHUNK:] * decay, 0.0)
        pre.append(dict(a=a, rhs=rhs, lhs_u=jnp.concatenate([qk, k_dec_t], axis=0), q_dec=qc * eg,
                        ld=jnp.exp(glast)))
        if h == GDN_HEADS - 1:
            yield

    mats = [p["a"] for p in pre]
    first = ((row >> 1) == (col >> 1)) & ((row & 1) == 1) & ((col & 1) == 0)
    xo = [jnp.where(first, -a, 0.0) for a in mats]
    size = 2
    while size < CHUNK:
        shift = size.bit_length()
        m = ((row >> shift) == (col >> shift)) & ((row & size) != 0) & ((col & size) == 0)
        aoff = [jnp.where(m, a, 0.0) for a in mats]
        y = [ao + _dot(ao, x) for ao, x in zip(aoff, xo)]
        yield
        xo = [x - yy - _dot(x, yy) for x, yy in zip(xo, y)]
        size *= 2
        yield
    trs = [p["rhs"] + _dot(x, p["rhs"]) for p, x in zip(pre, xo)]
    yield

    for c in range(t // CHUNK):
        rows = slice(c * CHUNK, (c + 1) * CHUNK)
        idx = [c * GDN_HEADS + h for h in heads]
        ws = [_dot(jnp.concatenate([trs[u][:, GDN_DV:], pre[u]["q_dec"]], axis=0), state[h])
              for h, u in zip(heads, idx)]
        yield
        ou = [_dot(pre[u]["lhs_u"], trs[u][:, 0:GDN_DV] - ws[h][0:CHUNK]) for h, u in zip(heads, idx)]
        for h, u in zip(heads, idx):
            o_ref[rows, h * LANES:(h + 1) * LANES] = ws[h][CHUNK:] + ou[h][0:CHUNK]
            state[h] = state[h] * pre[u]["ld"] + ou[h][CHUNK:]
        yield


def _run_interleaved(primary, n_primary, secondary, n_secondary):
    done = object()
    issued = 0
    for k in range(n_primary):
        next(primary, done)
        target = ((k + 1) * n_secondary + n_primary - 1) // n_primary
        while issued < target:
            next(secondary, done)
            issued += 1
    for _ in primary:
        pass
    for _ in secondary:
        pass


def _roundrobin(*gens):
    done = object()
    live = list(gens)
    while live:
        for gen in list(live):
            if next(gen, done) is done:
                live.remove(gen)
            else:
                yield


MIX_T = 512


def _mix_ffn_kernel(sink_ref, pa_ref, prev_ref, conv_ref, gb_ref, x_ref, z_ref, gnw_ref, wout_ref,
                    npost_ref, npre_ref, wup_ref, wdown_ref, nffn_ref,
                    y_ref, s_out_ref, s_scr, ao_scr, go_scr, *, n_tiles, tiles_per_seq):
    i = pl.program_id(0)

    @pl.when(i == 0)
    def _():
        s_scr[...] = jnp.zeros_like(s_scr)
        ao_scr[...] = jnp.zeros_like(ao_scr)
        go_scr[...] = jnp.zeros_like(go_scr)

    front_tile = jnp.minimum(i, n_tiles - 1)
    tile_in_seq = lax.rem(front_tile, tiles_per_seq)
    seq_start = tile_in_seq == 0
    front_valid = i < n_tiles

    ffn = _out_ffn_stages(x_ref, ao_scr, go_scr, z_ref, gnw_ref, wout_ref, npost_ref, npre_ref,
                          wup_ref, wdown_ref, nffn_ref, y_ref)
    next(ffn)

    old = [s_scr[h] for h in range(GDN_HEADS)]
    state = [jnp.where(seq_start, 0.0, s) for s in old]
    front = _roundrobin(_gdn_prompt_stages(conv_ref, gb_ref, state, go_scr),
                        _attn_prompt_stages(sink_ref, pa_ref, prev_ref, tile_in_seq * MIX_T, ao_scr))
    mix_stages = (MIX_T // CHUNK) * 3 + 2 * (CHUNK.bit_length() - 2) + 1 + 2 * (MIX_T // WINDOW) * KV_HEADS
    _run_interleaved(ffn, FF_STAGES, front, mix_stages)

    for h in range(GDN_HEADS):
        s_new = jnp.where(front_valid, state[h], old[h])
        s_scr[h] = s_new
        s_out_ref[h] = s_new


def _mix_ffn(sinks, pa, act, gb, x, z, gnw, wout, npost, npre, wup, wdown, nffn, seq):
    m = x.shape[0]
    n_tiles = m // MIX_T
    tiles_per_seq = seq // MIX_T
    blocks_per_tile = MIX_T // WINDOW
    front = lambda i: (jnp.minimum(i, n_tiles - 1), 0)
    back = lambda i: (jnp.maximum(i - 1, 0), 0)
    fixed = lambda i: (0, 0)
    resident = dict(pipeline_mode=pl.Buffered(1))
    kern = functools.partial(_mix_ffn_kernel, n_tiles=n_tiles, tiles_per_seq=tiles_per_seq)
    return pl.pallas_call(
        kern,
        grid=(n_tiles + 1,),
        in_specs=[pl.BlockSpec(memory_space=pltpu.SMEM),
                  pl.BlockSpec((MIX_T, ATTN_COLS), front),
                  pl.BlockSpec((WINDOW, 2 * KV_WIDTH),
                               lambda i: (jnp.maximum(jnp.minimum(i, n_tiles - 1) * blocks_per_tile - 1, 0),
                                          ATTN_WIDTH // (2 * KV_WIDTH))),
                  pl.BlockSpec((MIX_T, GDN_CONV_DIM), front),
                  pl.BlockSpec((MIX_T, AB_COLS), front),
                  pl.BlockSpec((MIX_T, D_MODEL), back),
                  pl.BlockSpec((MIX_T, GDN_VAL_WIDTH), back),
                  pl.BlockSpec((1, GDN_DV), fixed),
                  pl.BlockSpec((D_MODEL, D_MODEL), fixed, **resident),
                  pl.BlockSpec((1, D_MODEL), fixed),
                  pl.BlockSpec((1, D_MODEL), fixed),
                  pl.BlockSpec((D_MODEL, D_FF), fixed, **resident),
                  pl.BlockSpec((D_FF, D_MODEL), fixed, **resident),
                  pl.BlockSpec((1, D_MODEL), fixed)],
        out_specs=[pl.BlockSpec((MIX_T, D_MODEL), back),
                   pl.BlockSpec((None, GDN_HEADS, GDN_DK, GDN_DV),
                                lambda i: (jnp.minimum(i, n_tiles - 1) // tiles_per_seq, 0, 0, 0))],
        out_shape=[jax.ShapeDtypeStruct((m, D_MODEL), F32),
                   jax.ShapeDtypeStruct((m // seq, GDN_HEADS, GDN_DK, GDN_DV), F32)],
        scratch_shapes=[pltpu.VMEM((GDN_HEADS, GDN_DK, GDN_DV), F32),
                        pltpu.VMEM((MIX_T, ATTN_WIDTH), BF16),
                        pltpu.VMEM((MIX_T, GDN_VAL_WIDTH), F32)],
        compiler_params=pltpu.CompilerParams(dimension_semantics=("arbitrary",),
                                             vmem_limit_bytes=VMEM_LIMIT_MIX),
        name="mix_ffn",
    )(sinks, pa, pa, act, gb, x, z, gnw, wout, npost, npre, wup, wdown, nffn)


def _out_ffn_kernel(*refs):
    for _ in _out_ffn_stages(*refs):
        pass


def _out_ffn(x, ao, go, z, gnw, wout, npost, npre, wup, wdown, nffn, tm):
    m = x.shape[0]
    row = lambda i: (i, 0)
    fixed = lambda i: (0, 0)
    return pl.pallas_call(
        _out_ffn_kernel,
        grid=(m // tm,),
        in_specs=[pl.BlockSpec((tm, D_MODEL), row),
                  pl.BlockSpec((tm, ATTN_WIDTH), row),
                  pl.BlockSpec((tm, GDN_VAL_WIDTH), row),
                  pl.BlockSpec((tm, GDN_VAL_WIDTH), row),
                  pl.BlockSpec((1, GDN_DV), fixed),
                  pl.BlockSpec((D_MODEL, D_MODEL), fixed, pipeline_mode=pl.Buffered(1)),
                  pl.BlockSpec((1, D_MODEL), fixed),
                  pl.BlockSpec((1, D_MODEL), fixed),
                  pl.BlockSpec((D_MODEL, D_FF), fixed, pipeline_mode=pl.Buffered(1)),
                  pl.BlockSpec((D_FF, D_MODEL), fixed, pipeline_mode=pl.Buffered(1)),
                  pl.BlockSpec((1, D_MODEL), fixed)],
        out_specs=pl.BlockSpec((tm, D_MODEL), row),
        out_shape=jax.ShapeDtypeStruct((m, D_MODEL), F32),
        compiler_params=pltpu.CompilerParams(dimension_semantics=("arbitrary",),
                                             vmem_limit_bytes=VMEM_LIMIT),
        name="out_ffn",
    )(x, ao, go, z, gnw, wout, npost, npre, wup, wdown, nffn)


def _sample_prep_kernel(x_ref, nw_ref, w_ref, cs_ref, cw_ref, alog_ref, dtb_ref,
                        qexp_ref, knew_ref, vnew_ref, qt_ref, kt_ref, vg_ref, gate_ref, z_ref,
                        cs_out_ref):
    h = _rms(x_ref[...], nw_ref[...]).astype(BF16)
    pa = jnp.dot(h, w_ref[:, 0:OFF_GDN], preferred_element_type=F32)
    pg = jnp.dot(h, w_ref[:, OFF_GDN:OFF_Z], preferred_element_type=F32)
    z_ref[...] = jnp.dot(h, w_ref[:, OFF_Z:OFF_AB], preferred_element_type=F32)
    ab = jnp.dot(h, w_ref[:, OFF_AB:IN_COLS], preferred_element_type=F32)

    lo = lax.broadcasted_iota(jnp.int32, (1, LANES), 1) < HEAD_DIM
    for t in range(ATTN_WIDTH // LANES):
        tile = pa[:, t * LANES:(t + 1) * LANES]
        rolled = pltpu.roll(tile, HEAD_DIM, axis=1)
        if t < ATTN_WIDTH // LANES // KV_HEADS:
            even, odd = jnp.where(lo, tile, 0.0), jnp.where(lo, rolled, 0.0)
        else:
            even, odd = jnp.where(lo, 0.0, rolled), jnp.where(lo, 0.0, tile)
        qexp_ref[2 * t] = even
        qexp_ref[2 * t + 1] = odd
    knew_ref[...] = pa[:, ATTN_WIDTH:ATTN_WIDTH + KV_WIDTH]
    vnew_ref[...] = pa[:, ATTN_WIDTH + KV_WIDTH:ATTN_COLS]

    cw = cw_ref[...]
    y = None
    for r in range(CONV_W - 1):
        term = cs_ref[:, r * GDN_CONV_DIM:(r + 1) * GDN_CONV_DIM] * cw[r:r + 1]
        y = term if y is None else y + term
    y = y + pg * cw[CONV_W - 1:CONV_W]
    act = y * _sigmoid(y)
    cs_out_ref[:, 0:(CONV_W - 2) * GDN_CONV_DIM] = cs_ref[:, GDN_CONV_DIM:(CONV_W - 1) * GDN_CONV_DIM]
    cs_out_ref[:, (CONV_W - 2) * GDN_CONV_DIM:] = pg

    for hd in range(GDN_HEADS):
        qn = _l2n(act[:, hd * LANES:(hd + 1) * LANES]) * (GDN_DK ** -0.5)
        kn = _l2n(act[:, GDN_KEY_WIDTH + hd * LANES:GDN_KEY_WIDTH + (hd + 1) * LANES])
        qt_ref[hd] = qn.T
        kt_ref[hd] = kn.T
    vg_ref[...] = act[:, 2 * GDN_KEY_WIDTH:]
    gb = _gates(ab, alog_ref[...], dtb_ref[...])
    lane = lax.broadcasted_iota(jnp.int32, (1, LANES), 1)
    gate_ref[...] = jnp.where(lane < GDN_HEADS, jnp.exp(gb), gb)


def _sample_prep(x, nw, w, cs, cw, alog, dtb):
    n = x.shape[0]
    return pl.pallas_call(
        _sample_prep_kernel,
        out_shape=[jax.ShapeDtypeStruct((ATTN_HEADS, n, LANES), F32),
                   jax.ShapeDtypeStruct((n, KV_WIDTH), F32),
                   jax.ShapeDtypeStruct((n, KV_WIDTH), F32),
                   jax.ShapeDtypeStruct((GDN_HEADS, GDN_DK, n), F32),
                   jax.ShapeDtypeStruct((GDN_HEADS, GDN_DK, n), F32),
                   jax.ShapeDtypeStruct((n, GDN_VAL_WIDTH), F32),
                   jax.ShapeDtypeStruct((n, LANES), F32),
                   jax.ShapeDtypeStruct((n, GDN_VAL_WIDTH), F32),
                   jax.ShapeDtypeStruct((n, (CONV_W - 1) * GDN_CONV_DIM), F32)],
        compiler_params=pltpu.CompilerParams(vmem_limit_bytes=VMEM_LIMIT),
        name="sample_prep",
    )(x, nw, w, cs, cw, alog, dtb)


ATTN_BB = 16


def _attn_sample_kernel(sink_ref, qexp_ref, knew_ref, vnew_ref, kbuf_ref, vbuf_ref,
                        o_ref, kout_ref, vout_ref, q_scr, o_scr):
    for h in range(ATTN_HEADS):
        q_scr[pl.ds(h, ATTN_BB, stride=ATTN_HEADS), :] = qexp_ref[h]
    hrow = lax.broadcasted_iota(jnp.int32, (ATTN_HEADS, 1), 0)
    sink = jnp.zeros((ATTN_HEADS, 1), F32)
    slope = jnp.zeros((ATTN_HEADS, 1), F32)
    for h in range(ATTN_HEADS):
        sink = jnp.where(hrow == h, sink_ref[0, h], sink)
        slope = jnp.where(hrow == h, _alibi_slope(h), slope)
    pos = lax.broadcasted_iota(jnp.int32, (1, WINDOW), 1)
    bias = slope * (WINDOW - pos).astype(F32)
    rowid = lax.broadcasted_iota(jnp.int32, (WINDOW, 1), 0)
    last = rowid == WINDOW - 1
    toks = range(ATTN_BB)
    qe = [q_scr[b * ATTN_HEADS:(b + 1) * ATTN_HEADS, :] for b in toks]
    s_old = [_dot_nt(qe[b], kbuf_ref[b]) * (HEAD_DIM ** -0.5) - bias for b in toks]
    probs, p_new = [], []
    for b in toks:
        s_new = jnp.sum(qe[b] * knew_ref[b:b + 1, :], axis=-1, keepdims=True) * (HEAD_DIM ** -0.5)
        mx = jnp.maximum(jnp.maximum(jnp.max(s_old[b], axis=-1, keepdims=True), s_new), sink)
        p_old = jnp.exp(s_old[b] - mx)
        pn = jnp.exp(s_new - mx)
        den = jnp.sum(p_old, axis=-1, keepdims=True) + pn + jnp.exp(sink - mx)
        probs.append(p_old / den)
        p_new.append(pn / den)
    for b in toks:
        o_scr[b * ATTN_HEADS:(b + 1) * ATTN_HEADS, :] = (_dot(probs[b], vbuf_ref[b])
                                                         + p_new[b] * vnew_ref[b:b + 1, :])
    for b in toks:
        kout_ref[b] = jnp.where(last, knew_ref[b:b + 1, :], pltpu.roll(kbuf_ref[b], WINDOW - 1, axis=0))
        vout_ref[b] = jnp.where(last, vnew_ref[b:b + 1, :], pltpu.roll(vbuf_ref[b], WINDOW - 1, axis=0))
    lo = lax.broadcasted_iota(jnp.int32, (1, LANES), 1) < HEAD_DIM
    for t in range(ATTN_WIDTH // LANES):
        even = o_scr[pl.ds(2 * t, ATTN_BB, stride=ATTN_HEADS), :]
        odd = o_scr[pl.ds(2 * t + 1, ATTN_BB, stride=ATTN_HEADS), :]
        if t < ATTN_WIDTH // LANES // KV_HEADS:
            tile = jnp.where(lo, even, pltpu.roll(odd, HEAD_DIM, axis=1))
        else:
            tile = jnp.where(lo, pltpu.roll(even, HEAD_DIM, axis=1), odd)
        o_ref[:, t * LANES:(t + 1) * LANES] = tile.astype(o_ref.dtype)


def _attn_sample(sinks, qexp, knew, vnew, kbuf, vbuf):
    n = knew.shape[0]
    tok2 = lambda i: (i, 0)
    tok3 = lambda i: (i, 0, 0)
    return pl.pallas_call(
        _attn_sample_kernel,
        grid=(n // ATTN_BB,),
        in_specs=[pl.BlockSpec(memory_space=pltpu.SMEM),
                  pl.BlockSpec((ATTN_HEADS, ATTN_BB, LANES), lambda i: (0, i, 0)),
                  pl.BlockSpec((ATTN_BB, KV_WIDTH), tok2),
                  pl.BlockSpec((ATTN_BB, KV_WIDTH), tok2),
                  pl.BlockSpec((ATTN_BB, WINDOW, KV_WIDTH), tok3),
                  pl.BlockSpec((ATTN_BB, WINDOW, KV_WIDTH), tok3)],
        out_specs=[pl.BlockSpec((ATTN_BB, ATTN_WIDTH), tok2),
                   pl.BlockSpec((ATTN_BB, WINDOW, KV_WIDTH), tok3),
                   pl.BlockSpec((ATTN_BB, WINDOW, KV_WIDTH), tok3)],
        out_shape=[jax.ShapeDtypeStruct((n, ATTN_WIDTH), BF16),
                   jax.ShapeDtypeStruct((n, WINDOW, KV_WIDTH), F32),
                   jax.ShapeDtypeStruct((n, WINDOW, KV_WIDTH), F32)],
        scratch_shapes=[pltpu.VMEM((ATTN_BB * ATTN_HEADS, LANES), F32),
                        pltpu.VMEM((ATTN_BB * ATTN_HEADS, LANES), F32)],
        compiler_params=pltpu.CompilerParams(dimension_semantics=("arbitrary",),
                                             vmem_limit_bytes=VMEM_LIMIT),
        name="attn_sample",
    )(sinks, qexp, knew, vnew, kbuf, vbuf)


GDN_BB = 8


def _gdn_sample_kernel(gate_ref, qt_ref, kt_ref, vg_ref, s_ref, o_ref, s_out_ref):
    i = pl.program_id(0)
    n = qt_ref.shape[-1]
    shift = (n - i * GDN_BB) % n
    units = [(j, h) for h in range(GDN_HEADS) for j in range(GDN_BB)]
    qt = [pltpu.roll(qt_ref[h], shift, axis=1) for h in range(GDN_HEADS)]
    kt = [pltpu.roll(kt_ref[h], shift, axis=1) for h in range(GDN_HEADS)]
    kb = [jnp.broadcast_to(kt[h][:, j:j + 1], (GDN_DK, GDN_DV)) for j, h in units]
    qb = [jnp.broadcast_to(qt[h][:, j:j + 1], (GDN_DK, GDN_DV)) for j, h in units]
    s = [s_ref[j, h] * gate_ref[(i * GDN_BB + j) * SUBLANES + h] for j, h in units]
    kv = [jnp.sum(su * k, axis=0, keepdims=True) for su, k in zip(s, kb)]
    u = [(vg_ref[j:j + 1, h * LANES:(h + 1) * LANES] - kvu)
         * gate_ref[(i * GDN_BB + j) * SUBLANES + GDN_HEADS + h] for (j, h), kvu in zip(units, kv)]
    s = [su + k * uu for su, k, uu in zip(s, kb, u)]
    for (j, h), su in zip(units, s):
        s_out_ref[j, h] = su
    for (j, h), su, q in zip(units, s, qb):
        o_ref[j:j + 1, h * LANES:(h + 1) * LANES] = jnp.sum(su * q, axis=0, keepdims=True)


def _gdn_sample(gate, qt, kt, vg, s0):
    n = vg.shape[0]
    return pl.pallas_call(
        _gdn_sample_kernel,
        grid=(n // GDN_BB,),
        in_specs=[pl.BlockSpec(memory_space=pltpu.SMEM),
                  pl.BlockSpec((GDN_HEADS, GDN_DK, n), lambda i: (0, 0, 0)),
                  pl.BlockSpec((GDN_HEADS, GDN_DK, n), lambda i: (0, 0, 0)),
                  pl.BlockSpec((GDN_BB, GDN_VAL_WIDTH), lambda i: (i, 0)),
                  pl.BlockSpec((GDN_BB, GDN_HEADS, GDN_DK, GDN_DV), lambda i: (i, 0, 0, 0))],
        out_specs=[pl.BlockSpec((GDN_BB, GDN_VAL_WIDTH), lambda i: (i, 0)),
                   pl.BlockSpec((GDN_BB, GDN_HEADS, GDN_DK, GDN_DV), lambda i: (i, 0, 0, 0))],
        out_shape=[jax.ShapeDtypeStruct((n, GDN_VAL_WIDTH), F32),
                   jax.ShapeDtypeStruct((n, GDN_HEADS, GDN_DK, GDN_DV), F32)],
        compiler_params=pltpu.CompilerParams(dimension_semantics=("arbitrary",),
                                             vmem_limit_bytes=VMEM_LIMIT),
        name="gdn_sample",
    )(gate, qt, kt, vg, s0)


def _pad_lanes(v):
    return jnp.pad(v.astype(F32), (0, LANES - v.shape[0])).reshape(1, LANES)


def kernel(x_prompt, x_sample, state_conv, cache_win_k, cache_win_v, state_gdn, norm_mix_pre, w_in,
           attn_sinks, conv_w, gdn_a_log, gdn_dt_bias, gdn_norm, w_out, norm_mix_post, norm_ffn_pre,
           w_up, w_down, norm_ffn_post):
    assert w_in.shape[0] == 1, "single-layer trunk"
    bsz, seq, _ = x_prompt.shape
    nsamp = x_sample.shape[0]

    w_in_p = jnp.pad(w_in[0], ((0, 0), (0, IN_COLS - w_in.shape[2]))).astype(BF16)
    w_out_b = w_out[0].astype(BF16)
    w_up_b = w_up[0].astype(BF16)
    w_down_b = w_down[0].astype(BF16)
    n_pre = norm_mix_pre[0].reshape(1, D_MODEL)
    n_post = norm_mix_post[0].reshape(1, D_MODEL)
    n_fpre = norm_ffn_pre[0].reshape(1, D_MODEL)
    n_fpost = norm_ffn_post[0].reshape(1, D_MODEL)
    sinks = attn_sinks[0].reshape(1, ATTN_HEADS)
    cw = conv_w[0]
    alog = _pad_lanes(gdn_a_log[0])
    dtb = _pad_lanes(gdn_dt_bias[0])
    gnw = gdn_norm[0].reshape(1, GDN_DV)

    xp = x_prompt.reshape(bsz * seq, D_MODEL)
    pa, act, pz, gb, p_conv = _inproj(xp, n_pre, w_in_p, cw, alog, dtb, MIX_T, seq)
    yp, p_s = _mix_ffn(sinks, pa, act, gb, xp, pz, gnw, w_out_b, n_post, n_fpre, w_up_b, w_down_b,
                       n_fpost, seq)
    pa = pa.reshape(bsz, seq, ATTN_COLS)

    xs = x_sample.reshape(nsamp, D_MODEL)
    cs = state_conv[0].reshape(nsamp, (CONV_W - 1) * GDN_CONV_DIM)
    qexp, knew, vnew, qt, kt, vg, gate, zs, cs_new = _sample_prep(xs, n_pre, w_in_p, cs, cw, alog, dtb)
    kbuf = cache_win_k[0].reshape(nsamp, WINDOW, KV_WIDTH)
    vbuf = cache_win_v[0].reshape(nsamp, WINDOW, KV_WIDTH)
    ao_s, k_new_cache, v_new_cache = _attn_sample(sinks, qexp, knew, vnew, kbuf, vbuf)
    go_s, s_new = _gdn_sample(gate[:, :SUBLANES].reshape(nsamp * SUBLANES), qt, kt, vg, state_gdn[0])
    ys = _out_ffn(xs, ao_s, go_s, zs, gnw, w_out_b, n_post, n_fpre, w_up_b, w_down_b, n_fpost, nsamp)

    kv_shape = (1, -1, WINDOW, KV_HEADS, HEAD_DIM)
    return (yp.reshape(bsz, seq, D_MODEL),
            ys.reshape(nsamp, 1, D_MODEL),
            p_conv[:, SUBLANES - (CONV_W - 1):, :][None],
            pa[:, seq - WINDOW:, ATTN_WIDTH:ATTN_WIDTH + KV_WIDTH].reshape(kv_shape),
            pa[:, seq - WINDOW:, ATTN_WIDTH + KV_WIDTH:].reshape(kv_shape),
            p_s[None],
            cs_new.reshape(1, nsamp, CONV_W - 1, GDN_CONV_DIM),
            k_new_cache.reshape(kv_shape),
            v_new_cache.reshape(kv_shape),
            s_new[None])
```

```python
import functools

import jax
import jax.numpy as jnp
from jax import lax
from jax.experimental import pallas as pl
from jax.experimental.pallas import tpu as pltpu

F32 = jnp.float32
BF16 = jnp.bfloat16

D_MODEL = 1024
ATTN_HEADS = 8
KV_HEADS = 2
HEAD_DIM = 64
GQA_GROUP = ATTN_HEADS // KV_HEADS
ATTN_WIDTH = ATTN_HEADS * HEAD_DIM
KV_WIDTH = KV_HEADS * HEAD_DIM
WINDOW = 128
GDN_HEADS = 4
GDN_DK = 128
GDN_DV = 128
GDN_KEY_WIDTH = GDN_HEADS * GDN_DK
GDN_VAL_WIDTH = GDN_HEADS * GDN_DV
GDN_CONV_DIM = 2 * GDN_KEY_WIDTH + GDN_VAL_WIDTH
CONV_W = 4
D_FF = 4 * D_MODEL
EPS = 1e-6

LANES = 128
SUBLANES = 8
ATTN_COLS = ATTN_WIDTH + 2 * KV_WIDTH
AB_COLS = LANES
IN_COLS = ATTN_COLS + GDN_CONV_DIM + GDN_VAL_WIDTH + AB_COLS
OFF_GDN = ATTN_COLS
OFF_Z = OFF_GDN + GDN_CONV_DIM
OFF_AB = OFF_Z + GDN_VAL_WIDTH
CHUNK = 128
VMEM_LIMIT = 56 * 1024 * 1024
VMEM_LIMIT_MIX = 60 * 1024 * 1024


def _rms(x, w):
    return x * lax.rsqrt(jnp.mean(x * x, axis=-1, keepdims=True) + EPS) * w


def _sigmoid(x):
    return 1.0 / (1.0 + jnp.exp(-x))


def _softplus(x):
    return jnp.maximum(x, 0.0) + jnp.log(1.0 + jnp.exp(-jnp.abs(x)))


def _dot(a, b):
    return jnp.dot(a.astype(BF16), b.astype(BF16), preferred_element_type=F32)


def _dot_nt(a, b):
    return lax.dot_general(a.astype(BF16), b.astype(BF16), (((1,), (1,)), ((), ())),
                           preferred_element_type=F32)


def _alibi_slope(h):
    return 2.0 ** (-8.0 * (h + 1) / ATTN_HEADS)


def _causal_conv(x, carry, w):
    t, c = x.shape
    x3 = x.reshape(t // SUBLANES, SUBLANES, c)
    row8 = lax.broadcasted_iota(jnp.int32, (1, SUBLANES, 1), 1)
    y = None
    for shift in range(CONV_W - 1, -1, -1):
        if shift == 0:
            xs = x3
        else:
            xr = pltpu.roll(x3, shift, axis=1)
            cr = pltpu.roll(carry, shift, axis=0)
            prev = jnp.concatenate([cr[None], xr[:-1]], axis=0)
            xs = jnp.where(row8 < shift, prev, xr)
        term = xs * w[CONV_W - 1 - shift:CONV_W - shift][None]
        y = term if y is None else y + term
    return y.reshape(t, c)


def _silu(y):
    return y * _sigmoid(y)


def _l2n(x):
    return x * lax.rsqrt(jnp.sum(x * x, axis=-1, keepdims=True) + EPS)


def _gates(ab, alog, dtb):
    g = -jnp.exp(alog) * _softplus(ab + dtb)
    lane = lax.broadcasted_iota(jnp.int32, (1, LANES), 1)
    return jnp.where(lane < GDN_HEADS, g, _sigmoid(ab))


INPROJ_T = 1024
INPROJ_SUB = 128


def _inproj_kernel(x_ref, nw_ref, w_ref, cw_ref, alog_ref, dtb_ref,
                   oa_ref, og_ref, oz_ref, ogb_ref, conv_out_ref, carry_scr, *, tiles_per_seq):
    tm = x_ref.shape[0]

    @pl.when(lax.rem(pl.program_id(0), tiles_per_seq) == 0)
    def _():
        carry_scr[...] = jnp.zeros_like(carry_scr)

    carry = carry_scr[...]
    for r0 in range(0, tm, INPROJ_SUB):
        rows = slice(r0, r0 + INPROJ_SUB)
        h = _rms(x_ref[rows, :], nw_ref[...]).astype(BF16)
        pg = jnp.dot(h, w_ref[:, OFF_GDN:OFF_Z], preferred_element_type=F32)
        oa_ref[rows, :] = jnp.dot(h, w_ref[:, 0:OFF_GDN], preferred_element_type=F32)
        oz_ref[rows, :] = jnp.dot(h, w_ref[:, OFF_Z:OFF_AB], preferred_element_type=F32)
        ab = jnp.dot(h, w_ref[:, OFF_AB:IN_COLS], preferred_element_type=F32)
        ogb_ref[rows, :] = _gates(ab, alog_ref[...], dtb_ref[...])
        og_ref[rows, :] = _causal_conv(pg, carry, cw_ref[...])
        carry = pg[INPROJ_SUB - SUBLANES:INPROJ_SUB]
    carry_scr[...] = carry
    conv_out_ref[...] = carry


def _inproj(x, nw, w, cw, alog, dtb, tm, seq):
    m = x.shape[0]
    tiles_per_seq = seq // tm
    row = lambda i: (i, 0)
    fixed = lambda i: (0, 0)
    return pl.pallas_call(
        functools.partial(_inproj_kernel, tiles_per_seq=tiles_per_seq),
        grid=(m // tm,),
        in_specs=[pl.BlockSpec((tm, D_MODEL), row),
                  pl.BlockSpec((1, D_MODEL), fixed),
                  pl.BlockSpec((D_MODEL, IN_COLS), fixed),
                  pl.BlockSpec((CONV_W, GDN_CONV_DIM), fixed),
                  pl.BlockSpec((1, LANES), fixed),
                  pl.BlockSpec((1, LANES), fixed)],
        out_specs=[pl.BlockSpec((tm, ATTN_COLS), row),
                   pl.BlockSpec((tm, GDN_CONV_DIM), row),
                   pl.BlockSpec((tm, GDN_VAL_WIDTH), row),
                   pl.BlockSpec((tm, AB_COLS), row),
                   pl.BlockSpec((None, SUBLANES, GDN_CONV_DIM), lambda i: (i // tiles_per_seq, 0, 0))],
        out_shape=[jax.ShapeDtypeStruct((m, ATTN_COLS), F32),
                   jax.ShapeDtypeStruct((m, GDN_CONV_DIM), F32),
                   jax.ShapeDtypeStruct((m, GDN_VAL_WIDTH), F32),
                   jax.ShapeDtypeStruct((m, AB_COLS), F32),
                   jax.ShapeDtypeStruct((m // seq, SUBLANES, GDN_CONV_DIM), F32)],
        scratch_shapes=[pltpu.VMEM((SUBLANES, GDN_CONV_DIM), F32)],
        compiler_params=pltpu.CompilerParams(dimension_semantics=("arbitrary",),
                                             vmem_limit_bytes=VMEM_LIMIT),
        name="inproj",
    )(x, nw, w, cw, alog, dtb)


FF_UP_CHUNK = 512
FF_DOWN_CHUNK = 256
FF_STAGES = D_FF // FF_UP_CHUNK + D_MODEL // FF_DOWN_CHUNK


def _out_ffn_stages(x_ref, ao_ref, go_ref, z_ref, gnw_ref, wout_ref, npost_ref, npre_ref, wup_ref,
                    wdown_ref, nffn_ref, y_ref):
    gnw = gnw_ref[...]
    gy = []
    for hd in range(GDN_HEADS):
        lanes = slice(hd * LANES, (hd + 1) * LANES)
        z = z_ref[:, lanes]
        gy.append((_rms(go_ref[:, lanes], gnw) * (z * _sigmoid(z))).astype(BF16))
    m = (jnp.dot(ao_ref[...], wout_ref[0:ATTN_WIDTH, :], preferred_element_type=F32)
         + jnp.dot(jnp.concatenate(gy, axis=1), wout_ref[ATTN_WIDTH:, :], preferred_element_type=F32))
    x1 = x_ref[...] + _rms(m, npost_ref[...])
    h = _rms(x1, npre_ref[...]).astype(BF16)
    yield
    hidden = []
    for j in range(0, D_FF, FF_UP_CHUNK):
        u = jnp.maximum(jnp.dot(h, wup_ref[:, j:j + FF_UP_CHUNK], preferred_element_type=F32), 0.0)
        hidden.append((u * u).astype(BF16))
        yield
    hidden = jnp.concatenate(hidden, axis=1)
    f = []
    for j in range(0, D_MODEL, FF_DOWN_CHUNK):
        f.append(jnp.dot(hidden, wdown_ref[:, j:j + FF_DOWN_CHUNK], preferred_element_type=F32))
        yield
    y_ref[...] = x1 + _rms(jnp.concatenate(f, axis=1), nffn_ref[...])


def _dup_half(x, lo):
    xr = pltpu.roll(x, HEAD_DIM, axis=1)
    return jnp.where(lo, x, xr), jnp.where(lo, xr, x)


def _attn_prompt_stages(sink_ref, cur_ref, prev_ref, seq_row0, o_ref):
    tq = cur_ref.shape[0]
    kall = jnp.concatenate([prev_ref[:, 0:KV_WIDTH], cur_ref[:, ATTN_WIDTH:ATTN_WIDTH + KV_WIDTH]], axis=0)
    vall = jnp.concatenate([prev_ref[:, KV_WIDTH:], cur_ref[:, ATTN_WIDTH + KV_WIDTH:]], axis=0)
    lo = lax.broadcasted_iota(jnp.int32, (1, LANES), 1) < HEAD_DIM
    k2 = _dup_half(kall, lo)
    v2 = _dup_half(vall, lo)
    qi = lax.broadcasted_iota(jnp.int32, (WINDOW, 2 * WINDOW), 0)
    si = lax.broadcasted_iota(jnp.int32, (WINDOW, 2 * WINDOW), 1)
    dist = WINDOW + qi - si
    band = (dist >= 0) & (dist <= WINDOW)
    distf = dist.astype(F32)
    for r in range(tq // WINDOW):
        rows = slice(r * WINDOW, (r + 1) * WINDOW)
        valid = band
        if r == 0:
            valid = band & ((seq_row0 - WINDOW + si) >= 0)
        for j in range(KV_HEADS):
            kw = k2[j][r * WINDOW:(r + 2) * WINDOW]
            vw = v2[j][r * WINDOW:(r + 2) * WINDOW]
            t0 = cur_ref[rows, (2 * j) * LANES:(2 * j + 1) * LANES]
            t1 = cur_ref[rows, (2 * j + 1) * LANES:(2 * j + 2) * LANES]
            lhs = jnp.concatenate([jnp.where(lo, t0, 0.0), jnp.where(lo, 0.0, t0),
                                   jnp.where(lo, t1, 0.0), jnp.where(lo, 0.0, t1)], axis=0)
            s = _dot_nt(lhs, kw) * (HEAD_DIM ** -0.5)
            probs = []
            for g in range(GQA_GROUP):
                h = j * GQA_GROUP + g
                sg = s[g * WINDOW:(g + 1) * WINDOW] - _alibi_slope(h) * distf
                sg = jnp.where(valid, sg, -jnp.inf)
                sink = sink_ref[0, h]
                mx = jnp.maximum(jnp.max(sg, axis=-1, keepdims=True), sink)
                p = jnp.exp(sg - mx)
                den = jnp.sum(p, axis=-1, keepdims=True) + jnp.exp(sink - mx)
                probs.append(p / den)
            yield
            o = _dot(jnp.concatenate(probs, axis=0), vw)
            o01 = jnp.where(lo, o[0:WINDOW], o[WINDOW:2 * WINDOW])
            o23 = jnp.where(lo, o[2 * WINDOW:3 * WINDOW], o[3 * WINDOW:4 * WINDOW])
            o_ref[rows, (2 * j) * LANES:(2 * j + 1) * LANES] = o01.astype(o_ref.dtype)
            o_ref[rows, (2 * j + 1) * LANES:(2 * j + 2) * LANES] = o23.astype(o_ref.dtype)
            yield


def _gdn_prompt_stages(conv_ref, gb_ref, state, o_ref):
    t = conv_ref.shape[0]
    beta = gb_ref[...]
    g = beta
    rin = lax.broadcasted_iota(jnp.int32, (t, 1), 0) & (CHUNK - 1)
    step = 1
    while step < CHUNK:
        g = g + jnp.where(rin >= step, pltpu.roll(g, step, axis=0), 0.0)
        step *= 2
    gt = g.T

    row = lax.broadcasted_iota(jnp.int32, (CHUNK, CHUNK), 0)
    col = lax.broadcasted_iota(jnp.int32, (CHUNK, CHUNK), 1)
    lower_incl = row >= col
    lower_strict = row > col

    heads = range(GDN_HEADS)
    units = [(c, h) for c in range(t // CHUNK) for h in heads]
    pre = []
    for c, h in units:
        rows = slice(c * CHUNK, (c + 1) * CHUNK)
        qc = _l2n(_silu(conv_ref[rows, h * LANES:(h + 1) * LANES])) * (GDN_DK ** -0.5)
        kc = _l2n(_silu(conv_ref[rows, GDN_KEY_WIDTH + h * LANES:GDN_KEY_WIDTH + (h + 1) * LANES]))
        vc = _silu(conv_ref[rows, 2 * GDN_KEY_WIDTH + h * LANES:2 * GDN_KEY_WIDTH + (h + 1) * LANES])
        gc = g[rows, h:h + 1]
        bc = beta[rows, GDN_HEADS + h:GDN_HEADS + h + 1]
        gr = gt[h:h + 1, c * CHUNK:(c + 1) * CHUNK]
        decay = jnp.where(lower_incl, jnp.exp(jnp.where(lower_incl, gc - gr, 0.0)), 0.0)
        eg = jnp.exp(gc)
        glast = gc[CHUNK - 1:CHUNK]
        rhs = jnp.concatenate([vc * bc, kc * (bc * eg)], axis=1)
        k_dec_t = (kc * jnp.exp(glast - gc)).T
        kq = _dot_nt(jnp.concatenate([kc, qc], axis=0), kc)
        a = jnp.where(lower_strict, bc * kq[0:CHUNK] * decay, 0.0)
        qk = jnp.where(lower_incl, kq[CHUNK:] * decay, 0.0)
        pre.append(dict(a=a, rhs=rhs, lhs_u=jnp.concatenate([qk, k_dec_t], axis=0), q_dec=qc * eg,
                        ld=jnp.exp(glast)))
        if h == GDN_HEADS - 1:
            yield

    mats = [p["a"] for p in pre]
    first = ((row >> 1) == (col >> 1)) & ((row & 1) == 1) & ((col & 1) == 0)
    xo = [jnp.where(first, -a, 0.0) for a in mats]
    size = 2
    while size < CHUNK:
        shift = size.bit_length()
        m = ((row >> shift) == (col >> shift)) & ((row & size) != 0) & ((col & size) == 0)
        aoff = [jnp.where(m, a, 0.0) for a in mats]
        y = [ao + _dot(ao, x) for ao, x in zip(aoff, xo)]
        yield
        xo = [x - yy - _dot(x, yy) for x, yy in zip(xo, y)]
        size *= 2
        yield
    trs = [p["rhs"] + _dot(x, p["rhs"]) for p, x in zip(pre, xo)]
    yield

    for c in range(t // CHUNK):
        rows = slice(c * CHUNK, (c + 1) * CHUNK)
        idx = [c * GDN_HEADS + h for h in heads]
        ws = [_dot(jnp.concatenate([trs[u][:, GDN_DV:], pre[u]["q_dec"]], axis=0), state[h])
              for h, u in zip(heads, idx)]
        yield
        ou = [_dot(pre[u]["lhs_u"], trs[u][:, 0:GDN_DV] - ws[h][0:CHUNK]) for h, u in zip(heads, idx)]
        for h, u in zip(heads, idx):
            o_ref[rows, h * LANES:(h + 1) * LANES] = ws[h][CHUNK:] + ou[h][0:CHUNK]
            state[h] = state[h] * pre[u]["ld"] + ou[h][CHUNK:]
        yield


def _run_interleaved(primary, n_primary, secondary, n_secondary):
    done = object()
    issued = 0
    for k in range(n_primary):
        next(primary, done)
        target = ((k + 1) * n_secondary + n_primary - 1) // n_primary
        while issued < target:
            next(secondary, done)
            issued += 1
    for _ in primary:
        pass
    for _ in secondary:
        pass


def _roundrobin(*gens):
    done = object()
    live = list(gens)
    while live:
        for gen in list(live):
            if next(gen, done) is done:
                live.remove(gen)
            else:
                yield


MIX_T = 512


def _mix_ffn_kernel(sink_ref, pa_ref, prev_ref, conv_ref, gb_ref, x_ref, z_ref, gnw_ref, wout_ref,
                    npost_ref, npre_ref, wup_ref, wdown_ref, nffn_ref,
                    y_ref, s_out_ref, s_scr, ao_scr, go_scr, *, n_tiles, tiles_per_seq):
    i = pl.program_id(0)

    @pl.when(i == 0)
    def _():
        s_scr[...] = jnp.zeros_like(s_scr)
        ao_scr[...] = jnp.zeros_like(ao_scr)
        go_scr[...] = jnp.zeros_like(go_scr)

    front_tile = jnp.minimum(i, n_tiles - 1)
    tile_in_seq = lax.rem(front_tile, tiles_per_seq)
    seq_start = tile_in_seq == 0
    front_valid = i < n_tiles

    ffn = _out_ffn_stages(x_ref, ao_scr, go_scr, z_ref, gnw_ref, wout_ref, npost_ref, npre_ref,
                          wup_ref, wdown_ref, nffn_ref, y_ref)
    next(ffn)

    old = [s_scr[h] for h in range(GDN_HEADS)]
    state = [jnp.where(seq_start, 0.0, s) for s in old]
    front = _roundrobin(_gdn_prompt_stages(conv_ref, gb_ref, state, go_scr),
                        _attn_prompt_stages(sink_ref, pa_ref, prev_ref, tile_in_seq * MIX_T, ao_scr))
    mix_stages = (MIX_T // CHUNK) * 3 + 2 * (CHUNK.bit_length() - 2) + 1 + 2 * (MIX_T // WINDOW) * KV_HEADS
    _run_interleaved(ffn, FF_STAGES, front, mix_stages)

    for h in range(GDN_HEADS):
        s_new = jnp.where(front_valid, state[h], old[h])
        s_scr[h] = s_new
        s_out_ref[h] = s_new


def _mix_ffn(sinks, pa, act, gb, x, z, gnw, wout, npost, npre, wup, wdown, nffn, seq):
    m = x.shape[0]
    n_tiles = m // MIX_T
    tiles_per_seq = seq // MIX_T
    blocks_per_tile = MIX_T // WINDOW
    front = lambda i: (jnp.minimum(i, n_tiles - 1), 0)
    back = lambda i: (jnp.maximum(i - 1, 0), 0)
    fixed = lambda i: (0, 0)
    resident = dict(pipeline_mode=pl.Buffered(1))
    kern = functools.partial(_mix_ffn_kernel, n_tiles=n_tiles, tiles_per_seq=tiles_per_seq)
    return pl.pallas_call(
        kern,
        grid=(n_tiles + 1,),
        in_specs=[pl.BlockSpec(memory_space=pltpu.SMEM),
                  pl.BlockSpec((MIX_T, ATTN_COLS), front),
                  pl.BlockSpec((WINDOW, 2 * KV_WIDTH),
                               lambda i: (jnp.maximum(jnp.minimum(i, n_tiles - 1) * blocks_per_tile - 1, 0),
                                          ATTN_WIDTH // (2 * KV_WIDTH))),
                  pl.BlockSpec((MIX_T, GDN_CONV_DIM), front),
                  pl.BlockSpec((MIX_T, AB_COLS), front),
                  pl.BlockSpec((MIX_T, D_MODEL), back),
                  pl.BlockSpec((MIX_T, GDN_VAL_WIDTH), back),
                  pl.BlockSpec((1, GDN_DV), fixed),
                  pl.BlockSpec((D_MODEL, D_MODEL), fixed, **resident),
                  pl.BlockSpec((1, D_MODEL), fixed),
                  pl.BlockSpec((1, D_MODEL), fixed),
                  pl.BlockSpec((D_MODEL, D_FF), fixed, **resident),
                  pl.BlockSpec((D_FF, D_MODEL), fixed, **resident),
                  pl.BlockSpec((1, D_MODEL), fixed)],
        out_specs=[pl.BlockSpec((MIX_T, D_MODEL), back),
                   pl.BlockSpec((None, GDN_HEADS, GDN_DK, GDN_DV),
                                lambda i: (jnp.minimum(i, n_tiles - 1) // tiles_per_seq, 0, 0, 0))],
        out_shape=[jax.ShapeDtypeStruct((m, D_MODEL), F32),
                   jax.ShapeDtypeStruct((m // seq, GDN_HEADS, GDN_DK, GDN_DV), F32)],
        scratch_shapes=[pltpu.VMEM((GDN_HEADS, GDN_DK, GDN_DV), F32),
                        pltpu.VMEM((MIX_T, ATTN_WIDTH), BF16),
                        pltpu.VMEM((MIX_T, GDN_VAL_WIDTH), F32)],
        compiler_params=pltpu.CompilerParams(dimension_semantics=("arbitrary",),
                                             vmem_limit_bytes=VMEM_LIMIT_MIX),
        name="mix_ffn",
    )(sinks, pa, pa, act, gb, x, z, gnw, wout, npost, npre, wup, wdown, nffn)


def _out_ffn_kernel(*refs):
    for _ in _out_ffn_stages(*refs):
        pass


def _out_ffn(x, ao, go, z, gnw, wout, npost, npre, wup, wdown, nffn, tm):
    m = x.shape[0]
    row = lambda i: (i, 0)
    fixed = lambda i: (0, 0)
    return pl.pallas_call(
        _out_ffn_kernel,
        grid=(m // tm,),
        in_specs=[pl.BlockSpec((tm, D_MODEL), row),
                  pl.BlockSpec((tm, ATTN_WIDTH), row),
                  pl.BlockSpec((tm, GDN_VAL_WIDTH), row),
                  pl.BlockSpec((tm, GDN_VAL_WIDTH), row),
                  pl.BlockSpec((1, GDN_DV), fixed),
                  pl.BlockSpec((D_MODEL, D_MODEL), fixed, pipeline_mode=pl.Buffered(1)),
                  pl.BlockSpec((1, D_MODEL), fixed),
                  pl.BlockSpec((1, D_MODEL), fixed),
                  pl.BlockSpec((D_MODEL, D_FF), fixed, pipeline_mode=pl.Buffered(1)),
                  pl.BlockSpec((D_FF, D_MODEL), fixed, pipeline_mode=pl.Buffered(1)),
                  pl.BlockSpec((1, D_MODEL), fixed)],
        out_specs=pl.BlockSpec((tm, D_MODEL), row),
        out_shape=jax.ShapeDtypeStruct((m, D_MODEL), F32),
        compiler_params=pltpu.CompilerParams(dimension_semantics=("arbitrary",),
                                             vmem_limit_bytes=VMEM_LIMIT),
        name="out_ffn",
    )(x, ao, go, z, gnw, wout, npost, npre, wup, wdown, nffn)


def _sample_prep_kernel(x_ref, nw_ref, w_ref, cs_ref, cw_ref, alog_ref, dtb_ref,
                        qexp_ref, knew_ref, vnew_ref, qt_ref, kt_ref, vg_ref, gate_ref, z_ref,
                        cs_out_ref):
    h = _rms(x_ref[...], nw_ref[...]).astype(BF16)
    pa = jnp.dot(h, w_ref[:, 0:OFF_GDN], preferred_element_type=F32)
    pg = jnp.dot(h, w_ref[:, OFF_GDN:OFF_Z], preferred_element_type=F32)
    z_ref[...] = jnp.dot(h, w_ref[:, OFF_Z:OFF_AB], preferred_element_type=F32)
    ab = jnp.dot(h, w_ref[:, OFF_AB:IN_COLS], preferred_element_type=F32)

    lo = lax.broadcasted_iota(jnp.int32, (1, LANES), 1) < HEAD_DIM
    for t in range(ATTN_WIDTH // LANES):
        tile = pa[:, t * LANES:(t + 1) * LANES]
        rolled = pltpu.roll(tile, HEAD_DIM, axis=1)
        if t < ATTN_WIDTH // LANES // KV_HEADS:
            even, odd = jnp.where(lo, tile, 0.0), jnp.where(lo, rolled, 0.0)
        else:
            even, odd = jnp.where(lo, 0.0, rolled), jnp.where(lo, 0.0, tile)
        qexp_ref[2 * t] = even
        qexp_ref[2 * t + 1] = odd
    knew_ref[...] = pa[:, ATTN_WIDTH:ATTN_WIDTH + KV_WIDTH]
    vnew_ref[...] = pa[:, ATTN_WIDTH + KV_WIDTH:ATTN_COLS]

    cw = cw_ref[...]
    y = None
    for r in range(CONV_W - 1):
        term = cs_ref[:, r * GDN_CONV_DIM:(r + 1) * GDN_CONV_DIM] * cw[r:r + 1]
        y = term if y is None else y + term
    y = y + pg * cw[CONV_W - 1:CONV_W]
    act = y * _sigmoid(y)
    cs_out_ref[:, 0:(CONV_W - 2) * GDN_CONV_DIM] = cs_ref[:, GDN_CONV_DIM:(CONV_W - 1) * GDN_CONV_DIM]
    cs_out_ref[:, (CONV_W - 2) * GDN_CONV_DIM:] = pg

    for hd in range(GDN_HEADS):
        qn = _l2n(act[:, hd * LANES:(hd + 1) * LANES]) * (GDN_DK ** -0.5)
        kn = _l2n(act[:, GDN_KEY_WIDTH + hd * LANES:GDN_KEY_WIDTH + (hd + 1) * LANES])
        qt_ref[hd] = qn.T
        kt_ref[hd] = kn.T
    vg_ref[...] = act[:, 2 * GDN_KEY_WIDTH:]
    gb = _gates(ab, alog_ref[...], dtb_ref[...])
    lane = lax.broadcasted_iota(jnp.int32, (1, LANES), 1)
    gate_ref[...] = jnp.where(lane < GDN_HEADS, jnp.exp(gb), gb)


def _sample_prep(x, nw, w, cs, cw, alog, dtb):
    n = x.shape[0]
    return pl.pallas_call(
        _sample_prep_kernel,
        out_shape=[jax.ShapeDtypeStruct((ATTN_HEADS, n, LANES), F32),
                   jax.ShapeDtypeStruct((n, KV_WIDTH), F32),
                   jax.ShapeDtypeStruct((n, KV_WIDTH), F32),
                   jax.ShapeDtypeStruct((GDN_HEADS, GDN_DK, n), F32),
                   jax.ShapeDtypeStruct((GDN_HEADS, GDN_DK, n), F32),
                   jax.ShapeDtypeStruct((n, GDN_VAL_WIDTH), F32),
                   jax.ShapeDtypeStruct((n, LANES), F32),
                   jax.ShapeDtypeStruct((n, GDN_VAL_WIDTH), F32),
                   jax.ShapeDtypeStruct((n, (CONV_W - 1) * GDN_CONV_DIM), F32)],
        compiler_params=pltpu.CompilerParams(vmem_limit_bytes=VMEM_LIMIT),
        name="sample_prep",
    )(x, nw, w, cs, cw, alog, dtb)


ATTN_BB = 16


def _attn_sample_kernel(sink_ref, qexp_ref, knew_ref, vnew_ref, kbuf_ref, vbuf_ref,
                        o_ref, kout_ref, vout_ref, q_scr, o_scr):
    for h in range(ATTN_HEADS):
        q_scr[pl.ds(h, ATTN_BB, stride=ATTN_HEADS), :] = qexp_ref[h]
    hrow = lax.broadcasted_iota(jnp.int32, (ATTN_HEADS, 1), 0)
    sink = jnp.zeros((ATTN_HEADS, 1), F32)
    slope = jnp.zeros((ATTN_HEADS, 1), F32)
    for h in range(ATTN_HEADS):
        sink = jnp.where(hrow == h, sink_ref[0, h], sink)
        slope = jnp.where(hrow == h, _alibi_slope(h), slope)
    pos = lax.broadcasted_iota(jnp.int32, (1, WINDOW), 1)
    bias = slope * (WINDOW - pos).astype(F32)
    rowid = lax.broadcasted_iota(jnp.int32, (WINDOW, 1), 0)
    last = rowid == WINDOW - 1
    toks = range(ATTN_BB)
    qe = [q_scr[b * ATTN_HEADS:(b + 1) * ATTN_HEADS, :] for b in toks]
    s_old = [_dot_nt(qe[b], kbuf_ref[b]) * (HEAD_DIM ** -0.5) - bias for b in toks]
    probs, p_new = [], []
    for b in toks:
        s_new = jnp.sum(qe[b] * knew_ref[b:b + 1, :], axis=-1, keepdims=True) * (HEAD_DIM ** -0.5)
        mx = jnp.maximum(jnp.maximum(jnp.max(s_old[b], axis=-1, keepdims=True), s_new), sink)
        p_old = jnp.exp(s_old[b] - mx)
        pn = jnp.exp(s_new - mx)
        den = jnp.sum(p_old, axis=-1, keepdims=True) + pn + jnp.exp(sink - mx)
        probs.append(p_old / den)
        p_new.append(pn / den)
    for b in toks:
        o_scr[b * ATTN_HEADS:(b + 1) * ATTN_HEADS, :] = (_dot(probs[b], vbuf_ref[b])
                                                         + p_new[b] * vnew_ref[b:b + 1, :])
    for b in toks:
        kout_ref[b] = jnp.where(last, knew_ref[b:b + 1, :], pltpu.roll(kbuf_ref[b], WINDOW - 1, axis=0))
        vout_ref[b] = jnp.where(last, vnew_ref[b:b + 1, :], pltpu.roll(vbuf_ref[b], WINDOW - 1, axis=0))
    lo = lax.broadcasted_iota(jnp.int32, (1, LANES), 1) < HEAD_DIM
    for t in range(ATTN_WIDTH // LANES):
        even = o_scr[pl.ds(2 * t, ATTN_BB, stride=ATTN_HEADS), :]
        odd = o_scr[pl.ds(2 * t + 1, ATTN_BB, stride=ATTN_HEADS), :]
        if t < ATTN_WIDTH // LANES // KV_HEADS:
            tile = jnp.where(lo, even, pltpu.roll(odd, HEAD_DIM, axis=1))
        else:
            tile = jnp.where(lo, pltpu.roll(even, HEAD_DIM, axis=1), odd)
        o_ref[:, t * LANES:(t + 1) * LANES] = tile.astype(o_ref.dtype)


def _attn_sample(sinks, qexp, knew, vnew, kbuf, vbuf):
    n = knew.shape[0]
    tok2 = lambda i: (i, 0)
    tok3 = lambda i: (i, 0, 0)
    return pl.pallas_call(
        _attn_sample_kernel,
        grid=(n // ATTN_BB,),
        in_specs=[pl.BlockSpec(memory_space=pltpu.SMEM),
                  pl.BlockSpec((ATTN_HEADS, ATTN_BB, LANES), lambda i: (0, i, 0)),
                  pl.BlockSpec((ATTN_BB, KV_WIDTH), tok2),
                  pl.BlockSpec((ATTN_BB, KV_WIDTH), tok2),
                  pl.BlockSpec((ATTN_BB, WINDOW, KV_WIDTH), tok3),
                  pl.BlockSpec((ATTN_BB, WINDOW, KV_WIDTH), tok3)],
        out_specs=[pl.BlockSpec((ATTN_BB, ATTN_WIDTH), tok2),
                   pl.BlockSpec((ATTN_BB, WINDOW, KV_WIDTH), tok3),
                   pl.BlockSpec((ATTN_BB, WINDOW, KV_WIDTH), tok3)],
        out_shape=[jax.ShapeDtypeStruct((n, ATTN_WIDTH), BF16),
                   jax.ShapeDtypeStruct((n, WINDOW, KV_WIDTH), F32),
                   jax.ShapeDtypeStruct((n, WINDOW, KV_WIDTH), F32)],
        scratch_shapes=[pltpu.VMEM((ATTN_BB * ATTN_HEADS, LANES), F32),
                        pltpu.VMEM((ATTN_BB * ATTN_HEADS, LANES), F32)],
        compiler_params=pltpu.CompilerParams(dimension_semantics=("arbitrary",),
                                             vmem_limit_bytes=VMEM_LIMIT),
        name="attn_sample",
    )(sinks, qexp, knew, vnew, kbuf, vbuf)


GDN_BB = 16


def _gdn_sample_kernel(gate_ref, qt_ref, kt_ref, vg_ref, s_ref, o_ref, s_out_ref):
    i = pl.program_id(0)
    n = qt_ref.shape[-1]
    shift = (n - i * GDN_BB) % n
    units = [(j, h) for h in range(GDN_HEADS) for j in range(GDN_BB)]
    qt = [pltpu.roll(qt_ref[h], shift, axis=1) for h in range(GDN_HEADS)]
    kt = [pltpu.roll(kt_ref[h], shift, axis=1) for h in range(GDN_HEADS)]
    kb = [jnp.broadcast_to(kt[h][:, j:j + 1], (GDN_DK, GDN_DV)) for j, h in units]
    qb = [jnp.broadcast_to(qt[h][:, j:j + 1], (GDN_DK, GDN_DV)) for j, h in units]
    s = [s_ref[j, h] * gate_ref[(i * GDN_BB + j) * SUBLANES + h] for j, h in units]
    kv = [jnp.sum(su * k, axis=0, keepdims=True) for su, k in zip(s, kb)]
    u = [(vg_ref[j:j + 1, h * LANES:(h + 1) * LANES] - kvu)
         * gate_ref[(i * GDN_BB + j) * SUBLANES + GDN_HEADS + h] for (j, h), kvu in zip(units, kv)]
    s = [su + k * uu for su, k, uu in zip(s, kb, u)]
    for (j, h), su in zip(units, s):
        s_out_ref[j, h] = su
    for (j, h), su, q in zip(units, s, qb):
        o_ref[j:j + 1, h * LANES:(h + 1) * LANES] = jnp.sum(su * q, axis=0, keepdims=True)


def _gdn_sample(gate, qt, kt, vg, s0):
    n = vg.shape[0]
    return pl.pallas_call(
        _gdn_sample_kernel,
        grid=(n // GDN_BB,),
        in_specs=[pl.BlockSpec(memory_space=pltpu.SMEM),
                  pl.BlockSpec((GDN_HEADS, GDN_DK, n), lambda i: (0, 0, 0)),
                  pl.BlockSpec((GDN_HEADS, GDN_DK, n), lambda i: (0, 0, 0)),
                  pl.BlockSpec((GDN_BB, GDN_VAL_WIDTH), lambda i: (i, 0)),
                  pl.BlockSpec((GDN_BB, GDN_HEADS, GDN_DK, GDN_DV), lambda i: (i, 0, 0, 0))],
        out_specs=[pl.BlockSpec((GDN_BB, GDN_VAL_WIDTH), lambda i: (i, 0)),
                   pl.BlockSpec((GDN_BB, GDN_HEADS, GDN_DK, GDN_DV), lambda i: (i, 0, 0, 0))],
        out_shape=[jax.ShapeDtypeStruct((n, GDN_VAL_WIDTH), F32),
                   jax.ShapeDtypeStruct((n, GDN_HEADS, GDN_DK, GDN_DV), F32)],
        compiler_params=pltpu.CompilerParams(dimension_semantics=("arbitrary",),
                                             vmem_limit_bytes=VMEM_LIMIT),
        name="gdn_sample",
    )(gate, qt, kt, vg, s0)


def _pad_lanes(v):
    return jnp.pad(v.astype(F32), (0, LANES - v.shape[0])).reshape(1, LANES)


def kernel(x_prompt, x_sample, state_conv, cache_win_k, cache_win_v, state_gdn, norm_mix_pre, w_in,
           attn_sinks, conv_w, gdn_a_log, gdn_dt_bias, gdn_norm, w_out, norm_mix_post, norm_ffn_pre,
           w_up, w_down, norm_ffn_post):
    assert w_in.shape[0] == 1, "single-layer trunk"
    bsz, seq, _ = x_prompt.shape
    nsamp = x_sample.shape[0]

    w_in_p = jnp.pad(w_in[0].astype(BF16), ((0, 0), (0, IN_COLS - w_in.shape[2])))
    w_out_b = w_out[0].astype(BF16)
    w_up_b = w_up[0].astype(BF16)
    w_down_b = w_down[0].astype(BF16)
    n_pre = norm_mix_pre[0].reshape(1, D_MODEL)
    n_post = norm_mix_post[0].reshape(1, D_MODEL)
    n_fpre = norm_ffn_pre[0].reshape(1, D_MODEL)
    n_fpost = norm_ffn_post[0].reshape(1, D_MODEL)
    sinks = attn_sinks[0].reshape(1, ATTN_HEADS)
    cw = conv_w[0]
    alog = _pad_lanes(gdn_a_log[0])
    dtb = _pad_lanes(gdn_dt_bias[0])
    gnw = gdn_norm[0].reshape(1, GDN_DV)

    xp = x_prompt.reshape(bsz * seq, D_MODEL)
    pa, act, pz, gb, p_conv = _inproj(xp, n_pre, w_in_p, cw, alog, dtb, INPROJ_T, seq)
    yp, p_s = _mix_ffn(sinks, pa, act, gb, xp, pz, gnw, w_out_b, n_post, n_fpre, w_up_b, w_down_b,
                       n_fpost, seq)
    pa = pa.reshape(bsz, seq, ATTN_COLS)

    xs = x_sample.reshape(nsamp, D_MODEL)
    cs = state_conv[0].reshape(nsamp, (CONV_W - 1) * GDN_CONV_DIM)
    qexp, knew, vnew, qt, kt, vg, gate, zs, cs_new = _sample_prep(xs, n_pre, w_in_p, cs, cw, alog, dtb)
    kbuf = cache_win_k[0].reshape(nsamp, WINDOW, KV_WIDTH)
    vbuf = cache_win_v[0].reshape(nsamp, WINDOW, KV_WIDTH)
    ao_s, k_new_cache, v_new_cache = _attn_sample(sinks, qexp, knew, vnew, kbuf, vbuf)
    go_s, s_new = _gdn_sample(gate[:, :SUBLANES].reshape(nsamp * SUBLANES), qt, kt, vg, state_gdn[0])
    ys = _out_ffn(xs, ao_s, go_s, zs, gnw, w_out_b, n_post, n_fpre, w_up_b, w_down_b, n_fpost, nsamp)

    kv_shape = (1, -1, WINDOW, KV_HEADS, HEAD_DIM)
    return (yp.reshape(bsz, seq, D_MODEL),
            ys.reshape(nsamp, 1, D_MODEL),
            p_conv[:, SUBLANES - (CONV_W - 1):, :][None],
            pa[:, seq - WINDOW:, ATTN_WIDTH:ATTN_WIDTH + KV_WIDTH].reshape(kv_shape),
            pa[:, seq - WINDOW:, ATTN_WIDTH + KV_WIDTH:].reshape(kv_shape),
            p_s[None],
            cs_new.reshape(1, nsamp, CONV_W - 1, GDN_CONV_DIM),
            k_new_cache.reshape(kv_shape),
            v_new_cache.reshape(kv_shape),
            s_new[None])
```

```python
import functools

import jax
import jax.numpy as jnp
from jax import lax
from jax.experimental import pallas as pl
from jax.experimental.pallas import tpu as pltpu

F32 = jnp.float32
BF16 = jnp.bfloat16

D_MODEL = 1024
ATTN_HEADS = 8
KV_HEADS = 2
HEAD_DIM = 64
GQA_GROUP = ATTN_HEADS // KV_HEADS
ATTN_WIDTH = ATTN_HEADS * HEAD_DIM
KV_WIDTH = KV_HEADS * HEAD_DIM
WINDOW = 128
GDN_HEADS = 4
GDN_DK = 128
GDN_DV = 128
GDN_KEY_WIDTH = GDN_HEADS * GDN_DK
GDN_VAL_WIDTH = GDN_HEADS * GDN_DV
GDN_CONV_DIM = 2 * GDN_KEY_WIDTH + GDN_VAL_WIDTH
CONV_W = 4
D_FF = 4 * D_MODEL
EPS = 1e-6

LANES = 128
SUBLANES = 8
ATTN_COLS = ATTN_WIDTH + 2 * KV_WIDTH
AB_COLS = LANES
IN_COLS = ATTN_COLS + GDN_CONV_DIM + GDN_VAL_WIDTH + AB_COLS
OFF_GDN = ATTN_COLS
OFF_Z = OFF_GDN + GDN_CONV_DIM
OFF_AB = OFF_Z + GDN_VAL_WIDTH
CHUNK = 128
VMEM_LIMIT = 56 * 1024 * 1024
VMEM_LIMIT_MIX = 60 * 1024 * 1024


def _rms(x, w):
    return x * lax.rsqrt(jnp.mean(x * x, axis=-1, keepdims=True) + EPS) * w


def _sigmoid(x):
    return 1.0 / (1.0 + jnp.exp(-x))


def _softplus(x):
    return jnp.maximum(x, 0.0) + jnp.log(1.0 + jnp.exp(-jnp.abs(x)))


def _dot(a, b):
    return jnp.dot(a.astype(BF16), b.astype(BF16), preferred_element_type=F32)


def _dot_nt(a, b):
    return lax.dot_general(a.astype(BF16), b.astype(BF16), (((1,), (1,)), ((), ())),
                           preferred_element_type=F32)


def _alibi_slope(h):
    return 2.0 ** (-8.0 * (h + 1) / ATTN_HEADS)


def _causal_conv(x, carry, w):
    t, c = x.shape
    x3 = x.reshape(t // SUBLANES, SUBLANES, c)
    row8 = lax.broadcasted_iota(jnp.int32, (1, SUBLANES, 1), 1)
    y = None
    for shift in range(CONV_W - 1, -1, -1):
        if shift == 0:
            xs = x3
        else:
            xr = pltpu.roll(x3, shift, axis=1)
            cr = pltpu.roll(carry, shift, axis=0)
            prev = jnp.concatenate([cr[None], xr[:-1]], axis=0)
            xs = jnp.where(row8 < shift, prev, xr)
        term = xs * w[CONV_W - 1 - shift:CONV_W - shift][None]
        y = term if y is None else y + term
    return y.reshape(t, c)


def _silu(y):
    return y * _sigmoid(y)


def _l2n(x):
    return x * lax.rsqrt(jnp.sum(x * x, axis=-1, keepdims=True) + EPS)


def _gates(ab, alog, dtb):
    g = -jnp.exp(alog) * _softplus(ab + dtb)
    lane = lax.broadcasted_iota(jnp.int32, (1, LANES), 1)
    return jnp.where(lane < GDN_HEADS, g, _sigmoid(ab))


INPROJ_T = 1024
INPROJ_SUB = 128


def _inproj_kernel(x_ref, nw_ref, w_ref, cw_ref, alog_ref, dtb_ref,
                   oa_ref, og_ref, oz_ref, ogb_ref, conv_out_ref, carry_scr, *, tiles_per_seq):
    tm = x_ref.shape[0]

    @pl.when(lax.rem(pl.program_id(0), tiles_per_seq) == 0)
    def _():
        carry_scr[...] = jnp.zeros_like(carry_scr)

    carry = carry_scr[...]
    for r0 in range(0, tm, INPROJ_SUB):
        rows = slice(r0, r0 + INPROJ_SUB)
        h = _rms(x_ref[rows, :], nw_ref[...]).astype(BF16)
        pg = jnp.dot(h, w_ref[:, OFF_GDN:OFF_Z], preferred_element_type=F32)
        oa_ref[rows, :] = jnp.dot(h, w_ref[:, 0:OFF_GDN], preferred_element_type=F32)
        oz_ref[rows, :] = jnp.dot(h, w_ref[:, OFF_Z:OFF_AB], preferred_element_type=F32)
        ab = jnp.dot(h, w_ref[:, OFF_AB:IN_COLS], preferred_element_type=F32)
        ogb_ref[rows, :] = _gates(ab, alog_ref[...], dtb_ref[...])
        og_ref[rows, :] = _causal_conv(pg, carry, cw_ref[...])
        carry = pg[INPROJ_SUB - SUBLANES:INPROJ_SUB]
    carry_scr[...] = carry
    conv_out_ref[...] = carry


def _inproj(x, nw, w, cw, alog, dtb, tm, seq):
    m = x.shape[0]
    tiles_per_seq = seq // tm
    row = lambda i: (i, 0)
    fixed = lambda i: (0, 0)
    return pl.pallas_call(
        functools.partial(_inproj_kernel, tiles_per_seq=tiles_per_seq),
        grid=(m // tm,),
        in_specs=[pl.BlockSpec((tm, D_MODEL), row),
                  pl.BlockSpec((1, D_MODEL), fixed),
                  pl.BlockSpec((D_MODEL, IN_COLS), fixed),
                  pl.BlockSpec((CONV_W, GDN_CONV_DIM), fixed),
                  pl.BlockSpec((1, LANES), fixed),
                  pl.BlockSpec((1, LANES), fixed)],
        out_specs=[pl.BlockSpec((tm, ATTN_COLS), row),
                   pl.BlockSpec((tm, GDN_CONV_DIM), row),
                   pl.BlockSpec((tm, GDN_VAL_WIDTH), row),
                   pl.BlockSpec((tm, AB_COLS), row),
                   pl.BlockSpec((None, SUBLANES, GDN_CONV_DIM), lambda i: (i // tiles_per_seq, 0, 0))],
        out_shape=[jax.ShapeDtypeStruct((m, ATTN_COLS), F32),
                   jax.ShapeDtypeStruct((m, GDN_CONV_DIM), F32),
                   jax.ShapeDtypeStruct((m, GDN_VAL_WIDTH), F32),
                   jax.ShapeDtypeStruct((m, AB_COLS), F32),
                   jax.ShapeDtypeStruct((m // seq, SUBLANES, GDN_CONV_DIM), F32)],
        scratch_shapes=[pltpu.VMEM((SUBLANES, GDN_CONV_DIM), F32)],
        compiler_params=pltpu.CompilerParams(dimension_semantics=("arbitrary",),
                                             vmem_limit_bytes=VMEM_LIMIT),
        name="inproj",
    )(x, nw, w, cw, alog, dtb)


FF_UP_CHUNK = 512
FF_DOWN_CHUNK = 256
FF_STAGES = D_FF // FF_UP_CHUNK + D_MODEL // FF_DOWN_CHUNK


def _out_ffn_stages(x_ref, ao_ref, go_ref, z_ref, gnw_ref, wout_ref, npost_ref, npre_ref, wup_ref,
                    wdown_ref, nffn_ref, y_ref):
    gnw = gnw_ref[...]
    gy = []
    for hd in range(GDN_HEADS):
        lanes = slice(hd * LANES, (hd + 1) * LANES)
        z = z_ref[:, lanes]
        gy.append((_rms(go_ref[:, lanes], gnw) * (z * _sigmoid(z))).astype(BF16))
    m = (jnp.dot(ao_ref[...], wout_ref[0:ATTN_WIDTH, :], preferred_element_type=F32)
         + jnp.dot(jnp.concatenate(gy, axis=1), wout_ref[ATTN_WIDTH:, :], preferred_element_type=F32))
    x1 = x_ref[...] + _rms(m, npost_ref[...])
    h = _rms(x1, npre_ref[...]).astype(BF16)
    yield
    hidden = []
    for j in range(0, D_FF, FF_UP_CHUNK):
        u = jnp.maximum(jnp.dot(h, wup_ref[:, j:j + FF_UP_CHUNK], preferred_element_type=F32), 0.0)
        hidden.append((u * u).astype(BF16))
        yield
    hidden = jnp.concatenate(hidden, axis=1)
    f = []
    for j in range(0, D_MODEL, FF_DOWN_CHUNK):
        f.append(jnp.dot(hidden, wdown_ref[:, j:j + FF_DOWN_CHUNK], preferred_element_type=F32))
        yield
    y_ref[...] = x1 + _rms(jnp.concatenate(f, axis=1), nffn_ref[...])


def _dup_half(x, lo):
    xr = pltpu.roll(x, HEAD_DIM, axis=1)
    return jnp.where(lo, x, xr), jnp.where(lo, xr, x)


def _attn_prompt_stages(sink_ref, cur_ref, prev_ref, seq_row0, o_ref):
    tq = cur_ref.shape[0]
    kall = jnp.concatenate([prev_ref[:, 0:KV_WIDTH], cur_ref[:, ATTN_WIDTH:ATTN_WIDTH + KV_WIDTH]], axis=0)
    vall = jnp.concatenate([prev_ref[:, KV_WIDTH:], cur_ref[:, ATTN_WIDTH + KV_WIDTH:]], axis=0)
    lo = lax.broadcasted_iota(jnp.int32, (1, LANES), 1) < HEAD_DIM
    k2 = _dup_half(kall, lo)
    v2 = _dup_half(vall, lo)
    qi = lax.broadcasted_iota(jnp.int32, (WINDOW, 2 * WINDOW), 0)
    si = lax.broadcasted_iota(jnp.int32, (WINDOW, 2 * WINDOW), 1)
    dist = WINDOW + qi - si
    band = (dist >= 0) & (dist <= WINDOW)
    distf = dist.astype(F32)
    for r in range(tq // WINDOW):
        rows = slice(r * WINDOW, (r + 1) * WINDOW)
        valid = band
        if r == 0:
            valid = band & ((seq_row0 - WINDOW + si) >= 0)
        for j in range(KV_HEADS):
            kw = k2[j][r * WINDOW:(r + 2) * WINDOW]
            vw = v2[j][r * WINDOW:(r + 2) * WINDOW]
            t0 = cur_ref[rows, (2 * j) * LANES:(2 * j + 1) * LANES]
            t1 = cur_ref[rows, (2 * j + 1) * LANES:(2 * j + 2) * LANES]
            lhs = jnp.concatenate([jnp.where(lo, t0, 0.0), jnp.where(lo, 0.0, t0),
                                   jnp.where(lo, t1, 0.0), jnp.where(lo, 0.0, t1)], axis=0)
            s = _dot_nt(lhs, kw) * (HEAD_DIM ** -0.5)
            probs = []
            for g in range(GQA_GROUP):
                h = j * GQA_GROUP + g
                sg = s[g * WINDOW:(g + 1) * WINDOW] - _alibi_slope(h) * distf
                sg = jnp.where(valid, sg, -jnp.inf)
                sink = sink_ref[0, h]
                mx = jnp.maximum(jnp.max(sg, axis=-1, keepdims=True), sink)
                p = jnp.exp(sg - mx)
                den = jnp.sum(p, axis=-1, keepdims=True) + jnp.exp(sink - mx)
                probs.append(p / den)
            yield
            o = _dot(jnp.concatenate(probs, axis=0), vw)
            o01 = jnp.where(lo, o[0:WINDOW], o[WINDOW:2 * WINDOW])
            o23 = jnp.where(lo, o[2 * WINDOW:3 * WINDOW], o[3 * WINDOW:4 * WINDOW])
            o_ref[rows, (2 * j) * LANES:(2 * j + 1) * LANES] = o01.astype(o_ref.dtype)
            o_ref[rows, (2 * j + 1) * LANES:(2 * j + 2) * LANES] = o23.astype(o_ref.dtype)
            yield


def _gdn_prompt_stages(conv_ref, gb_ref, state, o_ref):
    t = conv_ref.shape[0]
    beta = gb_ref[...]
    g = beta
    rin = lax.broadcasted_iota(jnp.int32, (t, 1), 0) & (CHUNK - 1)
    step = 1
    while step < CHUNK:
        g = g + jnp.where(rin >= step, pltpu.roll(g, step, axis=0), 0.0)
        step *= 2
    gt = g.T

    row = lax.broadcasted_iota(jnp.int32, (CHUNK, CHUNK), 0)
    col = lax.broadcasted_iota(jnp.int32, (CHUNK, CHUNK), 1)
    lower_incl = row >= col
    lower_strict = row > col

    heads = range(GDN_HEADS)
    units = [(c, h) for c in range(t // CHUNK) for h in heads]
    pre = []
    for c, h in units:
        rows = slice(c * CHUNK, (c + 1) * CHUNK)
        qc = _l2n(_silu(conv_ref[rows, h * LANES:(h + 1) * LANES])) * (GDN_DK ** -0.5)
        kc = _l2n(_silu(conv_ref[rows, GDN_KEY_WIDTH + h * LANES:GDN_KEY_WIDTH + (h + 1) * LANES]))
        vc = _silu(conv_ref[rows, 2 * GDN_KEY_WIDTH + h * LANES:2 * GDN_KEY_WIDTH + (h + 1) * LANES])
        gc = g[rows, h:h + 1]
        bc = beta[rows, GDN_HEADS + h:GDN_HEADS + h + 1]
        gr = gt[h:h + 1, c * CHUNK:(c + 1) * CHUNK]
        decay = jnp.where(lower_incl, jnp.exp(jnp.where(lower_incl, gc - gr, 0.0)), 0.0)
        eg = jnp.exp(gc)
        glast = gc[CHUNK - 1:CHUNK]
        rhs = jnp.concatenate([vc * bc, kc * (bc * eg)], axis=1)
        k_dec_t = (kc * jnp.exp(glast - gc)).T
        kq = _dot_nt(jnp.concatenate([kc, qc], axis=0), kc)
        a = jnp.where(lower_strict, bc * kq[0:CHUNK] * decay, 0.0)
        qk = jnp.where(lower_incl, kq[CHUNK:] * decay, 0.0)
        pre.append(dict(a=a, rhs=rhs, lhs_u=jnp.concatenate([qk, k_dec_t], axis=0), q_dec=qc * eg,
                        ld=jnp.exp(glast)))
        if h == GDN_HEADS - 1:
            yield

    mats = [p["a"] for p in pre]
    first = ((row >> 1) == (col >> 1)) & ((row & 1) == 1) & ((col & 1) == 0)
    xo = [jnp.where(first, -a, 0.0) for a in mats]
    size = 2
    while size < CHUNK:
        shift = size.bit_length()
        m = ((row >> shift) == (col >> shift)) & ((row & size) != 0) & ((col & size) == 0)
        aoff = [jnp.where(m, a, 0.0) for a in mats]
        y = [ao + _dot(ao, x) for ao, x in zip(aoff, xo)]
        yield
        xo = [x - yy - _dot(x, yy) for x, yy in zip(xo, y)]
        size *= 2
        yield
    trs = [p["rhs"] + _dot(x, p["rhs"]) for p, x in zip(pre, xo)]
    yield

    for c in range(t // CHUNK):
        rows = slice(c * CHUNK, (c + 1) * CHUNK)
        idx = [c * GDN_HEADS + h for h in heads]
        ws = [_dot(jnp.concatenate([trs[u][:, GDN_DV:], pre[u]["q_dec"]], axis=0), state[h])
              for h, u in zip(heads, idx)]
        yield
        ou = [_dot(pre[u]["lhs_u"], trs[u][:, 0:GDN_DV] - ws[h][0:CHUNK]) for h, u in zip(heads, idx)]
        for h, u in zip(heads, idx):
            o_ref[rows, h * LANES:(h + 1) * LANES] = ws[h][CHUNK:] + ou[h][0:CHUNK]
            state[h] = state[h] * pre[u]["ld"] + ou[h][CHUNK:]
        yield


def _run_interleaved(primary, n_primary, secondary, n_secondary):
    done = object()
    issued = 0
    for k in range(n_primary):
        next(primary, done)
        target = ((k + 1) * n_secondary + n_primary - 1) // n_primary
        while issued < target:
            next(secondary, done)
            issued += 1
    for _ in primary:
        pass
    for _ in secondary:
        pass


def _roundrobin(*gens):
    done = object()
    live = list(gens)
    while live:
        for gen in list(live):
            if next(gen, done) is done:
                live.remove(gen)
            else:
                yield


MIX_T = 512


def _mix_ffn_kernel(sink_ref, pa_ref, prev_ref, conv_ref, gb_ref, x_ref, z_ref, gnw_ref, wout_ref,
                    npost_ref, npre_ref, wup_ref, wdown_ref, nffn_ref,
                    y_ref, s_out_ref, s_scr, ao_scr, go_scr, *, n_tiles, tiles_per_seq):
    i = pl.program_id(0)
    tile_in_seq = lax.rem(i, tiles_per_seq)
    mix_stages = (MIX_T // CHUNK) * 3 + 2 * (CHUNK.bit_length() - 2) + 1 + 2 * (MIX_T // WINDOW) * KV_HEADS

    def ffn_stages():
        return _out_ffn_stages(x_ref, ao_scr, go_scr, z_ref, gnw_ref, wout_ref, npost_ref, npre_ref,
                               wup_ref, wdown_ref, nffn_ref, y_ref)

    def mixers(ffn):
        state = [jnp.where(tile_in_seq == 0, 0.0, s_scr[h]) for h in range(GDN_HEADS)]
        front = _roundrobin(_gdn_prompt_stages(conv_ref, gb_ref, state, go_scr),
                            _attn_prompt_stages(sink_ref, pa_ref, prev_ref, tile_in_seq * MIX_T, ao_scr))
        _run_interleaved(ffn, FF_STAGES, front, mix_stages)
        for h in range(GDN_HEADS):
            s_scr[h] = state[h]
            s_out_ref[h] = state[h]

    @pl.when(i == 0)
    def _():
        s_scr[...] = jnp.zeros_like(s_scr)
        mixers(iter(()))

    @pl.when((i > 0) & (i < n_tiles))
    def _():
        ffn = ffn_stages()
        next(ffn)
        mixers(ffn)

    @pl.when(i == n_tiles)
    def _():
        for _ in ffn_stages():
            pass


def _mix_ffn(sinks, pa, act, gb, x, z, gnw, wout, npost, npre, wup, wdown, nffn, seq):
    m = x.shape[0]
    n_tiles = m // MIX_T
    tiles_per_seq = seq // MIX_T
    blocks_per_tile = MIX_T // WINDOW
    front = lambda i: (jnp.minimum(i, n_tiles - 1), 0)
    back = lambda i: (jnp.maximum(i - 1, 0), 0)
    fixed = lambda i: (0, 0)
    resident = dict(pipeline_mode=pl.Buffered(1))
    kern = functools.partial(_mix_ffn_kernel, n_tiles=n_tiles, tiles_per_seq=tiles_per_seq)
    return pl.pallas_call(
        kern,
        grid=(n_tiles + 1,),
        in_specs=[pl.BlockSpec(memory_space=pltpu.SMEM),
                  pl.BlockSpec((MIX_T, ATTN_COLS), front),
                  pl.BlockSpec((WINDOW, 2 * KV_WIDTH),
                               lambda i: (jnp.maximum(jnp.minimum(i, n_tiles - 1) * blocks_per_tile - 1, 0),
                                          ATTN_WIDTH // (2 * KV_WIDTH))),
                  pl.BlockSpec((MIX_T, GDN_CONV_DIM), front),
                  pl.BlockSpec((MIX_T, AB_COLS), front),
                  pl.BlockSpec((MIX_T, D_MODEL), back),
                  pl.BlockSpec((MIX_T, GDN_VAL_WIDTH), back),
                  pl.BlockSpec((1, GDN_DV), fixed),
                  pl.BlockSpec((D_MODEL, D_MODEL), fixed, **resident),
                  pl.BlockSpec((1, D_MODEL), fixed),
                  pl.BlockSpec((1, D_MODEL), fixed),
                  pl.BlockSpec((D_MODEL, D_FF), fixed, **resident),
                  pl.BlockSpec((D_FF, D_MODEL), fixed, **resident),
                  pl.BlockSpec((1, D_MODEL), fixed)],
        out_specs=[pl.BlockSpec((MIX_T, D_MODEL), back),
                   pl.BlockSpec((None, GDN_HEADS, GDN_DK, GDN_DV),
                                lambda i: (jnp.minimum(i, n_tiles - 1) // tiles_per_seq, 0, 0, 0))],
        out_shape=[jax.ShapeDtypeStruct((m, D_MODEL), F32),
                   jax.ShapeDtypeStruct((m // seq, GDN_HEADS, GDN_DK, GDN_DV), F32)],
        scratch_shapes=[pltpu.VMEM((GDN_HEADS, GDN_DK, GDN_DV), F32),
                        pltpu.VMEM((MIX_T, ATTN_WIDTH), BF16),
                        pltpu.VMEM((MIX_T, GDN_VAL_WIDTH), F32)],
        compiler_params=pltpu.CompilerParams(dimension_semantics=("arbitrary",),
                                             vmem_limit_bytes=VMEM_LIMIT_MIX),
        name="mix_ffn",
    )(sinks, pa, pa, act, gb, x, z, gnw, wout, npost, npre, wup, wdown, nffn)


def _out_ffn_kernel(*refs):
    for _ in _out_ffn_stages(*refs):
        pass


def _out_ffn(x, ao, go, z, gnw, wout, npost, npre, wup, wdown, nffn, tm):
    m = x.shape[0]
    row = lambda i: (i, 0)
    fixed = lambda i: (0, 0)
    return pl.pallas_call(
        _out_ffn_kernel,
        grid=(m // tm,),
        in_specs=[pl.BlockSpec((tm, D_MODEL), row),
                  pl.BlockSpec((tm, ATTN_WIDTH), row),
                  pl.BlockSpec((tm, GDN_VAL_WIDTH), row),
                  pl.BlockSpec((tm, GDN_VAL_WIDTH), row),
                  pl.BlockSpec((1, GDN_DV), fixed),
                  pl.BlockSpec((D_MODEL, D_MODEL), fixed, pipeline_mode=pl.Buffered(1)),
                  pl.BlockSpec((1, D_MODEL), fixed),
                  pl.BlockSpec((1, D_MODEL), fixed),
                  pl.BlockSpec((D_MODEL, D_FF), fixed, pipeline_mode=pl.Buffered(1)),
                  pl.BlockSpec((D_FF, D_MODEL), fixed, pipeline_mode=pl.Buffered(1)),
                  pl.BlockSpec((1, D_MODEL), fixed)],
        out_specs=pl.BlockSpec((tm, D_MODEL), row),
        out_shape=jax.ShapeDtypeStruct((m, D_MODEL), F32),
        compiler_params=pltpu.CompilerParams(dimension_semantics=("arbitrary",),
                                             vmem_limit_bytes=VMEM_LIMIT),
        name="out_ffn",
    )(x, ao, go, z, gnw, wout, npost, npre, wup, wdown, nffn)


def _sample_prep_kernel(x_ref, nw_ref, w_ref, cs_ref, cw_ref, alog_ref, dtb_ref,
                        qexp_ref, knew_ref, vnew_ref, qt_ref, kt_ref, vg_ref, gate_ref, z_ref,
                        cs_out_ref):
    h = _rms(x_ref[...], nw_ref[...]).astype(BF16)
    pa = jnp.dot(h, w_ref[:, 0:OFF_GDN], preferred_element_type=F32)
    pg = jnp.dot(h, w_ref[:, OFF_GDN:OFF_Z], preferred_element_type=F32)
    z_ref[...] = jnp.dot(h, w_ref[:, OFF_Z:OFF_AB], preferred_element_type=F32)
    ab = jnp.dot(h, w_ref[:, OFF_AB:IN_COLS], preferred_element_type=F32)

    lo = lax.broadcasted_iota(jnp.int32, (1, LANES), 1) < HEAD_DIM
    for t in range(ATTN_WIDTH // LANES):
        tile = pa[:, t * LANES:(t + 1) * LANES]
        rolled = pltpu.roll(tile, HEAD_DIM, axis=1)
        if t < ATTN_WIDTH // LANES // KV_HEADS:
            even, odd = jnp.where(lo, tile, 0.0), jnp.where(lo, rolled, 0.0)
        else:
            even, odd = jnp.where(lo, 0.0, rolled), jnp.where(lo, 0.0, tile)
        qexp_ref[2 * t] = even
        qexp_ref[2 * t + 1] = odd
    knew_ref[...] = pa[:, ATTN_WIDTH:ATTN_WIDTH + KV_WIDTH]
    vnew_ref[...] = pa[:, ATTN_WIDTH + KV_WIDTH:ATTN_COLS]

    cw = cw_ref[...]
    y = None
    for r in range(CONV_W - 1):
        term = cs_ref[:, r * GDN_CONV_DIM:(r + 1) * GDN_CONV_DIM] * cw[r:r + 1]
        y = term if y is None else y + term
    y = y + pg * cw[CONV_W - 1:CONV_W]
    act = y * _sigmoid(y)
    cs_out_ref[:, 0:(CONV_W - 2) * GDN_CONV_DIM] = cs_ref[:, GDN_CONV_DIM:(CONV_W - 1) * GDN_CONV_DIM]
    cs_out_ref[:, (CONV_W - 2) * GDN_CONV_DIM:] = pg

    for hd in range(GDN_HEADS):
        qn = _l2n(act[:, hd * LANES:(hd + 1) * LANES]) * (GDN_DK ** -0.5)
        kn = _l2n(act[:, GDN_KEY_WIDTH + hd * LANES:GDN_KEY_WIDTH + (hd + 1) * LANES])
        qt_ref[hd] = qn.T
        kt_ref[hd] = kn.T
    vg_ref[...] = act[:, 2 * GDN_KEY_WIDTH:]
    gb = _gates(ab, alog_ref[...], dtb_ref[...])
    lane = lax.broadcasted_iota(jnp.int32, (1, LANES), 1)
    gate_ref[...] = jnp.where(lane < GDN_HEADS, jnp.exp(gb), gb)


def _sample_prep(x, nw, w, cs, cw, alog, dtb):
    n = x.shape[0]
    return pl.pallas_call(
        _sample_prep_kernel,
        out_shape=[jax.ShapeDtypeStruct((ATTN_HEADS, n, LANES), F32),
                   jax.ShapeDtypeStruct((n, KV_WIDTH), F32),
                   jax.ShapeDtypeStruct((n, KV_WIDTH), F32),
                   jax.ShapeDtypeStruct((GDN_HEADS, GDN_DK, n), F32),
                   jax.ShapeDtypeStruct((GDN_HEADS, GDN_DK, n), F32),
                   jax.ShapeDtypeStruct((n, GDN_VAL_WIDTH), F32),
                   jax.ShapeDtypeStruct((n, LANES), F32),
                   jax.ShapeDtypeStruct((n, GDN_VAL_WIDTH), F32),
                   jax.ShapeDtypeStruct((n, (CONV_W - 1) * GDN_CONV_DIM), F32)],
        compiler_params=pltpu.CompilerParams(vmem_limit_bytes=VMEM_LIMIT),
        name="sample_prep",
    )(x, nw, w, cs, cw, alog, dtb)


ATTN_BB = 16


def _attn_sample_kernel(sink_ref, qexp_ref, knew_ref, vnew_ref, kbuf_ref, vbuf_ref,
                        o_ref, kout_ref, vout_ref, q_scr, o_scr):
    for h in range(ATTN_HEADS):
        q_scr[pl.ds(h, ATTN_BB, stride=ATTN_HEADS), :] = qexp_ref[h]
    hrow = lax.broadcasted_iota(jnp.int32, (ATTN_HEADS, 1), 0)
    sink = jnp.zeros((ATTN_HEADS, 1), F32)
    slope = jnp.zeros((ATTN_HEADS, 1), F32)
    for h in range(ATTN_HEADS):
        sink = jnp.where(hrow == h, sink_ref[0, h], sink)
        slope = jnp.where(hrow == h, _alibi_slope(h), slope)
    pos = lax.broadcasted_iota(jnp.int32, (1, WINDOW), 1)
    bias = slope * (WINDOW - pos).astype(F32)
    rowid = lax.broadcasted_iota(jnp.int32, (WINDOW, 1), 0)
    last = rowid == WINDOW - 1
    toks = range(ATTN_BB)
    qe = [q_scr[b * ATTN_HEADS:(b + 1) * ATTN_HEADS, :] for b in toks]
    s_old = [_dot_nt(qe[b], kbuf_ref[b]) * (HEAD_DIM ** -0.5) - bias for b in toks]
    probs, p_new = [], []
    for b in toks:
        s_new = jnp.sum(qe[b] * knew_ref[b:b + 1, :], axis=-1, keepdims=True) * (HEAD_DIM ** -0.5)
        mx = jnp.maximum(jnp.maximum(jnp.max(s_old[b], axis=-1, keepdims=True), s_new), sink)
        p_old = jnp.exp(s_old[b] - mx)
        pn = jnp.exp(s_new - mx)
        den = jnp.sum(p_old, axis=-1, keepdims=True) + pn + jnp.exp(sink - mx)
        probs.append(p_old / den)
        p_new.append(pn / den)
    for b in toks:
        o_scr[b * ATTN_HEADS:(b + 1) * ATTN_HEADS, :] = (_dot(probs[b], vbuf_ref[b])
                                                         + p_new[b] * vnew_ref[b:b + 1, :])
    for b in toks:
        kout_ref[b] = jnp.where(last, knew_ref[b:b + 1, :], pltpu.roll(kbuf_ref[b], WINDOW - 1, axis=0))
        vout_ref[b] = jnp.where(last, vnew_ref[b:b + 1, :], pltpu.roll(vbuf_ref[b], WINDOW - 1, axis=0))
    lo = lax.broadcasted_iota(jnp.int32, (1, LANES), 1) < HEAD_DIM
    for t in range(ATTN_WIDTH // LANES):
        even = o_scr[pl.ds(2 * t, ATTN_BB, stride=ATTN_HEADS), :]
        odd = o_scr[pl.ds(2 * t + 1, ATTN_BB, stride=ATTN_HEADS), :]
        if t < ATTN_WIDTH // LANES // KV_HEADS:
            tile = jnp.where(lo, even, pltpu.roll(odd, HEAD_DIM, axis=1))
        else:
            tile = jnp.where(lo, pltpu.roll(even, HEAD_DIM, axis=1), odd)
        o_ref[:, t * LANES:(t + 1) * LANES] = tile.astype(o_ref.dtype)


def _attn_sample(sinks, qexp, knew, vnew, kbuf, vbuf):
    n = knew.shape[0]
    tok2 = lambda i: (i, 0)
    tok3 = lambda i: (i, 0, 0)
    return pl.pallas_call(
        _attn_sample_kernel,
        grid=(n // ATTN_BB,),
        in_specs=[pl.BlockSpec(memory_space=pltpu.SMEM),
                  pl.BlockSpec((ATTN_HEADS, ATTN_BB, LANES), lambda i: (0, i, 0)),
                  pl.BlockSpec((ATTN_BB, KV_WIDTH), tok2),
                  pl.BlockSpec((ATTN_BB, KV_WIDTH), tok2),
                  pl.BlockSpec((ATTN_BB, WINDOW, KV_WIDTH), tok3),
                  pl.BlockSpec((ATTN_BB, WINDOW, KV_WIDTH), tok3)],
        out_specs=[pl.BlockSpec((ATTN_BB, ATTN_WIDTH), tok2),
                   pl.BlockSpec((ATTN_BB, WINDOW, KV_WIDTH), tok3),
                   pl.BlockSpec((ATTN_BB, WINDOW, KV_WIDTH), tok3)],
        out_shape=[jax.ShapeDtypeStruct((n, ATTN_WIDTH), BF16),
                   jax.ShapeDtypeStruct((n, WINDOW, KV_WIDTH), F32),
                   jax.ShapeDtypeStruct((n, WINDOW, KV_WIDTH), F32)],
        scratch_shapes=[pltpu.VMEM((ATTN_BB * ATTN_HEADS, LANES), F32),
                        pltpu.VMEM((ATTN_BB * ATTN_HEADS, LANES), F32)],
        compiler_params=pltpu.CompilerParams(dimension_semantics=("arbitrary",),
                                             vmem_limit_bytes=VMEM_LIMIT),
        name="attn_sample",
    )(sinks, qexp, knew, vnew, kbuf, vbuf)


GDN_BB = 16


def _gdn_sample_kernel(gate_ref, qt_ref, kt_ref, vg_ref, s_ref, o_ref, s_out_ref):
    i = pl.program_id(0)
    n = qt_ref.shape[-1]
    shift = (n - i * GDN_BB) % n
    units = [(j, h) for h in range(GDN_HEADS) for j in range(GDN_BB)]
    qt = [pltpu.roll(qt_ref[h], shift, axis=1) for h in range(GDN_HEADS)]
    kt = [pltpu.roll(kt_ref[h], shift, axis=1) for h in range(GDN_HEADS)]
    kb = [jnp.broadcast_to(kt[h][:, j:j + 1], (GDN_DK, GDN_DV)) for j, h in units]
    qb = [jnp.broadcast_to(qt[h][:, j:j + 1], (GDN_DK, GDN_DV)) for j, h in units]
    s = [s_ref[j, h] * gate_ref[(i * GDN_BB + j) * SUBLANES + h] for j, h in units]
    kv = [jnp.sum(su * k, axis=0, keepdims=True) for su, k in zip(s, kb)]
    u = [(vg_ref[j:j + 1, h * LANES:(h + 1) * LANES] - kvu)
         * gate_ref[(i * GDN_BB + j) * SUBLANES + GDN_HEADS + h] for (j, h), kvu in zip(units, kv)]
    s = [su + k * uu for su, k, uu in zip(s, kb, u)]
    for (j, h), su in zip(units, s):
        s_out_ref[j, h] = su
    for (j, h), su, q in zip(units, s, qb):
        o_ref[j:j + 1, h * LANES:(h + 1) * LANES] = jnp.sum(su * q, axis=0, keepdims=True)


def _gdn_sample(gate, qt, kt, vg, s0):
    n = vg.shape[0]
    return pl.pallas_call(
        _gdn_sample_kernel,
        grid=(n // GDN_BB,),
        in_specs=[pl.BlockSpec(memory_space=pltpu.SMEM),
                  pl.BlockSpec((GDN_HEADS, GDN_DK, n), lambda i: (0, 0, 0)),
                  pl.BlockSpec((GDN_HEADS, GDN_DK, n), lambda i: (0, 0, 0)),
                  pl.BlockSpec((GDN_BB, GDN_VAL_WIDTH), lambda i: (i, 0)),
                  pl.BlockSpec((GDN_BB, GDN_HEADS, GDN_DK, GDN_DV), lambda i: (i, 0, 0, 0))],
        out_specs=[pl.BlockSpec((GDN_BB, GDN_VAL_WIDTH), lambda i: (i, 0)),
                   pl.BlockSpec((GDN_BB, GDN_HEADS, GDN_DK, GDN_DV), lambda i: (i, 0, 0, 0))],
        out_shape=[jax.ShapeDtypeStruct((n, GDN_VAL_WIDTH), F32),
                   jax.ShapeDtypeStruct((n, GDN_HEADS, GDN_DK, GDN_DV), F32)],
        compiler_params=pltpu.CompilerParams(dimension_semantics=("arbitrary",),
                                             vmem_limit_bytes=VMEM_LIMIT),
        name="gdn_sample",
    )(gate, qt, kt, vg, s0)


def _pad_lanes(v):
    return jnp.pad(v.astype(F32), (0, LANES - v.shape[0])).reshape(1, LANES)


def kernel(x_prompt, x_sample, state_conv, cache_win_k, cache_win_v, state_gdn, norm_mix_pre, w_in,
           attn_sinks, conv_w, gdn_a_log, gdn_dt_bias, gdn_norm, w_out, norm_mix_post, norm_ffn_pre,
           w_up, w_down, norm_ffn_post):
    assert w_in.shape[0] == 1, "single-layer trunk"
    bsz, seq, _ = x_prompt.shape
    nsamp = x_sample.shape[0]

    w_in_p = jnp.pad(w_in[0], ((0, 0), (0, IN_COLS - w_in.shape[2]))).astype(BF16)
    w_out_b = w_out[0].astype(BF16)
    w_up_b = w_up[0].astype(BF16)
    w_down_b = w_down[0].astype(BF16)
    n_pre = norm_mix_pre[0].reshape(1, D_MODEL)
    n_post = norm_mix_post[0].reshape(1, D_MODEL)
    n_fpre = norm_ffn_pre[0].reshape(1, D_MODEL)
    n_fpost = norm_ffn_post[0].reshape(1, D_MODEL)
    sinks = attn_sinks[0].reshape(1, ATTN_HEADS)
    cw = conv_w[0]
    alog = _pad_lanes(gdn_a_log[0])
    dtb = _pad_lanes(gdn_dt_bias[0])
    gnw = gdn_norm[0].reshape(1, GDN_DV)

    xp = x_prompt.reshape(bsz * seq, D_MODEL)
    pa, act, pz, gb, p_conv = _inproj(xp, n_pre, w_in_p, cw, alog, dtb, INPROJ_T, seq)
    yp, p_s = _mix_ffn(sinks, pa, act, gb, xp, pz, gnw, w_out_b, n_post, n_fpre, w_up_b, w_down_b,
                       n_fpost, seq)
    pa = pa.reshape(bsz, seq, ATTN_COLS)

    xs = x_sample.reshape(nsamp, D_MODEL)
    cs = state_conv[0].reshape(nsamp, (CONV_W - 1) * GDN_CONV_DIM)
    qexp, knew, vnew, qt, kt, vg, gate, zs, cs_new = _sample_prep(xs, n_pre, w_in_p, cs, cw, alog, dtb)
    kbuf = cache_win_k[0].reshape(nsamp, WINDOW, KV_WIDTH)
    vbuf = cache_win_v[0].reshape(nsamp, WINDOW, KV_WIDTH)
    ao_s, k_new_cache, v_new_cache = _attn_sample(sinks, qexp, knew, vnew, kbuf, vbuf)
    go_s, s_new = _gdn_sample(gate[:, :SUBLANES].reshape(nsamp * SUBLANES), qt, kt, vg, state_gdn[0])
    ys = _out_ffn(xs, ao_s, go_s, zs, gnw, w_out_b, n_post, n_fpre, w_up_b, w_down_b, n_fpost, nsamp)

    kv_shape = (1, -1, WINDOW, KV_HEADS, HEAD_DIM)
    return (yp.reshape(bsz, seq, D_MODEL),
            ys.reshape(nsamp, 1, D_MODEL),
            p_conv[:, SUBLANES - (CONV_W - 1):, :][None],
            pa[:, seq - WINDOW:, ATTN_WIDTH:ATTN_WIDTH + KV_WIDTH].reshape(kv_shape),
            pa[:, seq - WINDOW:, ATTN_WIDTH + KV_WIDTH:].reshape(kv_shape),
            p_s[None],
            cs_new.reshape(1, nsamp, CONV_W - 1, GDN_CONV_DIM),
            k_new_cache.reshape(kv_shape),
            v_new_cache.reshape(kv_shape),
            s_new[None])
```

```python
import functools

import jax
import jax.numpy as jnp
from jax import lax
from jax.experimental import pallas as pl
from jax.experimental.pallas import tpu as pltpu

F32 = jnp.float32
BF16 = jnp.bfloat16

D_MODEL = 1024
ATTN_HEADS = 8
KV_HEADS = 2
HEAD_DIM = 64
GQA_GROUP = ATTN_HEADS // KV_HEADS
ATTN_WIDTH = ATTN_HEADS * HEAD_DIM
KV_WIDTH = KV_HEADS * HEAD_DIM
WINDOW = 128
GDN_HEADS = 4
GDN_DK = 128
GDN_DV = 128
GDN_KEY_WIDTH = GDN_HEADS * GDN_DK
GDN_VAL_WIDTH = GDN_HEADS * GDN_DV
GDN_CONV_DIM = 2 * GDN_KEY_WIDTH + GDN_VAL_WIDTH
CONV_W = 4
D_FF = 4 * D_MODEL
EPS = 1e-6

LANES = 128
SUBLANES = 8
ATTN_COLS = ATTN_WIDTH + 2 * KV_WIDTH
AB_COLS = LANES
IN_COLS = ATTN_COLS + GDN_CONV_DIM + GDN_VAL_WIDTH + AB_COLS
OFF_GDN = ATTN_COLS
OFF_Z = OFF_GDN + GDN_CONV_DIM
OFF_AB = OFF_Z + GDN_VAL_WIDTH
CHUNK = 128
VMEM_LIMIT = 56 * 1024 * 1024
VMEM_LIMIT_MIX = 60 * 1024 * 1024


def _rms(x, w):
    return x * lax.rsqrt(jnp.mean(x * x, axis=-1, keepdims=True) + EPS) * w


def _sigmoid(x):
    return 1.0 / (1.0 + jnp.exp(-x))


def _softplus(x):
    return jnp.maximum(x, 0.0) + jnp.log(1.0 + jnp.exp(-jnp.abs(x)))


def _dot(a, b):
    return jnp.dot(a.astype(BF16), b.astype(BF16), preferred_element_type=F32)


def _dot_nt(a, b):
    return lax.dot_general(a.astype(BF16), b.astype(BF16), (((1,), (1,)), ((), ())),
                           preferred_element_type=F32)


def _alibi_slope(h):
    return 2.0 ** (-8.0 * (h + 1) / ATTN_HEADS)


def _causal_conv(x, carry, w):
    t, c = x.shape
    x3 = x.reshape(t // SUBLANES, SUBLANES, c)
    row8 = lax.broadcasted_iota(jnp.int32, (1, SUBLANES, 1), 1)
    y = None
    for shift in range(CONV_W - 1, -1, -1):
        if shift == 0:
            xs = x3
        else:
            xr = pltpu.roll(x3, shift, axis=1)
            cr = pltpu.roll(carry, shift, axis=0)
            prev = jnp.concatenate([cr[None], xr[:-1]], axis=0)
            xs = jnp.where(row8 < shift, prev, xr)
        term = xs * w[CONV_W - 1 - shift:CONV_W - shift][None]
        y = term if y is None else y + term
    return y.reshape(t, c)


def _silu(y):
    return y * _sigmoid(y)


def _l2n(x):
    return x * lax.rsqrt(jnp.sum(x * x, axis=-1, keepdims=True) + EPS)


def _gates(ab, alog, dtb):
    g = -jnp.exp(alog) * _softplus(ab + dtb)
    lane = lax.broadcasted_iota(jnp.int32, (1, LANES), 1)
    return jnp.where(lane < GDN_HEADS, g, _sigmoid(ab))


INPROJ_T = 1024
INPROJ_SUB = 128


def _inproj_kernel(x_ref, nw_ref, w_ref, cw_ref, alog_ref, dtb_ref,
                   oa_ref, og_ref, oz_ref, ogb_ref, conv_out_ref, carry_scr, *, tiles_per_seq):
    tm = x_ref.shape[0]

    @pl.when(lax.rem(pl.program_id(0), tiles_per_seq) == 0)
    def _():
        carry_scr[...] = jnp.zeros_like(carry_scr)

    carry = carry_scr[...]
    for r0 in range(0, tm, INPROJ_SUB):
        rows = slice(r0, r0 + INPROJ_SUB)
        h = _rms(x_ref[rows, :], nw_ref[...]).astype(BF16)
        pg = jnp.dot(h, w_ref[:, OFF_GDN:OFF_Z], preferred_element_type=F32)
        oa_ref[rows, :] = jnp.dot(h, w_ref[:, 0:OFF_GDN], preferred_element_type=F32)
        oz_ref[rows, :] = jnp.dot(h, w_ref[:, OFF_Z:OFF_AB], preferred_element_type=F32)
        ab = jnp.dot(h, w_ref[:, OFF_AB:IN_COLS], preferred_element_type=F32)
        ogb_ref[rows, :] = _gates(ab, alog_ref[...], dtb_ref[...])
        og_ref[rows, :] = _causal_conv(pg, carry, cw_ref[...])
        carry = pg[INPROJ_SUB - SUBLANES:INPROJ_SUB]
    carry_scr[...] = carry
    conv_out_ref[...] = carry


def _inproj(x, nw, w, cw, alog, dtb, tm, seq):
    m = x.shape[0]
    tiles_per_seq = seq // tm
    row = lambda i: (i, 0)
    fixed = lambda i: (0, 0)
    return pl.pallas_call(
        functools.partial(_inproj_kernel, tiles_per_seq=tiles_per_seq),
        grid=(m // tm,),
        in_specs=[pl.BlockSpec((tm, D_MODEL), row),
                  pl.BlockSpec((1, D_MODEL), fixed),
                  pl.BlockSpec((D_MODEL, IN_COLS), fixed),
                  pl.BlockSpec((CONV_W, GDN_CONV_DIM), fixed),
                  pl.BlockSpec((1, LANES), fixed),
                  pl.BlockSpec((1, LANES), fixed)],
        out_specs=[pl.BlockSpec((tm, ATTN_COLS), row),
                   pl.BlockSpec((tm, GDN_CONV_DIM), row),
                   pl.BlockSpec((tm, GDN_VAL_WIDTH), row),
                   pl.BlockSpec((tm, AB_COLS), row),
                   pl.BlockSpec((None, SUBLANES, GDN_CONV_DIM), lambda i: (i // tiles_per_seq, 0, 0))],
        out_shape=[jax.ShapeDtypeStruct((m, ATTN_COLS), F32),
                   jax.ShapeDtypeStruct((m, GDN_CONV_DIM), F32),
                   jax.ShapeDtypeStruct((m, GDN_VAL_WIDTH), F32),
                   jax.ShapeDtypeStruct((m, AB_COLS), F32),
                   jax.ShapeDtypeStruct((m // seq, SUBLANES, GDN_CONV_DIM), F32)],
        scratch_shapes=[pltpu.VMEM((SUBLANES, GDN_CONV_DIM), F32)],
        compiler_params=pltpu.CompilerParams(dimension_semantics=("arbitrary",),
                                             vmem_limit_bytes=VMEM_LIMIT),
        name="inproj",
    )(x, nw, w, cw, alog, dtb)


FF_UP_CHUNK = 512
FF_DOWN_CHUNK = 256
FF_STAGES = D_FF // FF_UP_CHUNK + D_MODEL // FF_DOWN_CHUNK


def _out_ffn_stages(x_ref, ao_ref, go_ref, z_ref, gnw_ref, wout_ref, npost_ref, npre_ref, wup_ref,
                    wdown_ref, nffn_ref, y_ref):
    gnw = gnw_ref[...]
    gy = []
    for hd in range(GDN_HEADS):
        lanes = slice(hd * LANES, (hd + 1) * LANES)
        z = z_ref[:, lanes]
        gy.append((_rms(go_ref[:, lanes], gnw) * (z * _sigmoid(z))).astype(BF16))
    m = (jnp.dot(ao_ref[...], wout_ref[0:ATTN_WIDTH, :], preferred_element_type=F32)
         + jnp.dot(jnp.concatenate(gy, axis=1), wout_ref[ATTN_WIDTH:, :], preferred_element_type=F32))
    x1 = x_ref[...] + _rms(m, npost_ref[...])
    h = _rms(x1, npre_ref[...]).astype(BF16)
    yield
    hidden = []
    for j in range(0, D_FF, FF_UP_CHUNK):
        u = jnp.maximum(jnp.dot(h, wup_ref[:, j:j + FF_UP_CHUNK], preferred_element_type=F32), 0.0)
        hidden.append((u * u).astype(BF16))
        yield
    hidden = jnp.concatenate(hidden, axis=1)
    f = []
    for j in range(0, D_MODEL, FF_DOWN_CHUNK):
        f.append(jnp.dot(hidden, wdown_ref[:, j:j + FF_DOWN_CHUNK], preferred_element_type=F32))
        yield
    y_ref[...] = x1 + _rms(jnp.concatenate(f, axis=1), nffn_ref[...])


def _dup_half(x, lo):
    xr = pltpu.roll(x, HEAD_DIM, axis=1)
    return jnp.where(lo, x, xr), jnp.where(lo, xr, x)


def _attn_prompt_stages(sink_ref, cur_ref, prev_ref, seq_row0, o_ref):
    tq = cur_ref.shape[0]
    kall = jnp.concatenate([prev_ref[:, 0:KV_WIDTH], cur_ref[:, ATTN_WIDTH:ATTN_WIDTH + KV_WIDTH]], axis=0)
    vall = jnp.concatenate([prev_ref[:, KV_WIDTH:], cur_ref[:, ATTN_WIDTH + KV_WIDTH:]], axis=0)
    lo = lax.broadcasted_iota(jnp.int32, (1, LANES), 1) < HEAD_DIM
    k2 = _dup_half(kall, lo)
    v2 = _dup_half(vall, lo)
    qi = lax.broadcasted_iota(jnp.int32, (WINDOW, 2 * WINDOW), 0)
    si = lax.broadcasted_iota(jnp.int32, (WINDOW, 2 * WINDOW), 1)
    dist = WINDOW + qi - si
    band = (dist >= 0) & (dist <= WINDOW)
    distf = dist.astype(F32)
    for r in range(tq // WINDOW):
        rows = slice(r * WINDOW, (r + 1) * WINDOW)
        valid = band
        if r == 0:
            valid = band & ((seq_row0 - WINDOW + si) >= 0)
        for j in range(KV_HEADS):
            kw = k2[j][r * WINDOW:(r + 2) * WINDOW]
            vw = v2[j][r * WINDOW:(r + 2) * WINDOW]
            t0 = cur_ref[rows, (2 * j) * LANES:(2 * j + 1) * LANES]
            t1 = cur_ref[rows, (2 * j + 1) * LANES:(2 * j + 2) * LANES]
            lhs = jnp.concatenate([jnp.where(lo, t0, 0.0), jnp.where(lo, 0.0, t0),
                                   jnp.where(lo, t1, 0.0), jnp.where(lo, 0.0, t1)], axis=0)
            s = _dot_nt(lhs, kw) * (HEAD_DIM ** -0.5)
            probs = []
            for g in range(GQA_GROUP):
                h = j * GQA_GROUP + g
                sg = s[g * WINDOW:(g + 1) * WINDOW] - _alibi_slope(h) * distf
                sg = jnp.where(valid, sg, -jnp.inf)
                sink = sink_ref[0, h]
                mx = jnp.maximum(jnp.max(sg, axis=-1, keepdims=True), sink)
                p = jnp.exp(sg - mx)
                den = jnp.sum(p, axis=-1, keepdims=True) + jnp.exp(sink - mx)
                probs.append(p / den)
            yield
            o = _dot(jnp.concatenate(probs, axis=0), vw)
            o01 = jnp.where(lo, o[0:WINDOW], o[WINDOW:2 * WINDOW])
            o23 = jnp.where(lo, o[2 * WINDOW:3 * WINDOW], o[3 * WINDOW:4 * WINDOW])
            o_ref[rows, (2 * j) * LANES:(2 * j + 1) * LANES] = o01.astype(o_ref.dtype)
            o_ref[rows, (2 * j + 1) * LANES:(2 * j + 2) * LANES] = o23.astype(o_ref.dtype)
            yield


def _gdn_prompt_stages(conv_ref, gb_ref, state, o_ref):
    t = conv_ref.shape[0]
    beta = gb_ref[...]
    g = beta
    rin = lax.broadcasted_iota(jnp.int32, (t, 1), 0) & (CHUNK - 1)
    step = 1
    while step < CHUNK:
        g = g + jnp.where(rin >= step, pltpu.roll(g, step, axis=0), 0.0)
        step *= 2
    gt = g.T

    row = lax.broadcasted_iota(jnp.int32, (CHUNK, CHUNK), 0)
    col = lax.broadcasted_iota(jnp.int32, (CHUNK, CHUNK), 1)
    lower_incl = row >= col
    lower_strict = row > col

    heads = range(GDN_HEADS)
    units = [(c, h) for c in range(t // CHUNK) for h in heads]
    pre = []
    for c, h in units:
        rows = slice(c * CHUNK, (c + 1) * CHUNK)
        qc = _l2n(_silu(conv_ref[rows, h * LANES:(h + 1) * LANES])) * (GDN_DK ** -0.5)
        kc = _l2n(_silu(conv_ref[rows, GDN_KEY_WIDTH + h * LANES:GDN_KEY_WIDTH + (h + 1) * LANES]))
        vc = _silu(conv_ref[rows, 2 * GDN_KEY_WIDTH + h * LANES:2 * GDN_KEY_WIDTH + (h + 1) * LANES])
        gc = g[rows, h:h + 1]
        bc = beta[rows, GDN_HEADS + h:GDN_HEADS + h + 1]
        gr = gt[h:h + 1, c * CHUNK:(c + 1) * CHUNK]
        decay = jnp.where(lower_incl, jnp.exp(jnp.where(lower_incl, gc - gr, 0.0)), 0.0)
        eg = jnp.exp(gc)
        glast = gc[CHUNK - 1:CHUNK]
        rhs = jnp.concatenate([vc * bc, kc * (bc * eg)], axis=1)
        k_dec_t = (kc * jnp.exp(glast - gc)).T
        kq = _dot_nt(jnp.concatenate([kc, qc], axis=0), kc)
        a = jnp.where(lower_strict, bc * kq[0:CHUNK] * decay, 0.0)
        qk = jnp.where(lower_incl, kq[CHUNK:] * decay, 0.0)
        pre.append(dict(a=a, rhs=rhs, lhs_u=jnp.concatenate([qk, k_dec_t], axis=0), q_dec=qc * eg,
                        ld=jnp.exp(glast)))
        if h == GDN_HEADS - 1:
            yield

    mats = [p["a"] for p in pre]
    first = ((row >> 1) == (col >> 1)) & ((row & 1) == 1) & ((col & 1) == 0)
    xo = [jnp.where(first, -a, 0.0) for a in mats]
    size = 2
    while size < CHUNK:
        shift = size.bit_length()
        m = ((row >> shift) == (col >> shift)) & ((row & size) != 0) & ((col & size) == 0)
        aoff = [jnp.where(m, a, 0.0) for a in mats]
        y = [ao + _dot(ao, x) for ao, x in zip(aoff, xo)]
        yield
        xo = [x - yy - _dot(x, yy) for x, yy in zip(xo, y)]
        size *= 2
        yield
    trs = [p["rhs"] + _dot(x, p["rhs"]) for p, x in zip(pre, xo)]
    yield

    for c in range(t // CHUNK):
        rows = slice(c * CHUNK, (c + 1) * CHUNK)
        idx = [c * GDN_HEADS + h for h in heads]
        ws = [_dot(jnp.concatenate([trs[u][:, GDN_DV:], pre[u]["q_dec"]], axis=0), state[h])
              for h, u in zip(heads, idx)]
        yield
        ou = [_dot(pre[u]["lhs_u"], trs[u][:, 0:GDN_DV] - ws[h][0:CHUNK]) for h, u in zip(heads, idx)]
        for h, u in zip(heads, idx):
            o_ref[rows, h * LANES:(h + 1) * LANES] = ws[h][CHUNK:] + ou[h][0:CHUNK]
            state[h] = state[h] * pre[u]["ld"] + ou[h][CHUNK:]
        yield


def _run_interleaved(primary, n_primary, secondary, n_secondary):
    done = object()
    issued = 0
    for k in range(n_primary):
        next(primary, done)
        target = ((k + 1) * n_secondary + n_primary - 1) // n_primary
        while issued < target:
            next(secondary, done)
            issued += 1
    for _ in primary:
        pass
    for _ in secondary:
        pass


def _roundrobin(*gens):
    done = object()
    live = list(gens)
    while live:
        for gen in list(live):
            if next(gen, done) is done:
                live.remove(gen)
            else:
                yield


MIX_T = 512


def _mix_ffn_kernel(sink_ref, pa_ref, prev_ref, conv_ref, gb_ref, x_ref, z_ref, gnw_ref, wout_ref,
                    npost_ref, npre_ref, wup_ref, wdown_ref, nffn_ref,
                    y_ref, s_out_ref, s_scr, ao_scr, go_scr, *, n_tiles, tiles_per_seq):
    i = pl.program_id(0)

    @pl.when(i == 0)
    def _():
        s_scr[...] = jnp.zeros_like(s_scr)
        ao_scr[...] = jnp.zeros_like(ao_scr)
        go_scr[...] = jnp.zeros_like(go_scr)

    front_tile = jnp.minimum(i, n_tiles - 1)
    tile_in_seq = lax.rem(front_tile, tiles_per_seq)
    seq_start = tile_in_seq == 0
    front_valid = i < n_tiles

    ffn = _out_ffn_stages(x_ref, ao_scr, go_scr, z_ref, gnw_ref, wout_ref, npost_ref, npre_ref,
                          wup_ref, wdown_ref, nffn_ref, y_ref)
    next(ffn)

    old = [s_scr[h] for h in range(GDN_HEADS)]
    state = [jnp.where(seq_start, 0.0, s) for s in old]
    front = _roundrobin(_gdn_prompt_stages(conv_ref, gb_ref, state, go_scr),
                        _attn_prompt_stages(sink_ref, pa_ref, prev_ref, tile_in_seq * MIX_T, ao_scr))
    mix_stages = (MIX_T // CHUNK) * 3 + 2 * (CHUNK.bit_length() - 2) + 1 + 2 * (MIX_T // WINDOW) * KV_HEADS
    _run_interleaved(ffn, FF_STAGES, front, mix_stages)

    for h in range(GDN_HEADS):
        s_new = jnp.where(front_valid, state[h], old[h])
        s_scr[h] = s_new
        s_out_ref[h] = s_new


def _mix_ffn(sinks, pa, act, gb, x, z, gnw, wout, npost, npre, wup, wdown, nffn, seq):
    m = x.shape[0]
    n_tiles = m // MIX_T
    tiles_per_seq = seq // MIX_T
    blocks_per_tile = MIX_T // WINDOW
    front = lambda i: (jnp.minimum(i, n_tiles - 1), 0)
    back = lambda i: (jnp.maximum(i - 1, 0), 0)
    fixed = lambda i: (0, 0)
    resident = dict(pipeline_mode=pl.Buffered(1))
    kern = functools.partial(_mix_ffn_kernel, n_tiles=n_tiles, tiles_per_seq=tiles_per_seq)
    return pl.pallas_call(
        kern,
        grid=(n_tiles + 1,),
        in_specs=[pl.BlockSpec(memory_space=pltpu.SMEM),
                  pl.BlockSpec((MIX_T, ATTN_COLS), front),
                  pl.BlockSpec((WINDOW, 2 * KV_WIDTH),
                               lambda i: (jnp.maximum(jnp.minimum(i, n_tiles - 1) * blocks_per_tile - 1, 0),
                                          ATTN_WIDTH // (2 * KV_WIDTH))),
                  pl.BlockSpec((MIX_T, GDN_CONV_DIM), front),
                  pl.BlockSpec((MIX_T, AB_COLS), front),
                  pl.BlockSpec((MIX_T, D_MODEL), back),
                  pl.BlockSpec((MIX_T, GDN_VAL_WIDTH), back),
                  pl.BlockSpec((1, GDN_DV), fixed),
                  pl.BlockSpec((D_MODEL, D_MODEL), fixed, **resident),
                  pl.BlockSpec((1, D_MODEL), fixed),
                  pl.BlockSpec((1, D_MODEL), fixed),
                  pl.BlockSpec((D_MODEL, D_FF), fixed, **resident),
                  pl.BlockSpec((D_FF, D_MODEL), fixed, **resident),
                  pl.BlockSpec((1, D_MODEL), fixed)],
        out_specs=[pl.BlockSpec((MIX_T, D_MODEL), back),
                   pl.BlockSpec((None, GDN_HEADS, GDN_DK, GDN_DV),
                                lambda i: (jnp.minimum(i, n_tiles - 1) // tiles_per_seq, 0, 0, 0))],
        out_shape=[jax.ShapeDtypeStruct((m, D_MODEL), F32),
                   jax.ShapeDtypeStruct((m // seq, GDN_HEADS, GDN_DK, GDN_DV), F32)],
        scratch_shapes=[pltpu.VMEM((GDN_HEADS, GDN_DK, GDN_DV), F32),
                        pltpu.VMEM((MIX_T, ATTN_WIDTH), BF16),
                        pltpu.VMEM((MIX_T, GDN_VAL_WIDTH), F32)],
        compiler_params=pltpu.CompilerParams(dimension_semantics=("arbitrary",),
                                             vmem_limit_bytes=VMEM_LIMIT_MIX),
        name="mix_ffn",
    )(sinks, pa, pa, act, gb, x, z, gnw, wout, npost, npre, wup, wdown, nffn)


def _out_ffn_kernel(*refs):
    for _ in _out_ffn_stages(*refs):
        pass


def _out_ffn(x, ao, go, z, gnw, wout, npost, npre, wup, wdown, nffn, tm):
    m = x.shape[0]
    row = lambda i: (i, 0)
    fixed = lambda i: (0, 0)
    return pl.pallas_call(
        _out_ffn_kernel,
        grid=(m // tm,),
        in_specs=[pl.BlockSpec((tm, D_MODEL), row),
                  pl.BlockSpec((tm, ATTN_WIDTH), row),
                  pl.BlockSpec((tm, GDN_VAL_WIDTH), row),
                  pl.BlockSpec((tm, GDN_VAL_WIDTH), row),
                  pl.BlockSpec((1, GDN_DV), fixed),
                  pl.BlockSpec((D_MODEL, D_MODEL), fixed, pipeline_mode=pl.Buffered(1)),
                  pl.BlockSpec((1, D_MODEL), fixed),
                  pl.BlockSpec((1, D_MODEL), fixed),
                  pl.BlockSpec((D_MODEL, D_FF), fixed, pipeline_mode=pl.Buffered(1)),
                  pl.BlockSpec((D_FF, D_MODEL), fixed, pipeline_mode=pl.Buffered(1)),
                  pl.BlockSpec((1, D_MODEL), fixed)],
        out_specs=pl.BlockSpec((tm, D_MODEL), row),
        out_shape=jax.ShapeDtypeStruct((m, D_MODEL), F32),
        compiler_params=pltpu.CompilerParams(dimension_semantics=("arbitrary",),
                                             vmem_limit_bytes=VMEM_LIMIT),
        name="out_ffn",
    )(x, ao, go, z, gnw, wout, npost, npre, wup, wdown, nffn)


def _sample_prep_kernel(x_ref, nw_ref, w_ref, cs_ref, cw_ref, alog_ref, dtb_ref,
                        qexp_ref, knew_ref, vnew_ref, qt_ref, kt_ref, vg_ref, gate_ref, z_ref,
                        cs_out_ref):
    h = _rms(x_ref[...], nw_ref[...]).astype(BF16)
    pa = jnp.dot(h, w_ref[:, 0:OFF_GDN], preferred_element_type=F32)
    pg = jnp.dot(h, w_ref[:, OFF_GDN:OFF_Z], preferred_element_type=F32)
    z_ref[...] = jnp.dot(h, w_ref[:, OFF_Z:OFF_AB], preferred_element_type=F32)
    ab = jnp.dot(h, w_ref[:, OFF_AB:IN_COLS], preferred_element_type=F32)

    lo = lax.broadcasted_iota(jnp.int32, (1, LANES), 1) < HEAD_DIM
    for t in range(ATTN_WIDTH // LANES):
        tile = pa[:, t * LANES:(t + 1) * LANES]
        rolled = pltpu.roll(tile, HEAD_DIM, axis=1)
        if t < ATTN_WIDTH // LANES // KV_HEADS:
            even, odd = jnp.where(lo, tile, 0.0), jnp.where(lo, rolled, 0.0)
        else:
            even, odd = jnp.where(lo, 0.0, rolled), jnp.where(lo, 0.0, tile)
        qexp_ref[2 * t] = even
        qexp_ref[2 * t + 1] = odd
    knew_ref[...] = pa[:, ATTN_WIDTH:ATTN_WIDTH + KV_WIDTH]
    vnew_ref[...] = pa[:, ATTN_WIDTH + KV_WIDTH:ATTN_COLS]

    cw = cw_ref[...]
    y = None
    for r in range(CONV_W - 1):
        term = cs_ref[:, r * GDN_CONV_DIM:(r + 1) * GDN_CONV_DIM] * cw[r:r + 1]
        y = term if y is None else y + term
    y = y + pg * cw[CONV_W - 1:CONV_W]
    act = y * _sigmoid(y)
    cs_out_ref[:, 0:(CONV_W - 2) * GDN_CONV_DIM] = cs_ref[:, GDN_CONV_DIM:(CONV_W - 1) * GDN_CONV_DIM]
    cs_out_ref[:, (CONV_W - 2) * GDN_CONV_DIM:] = pg

    for hd in range(GDN_HEADS):
        qn = _l2n(act[:, hd * LANES:(hd + 1) * LANES]) * (GDN_DK ** -0.5)
        kn = _l2n(act[:, GDN_KEY_WIDTH + hd * LANES:GDN_KEY_WIDTH + (hd + 1) * LANES])
        qt_ref[hd] = qn.T
        kt_ref[hd] = kn.T
    vg_ref[...] = act[:, 2 * GDN_KEY_WIDTH:]
    gb = _gates(ab, alog_ref[...], dtb_ref[...])
    lane = lax.broadcasted_iota(jnp.int32, (1, LANES), 1)
    gate_ref[...] = jnp.where(lane < GDN_HEADS, jnp.exp(gb), gb)


def _sample_prep(x, nw, w, cs, cw, alog, dtb):
    n = x.shape[0]
    return pl.pallas_call(
        _sample_prep_kernel,
        out_shape=[jax.ShapeDtypeStruct((ATTN_HEADS, n, LANES), F32),
                   jax.ShapeDtypeStruct((n, KV_WIDTH), F32),
                   jax.ShapeDtypeStruct((n, KV_WIDTH), F32),
                   jax.ShapeDtypeStruct((GDN_HEADS, GDN_DK, n), F32),
                   jax.ShapeDtypeStruct((GDN_HEADS, GDN_DK, n), F32),
                   jax.ShapeDtypeStruct((n, GDN_VAL_WIDTH), F32),
                   jax.ShapeDtypeStruct((n, LANES), F32),
                   jax.ShapeDtypeStruct((n, GDN_VAL_WIDTH), F32),
                   jax.ShapeDtypeStruct((n, (CONV_W - 1) * GDN_CONV_DIM), F32)],
        compiler_params=pltpu.CompilerParams(vmem_limit_bytes=VMEM_LIMIT),
        name="sample_prep",
    )(x, nw, w, cs, cw, alog, dtb)


ATTN_BB = 16


def _attn_sample_kernel(sink_ref, qexp_ref, knew_ref, vnew_ref, kbuf_ref, vbuf_ref,
                        o_ref, kout_ref, vout_ref, q_scr, o_scr):
    for h in range(ATTN_HEADS):
        q_scr[pl.ds(h, ATTN_BB, stride=ATTN_HEADS), :] = qexp_ref[h]
    hrow = lax.broadcasted_iota(jnp.int32, (ATTN_HEADS, 1), 0)
    sink = jnp.zeros((ATTN_HEADS, 1), F32)
    slope = jnp.zeros((ATTN_HEADS, 1), F32)
    for h in range(ATTN_HEADS):
        sink = jnp.where(hrow == h, sink_ref[0, h], sink)
        slope = jnp.where(hrow == h, _alibi_slope(h), slope)
    pos = lax.broadcasted_iota(jnp.int32, (1, WINDOW), 1)
    bias = slope * (WINDOW - pos).astype(F32)
    rowid = lax.broadcasted_iota(jnp.int32, (WINDOW, 1), 0)
    last = rowid == WINDOW - 1
    toks = range(ATTN_BB)
    qe = [q_scr[b * ATTN_HEADS:(b + 1) * ATTN_HEADS, :] for b in toks]
    s_old = [_dot_nt(qe[b], kbuf_ref[b]) * (HEAD_DIM ** -0.5) - bias for b in toks]
    probs, p_new = [], []
    for b in toks:
        s_new = jnp.sum(qe[b] * knew_ref[b:b + 1, :], axis=-1, keepdims=True) * (HEAD_DIM ** -0.5)
        mx = jnp.maximum(jnp.maximum(jnp.max(s_old[b], axis=-1, keepdims=True), s_new), sink)
        p_old = jnp.exp(s_old[b] - mx)
        pn = jnp.exp(s_new - mx)
        den = jnp.sum(p_old, axis=-1, keepdims=True) + pn + jnp.exp(sink - mx)
        probs.append(p_old / den)
        p_new.append(pn / den)
    for b in toks:
        o_scr[b * ATTN_HEADS:(b + 1) * ATTN_HEADS, :] = (_dot(probs[b], vbuf_ref[b])
                                                         + p_new[b] * vnew_ref[b:b + 1, :])
    for b in toks:
        kout_ref[b] = jnp.where(last, knew_ref[b:b + 1, :], pltpu.roll(kbuf_ref[b], WINDOW - 1, axis=0))
        vout_ref[b] = jnp.where(last, vnew_ref[b:b + 1, :], pltpu.roll(vbuf_ref[b], WINDOW - 1, axis=0))
    lo = lax.broadcasted_iota(jnp.int32, (1, LANES), 1) < HEAD_DIM
    for t in range(ATTN_WIDTH // LANES):
        even = o_scr[pl.ds(2 * t, ATTN_BB, stride=ATTN_HEADS), :]
        odd = o_scr[pl.ds(2 * t + 1, ATTN_BB, stride=ATTN_HEADS), :]
        if t < ATTN_WIDTH // LANES // KV_HEADS:
            tile = jnp.where(lo, even, pltpu.roll(odd, HEAD_DIM, axis=1))
        else:
            tile = jnp.where(lo, pltpu.roll(even, HEAD_DIM, axis=1), odd)
        o_ref[:, t * LANES:(t + 1) * LANES] = tile.astype(o_ref.dtype)


def _attn_sample(sinks, qexp, knew, vnew, kbuf, vbuf):
    n = knew.shape[0]
    tok2 = lambda i: (i, 0)
    tok3 = lambda i: (i, 0, 0)
    return pl.pallas_call(
        _attn_sample_kernel,
        grid=(n // ATTN_BB,),
        in_specs=[pl.BlockSpec(memory_space=pltpu.SMEM),
                  pl.BlockSpec((ATTN_HEADS, ATTN_BB, LANES), lambda i: (0, i, 0)),
                  pl.BlockSpec((ATTN_BB, KV_WIDTH), tok2),
                  pl.BlockSpec((ATTN_BB, KV_WIDTH), tok2),
                  pl.BlockSpec((ATTN_BB, WINDOW, KV_WIDTH), tok3),
                  pl.BlockSpec((ATTN_BB, WINDOW, KV_WIDTH), tok3)],
        out_specs=[pl.BlockSpec((ATTN_BB, ATTN_WIDTH), tok2),
                   pl.BlockSpec((ATTN_BB, WINDOW, KV_WIDTH), tok3),
                   pl.BlockSpec((ATTN_BB, WINDOW, KV_WIDTH), tok3)],
        out_shape=[jax.ShapeDtypeStruct((n, ATTN_WIDTH), BF16),
                   jax.ShapeDtypeStruct((n, WINDOW, KV_WIDTH), F32),
                   jax.ShapeDtypeStruct((n, WINDOW, KV_WIDTH), F32)],
        scratch_shapes=[pltpu.VMEM((ATTN_BB * ATTN_HEADS, LANES), F32),
                        pltpu.VMEM((ATTN_BB * ATTN_HEADS, LANES), F32)],
        compiler_params=pltpu.CompilerParams(dimension_semantics=("arbitrary",),
                                             vmem_limit_bytes=VMEM_LIMIT),
        name="attn_sample",
    )(sinks, qexp, knew, vnew, kbuf, vbuf)


GDN_BB = 16


def _gdn_sample_kernel(gate_ref, qt_ref, kt_ref, vg_ref, s_ref, o_ref, s_out_ref):
    i = pl.program_id(0)
    n = qt_ref.shape[-1]
    shift = (n - i * GDN_BB) % n
    units = [(j, h) for h in range(GDN_HEADS) for j in range(GDN_BB)]
    qt = [pltpu.roll(qt_ref[h], shift, axis=1) for h in range(GDN_HEADS)]
    kt = [pltpu.roll(kt_ref[h], shift, axis=1) for h in range(GDN_HEADS)]
    kb = [jnp.broadcast_to(kt[h][:, j:j + 1], (GDN_DK, GDN_DV)) for j, h in units]
    qb = [jnp.broadcast_to(qt[h][:, j:j + 1], (GDN_DK, GDN_DV)) for j, h in units]
    s = [s_ref[j, h] * gate_ref[(i * GDN_BB + j) * SUBLANES + h] for j, h in units]
    kv = [jnp.sum(su * k, axis=0, keepdims=True) for su, k in zip(s, kb)]
    u = [(vg_ref[j:j + 1, h * LANES:(h + 1) * LANES] - kvu)
         * gate_ref[(i * GDN_BB + j) * SUBLANES + GDN_HEADS + h] for (j, h), kvu in zip(units, kv)]
    s = [su + k * uu for su, k, uu in zip(s, kb, u)]
    for (j, h), su in zip(units, s):
        s_out_ref[j, h] = su
    for (j, h), su, q in zip(units, s, qb):
        o_ref[j:j + 1, h * LANES:(h + 1) * LANES] = jnp.sum(su * q, axis=0, keepdims=True)


def _gdn_sample(gate, qt, kt, vg, s0):
    n = vg.shape[0]
    return pl.pallas_call(
        _gdn_sample_kernel,
        grid=(n // GDN_BB,),
        in_specs=[pl.BlockSpec(memory_space=pltpu.SMEM),
                  pl.BlockSpec((GDN_HEADS, GDN_DK, n), lambda i: (0, 0, 0)),
                  pl.BlockSpec((GDN_HEADS, GDN_DK, n), lambda i: (0, 0, 0)),
                  pl.BlockSpec((GDN_BB, GDN_VAL_WIDTH), lambda i: (i, 0)),
                  pl.BlockSpec((GDN_BB, GDN_HEADS, GDN_DK, GDN_DV), lambda i: (i, 0, 0, 0))],
        out_specs=[pl.BlockSpec((GDN_BB, GDN_VAL_WIDTH), lambda i: (i, 0)),
                   pl.BlockSpec((GDN_BB, GDN_HEADS, GDN_DK, GDN_DV), lambda i: (i, 0, 0, 0))],
        out_shape=[jax.ShapeDtypeStruct((n, GDN_VAL_WIDTH), F32),
                   jax.ShapeDtypeStruct((n, GDN_HEADS, GDN_DK, GDN_DV), F32)],
        compiler_params=pltpu.CompilerParams(dimension_semantics=("arbitrary",),
                                             vmem_limit_bytes=VMEM_LIMIT),
        name="gdn_sample",
    )(gate, qt, kt, vg, s0)


def _pad_lanes(v):
    return jnp.pad(v.astype(F32), (0, LANES - v.shape[0])).reshape(1, LANES)


def kernel(x_prompt, x_sample, state_conv, cache_win_k, cache_win_v, state_gdn, norm_mix_pre, w_in,
           attn_sinks, conv_w, gdn_a_log, gdn_dt_bias, gdn_norm, w_out, norm_mix_post, norm_ffn_pre,
           w_up, w_down, norm_ffn_post):
    assert w_in.shape[0] == 1, "single-layer trunk"
    bsz, seq, _ = x_prompt.shape
    nsamp = x_sample.shape[0]

    w_in_p = jnp.pad(w_in[0], ((0, 0), (0, IN_COLS - w_in.shape[2]))).astype(BF16)
    w_out_b = w_out[0].astype(BF16)
    w_up_b = w_up[0].astype(BF16)
    w_down_b = w_down[0].astype(BF16)
    n_pre = norm_mix_pre[0].reshape(1, D_MODEL)
    n_post = norm_mix_post[0].reshape(1, D_MODEL)
    n_fpre = norm_ffn_pre[0].reshape(1, D_MODEL)
    n_fpost = norm_ffn_post[0].reshape(1, D_MODEL)
    sinks = attn_sinks[0].reshape(1, ATTN_HEADS)
    cw = conv_w[0]
    alog = _pad_lanes(gdn_a_log[0])
    dtb = _pad_lanes(gdn_dt_bias[0])
    gnw = gdn_norm[0].reshape(1, GDN_DV)

    xp = x_prompt.reshape(bsz * seq, D_MODEL)
    pa, act, pz, gb, p_conv = _inproj(xp, n_pre, w_in_p, cw, alog, dtb, INPROJ_T, seq)
    yp, p_s = _mix_ffn(sinks, pa, act, gb, xp, pz, gnw, w_out_b, n_post, n_fpre, w_up_b, w_down_b,
                       n_fpost, seq)
    pa = pa.reshape(bsz, seq, ATTN_COLS)

    xs = x_sample.reshape(nsamp, D_MODEL)
    cs = state_conv[0].reshape(nsamp, (CONV_W - 1) * GDN_CONV_DIM)
    qexp, knew, vnew, qt, kt, vg, gate, zs, cs_new = _sample_prep(xs, n_pre, w_in_p, cs, cw, alog, dtb)
    kbuf = cache_win_k[0].reshape(nsamp, WINDOW, KV_WIDTH)
    vbuf = cache_win_v[0].reshape(nsamp, WINDOW, KV_WIDTH)
    ao_s, k_new_cache, v_new_cache = _attn_sample(sinks, qexp, knew, vnew, kbuf, vbuf)
    go_s, s_new = _gdn_sample(gate[:, :SUBLANES].reshape(nsamp * SUBLANES), qt, kt, vg, state_gdn[0])
    ys = _out_ffn(xs, ao_s, go_s, zs, gnw, w_out_b, n_post, n_fpre, w_up_b, w_down_b, n_fpost, nsamp)

    kv_shape = (1, -1, WINDOW, KV_HEADS, HEAD_DIM)
    return (yp.reshape(bsz, seq, D_MODEL),
            ys.reshape(nsamp, 1, D_MODEL),
            p_conv[:, SUBLANES - (CONV_W - 1):, :][None],
            pa[:, seq - WINDOW:, ATTN_WIDTH:ATTN_WIDTH + KV_WIDTH].reshape(kv_shape),
            pa[:, seq - WINDOW:, ATTN_WIDTH + KV_WIDTH:].reshape(kv_shape),
            p_s[None],
            cs_new.reshape(1, nsamp, CONV_W - 1, GDN_CONV_DIM),
            k_new_cache.reshape(kv_shape),
            v_new_cache.reshape(kv_shape),
            s_new[None])
```

```python
import functools

import jax
import jax.numpy as jnp
from jax import lax
from jax.experimental import pallas as pl
from jax.experimental.pallas import tpu as pltpu

F32 = jnp.float32
BF16 = jnp.bfloat16

D_MODEL = 1024
ATTN_HEADS = 8
KV_HEADS = 2
HEAD_DIM = 64
GQA_GROUP = ATTN_HEADS // KV_HEADS
ATTN_WIDTH = ATTN_HEADS * HEAD_DIM
KV_WIDTH = KV_HEADS * HEAD_DIM
WINDOW = 128
GDN_HEADS = 4
GDN_DK = 128
GDN_DV = 128
GDN_KEY_WIDTH = GDN_HEADS * GDN_DK
GDN_VAL_WIDTH = GDN_HEADS * GDN_DV
GDN_CONV_DIM = 2 * GDN_KEY_WIDTH + GDN_VAL_WIDTH
CONV_W = 4
D_FF = 4 * D_MODEL
EPS = 1e-6

LANES = 128
SUBLANES = 8
ATTN_COLS = ATTN_WIDTH + 2 * KV_WIDTH
AB_COLS = LANES
IN_COLS = ATTN_COLS + GDN_CONV_DIM + GDN_VAL_WIDTH + AB_COLS
OFF_GDN = ATTN_COLS
OFF_Z = OFF_GDN + GDN_CONV_DIM
OFF_AB = OFF_Z + GDN_VAL_WIDTH
CHUNK = 128
VMEM_LIMIT = 56 * 1024 * 1024
VMEM_LIMIT_MIX = 60 * 1024 * 1024


def _rms(x, w):
    return x * lax.rsqrt(jnp.mean(x * x, axis=-1, keepdims=True) + EPS) * w


def _sigmoid(x):
    return 1.0 / (1.0 + jnp.exp(-x))


def _softplus(x):
    return jnp.maximum(x, 0.0) + jnp.log(1.0 + jnp.exp(-jnp.abs(x)))


def _dot(a, b):
    return jnp.dot(a.astype(BF16), b.astype(BF16), preferred_element_type=F32)


def _dot_nt(a, b):
    return lax.dot_general(a.astype(BF16), b.astype(BF16), (((1,), (1,)), ((), ())),
                           preferred_element_type=F32)


def _alibi_slope(h):
    return 2.0 ** (-8.0 * (h + 1) / ATTN_HEADS)


def _causal_conv(x, carry, w):
    t, c = x.shape
    x3 = x.reshape(t // SUBLANES, SUBLANES, c)
    row8 = lax.broadcasted_iota(jnp.int32, (1, SUBLANES, 1), 1)
    y = None
    for shift in range(CONV_W - 1, -1, -1):
        if shift == 0:
            xs = x3
        else:
            xr = pltpu.roll(x3, shift, axis=1)
            cr = pltpu.roll(carry, shift, axis=0)
            prev = jnp.concatenate([cr[None], xr[:-1]], axis=0)
            xs = jnp.where(row8 < shift, prev, xr)
        term = xs * w[CONV_W - 1 - shift:CONV_W - shift][None]
        y = term if y is None else y + term
    return y.reshape(t, c)


def _silu(y):
    return y * _sigmoid(y)


def _l2n(x):
    return x * lax.rsqrt(jnp.sum(x * x, axis=-1, keepdims=True) + EPS)


def _gates(ab, alog, dtb):
    g = -jnp.exp(alog) * _softplus(ab + dtb)
    lane = lax.broadcasted_iota(jnp.int32, (1, LANES), 1)
    return jnp.where(lane < GDN_HEADS, g, _sigmoid(ab))


INPROJ_T = 1024
INPROJ_SUB = 128


def _inproj_kernel(x_ref, nw_ref, w_ref, cw_ref, alog_ref, dtb_ref,
                   oa_ref, og_ref, oz_ref, ogb_ref, conv_out_ref, carry_scr, *, tiles_per_seq):
    tm = x_ref.shape[0]

    @pl.when(lax.rem(pl.program_id(0), tiles_per_seq) == 0)
    def _():
        carry_scr[...] = jnp.zeros_like(carry_scr)

    carry = carry_scr[...]
    for r0 in range(0, tm, INPROJ_SUB):
        rows = slice(r0, r0 + INPROJ_SUB)
        h = _rms(x_ref[rows, :], nw_ref[...]).astype(BF16)
        pg = jnp.dot(h, w_ref[:, OFF_GDN:OFF_Z], preferred_element_type=F32)
        oa_ref[rows, :] = jnp.dot(h, w_ref[:, 0:OFF_GDN], preferred_element_type=F32)
        oz_ref[rows, :] = jnp.dot(h, w_ref[:, OFF_Z:OFF_AB], preferred_element_type=F32)
        ab = jnp.dot(h, w_ref[:, OFF_AB:IN_COLS], preferred_element_type=F32)
        ogb_ref[rows, :] = _gates(ab, alog_ref[...], dtb_ref[...])
        og_ref[rows, :] = _causal_conv(pg, carry, cw_ref[...])
        carry = pg[INPROJ_SUB - SUBLANES:INPROJ_SUB]
    carry_scr[...] = carry
    conv_out_ref[...] = carry


def _inproj(x, nw, w, cw, alog, dtb, tm, seq):
    m = x.shape[0]
    tiles_per_seq = seq // tm
    row = lambda i: (i, 0)
    fixed = lambda i: (0, 0)
    return pl.pallas_call(
        functools.partial(_inproj_kernel, tiles_per_seq=tiles_per_seq),
        grid=(m // tm,),
        in_specs=[pl.BlockSpec((tm, D_MODEL), row),
                  pl.BlockSpec((1, D_MODEL), fixed),
                  pl.BlockSpec((D_MODEL, IN_COLS), fixed),
                  pl.BlockSpec((CONV_W, GDN_CONV_DIM), fixed),
                  pl.BlockSpec((1, LANES), fixed),
                  pl.BlockSpec((1, LANES), fixed)],
        out_specs=[pl.BlockSpec((tm, ATTN_COLS), row),
                   pl.BlockSpec((tm, GDN_CONV_DIM), row),
                   pl.BlockSpec((tm, GDN_VAL_WIDTH), row),
                   pl.BlockSpec((tm, AB_COLS), row),
                   pl.BlockSpec((None, SUBLANES, GDN_CONV_DIM), lambda i: (i // tiles_per_seq, 0, 0))],
        out_shape=[jax.ShapeDtypeStruct((m, ATTN_COLS), F32),
                   jax.ShapeDtypeStruct((m, GDN_CONV_DIM), F32),
                   jax.ShapeDtypeStruct((m, GDN_VAL_WIDTH), F32),
                   jax.ShapeDtypeStruct((m, AB_COLS), F32),
                   jax.ShapeDtypeStruct((m // seq, SUBLANES, GDN_CONV_DIM), F32)],
        scratch_shapes=[pltpu.VMEM((SUBLANES, GDN_CONV_DIM), F32)],
        compiler_params=pltpu.CompilerParams(dimension_semantics=("arbitrary",),
                                             vmem_limit_bytes=VMEM_LIMIT),
        name="inproj",
    )(x, nw, w, cw, alog, dtb)


FF_UP_CHUNK = 256
FF_DOWN_CHUNK = 256
FF_STAGES = D_FF // FF_UP_CHUNK + D_MODEL // FF_DOWN_CHUNK


def _out_ffn_stages(x_ref, ao_ref, go_ref, z_ref, gnw_ref, wout_ref, npost_ref, npre_ref, wup_ref,
                    wdown_ref, nffn_ref, y_ref):
    gnw = gnw_ref[...]
    gy = []
    for hd in range(GDN_HEADS):
        lanes = slice(hd * LANES, (hd + 1) * LANES)
        z = z_ref[:, lanes]
        gy.append((_rms(go_ref[:, lanes], gnw) * (z * _sigmoid(z))).astype(BF16))
    m = (jnp.dot(ao_ref[...], wout_ref[0:ATTN_WIDTH, :], preferred_element_type=F32)
         + jnp.dot(jnp.concatenate(gy, axis=1), wout_ref[ATTN_WIDTH:, :], preferred_element_type=F32))
    x1 = x_ref[...] + _rms(m, npost_ref[...])
    h = _rms(x1, npre_ref[...]).astype(BF16)
    yield
    hidden = []
    for j in range(0, D_FF, FF_UP_CHUNK):
        u = jnp.maximum(jnp.dot(h, wup_ref[:, j:j + FF_UP_CHUNK], preferred_element_type=F32), 0.0)
        hidden.append((u * u).astype(BF16))
        yield
    hidden = jnp.concatenate(hidden, axis=1)
    f = []
    for j in range(0, D_MODEL, FF_DOWN_CHUNK):
        f.append(jnp.dot(hidden, wdown_ref[:, j:j + FF_DOWN_CHUNK], preferred_element_type=F32))
        yield
    y_ref[...] = x1 + _rms(jnp.concatenate(f, axis=1), nffn_ref[...])


def _dup_half(x, lo):
    xr = pltpu.roll(x, HEAD_DIM, axis=1)
    return jnp.where(lo, x, xr), jnp.where(lo, xr, x)


def _attn_prompt_stages(sink_ref, cur_ref, prev_ref, seq_row0, o_ref):
    tq = cur_ref.shape[0]
    kall = jnp.concatenate([prev_ref[:, 0:KV_WIDTH], cur_ref[:, ATTN_WIDTH:ATTN_WIDTH + KV_WIDTH]], axis=0)
    vall = jnp.concatenate([prev_ref[:, KV_WIDTH:], cur_ref[:, ATTN_WIDTH + KV_WIDTH:]], axis=0)
    lo = lax.broadcasted_iota(jnp.int32, (1, LANES), 1) < HEAD_DIM
    k2 = _dup_half(kall, lo)
    v2 = _dup_half(vall, lo)
    qi = lax.broadcasted_iota(jnp.int32, (WINDOW, 2 * WINDOW), 0)
    si = lax.broadcasted_iota(jnp.int32, (WINDOW, 2 * WINDOW), 1)
    dist = WINDOW + qi - si
    band = (dist >= 0) & (dist <= WINDOW)
    distf = dist.astype(F32)
    for r in range(tq // WINDOW):
        rows = slice(r * WINDOW, (r + 1) * WINDOW)
        valid = band
        if r == 0:
            valid = band & ((seq_row0 - WINDOW + si) >= 0)
        for j in range(KV_HEADS):
            kw = k2[j][r * WINDOW:(r + 2) * WINDOW]
            vw = v2[j][r * WINDOW:(r + 2) * WINDOW]
            t0 = cur_ref[rows, (2 * j) * LANES:(2 * j + 1) * LANES]
            t1 = cur_ref[rows, (2 * j + 1) * LANES:(2 * j + 2) * LANES]
            lhs = jnp.concatenate([jnp.where(lo, t0, 0.0), jnp.where(lo, 0.0, t0),
                                   jnp.where(lo, t1, 0.0), jnp.where(lo, 0.0, t1)], axis=0)
            s = _dot_nt(lhs, kw) * (HEAD_DIM ** -0.5)
            probs = []
            for g in range(GQA_GROUP):
                h = j * GQA_GROUP + g
                sg = s[g * WINDOW:(g + 1) * WINDOW] - _alibi_slope(h) * distf
                sg = jnp.where(valid, sg, -jnp.inf)
                sink = sink_ref[0, h]
                mx = jnp.maximum(jnp.max(sg, axis=-1, keepdims=True), sink)
                p = jnp.exp(sg - mx)
                den = jnp.sum(p, axis=-1, keepdims=True) + jnp.exp(sink - mx)
                probs.append(p / den)
            yield
            o = _dot(jnp.concatenate(probs, axis=0), vw)
            o01 = jnp.where(lo, o[0:WINDOW], o[WINDOW:2 * WINDOW])
            o23 = jnp.where(lo, o[2 * WINDOW:3 * WINDOW], o[3 * WINDOW:4 * WINDOW])
            o_ref[rows, (2 * j) * LANES:(2 * j + 1) * LANES] = o01.astype(o_ref.dtype)
            o_ref[rows, (2 * j + 1) * LANES:(2 * j + 2) * LANES] = o23.astype(o_ref.dtype)
            yield


def _gdn_prompt_stages(conv_ref, gb_ref, state, o_ref):
    t = conv_ref.shape[0]
    beta = gb_ref[...]
    g = beta
    rin = lax.broadcasted_iota(jnp.int32, (t, 1), 0) & (CHUNK - 1)
    step = 1
    while step < CHUNK:
        g = g + jnp.where(rin >= step, pltpu.roll(g, step, axis=0), 0.0)
        step *= 2
    gt = g.T

    row = lax.broadcasted_iota(jnp.int32, (CHUNK, CHUNK), 0)
    col = lax.broadcasted_iota(jnp.int32, (CHUNK, CHUNK), 1)
    lower_incl = row >= col
    lower_strict = row > col

    heads = range(GDN_HEADS)
    units = [(c, h) for c in range(t // CHUNK) for h in heads]
    pre = []
    for c, h in units:
        rows = slice(c * CHUNK, (c + 1) * CHUNK)
        qc = _l2n(_silu(conv_ref[rows, h * LANES:(h + 1) * LANES])) * (GDN_DK ** -0.5)
        kc = _l2n(_silu(conv_ref[rows, GDN_KEY_WIDTH + h * LANES:GDN_KEY_WIDTH + (h + 1) * LANES]))
        vc = _silu(conv_ref[rows, 2 * GDN_KEY_WIDTH + h * LANES:2 * GDN_KEY_WIDTH + (h + 1) * LANES])
        gc = g[rows, h:h + 1]
        bc = beta[rows, GDN_HEADS + h:GDN_HEADS + h + 1]
        gr = gt[h:h + 1, c * CHUNK:(c + 1) * CHUNK]
        decay = jnp.where(lower_incl, jnp.exp(jnp.where(lower_incl, gc - gr, 0.0)), 0.0)
        eg = jnp.exp(gc)
        glast = gc[CHUNK - 1:CHUNK]
        rhs = jnp.concatenate([vc * bc, kc * (bc * eg)], axis=1)
        k_dec_t = (kc * jnp.exp(glast - gc)).T
        kq = _dot_nt(jnp.concatenate([kc, qc], axis=0), kc)
        a = jnp.where(lower_strict, bc * kq[0:CHUNK] * decay, 0.0)
        qk = jnp.where(lower_incl, kq[CHUNK:] * decay, 0.0)
        pre.append(dict(a=a, rhs=rhs, lhs_u=jnp.concatenate([qk, k_dec_t], axis=0), q_dec=qc * eg,
                        ld=jnp.exp(glast)))
        if h == GDN_HEADS - 1:
            yield

    mats = [p["a"] for p in pre]
    first = ((row >> 1) == (col >> 1)) & ((row & 1) == 1) & ((col & 1) == 0)
    xo = [jnp.where(first, -a, 0.0) for a in mats]
    size = 2
    while size < CHUNK:
        shift = size.bit_length()
        m = ((row >> shift) == (col >> shift)) & ((row & size) != 0) & ((col & size) == 0)
        aoff = [jnp.where(m, a, 0.0) for a in mats]
        y = [ao + _dot(ao, x) for ao, x in zip(aoff, xo)]
        yield
        xo = [x - yy - _dot(x, yy) for x, yy in zip(xo, y)]
        size *= 2
        yield
    trs = [p["rhs"] + _dot(x, p["rhs"]) for p, x in zip(pre, xo)]
    yield

    for c in range(t // CHUNK):
        rows = slice(c * CHUNK, (c + 1) * CHUNK)
        idx = [c * GDN_HEADS + h for h in heads]
        ws = [_dot(jnp.concatenate([trs[u][:, GDN_DV:], pre[u]["q_dec"]], axis=0), state[h])
              for h, u in zip(heads, idx)]
        yield
        ou = [_dot(pre[u]["lhs_u"], trs[u][:, 0:GDN_DV] - ws[h][0:CHUNK]) for h, u in zip(heads, idx)]
        for h, u in zip(heads, idx):
            o_ref[rows, h * LANES:(h + 1) * LANES] = ws[h][CHUNK:] + ou[h][0:CHUNK]
            state[h] = state[h] * pre[u]["ld"] + ou[h][CHUNK:]
        yield


def _run_interleaved(primary, n_primary, secondary, n_secondary):
    done = object()
    issued = 0
    for k in range(n_primary):
        next(primary, done)
        target = ((k + 1) * n_secondary + n_primary - 1) // n_primary
        while issued < target:
            next(secondary, done)
            issued += 1
    for _ in primary:
        pass
    for _ in secondary:
        pass


def _roundrobin(*gens):
    done = object()
    live = list(gens)
    while live:
        for gen in list(live):
            if next(gen, done) is done:
                live.remove(gen)
            else:
                yield


MIX_T = 512


def _mix_ffn_kernel(sink_ref, pa_ref, prev_ref, conv_ref, gb_ref, x_ref, z_ref, gnw_ref, wout_ref,
                    npost_ref, npre_ref, wup_ref, wdown_ref, nffn_ref,
                    y_ref, s_out_ref, s_scr, ao_scr, go_scr, *, n_tiles, tiles_per_seq):
    i = pl.program_id(0)

    @pl.when(i == 0)
    def _():
        s_scr[...] = jnp.zeros_like(s_scr)
        ao_scr[...] = jnp.zeros_like(ao_scr)
        go_scr[...] = jnp.zeros_like(go_scr)

    front_tile = jnp.minimum(i, n_tiles - 1)
    tile_in_seq = lax.rem(front_tile, tiles_per_seq)
    seq_start = tile_in_seq == 0
    front_valid = i < n_tiles

    ffn = _out_ffn_stages(x_ref, ao_scr, go_scr, z_ref, gnw_ref, wout_ref, npost_ref, npre_ref,
                          wup_ref, wdown_ref, nffn_ref, y_ref)
    next(ffn)

    old = [s_scr[h] for h in range(GDN_HEADS)]
    state = [jnp.where(seq_start, 0.0, s) for s in old]
    front = _roundrobin(_gdn_prompt_stages(conv_ref, gb_ref, state, go_scr),
                        _attn_prompt_stages(sink_ref, pa_ref, prev_ref, tile_in_seq * MIX_T, ao_scr))
    mix_stages = (MIX_T // CHUNK) * 3 + 2 * (CHUNK.bit_length() - 2) + 1 + 2 * (MIX_T // WINDOW) * KV_HEADS
    _run_interleaved(ffn, FF_STAGES, front, mix_stages)

    for h in range(GDN_HEADS):
        s_new = jnp.where(front_valid, state[h], old[h])
        s_scr[h] = s_new
        s_out_ref[h] = s_new


def _mix_ffn(sinks, pa, act, gb, x, z, gnw, wout, npost, npre, wup, wdown, nffn, seq):
    m = x.shape[0]
    n_tiles = m // MIX_T
    tiles_per_seq = seq // MIX_T
    blocks_per_tile = MIX_T // WINDOW
    front = lambda i: (jnp.minimum(i, n_tiles - 1), 0)
    back = lambda i: (jnp.maximum(i - 1, 0), 0)
    fixed = lambda i: (0, 0)
    resident = dict(pipeline_mode=pl.Buffered(1))
    kern = functools.partial(_mix_ffn_kernel, n_tiles=n_tiles, tiles_per_seq=tiles_per_seq)
    return pl.pallas_call(
        kern,
        grid=(n_tiles + 1,),
        in_specs=[pl.BlockSpec(memory_space=pltpu.SMEM),
                  pl.BlockSpec((MIX_T, ATTN_COLS), front),
                  pl.BlockSpec((WINDOW, 2 * KV_WIDTH),
                               lambda i: (jnp.maximum(jnp.minimum(i, n_tiles - 1) * blocks_per_tile - 1, 0),
                                          ATTN_WIDTH // (2 * KV_WIDTH))),
                  pl.BlockSpec((MIX_T, GDN_CONV_DIM), front),
                  pl.BlockSpec((MIX_T, AB_COLS), front),
                  pl.BlockSpec((MIX_T, D_MODEL), back),
                  pl.BlockSpec((MIX_T, GDN_VAL_WIDTH), back),
                  pl.BlockSpec((1, GDN_DV), fixed),
                  pl.BlockSpec((D_MODEL, D_MODEL), fixed, **resident),
                  pl.BlockSpec((1, D_MODEL), fixed),
                  pl.BlockSpec((1, D_MODEL), fixed),
                  pl.BlockSpec((D_MODEL, D_FF), fixed, **resident),
                  pl.BlockSpec((D_FF, D_MODEL), fixed, **resident),
                  pl.BlockSpec((1, D_MODEL), fixed)],
        out_specs=[pl.BlockSpec((MIX_T, D_MODEL), back),
                   pl.BlockSpec((None, GDN_HEADS, GDN_DK, GDN_DV),
                                lambda i: (jnp.minimum(i, n_tiles - 1) // tiles_per_seq, 0, 0, 0))],
        out_shape=[jax.ShapeDtypeStruct((m, D_MODEL), F32),
                   jax.ShapeDtypeStruct((m // seq, GDN_HEADS, GDN_DK, GDN_DV), F32)],
        scratch_shapes=[pltpu.VMEM((GDN_HEADS, GDN_DK, GDN_DV), F32),
                        pltpu.VMEM((MIX_T, ATTN_WIDTH), BF16),
                        pltpu.VMEM((MIX_T, GDN_VAL_WIDTH), F32)],
        compiler_params=pltpu.CompilerParams(dimension_semantics=("arbitrary",),
                                             vmem_limit_bytes=VMEM_LIMIT_MIX),
        name="mix_ffn",
    )(sinks, pa, pa, act, gb, x, z, gnw, wout, npost, npre, wup, wdown, nffn)


def _out_ffn_kernel(*refs):
    for _ in _out_ffn_stages(*refs):
        pass


def _out_ffn(x, ao, go, z, gnw, wout, npost, npre, wup, wdown, nffn, tm):
    m = x.shape[0]
    row = lambda i: (i, 0)
    fixed = lambda i: (0, 0)
    return pl.pallas_call(
        _out_ffn_kernel,
        grid=(m // tm,),
        in_specs=[pl.BlockSpec((tm, D_MODEL), row),
                  pl.BlockSpec((tm, ATTN_WIDTH), row),
                  pl.BlockSpec((tm, GDN_VAL_WIDTH), row),
                  pl.BlockSpec((tm, GDN_VAL_WIDTH), row),
                  pl.BlockSpec((1, GDN_DV), fixed),
                  pl.BlockSpec((D_MODEL, D_MODEL), fixed, pipeline_mode=pl.Buffered(1)),
                  pl.BlockSpec((1, D_MODEL), fixed),
                  pl.BlockSpec((1, D_MODEL), fixed),
                  pl.BlockSpec((D_MODEL, D_FF), fixed, pipeline_mode=pl.Buffered(1)),
                  pl.BlockSpec((D_FF, D_MODEL), fixed, pipeline_mode=pl.Buffered(1)),
                  pl.BlockSpec((1, D_MODEL), fixed)],
        out_specs=pl.BlockSpec((tm, D_MODEL), row),
        out_shape=jax.ShapeDtypeStruct((m, D_MODEL), F32),
        compiler_params=pltpu.CompilerParams(dimension_semantics=("arbitrary",),
                                             vmem_limit_bytes=VMEM_LIMIT),
        name="out_ffn",
    )(x, ao, go, z, gnw, wout, npost, npre, wup, wdown, nffn)


def _sample_prep_kernel(x_ref, nw_ref, w_ref, cs_ref, cw_ref, alog_ref, dtb_ref,
                        qexp_ref, knew_ref, vnew_ref, qt_ref, kt_ref, vg_ref, gate_ref, z_ref,
                        cs_out_ref):
    h = _rms(x_ref[...], nw_ref[...]).astype(BF16)
    pa = jnp.dot(h, w_ref[:, 0:OFF_GDN], preferred_element_type=F32)
    pg = jnp.dot(h, w_ref[:, OFF_GDN:OFF_Z], preferred_element_type=F32)
    z_ref[...] = jnp.dot(h, w_ref[:, OFF_Z:OFF_AB], preferred_element_type=F32)
    ab = jnp.dot(h, w_ref[:, OFF_AB:IN_COLS], preferred_element_type=F32)

    lo = lax.broadcasted_iota(jnp.int32, (1, LANES), 1) < HEAD_DIM
    for t in range(ATTN_WIDTH // LANES):
        tile = pa[:, t * LANES:(t + 1) * LANES]
        rolled = pltpu.roll(tile, HEAD_DIM, axis=1)
        if t < ATTN_WIDTH // LANES // KV_HEADS:
            even, odd = jnp.where(lo, tile, 0.0), jnp.where(lo, rolled, 0.0)
        else:
            even, odd = jnp.where(lo, 0.0, rolled), jnp.where(lo, 0.0, tile)
        qexp_ref[2 * t] = even
        qexp_ref[2 * t + 1] = odd
    knew_ref[...] = pa[:, ATTN_WIDTH:ATTN_WIDTH + KV_WIDTH]
    vnew_ref[...] = pa[:, ATTN_WIDTH + KV_WIDTH:ATTN_COLS]

    cw = cw_ref[...]
    y = None
    for r in range(CONV_W - 1):
        term = cs_ref[:, r * GDN_CONV_DIM:(r + 1) * GDN_CONV_DIM] * cw[r:r + 1]
        y = term if y is None else y + term
    y = y + pg * cw[CONV_W - 1:CONV_W]
    act = y * _sigmoid(y)
    cs_out_ref[:, 0:(CONV_W - 2) * GDN_CONV_DIM] = cs_ref[:, GDN_CONV_DIM:(CONV_W - 1) * GDN_CONV_DIM]
    cs_out_ref[:, (CONV_W - 2) * GDN_CONV_DIM:] = pg

    for hd in range(GDN_HEADS):
        qn = _l2n(act[:, hd * LANES:(hd + 1) * LANES]) * (GDN_DK ** -0.5)
        kn = _l2n(act[:, GDN_KEY_WIDTH + hd * LANES:GDN_KEY_WIDTH + (hd + 1) * LANES])
        qt_ref[hd] = qn.T
        kt_ref[hd] = kn.T
    vg_ref[...] = act[:, 2 * GDN_KEY_WIDTH:]
    gb = _gates(ab, alog_ref[...], dtb_ref[...])
    lane = lax.broadcasted_iota(jnp.int32, (1, LANES), 1)
    gate_ref[...] = jnp.where(lane < GDN_HEADS, jnp.exp(gb), gb)


def _sample_prep(x, nw, w, cs, cw, alog, dtb):
    n = x.shape[0]
    return pl.pallas_call(
        _sample_prep_kernel,
        out_shape=[jax.ShapeDtypeStruct((ATTN_HEADS, n, LANES), F32),
                   jax.ShapeDtypeStruct((n, KV_WIDTH), F32),
                   jax.ShapeDtypeStruct((n, KV_WIDTH), F32),
                   jax.ShapeDtypeStruct((GDN_HEADS, GDN_DK, n), F32),
                   jax.ShapeDtypeStruct((GDN_HEADS, GDN_DK, n), F32),
                   jax.ShapeDtypeStruct((n, GDN_VAL_WIDTH), F32),
                   jax.ShapeDtypeStruct((n, LANES), F32),
                   jax.ShapeDtypeStruct((n, GDN_VAL_WIDTH), F32),
                   jax.ShapeDtypeStruct((n, (CONV_W - 1) * GDN_CONV_DIM), F32)],
        compiler_params=pltpu.CompilerParams(vmem_limit_bytes=VMEM_LIMIT),
        name="sample_prep",
    )(x, nw, w, cs, cw, alog, dtb)


ATTN_BB = 16


def _attn_sample_kernel(sink_ref, qexp_ref, knew_ref, vnew_ref, kbuf_ref, vbuf_ref,
                        o_ref, kout_ref, vout_ref, q_scr, o_scr):
    for h in range(ATTN_HEADS):
        q_scr[pl.ds(h, ATTN_BB, stride=ATTN_HEADS), :] = qexp_ref[h]
    hrow = lax.broadcasted_iota(jnp.int32, (ATTN_HEADS, 1), 0)
    sink = jnp.zeros((ATTN_HEADS, 1), F32)
    slope = jnp.zeros((ATTN_HEADS, 1), F32)
    for h in range(ATTN_HEADS):
        sink = jnp.where(hrow == h, sink_ref[0, h], sink)
        slope = jnp.where(hrow == h, _alibi_slope(h), slope)
    pos = lax.broadcasted_iota(jnp.int32, (1, WINDOW), 1)
    bias = slope * (WINDOW - pos).astype(F32)
    rowid = lax.broadcasted_iota(jnp.int32, (WINDOW, 1), 0)
    last = rowid == WINDOW - 1
    toks = range(ATTN_BB)
    qe = [q_scr[b * ATTN_HEADS:(b + 1) * ATTN_HEADS, :] for b in toks]
    s_old = [_dot_nt(qe[b], kbuf_ref[b]) * (HEAD_DIM ** -0.5) - bias for b in toks]
    probs, p_new = [], []
    for b in toks:
        s_new = jnp.sum(qe[b] * knew_ref[b:b + 1, :], axis=-1, keepdims=True) * (HEAD_DIM ** -0.5)
        mx = jnp.maximum(jnp.maximum(jnp.max(s_old[b], axis=-1, keepdims=True), s_new), sink)
        p_old = jnp.exp(s_old[b] - mx)
        pn = jnp.exp(s_new - mx)
        den = jnp.sum(p_old, axis=-1, keepdims=True) + pn + jnp.exp(sink - mx)
        probs.append(p_old / den)
        p_new.append(pn / den)
    for b in toks:
        o_scr[b * ATTN_HEADS:(b + 1) * ATTN_HEADS, :] = (_dot(probs[b], vbuf_ref[b])
                                                         + p_new[b] * vnew_ref[b:b + 1, :])
    for b in toks:
        kout_ref[b] = jnp.where(last, knew_ref[b:b + 1, :], pltpu.roll(kbuf_ref[b], WINDOW - 1, axis=0))
        vout_ref[b] = jnp.where(last, vnew_ref[b:b + 1, :], pltpu.roll(vbuf_ref[b], WINDOW - 1, axis=0))
    lo = lax.broadcasted_iota(jnp.int32, (1, LANES), 1) < HEAD_DIM
    for t in range(ATTN_WIDTH // LANES):
        even = o_scr[pl.ds(2 * t, ATTN_BB, stride=ATTN_HEADS), :]
        odd = o_scr[pl.ds(2 * t + 1, ATTN_BB, stride=ATTN_HEADS), :]
        if t < ATTN_WIDTH // LANES // KV_HEADS:
            tile = jnp.where(lo, even, pltpu.roll(odd, HEAD_DIM, axis=1))
        else:
            tile = jnp.where(lo, pltpu.roll(even, HEAD_DIM, axis=1), odd)
        o_ref[:, t * LANES:(t + 1) * LANES] = tile.astype(o_ref.dtype)


def _attn_sample(sinks, qexp, knew, vnew, kbuf, vbuf):
    n = knew.shape[0]
    tok2 = lambda i: (i, 0)
    tok3 = lambda i: (i, 0, 0)
    return pl.pallas_call(
        _attn_sample_kernel,
        grid=(n // ATTN_BB,),
        in_specs=[pl.BlockSpec(memory_space=pltpu.SMEM),
                  pl.BlockSpec((ATTN_HEADS, ATTN_BB, LANES), lambda i: (0, i, 0)),
                  pl.BlockSpec((ATTN_BB, KV_WIDTH), tok2),
                  pl.BlockSpec((ATTN_BB, KV_WIDTH), tok2),
                  pl.BlockSpec((ATTN_BB, WINDOW, KV_WIDTH), tok3),
                  pl.BlockSpec((ATTN_BB, WINDOW, KV_WIDTH), tok3)],
        out_specs=[pl.BlockSpec((ATTN_BB, ATTN_WIDTH), tok2),
                   pl.BlockSpec((ATTN_BB, WINDOW, KV_WIDTH), tok3),
                   pl.BlockSpec((ATTN_BB, WINDOW, KV_WIDTH), tok3)],
        out_shape=[jax.ShapeDtypeStruct((n, ATTN_WIDTH), BF16),
                   jax.ShapeDtypeStruct((n, WINDOW, KV_WIDTH), F32),
                   jax.ShapeDtypeStruct((n, WINDOW, KV_WIDTH), F32)],
        scratch_shapes=[pltpu.VMEM((ATTN_BB * ATTN_HEADS, LANES), F32),
                        pltpu.VMEM((ATTN_BB * ATTN_HEADS, LANES), F32)],
        compiler_params=pltpu.CompilerParams(dimension_semantics=("arbitrary",),
                                             vmem_limit_bytes=VMEM_LIMIT),
        name="attn_sample",
    )(sinks, qexp, knew, vnew, kbuf, vbuf)


GDN_BB = 16


def _gdn_sample_kernel(gate_ref, qt_ref, kt_ref, vg_ref, s_ref, o_ref, s_out_ref):
    i = pl.program_id(0)
    n = qt_ref.shape[-1]
    shift = (n - i * GDN_BB) % n
    units = [(j, h) for h in range(GDN_HEADS) for j in range(GDN_BB)]
    qt = [pltpu.roll(qt_ref[h], shift, axis=1) for h in range(GDN_HEADS)]
    kt = [pltpu.roll(kt_ref[h], shift, axis=1) for h in range(GDN_HEADS)]
    kb = [jnp.broadcast_to(kt[h][:, j:j + 1], (GDN_DK, GDN_DV)) for j, h in units]
    qb = [jnp.broadcast_to(qt[h][:, j:j + 1], (GDN_DK, GDN_DV)) for j, h in units]
    s = [s_ref[j, h] * gate_ref[(i * GDN_BB + j) * SUBLANES + h] for j, h in units]
    kv = [jnp.sum(su * k, axis=0, keepdims=True) for su, k in zip(s, kb)]
    u = [(vg_ref[j:j + 1, h * LANES:(h + 1) * LANES] - kvu)
         * gate_ref[(i * GDN_BB + j) * SUBLANES + GDN_HEADS + h] for (j, h), kvu in zip(units, kv)]
    s = [su + k * uu for su, k, uu in zip(s, kb, u)]
    for (j, h), su in zip(units, s):
        s_out_ref[j, h] = su
    for (j, h), su, q in zip(units, s, qb):
        o_ref[j:j + 1, h * LANES:(h + 1) * LANES] = jnp.sum(su * q, axis=0, keepdims=True)


def _gdn_sample(gate, qt, kt, vg, s0):
    n = vg.shape[0]
    return pl.pallas_call(
        _gdn_sample_kernel,
        grid=(n // GDN_BB,),
        in_specs=[pl.BlockSpec(memory_space=pltpu.SMEM),
                  pl.BlockSpec((GDN_HEADS, GDN_DK, n), lambda i: (0, 0, 0)),
                  pl.BlockSpec((GDN_HEADS, GDN_DK, n), lambda i: (0, 0, 0)),
                  pl.BlockSpec((GDN_BB, GDN_VAL_WIDTH), lambda i: (i, 0)),
                  pl.BlockSpec((GDN_BB, GDN_HEADS, GDN_DK, GDN_DV), lambda i: (i, 0, 0, 0))],
        out_specs=[pl.BlockSpec((GDN_BB, GDN_VAL_WIDTH), lambda i: (i, 0)),
                   pl.BlockSpec((GDN_BB, GDN_HEADS, GDN_DK, GDN_DV), lambda i: (i, 0, 0, 0))],
        out_shape=[jax.ShapeDtypeStruct((n, GDN_VAL_WIDTH), F32),
                   jax.ShapeDtypeStruct((n, GDN_HEADS, GDN_DK, GDN_DV), F32)],
        compiler_params=pltpu.CompilerParams(dimension_semantics=("arbitrary",),
                                             vmem_limit_bytes=VMEM_LIMIT),
        name="gdn_sample",
    )(gate, qt, kt, vg, s0)


def _pad_lanes(v):
    return jnp.pad(v.astype(F32), (0, LANES - v.shape[0])).reshape(1, LANES)


def kernel(x_prompt, x_sample, state_conv, cache_win_k, cache_win_v, state_gdn, norm_mix_pre, w_in,
           attn_sinks, conv_w, gdn_a_log, gdn_dt_bias, gdn_norm, w_out, norm_mix_post, norm_ffn_pre,
           w_up, w_down, norm_ffn_post):
    assert w_in.shape[0] == 1, "single-layer trunk"
    bsz, seq, _ = x_prompt.shape
    nsamp = x_sample.shape[0]

    w_in_p = jnp.pad(w_in[0], ((0, 0), (0, IN_COLS - w_in.shape[2]))).astype(BF16)
    w_out_b = w_out[0].astype(BF16)
    w_up_b = w_up[0].astype(BF16)
    w_down_b = w_down[0].astype(BF16)
    n_pre = norm_mix_pre[0].reshape(1, D_MODEL)
    n_post = norm_mix_post[0].reshape(1, D_MODEL)
    n_fpre = norm_ffn_pre[0].reshape(1, D_MODEL)
    n_fpost = norm_ffn_post[0].reshape(1, D_MODEL)
    sinks = attn_sinks[0].reshape(1, ATTN_HEADS)
    cw = conv_w[0]
    alog = _pad_lanes(gdn_a_log[0])
    dtb = _pad_lanes(gdn_dt_bias[0])
    gnw = gdn_norm[0].reshape(1, GDN_DV)

    xp = x_prompt.reshape(bsz * seq, D_MODEL)
    pa, act, pz, gb, p_conv = _inproj(xp, n_pre, w_in_p, cw, alog, dtb, INPROJ_T, seq)
    yp, p_s = _mix_ffn(sinks, pa, act, gb, xp, pz, gnw, w_out_b, n_post, n_fpre, w_up_b, w_down_b,
                       n_fpost, seq)
    pa = pa.reshape(bsz, seq, ATTN_COLS)

    xs = x_sample.reshape(nsamp, D_MODEL)
    cs = state_conv[0].reshape(nsamp, (CONV_W - 1) * GDN_CONV_DIM)
    qexp, knew, vnew, qt, kt, vg, gate, zs, cs_new = _sample_prep(xs, n_pre, w_in_p, cs, cw, alog, dtb)
    kbuf = cache_win_k[0].reshape(nsamp, WINDOW, KV_WIDTH)
    vbuf = cache_win_v[0].reshape(nsamp, WINDOW, KV_WIDTH)
    ao_s, k_new_cache, v_new_cache = _attn_sample(sinks, qexp, knew, vnew, kbuf, vbuf)
    go_s, s_new = _gdn_sample(gate[:, :SUBLANES].reshape(nsamp * SUBLANES), qt, kt, vg, state_gdn[0])
    ys = _out_ffn(xs, ao_s, go_s, zs, gnw, w_out_b, n_post, n_fpre, w_up_b, w_down_b, n_fpost, nsamp)

    kv_shape = (1, -1, WINDOW, KV_HEADS, HEAD_DIM)
    return (yp.reshape(bsz, seq, D_MODEL),
            ys.reshape(nsamp, 1, D_MODEL),
            p_conv[:, SUBLANES - (CONV_W - 1):, :][None],
            pa[:, seq - WINDOW:, ATTN_WIDTH:ATTN_WIDTH + KV_WIDTH].reshape(kv_shape),
            pa[:, seq - WINDOW:, ATTN_WIDTH + KV_WIDTH:].reshape(kv_shape),
            p_s[None],
            cs_new.reshape(1, nsamp, CONV_W - 1, GDN_CONV_DIM),
            k_new_cache.reshape(kv_shape),
            v_new_cache.reshape(kv_shape),
            s_new[None])
```

```python
import functools

import jax
import jax.numpy as jnp
from jax import lax
from jax.experimental import pallas as pl
from jax.experimental.pallas import tpu as pltpu

F32 = jnp.float32
BF16 = jnp.bfloat16

D_MODEL = 1024
ATTN_HEADS = 8
KV_HEADS = 2
HEAD_DIM = 64
GQA_GROUP = ATTN_HEADS // KV_HEADS
ATTN_WIDTH = ATTN_HEADS * HEAD_DIM
KV_WIDTH = KV_HEADS * HEAD_DIM
WINDOW = 128
GDN_HEADS = 4
GDN_DK = 128
GDN_DV = 128
GDN_KEY_WIDTH = GDN_HEADS * GDN_DK
GDN_VAL_WIDTH = GDN_HEADS * GDN_DV
GDN_CONV_DIM = 2 * GDN_KEY_WIDTH + GDN_VAL_WIDTH
CONV_W = 4
D_FF = 4 * D_MODEL
EPS = 1e-6

LANES = 128
SUBLANES = 8
ATTN_COLS = ATTN_WIDTH + 2 * KV_WIDTH
AB_COLS = LANES
IN_COLS = ATTN_COLS + GDN_CONV_DIM + GDN_VAL_WIDTH + AB_COLS
OFF_GDN = ATTN_COLS
OFF_Z = OFF_GDN + GDN_CONV_DIM
OFF_AB = OFF_Z + GDN_VAL_WIDTH
CHUNK = 128
V7X_VMEM_BYTES = 64 * 1024 * 1024
VMEM_LIMIT = V7X_VMEM_BYTES - 8 * 1024 * 1024
VMEM_LIMIT_MIX = V7X_VMEM_BYTES - 4 * 1024 * 1024


def _rms(x, w):
    return x * lax.rsqrt(jnp.mean(x * x, axis=-1, keepdims=True) + EPS) * w


def _sigmoid(x):
    return 1.0 / (1.0 + jnp.exp(-x))


def _softplus(x):
    return jnp.maximum(x, 0.0) + jnp.log(1.0 + jnp.exp(-jnp.abs(x)))


def _dot(a, b):
    return jnp.dot(a.astype(BF16), b.astype(BF16), preferred_element_type=F32)


def _dot_nt(a, b):
    return lax.dot_general(a.astype(BF16), b.astype(BF16), (((1,), (1,)), ((), ())),
                           preferred_element_type=F32)


def _alibi_slope(h):
    return 2.0 ** (-8.0 * (h + 1) / ATTN_HEADS)


def _causal_conv(x, carry, w):
    t, c = x.shape
    x3 = x.reshape(t // SUBLANES, SUBLANES, c)
    row8 = lax.broadcasted_iota(jnp.int32, (1, SUBLANES, 1), 1)
    y = None
    for shift in range(CONV_W - 1, -1, -1):
        if shift == 0:
            xs = x3
        else:
            xr = pltpu.roll(x3, shift, axis=1)
            cr = pltpu.roll(carry, shift, axis=0)
            prev = jnp.concatenate([cr[None], xr[:-1]], axis=0)
            xs = jnp.where(row8 < shift, prev, xr)
        term = xs * w[CONV_W - 1 - shift:CONV_W - shift][None]
        y = term if y is None else y + term
    return y.reshape(t, c)


def _silu(y):
    return y * _sigmoid(y)


def _l2n(x):
    return x * lax.rsqrt(jnp.sum(x * x, axis=-1, keepdims=True) + EPS)


def _gates(ab, alog, dtb):
    g = -jnp.exp(alog) * _softplus(ab + dtb)
    lane = lax.broadcasted_iota(jnp.int32, (1, LANES), 1)
    return jnp.where(lane < GDN_HEADS, g, _sigmoid(ab))


INPROJ_T = 1024
INPROJ_SUB = 128


def _inproj_kernel(x_ref, nw_ref, w_ref, cw_ref, alog_ref, dtb_ref, wout_ref, wup_ref, wdown_ref,
                   oa_ref, og_ref, oz_ref, ogb_ref, conv_out_ref, wout_b_ref, wup_b_ref, wdown_b_ref,
                   carry_scr, *, tiles_per_seq):
    tm = x_ref.shape[0]
    wout_b_ref[...] = wout_ref[...].astype(BF16)
    wup_b_ref[...] = wup_ref[...].astype(BF16)
    wdown_b_ref[...] = wdown_ref[...].astype(BF16)

    @pl.when(lax.rem(pl.program_id(0), tiles_per_seq) == 0)
    def _():
        carry_scr[...] = jnp.zeros_like(carry_scr)

    carry = carry_scr[...]
    for r0 in range(0, tm, INPROJ_SUB):
        rows = slice(r0, r0 + INPROJ_SUB)
        h = _rms(x_ref[rows, :], nw_ref[...]).astype(BF16)
        pg = jnp.dot(h, w_ref[:, OFF_GDN:OFF_Z], preferred_element_type=F32)
        oa_ref[rows, :] = jnp.dot(h, w_ref[:, 0:OFF_GDN], preferred_element_type=F32)
        oz_ref[rows, :] = jnp.dot(h, w_ref[:, OFF_Z:OFF_AB], preferred_element_type=F32)
        ab = jnp.dot(h, w_ref[:, OFF_AB:IN_COLS], preferred_element_type=F32)
        ogb_ref[rows, :] = _gates(ab, alog_ref[...], dtb_ref[...])
        og_ref[rows, :] = _causal_conv(pg, carry, cw_ref[...])
        carry = pg[INPROJ_SUB - SUBLANES:INPROJ_SUB]
    carry_scr[...] = carry
    conv_out_ref[...] = carry


def _inproj(x, nw, w, cw, alog, dtb, wout, wup, wdown, tm, seq):
    m = x.shape[0]
    steps = m // tm
    tiles_per_seq = seq // tm
    row = lambda i: (i, 0)
    fixed = lambda i: (0, 0)
    slab = lambda a: pl.BlockSpec((a.shape[0] // steps, a.shape[1]), row)
    return pl.pallas_call(
        functools.partial(_inproj_kernel, tiles_per_seq=tiles_per_seq),
        grid=(steps,),
        in_specs=[pl.BlockSpec((tm, D_MODEL), row),
                  pl.BlockSpec((1, D_MODEL), fixed),
                  pl.BlockSpec((D_MODEL, IN_COLS), fixed),
                  pl.BlockSpec((CONV_W, GDN_CONV_DIM), fixed),
                  pl.BlockSpec((1, LANES), fixed),
                  pl.BlockSpec((1, LANES), fixed),
                  slab(wout), slab(wup), slab(wdown)],
        out_specs=[pl.BlockSpec((tm, ATTN_COLS), row),
                   pl.BlockSpec((tm, GDN_CONV_DIM), row),
                   pl.BlockSpec((tm, GDN_VAL_WIDTH), row),
                   pl.BlockSpec((tm, AB_COLS), row),
                   pl.BlockSpec((None, SUBLANES, GDN_CONV_DIM), lambda i: (i // tiles_per_seq, 0, 0)),
                   slab(wout), slab(wup), slab(wdown)],
        out_shape=[jax.ShapeDtypeStruct((m, ATTN_COLS), F32),
                   jax.ShapeDtypeStruct((m, GDN_CONV_DIM), F32),
                   jax.ShapeDtypeStruct((m, GDN_VAL_WIDTH), F32),
                   jax.ShapeDtypeStruct((m, AB_COLS), F32),
                   jax.ShapeDtypeStruct((m // seq, SUBLANES, GDN_CONV_DIM), F32),
                   jax.ShapeDtypeStruct(wout.shape, BF16),
                   jax.ShapeDtypeStruct(wup.shape, BF16),
                   jax.ShapeDtypeStruct(wdown.shape, BF16)],
        scratch_shapes=[pltpu.VMEM((SUBLANES, GDN_CONV_DIM), F32)],
        compiler_params=pltpu.CompilerParams(dimension_semantics=("arbitrary",),
                                             vmem_limit_bytes=VMEM_LIMIT),
        name="inproj",
    )(x, nw, w, cw, alog, dtb, wout, wup, wdown)


FF_UP_CHUNK = 256
FF_DOWN_CHUNK = 256
FF_STAGES = D_FF // FF_UP_CHUNK + D_MODEL // FF_DOWN_CHUNK


def _out_ffn_stages(x_ref, ao_ref, go_ref, z_ref, gnw_ref, wout_ref, npost_ref, npre_ref, wup_ref,
                    wdown_ref, nffn_ref, y_ref):
    gnw = gnw_ref[...]
    gy = []
    for hd in range(GDN_HEADS):
        lanes = slice(hd * LANES, (hd + 1) * LANES)
        z = z_ref[:, lanes]
        gy.append((_rms(go_ref[:, lanes], gnw) * (z * _sigmoid(z))).astype(BF16))
    m = (jnp.dot(ao_ref[...], wout_ref[0:ATTN_WIDTH, :], preferred_element_type=F32)
         + jnp.dot(jnp.concatenate(gy, axis=1), wout_ref[ATTN_WIDTH:, :], preferred_element_type=F32))
    x1 = x_ref[...] + _rms(m, npost_ref[...])
    h = _rms(x1, npre_ref[...]).astype(BF16)
    yield
    hidden = []
    for j in range(0, D_FF, FF_UP_CHUNK):
        u = jnp.maximum(jnp.dot(h, wup_ref[:, j:j + FF_UP_CHUNK], preferred_element_type=F32), 0.0)
        hidden.append((u * u).astype(BF16))
        yield
    hidden = jnp.concatenate(hidden, axis=1)
    f = []
    for j in range(0, D_MODEL, FF_DOWN_CHUNK):
        f.append(jnp.dot(hidden, wdown_ref[:, j:j + FF_DOWN_CHUNK], preferred_element_type=F32))
        yield
    y_ref[...] = x1 + _rms(jnp.concatenate(f, axis=1), nffn_ref[...])


def _dup_half(x, lo):
    xr = pltpu.roll(x, HEAD_DIM, axis=1)
    return jnp.where(lo, x, xr), jnp.where(lo, xr, x)


def _attn_prompt_stages(sink_ref, cur_ref, prev_ref, seq_row0, o_ref):
    tq = cur_ref.shape[0]
    kall = jnp.concatenate([prev_ref[:, 0:KV_WIDTH], cur_ref[:, ATTN_WIDTH:ATTN_WIDTH + KV_WIDTH]], axis=0)
    vall = jnp.concatenate([prev_ref[:, KV_WIDTH:], cur_ref[:, ATTN_WIDTH + KV_WIDTH:]], axis=0)
    lo = lax.broadcasted_iota(jnp.int32, (1, LANES), 1) < HEAD_DIM
    k2 = _dup_half(kall, lo)
    v2 = _dup_half(vall, lo)
    qi = lax.broadcasted_iota(jnp.int32, (WINDOW, 2 * WINDOW), 0)
    si = lax.broadcasted_iota(jnp.int32, (WINDOW, 2 * WINDOW), 1)
    dist = WINDOW + qi - si
    band = (dist >= 0) & (dist <= WINDOW)
    distf = dist.astype(F32)
    for r in range(tq // WINDOW):
        rows = slice(r * WINDOW, (r + 1) * WINDOW)
        valid = band
        if r == 0:
            valid = band & ((seq_row0 - WINDOW + si) >= 0)
        for j in range(KV_HEADS):
            kw = k2[j][r * WINDOW:(r + 2) * WINDOW]
            vw = v2[j][r * WINDOW:(r + 2) * WINDOW]
            t0 = cur_ref[rows, (2 * j) * LANES:(2 * j + 1) * LANES]
            t1 = cur_ref[rows, (2 * j + 1) * LANES:(2 * j + 2) * LANES]
            lhs = jnp.concatenate([jnp.where(lo, t0, 0.0), jnp.where(lo, 0.0, t0),
                                   jnp.where(lo, t1, 0.0), jnp.where(lo, 0.0, t1)], axis=0)
            s = _dot_nt(lhs, kw) * (HEAD_DIM ** -0.5)
            probs = []
            for g in range(GQA_GROUP):
                h = j * GQA_GROUP + g
                sg = s[g * WINDOW:(g + 1) * WINDOW] - _alibi_slope(h) * distf
                sg = jnp.where(valid, sg, -jnp.inf)
                sink = sink_ref[0, h]
                mx = jnp.maximum(jnp.max(sg, axis=-1, keepdims=True), sink)
                p = jnp.exp(sg - mx)
                den = jnp.sum(p, axis=-1, keepdims=True) + jnp.exp(sink - mx)
                probs.append(p / den)
            yield
            o = _dot(jnp.concatenate(probs, axis=0), vw)
            o01 = jnp.where(lo, o[0:WINDOW], o[WINDOW:2 * WINDOW])
            o23 = jnp.where(lo, o[2 * WINDOW:3 * WINDOW], o[3 * WINDOW:4 * WINDOW])
            o_ref[rows, (2 * j) * LANES:(2 * j + 1) * LANES] = o01.astype(o_ref.dtype)
            o_ref[rows, (2 * j + 1) * LANES:(2 * j + 2) * LANES] = o23.astype(o_ref.dtype)
            yield


def _gdn_prompt_stages(conv_ref, gb_ref, state, o_ref):
    t = conv_ref.shape[0]
    beta = gb_ref[...]
    g = beta
    rin = lax.broadcasted_iota(jnp.int32, (t, 1), 0) & (CHUNK - 1)
    step = 1
    while step < CHUNK:
        g = g + jnp.where(rin >= step, pltpu.roll(g, step, axis=0), 0.0)
        step *= 2
    gt = g.T

    row = lax.broadcasted_iota(jnp.int32, (CHUNK, CHUNK), 0)
    col = lax.broadcasted_iota(jnp.int32, (CHUNK, CHUNK), 1)
    lower_incl = row >= col
    lower_strict = row > col

    heads = range(GDN_HEADS)
    units = [(c, h) for c in range(t // CHUNK) for h in heads]
    pre = []
    for c, h in units:
        rows = slice(c * CHUNK, (c + 1) * CHUNK)
        qc = _l2n(_silu(conv_ref[rows, h * LANES:(h + 1) * LANES])) * (GDN_DK ** -0.5)
        kc = _l2n(_silu(conv_ref[rows, GDN_KEY_WIDTH + h * LANES:GDN_KEY_WIDTH + (h + 1) * LANES]))
        vc = _silu(conv_ref[rows, 2 * GDN_KEY_WIDTH + h * LANES:2 * GDN_KEY_WIDTH + (h + 1) * LANES])
        gc = g[rows, h:h + 1]
        bc = beta[rows, GDN_HEADS + h:GDN_HEADS + h + 1]
        gr = gt[h:h + 1, c * CHUNK:(c + 1) * CHUNK]
        decay = jnp.where(lower_incl, jnp.exp(jnp.where(lower_incl, gc - gr, 0.0)), 0.0)
        eg = jnp.exp(gc)
        glast = gc[CHUNK - 1:CHUNK]
        rhs = jnp.concatenate([vc * bc, kc * (bc * eg)], axis=1)
        k_dec_t = (kc * jnp.exp(glast - gc)).T
        kq = _dot_nt(jnp.concatenate([kc, qc], axis=0), kc)
        a = jnp.where(lower_strict, bc * kq[0:CHUNK] * decay, 0.0)
        qk = jnp.where(lower_incl, kq[CHUNK:] * decay, 0.0)
        pre.append(dict(a=a, rhs=rhs, lhs_u=jnp.concatenate([qk, k_dec_t], axis=0), q_dec=qc * eg,
                        ld=jnp.exp(glast)))
        if h == GDN_HEADS - 1:
            yield

    mats = [p["a"] for p in pre]
    first = ((row >> 1) == (col >> 1)) & ((row & 1) == 1) & ((col & 1) == 0)
    xo = [jnp.where(first, -a, 0.0) for a in mats]
    size = 2
    while size < CHUNK:
        shift = size.bit_length()
        m = ((row >> shift) == (col >> shift)) & ((row & size) != 0) & ((col & size) == 0)
        aoff = [jnp.where(m, a, 0.0) for a in mats]
        y = [ao + _dot(ao, x) for ao, x in zip(aoff, xo)]
        yield
        xo = [x - yy - _dot(x, yy) for x, yy in zip(xo, y)]
        size *= 2
        yield
    trs = [p["rhs"] + _dot(x, p["rhs"]) for p, x in zip(pre, xo)]
    yield

    for c in range(t // CHUNK):
        rows = slice(c * CHUNK, (c + 1) * CHUNK)
        idx = [c * GDN_HEADS + h for h in heads]
        ws = [_dot(jnp.concatenate([trs[u][:, GDN_DV:], pre[u]["q_dec"]], axis=0), state[h])
              for h, u in zip(heads, idx)]
        yield
        ou = [_dot(pre[u]["lhs_u"], trs[u][:, 0:GDN_DV] - ws[h][0:CHUNK]) for h, u in zip(heads, idx)]
        for h, u in zip(heads, idx):
            o_ref[rows, h * LANES:(h + 1) * LANES] = ws[h][CHUNK:] + ou[h][0:CHUNK]
            state[h] = state[h] * pre[u]["ld"] + ou[h][CHUNK:]
        yield


def _run_interleaved(primary, n_primary, secondary, n_secondary):
    done = object()
    issued = 0
    for k in range(n_primary):
        next(primary, done)
        target = ((k + 1) * n_secondary + n_primary - 1) // n_primary
        while issued < target:
            next(secondary, done)
            issued += 1
    for _ in primary:
        pass
    for _ in secondary:
        pass


def _roundrobin(*gens):
    done = object()
    live = list(gens)
    while live:
        for gen in list(live):
            if next(gen, done) is done:
                live.remove(gen)
            else:
                yield


MIX_T = 512


def _mix_ffn_kernel(sink_ref, pa_ref, prev_ref, conv_ref, gb_ref, x_ref, z_ref, gnw_ref, wout_ref,
                    npost_ref, npre_ref, wup_ref, wdown_ref, nffn_ref,
                    y_ref, s_out_ref, s_scr, ao_scr, go_scr, *, n_tiles, tiles_per_seq):
    i = pl.program_id(0)

    @pl.when(i == 0)
    def _():
        s_scr[...] = jnp.zeros_like(s_scr)
        ao_scr[...] = jnp.zeros_like(ao_scr)
        go_scr[...] = jnp.zeros_like(go_scr)

    front_tile = jnp.minimum(i, n_tiles - 1)
    tile_in_seq = lax.rem(front_tile, tiles_per_seq)
    seq_start = tile_in_seq == 0
    front_valid = i < n_tiles

    ffn = _out_ffn_stages(x_ref, ao_scr, go_scr, z_ref, gnw_ref, wout_ref, npost_ref, npre_ref,
                          wup_ref, wdown_ref, nffn_ref, y_ref)
    next(ffn)

    old = [s_scr[h] for h in range(GDN_HEADS)]
    state = [jnp.where(seq_start, 0.0, s) for s in old]
    front = _roundrobin(_gdn_prompt_stages(conv_ref, gb_ref, state, go_scr),
                        _attn_prompt_stages(sink_ref, pa_ref, prev_ref, tile_in_seq * MIX_T, ao_scr))
    mix_stages = (MIX_T // CHUNK) * 3 + 2 * (CHUNK.bit_length() - 2) + 1 + 2 * (MIX_T // WINDOW) * KV_HEADS
    _run_interleaved(ffn, FF_STAGES, front, mix_stages)

    for h in range(GDN_HEADS):
        s_new = jnp.where(front_valid, state[h], old[h])
        s_scr[h] = s_new
        s_out_ref[h] = s_new


def _mix_ffn(sinks, pa, act, gb, x, z, gnw, wout, npost, npre, wup, wdown, nffn, seq):
    m = x.shape[0]
    n_tiles = m // MIX_T
    tiles_per_seq = seq // MIX_T
    blocks_per_tile = MIX_T // WINDOW
    front = lambda i: (jnp.minimum(i, n_tiles - 1), 0)
    back = lambda i: (jnp.maximum(i - 1, 0), 0)
    fixed = lambda i: (0, 0)
    resident = dict(pipeline_mode=pl.Buffered(1))
    kern = functools.partial(_mix_ffn_kernel, n_tiles=n_tiles, tiles_per_seq=tiles_per_seq)
    return pl.pallas_call(
        kern,
        grid=(n_tiles + 1,),
        in_specs=[pl.BlockSpec(memory_space=pltpu.SMEM),
                  pl.BlockSpec((MIX_T, ATTN_COLS), front),
                  pl.BlockSpec((WINDOW, 2 * KV_WIDTH),
                               lambda i: (jnp.maximum(jnp.minimum(i, n_tiles - 1) * blocks_per_tile - 1, 0),
                                          ATTN_WIDTH // (2 * KV_WIDTH))),
                  pl.BlockSpec((MIX_T, GDN_CONV_DIM), front),
                  pl.BlockSpec((MIX_T, AB_COLS), front),
                  pl.BlockSpec((MIX_T, D_MODEL), back),
                  pl.BlockSpec((MIX_T, GDN_VAL_WIDTH), back),
                  pl.BlockSpec((1, GDN_DV), fixed),
                  pl.BlockSpec((D_MODEL, D_MODEL), fixed, **resident),
                  pl.BlockSpec((1, D_MODEL), fixed),
                  pl.BlockSpec((1, D_MODEL), fixed),
                  pl.BlockSpec((D_MODEL, D_FF), fixed, **resident),
                  pl.BlockSpec((D_FF, D_MODEL), fixed, **resident),
                  pl.BlockSpec((1, D_MODEL), fixed)],
        out_specs=[pl.BlockSpec((MIX_T, D_MODEL), back),
                   pl.BlockSpec((None, GDN_HEADS, GDN_DK, GDN_DV),
                                lambda i: (jnp.minimum(i, n_tiles - 1) // tiles_per_seq, 0, 0, 0))],
        out_shape=[jax.ShapeDtypeStruct((m, D_MODEL), F32),
                   jax.ShapeDtypeStruct((m // seq, GDN_HEADS, GDN_DK, GDN_DV), F32)],
        scratch_shapes=[pltpu.VMEM((GDN_HEADS, GDN_DK, GDN_DV), F32),
                        pltpu.VMEM((MIX_T, ATTN_WIDTH), BF16),
                        pltpu.VMEM((MIX_T, GDN_VAL_WIDTH), F32)],
        compiler_params=pltpu.CompilerParams(dimension_semantics=("arbitrary",),
                                             vmem_limit_bytes=VMEM_LIMIT_MIX),
        name="mix_ffn",
    )(sinks, pa, pa, act, gb, x, z, gnw, wout, npost, npre, wup, wdown, nffn)


def _out_ffn_kernel(*refs):
    for _ in _out_ffn_stages(*refs):
        pass


def _out_ffn(x, ao, go, z, gnw, wout, npost, npre, wup, wdown, nffn, tm):
    m = x.shape[0]
    row = lambda i: (i, 0)
    fixed = lambda i: (0, 0)
    return pl.pallas_call(
        _out_ffn_kernel,
        grid=(m // tm,),
        in_specs=[pl.BlockSpec((tm, D_MODEL), row),
                  pl.BlockSpec((tm, ATTN_WIDTH), row),
                  pl.BlockSpec((tm, GDN_VAL_WIDTH), row),
                  pl.BlockSpec((tm, GDN_VAL_WIDTH), row),
                  pl.BlockSpec((1, GDN_DV), fixed),
                  pl.BlockSpec((D_MODEL, D_MODEL), fixed, pipeline_mode=pl.Buffered(1)),
                  pl.BlockSpec((1, D_MODEL), fixed),
                  pl.BlockSpec((1, D_MODEL), fixed),
                  pl.BlockSpec((D_MODEL, D_FF), fixed, pipeline_mode=pl.Buffered(1)),
                  pl.BlockSpec((D_FF, D_MODEL), fixed, pipeline_mode=pl.Buffered(1)),
                  pl.BlockSpec((1, D_MODEL), fixed)],
        out_specs=pl.BlockSpec((tm, D_MODEL), row),
        out_shape=jax.ShapeDtypeStruct((m, D_MODEL), F32),
        compiler_params=pltpu.CompilerParams(dimension_semantics=("arbitrary",),
                                             vmem_limit_bytes=VMEM_LIMIT),
        name="out_ffn",
    )(x, ao, go, z, gnw, wout, npost, npre, wup, wdown, nffn)


def _sample_prep_kernel(x_ref, nw_ref, w_ref, cs_ref, cw_ref, alog_ref, dtb_ref,
                        qexp_ref, knew_ref, vnew_ref, qt_ref, kt_ref, vg_ref, gate_ref, z_ref,
                        cs_out_ref):
    h = _rms(x_ref[...], nw_ref[...]).astype(BF16)
    pa = jnp.dot(h, w_ref[:, 0:OFF_GDN], preferred_element_type=F32)
    pg = jnp.dot(h, w_ref[:, OFF_GDN:OFF_Z], preferred_element_type=F32)
    z_ref[...] = jnp.dot(h, w_ref[:, OFF_Z:OFF_AB], preferred_element_type=F32)
    ab = jnp.dot(h, w_ref[:, OFF_AB:IN_COLS], preferred_element_type=F32)

    lo = lax.broadcasted_iota(jnp.int32, (1, LANES), 1) < HEAD_DIM
    for t in range(ATTN_WIDTH // LANES):
        tile = pa[:, t * LANES:(t + 1) * LANES]
        rolled = pltpu.roll(tile, HEAD_DIM, axis=1)
        if t < ATTN_WIDTH // LANES // KV_HEADS:
            even, odd = jnp.where(lo, tile, 0.0), jnp.where(lo, rolled, 0.0)
        else:
            even, odd = jnp.where(lo, 0.0, rolled), jnp.where(lo, 0.0, tile)
        qexp_ref[2 * t] = even
        qexp_ref[2 * t + 1] = odd
    knew_ref[...] = pa[:, ATTN_WIDTH:ATTN_WIDTH + KV_WIDTH]
    vnew_ref[...] = pa[:, ATTN_WIDTH + KV_WIDTH:ATTN_COLS]

    cw = cw_ref[...]
    y = None
    for r in range(CONV_W - 1):
        term = cs_ref[:, r * GDN_CONV_DIM:(r + 1) * GDN_CONV_DIM] * cw[r:r + 1]
        y = term if y is None else y + term
    y = y + pg * cw[CONV_W - 1:CONV_W]
    act = y * _sigmoid(y)
    cs_out_ref[:, 0:(CONV_W - 2) * GDN_CONV_DIM] = cs_ref[:, GDN_CONV_DIM:(CONV_W - 1) * GDN_CONV_DIM]
    cs_out_ref[:, (CONV_W - 2) * GDN_CONV_DIM:] = pg

    for hd in range(GDN_HEADS):
        qn = _l2n(act[:, hd * LANES:(hd + 1) * LANES]) * (GDN_DK ** -0.5)
        kn = _l2n(act[:, GDN_KEY_WIDTH + hd * LANES:GDN_KEY_WIDTH + (hd + 1) * LANES])
        qt_ref[hd] = qn.T
        kt_ref[hd] = kn.T
    vg_ref[...] = act[:, 2 * GDN_KEY_WIDTH:]
    gb = _gates(ab, alog_ref[...], dtb_ref[...])
    lane = lax.broadcasted_iota(jnp.int32, (1, LANES), 1)
    gate_ref[...] = jnp.where(lane < GDN_HEADS, jnp.exp(gb), gb)


def _sample_prep(x, nw, w, cs, cw, alog, dtb):
    n = x.shape[0]
    return pl.pallas_call(
        _sample_prep_kernel,
        out_shape=[jax.ShapeDtypeStruct((ATTN_HEADS, n, LANES), F32),
                   jax.ShapeDtypeStruct((n, KV_WIDTH), F32),
                   jax.ShapeDtypeStruct((n, KV_WIDTH), F32),
                   jax.ShapeDtypeStruct((GDN_HEADS, GDN_DK, n), F32),
                   jax.ShapeDtypeStruct((GDN_HEADS, GDN_DK, n), F32),
                   jax.ShapeDtypeStruct((n, GDN_VAL_WIDTH), F32),
                   jax.ShapeDtypeStruct((n, LANES), F32),
                   jax.ShapeDtypeStruct((n, GDN_VAL_WIDTH), F32),
                   jax.ShapeDtypeStruct((n, (CONV_W - 1) * GDN_CONV_DIM), F32)],
        compiler_params=pltpu.CompilerParams(vmem_limit_bytes=VMEM_LIMIT),
        name="sample_prep",
    )(x, nw, w, cs, cw, alog, dtb)


ATTN_BB = 16


def _attn_sample_kernel(sink_ref, qexp_ref, knew_ref, vnew_ref, kbuf_ref, vbuf_ref,
                        o_ref, kout_ref, vout_ref, q_scr, o_scr):
    for h in range(ATTN_HEADS):
        q_scr[pl.ds(h, ATTN_BB, stride=ATTN_HEADS), :] = qexp_ref[h]
    hrow = lax.broadcasted_iota(jnp.int32, (ATTN_HEADS, 1), 0)
    sink = jnp.zeros((ATTN_HEADS, 1), F32)
    slope = jnp.zeros((ATTN_HEADS, 1), F32)
    for h in range(ATTN_HEADS):
        sink = jnp.where(hrow == h, sink_ref[0, h], sink)
        slope = jnp.where(hrow == h, _alibi_slope(h), slope)
    pos = lax.broadcasted_iota(jnp.int32, (1, WINDOW), 1)
    bias = slope * (WINDOW - pos).astype(F32)
    rowid = lax.broadcasted_iota(jnp.int32, (WINDOW, 1), 0)
    last = rowid == WINDOW - 1
    toks = range(ATTN_BB)
    qe = [q_scr[b * ATTN_HEADS:(b + 1) * ATTN_HEADS, :] for b in toks]
    s_old = [_dot_nt(qe[b], kbuf_ref[b]) * (HEAD_DIM ** -0.5) - bias for b in toks]
    probs, p_new = [], []
    for b in toks:
        s_new = jnp.sum(qe[b] * knew_ref[b:b + 1, :], axis=-1, keepdims=True) * (HEAD_DIM ** -0.5)
        mx = jnp.maximum(jnp.maximum(jnp.max(s_old[b], axis=-1, keepdims=True), s_new), sink)
        p_old = jnp.exp(s_old[b] - mx)
        pn = jnp.exp(s_new - mx)
        den = jnp.sum(p_old, axis=-1, keepdims=True) + pn + jnp.exp(sink - mx)
        probs.append(p_old / den)
        p_new.append(pn / den)
    for b in toks:
        o_scr[b * ATTN_HEADS:(b + 1) * ATTN_HEADS, :] = (_dot(probs[b], vbuf_ref[b])
                                                         + p_new[b] * vnew_ref[b:b + 1, :])
    for b in toks:
        kout_ref[b] = jnp.where(last, knew_ref[b:b + 1, :], pltpu.roll(kbuf_ref[b], WINDOW - 1, axis=0))
        vout_ref[b] = jnp.where(last, vnew_ref[b:b + 1, :], pltpu.roll(vbuf_ref[b], WINDOW - 1, axis=0))
    lo = lax.broadcasted_iota(jnp.int32, (1, LANES), 1) < HEAD_DIM
    for t in range(ATTN_WIDTH // LANES):
        even = o_scr[pl.ds(2 * t, ATTN_BB, stride=ATTN_HEADS), :]
        odd = o_scr[pl.ds(2 * t + 1, ATTN_BB, stride=ATTN_HEADS), :]
        if t < ATTN_WIDTH // LANES // KV_HEADS:
            tile = jnp.where(lo, even, pltpu.roll(odd, HEAD_DIM, axis=1))
        else:
            tile = jnp.where(lo, pltpu.roll(even, HEAD_DIM, axis=1), odd)
        o_ref[:, t * LANES:(t + 1) * LANES] = tile.astype(o_ref.dtype)


def _attn_sample(sinks, qexp, knew, vnew, kbuf, vbuf):
    n = knew.shape[0]
    tok2 = lambda i: (i, 0)
    tok3 = lambda i: (i, 0, 0)
    return pl.pallas_call(
        _attn_sample_kernel,
        grid=(n // ATTN_BB,),
        in_specs=[pl.BlockSpec(memory_space=pltpu.SMEM),
                  pl.BlockSpec((ATTN_HEADS, ATTN_BB, LANES), lambda i: (0, i, 0)),
                  pl.BlockSpec((ATTN_BB, KV_WIDTH), tok2),
                  pl.BlockSpec((ATTN_BB, KV_WIDTH), tok2),
                  pl.BlockSpec((ATTN_BB, WINDOW, KV_WIDTH), tok3),
                  pl.BlockSpec((ATTN_BB, WINDOW, KV_WIDTH), tok3)],
        out_specs=[pl.BlockSpec((ATTN_BB, ATTN_WIDTH), tok2),
                   pl.BlockSpec((ATTN_BB, WINDOW, KV_WIDTH), tok3),
                   pl.BlockSpec((ATTN_BB, WINDOW, KV_WIDTH), tok3)],
        out_shape=[jax.ShapeDtypeStruct((n, ATTN_WIDTH), BF16),
                   jax.ShapeDtypeStruct((n, WINDOW, KV_WIDTH), F32),
                   jax.ShapeDtypeStruct((n, WINDOW, KV_WIDTH), F32)],
        scratch_shapes=[pltpu.VMEM((ATTN_BB * ATTN_HEADS, LANES), F32),
                        pltpu.VMEM((ATTN_BB * ATTN_HEADS, LANES), F32)],
        compiler_params=pltpu.CompilerParams(dimension_semantics=("arbitrary",),
                                             vmem_limit_bytes=VMEM_LIMIT),
        name="attn_sample",
    )(sinks, qexp, knew, vnew, kbuf, vbuf)


GDN_BB = 16


def _gdn_sample_kernel(gate_ref, qt_ref, kt_ref, vg_ref, s_ref, o_ref, s_out_ref):
    i = pl.program_id(0)
    n = qt_ref.shape[-1]
    shift = (n - i * GDN_BB) % n
    units = [(j, h) for h in range(GDN_HEADS) for j in range(GDN_BB)]
    qt = [pltpu.roll(qt_ref[h], shift, axis=1) for h in range(GDN_HEADS)]
    kt = [pltpu.roll(kt_ref[h], shift, axis=1) for h in range(GDN_HEADS)]
    kb = [jnp.broadcast_to(kt[h][:, j:j + 1], (GDN_DK, GDN_DV)) for j, h in units]
    qb = [jnp.broadcast_to(qt[h][:, j:j + 1], (GDN_DK, GDN_DV)) for j, h in units]
    s = [s_ref[j, h] * gate_ref[(i * GDN_BB + j) * SUBLANES + h] for j, h in units]
    kv = [jnp.sum(su * k, axis=0, keepdims=True) for su, k in zip(s, kb)]
    u = [(vg_ref[j:j + 1, h * LANES:(h + 1) * LANES] - kvu)
         * gate_ref[(i * GDN_BB + j) * SUBLANES + GDN_HEADS + h] for (j, h), kvu in zip(units, kv)]
    s = [su + k * uu for su, k, uu in zip(s, kb, u)]
    for (j, h), su in zip(units, s):
        s_out_ref[j, h] = su
    for (j, h), su, q in zip(units, s, qb):
        o_ref[j:j + 1, h * LANES:(h + 1) * LANES] = jnp.sum(su * q, axis=0, keepdims=True)


def _gdn_sample(gate, qt, kt, vg, s0):
    n = vg.shape[0]
    return pl.pallas_call(
        _gdn_sample_kernel,
        grid=(n // GDN_BB,),
        in_specs=[pl.BlockSpec(memory_space=pltpu.SMEM),
                  pl.BlockSpec((GDN_HEADS, GDN_DK, n), lambda i: (0, 0, 0)),
                  pl.BlockSpec((GDN_HEADS, GDN_DK, n), lambda i: (0, 0, 0)),
                  pl.BlockSpec((GDN_BB, GDN_VAL_WIDTH), lambda i: (i, 0)),
                  pl.BlockSpec((GDN_BB, GDN_HEADS, GDN_DK, GDN_DV), lambda i: (i, 0, 0, 0))],
        out_specs=[pl.BlockSpec((GDN_BB, GDN_VAL_WIDTH), lambda i: (i, 0)),
                   pl.BlockSpec((GDN_BB, GDN_HEADS, GDN_DK, GDN_DV), lambda i: (i, 0, 0, 0))],
        out_shape=[jax.ShapeDtypeStruct((n, GDN_VAL_WIDTH), F32),
                   jax.ShapeDtypeStruct((n, GDN_HEADS, GDN_DK, GDN_DV), F32)],
        compiler_params=pltpu.CompilerParams(dimension_semantics=("arbitrary",),
                                             vmem_limit_bytes=VMEM_LIMIT),
        name="gdn_sample",
    )(gate, qt, kt, vg, s0)


def _pad_lanes(v):
    return jnp.pad(v.astype(F32), (0, LANES - v.shape[0])).reshape(1, LANES)


def kernel(x_prompt, x_sample, state_conv, cache_win_k, cache_win_v, state_gdn, norm_mix_pre, w_in,
           attn_sinks, conv_w, gdn_a_log, gdn_dt_bias, gdn_norm, w_out, norm_mix_post, norm_ffn_pre,
           w_up, w_down, norm_ffn_post):
    assert w_in.shape[0] == 1, "single-layer trunk"
    bsz, seq, _ = x_prompt.shape
    nsamp = x_sample.shape[0]

    w_in_p = jnp.pad(w_in[0], ((0, 0), (0, IN_COLS - w_in.shape[2]))).astype(BF16)
    n_pre = norm_mix_pre[0].reshape(1, D_MODEL)
    n_post = norm_mix_post[0].reshape(1, D_MODEL)
    n_fpre = norm_ffn_pre[0].reshape(1, D_MODEL)
    n_fpost = norm_ffn_post[0].reshape(1, D_MODEL)
    sinks = attn_sinks[0].reshape(1, ATTN_HEADS)
    cw = conv_w[0]
    alog = _pad_lanes(gdn_a_log[0])
    dtb = _pad_lanes(gdn_dt_bias[0])
    gnw = gdn_norm[0].reshape(1, GDN_DV)

    xp = x_prompt.reshape(bsz * seq, D_MODEL)
    pa, act, pz, gb, p_conv, w_out_b, w_up_b, w_down_b = _inproj(
        xp, n_pre, w_in_p, cw, alog, dtb, w_out[0], w_up[0], w_down[0], INPROJ_T, seq)
    yp, p_s = _mix_ffn(sinks, pa, act, gb, xp, pz, gnw, w_out_b, n_post, n_fpre, w_up_b, w_down_b,
                       n_fpost, seq)
    pa = pa.reshape(bsz, seq, ATTN_COLS)

    xs = x_sample.reshape(nsamp, D_MODEL)
    cs = state_conv[0].reshape(nsamp, (CONV_W - 1) * GDN_CONV_DIM)
    qexp, knew, vnew, qt, kt, vg, gate, zs, cs_new = _sample_prep(xs, n_pre, w_in_p, cs, cw, alog, dtb)
    kbuf = cache_win_k[0].reshape(nsamp, WINDOW, KV_WIDTH)
    vbuf = cache_win_v[0].reshape(nsamp, WINDOW, KV_WIDTH)
    ao_s, k_new_cache, v_new_cache = _attn_sample(sinks, qexp, knew, vnew, kbuf, vbuf)
    go_s, s_new = _gdn_sample(gate[:, :SUBLANES].reshape(nsamp * SUBLANES), qt, kt, vg, state_gdn[0])
    ys = _out_ffn(xs, ao_s, go_s, zs, gnw, w_out_b, n_post, n_fpre, w_up_b, w_down_b, n_fpost, nsamp)

    kv_shape = (1, -1, WINDOW, KV_HEADS, HEAD_DIM)
    return (yp.reshape(bsz, seq, D_MODEL),
            ys.reshape(nsamp, 1, D_MODEL),
            p_conv[:, SUBLANES - (CONV_W - 1):, :][None],
            pa[:, seq - WINDOW:, ATTN_WIDTH:ATTN_WIDTH + KV_WIDTH].reshape(kv_shape),
            pa[:, seq - WINDOW:, ATTN_WIDTH + KV_WIDTH:].reshape(kv_shape),
            p_s[None],
            cs_new.reshape(1, nsamp, CONV_W - 1, GDN_CONV_DIM),
            k_new_cache.reshape(kv_shape),
            v_new_cache.reshape(kv_shape),
            s_new[None])
```

```python
import functools

import jax
import jax.numpy as jnp
from jax import lax
from jax.experimental import pallas as pl
from jax.experimental.pallas import tpu as pltpu

F32 = jnp.float32
BF16 = jnp.bfloat16

D_MODEL = 1024
ATTN_HEADS = 8
KV_HEADS = 2
HEAD_DIM = 64
GQA_GROUP = ATTN_HEADS // KV_HEADS
ATTN_WIDTH = ATTN_HEADS * HEAD_DIM
KV_WIDTH = KV_HEADS * HEAD_DIM
WINDOW = 128
GDN_HEADS = 4
GDN_DK = 128
GDN_DV = 128
GDN_KEY_WIDTH = GDN_HEADS * GDN_DK
GDN_VAL_WIDTH = GDN_HEADS * GDN_DV
GDN_CONV_DIM = 2 * GDN_KEY_WIDTH + GDN_VAL_WIDTH
CONV_W = 4
D_FF = 4 * D_MODEL
EPS = 1e-6

LANES = 128
SUBLANES = 8
ATTN_COLS = ATTN_WIDTH + 2 * KV_WIDTH
AB_COLS = LANES
IN_COLS = ATTN_COLS + GDN_CONV_DIM + GDN_VAL_WIDTH + AB_COLS
OFF_GDN = ATTN_COLS
OFF_Z = OFF_GDN + GDN_CONV_DIM
OFF_AB = OFF_Z + GDN_VAL_WIDTH
CHUNK = 128
V7X_VMEM_BYTES = 64 * 1024 * 1024
VMEM_LIMIT = V7X_VMEM_BYTES - 8 * 1024 * 1024
VMEM_LIMIT_MIX = V7X_VMEM_BYTES - 4 * 1024 * 1024


def _rms(x, w):
    return x * lax.rsqrt(jnp.mean(x * x, axis=-1, keepdims=True) + EPS) * w


def _sigmoid(x):
    return 1.0 / (1.0 + jnp.exp(-x))


def _softplus(x):
    return jnp.maximum(x, 0.0) + jnp.log(1.0 + jnp.exp(-jnp.abs(x)))


def _dot(a, b):
    return jnp.dot(a.astype(BF16), b.astype(BF16), preferred_element_type=F32)


def _dot_nt(a, b):
    return lax.dot_general(a.astype(BF16), b.astype(BF16), (((1,), (1,)), ((), ())),
                           preferred_element_type=F32)


def _alibi_slope(h):
    return 2.0 ** (-8.0 * (h + 1) / ATTN_HEADS)


def _causal_conv(x, carry, w):
    t, c = x.shape
    x3 = x.reshape(t // SUBLANES, SUBLANES, c)
    row8 = lax.broadcasted_iota(jnp.int32, (1, SUBLANES, 1), 1)
    y = None
    for shift in range(CONV_W - 1, -1, -1):
        if shift == 0:
            xs = x3
        else:
            xr = pltpu.roll(x3, shift, axis=1)
            cr = pltpu.roll(carry, shift, axis=0)
            prev = jnp.concatenate([cr[None], xr[:-1]], axis=0)
            xs = jnp.where(row8 < shift, prev, xr)
        term = xs * w[CONV_W - 1 - shift:CONV_W - shift][None]
        y = term if y is None else y + term
    return y.reshape(t, c)


def _silu(y):
    return y * _sigmoid(y)


def _l2n(x):
    return x * lax.rsqrt(jnp.sum(x * x, axis=-1, keepdims=True) + EPS)


def _gates(ab, alog, dtb):
    g = -jnp.exp(alog) * _softplus(ab + dtb)
    lane = lax.broadcasted_iota(jnp.int32, (1, LANES), 1)
    return jnp.where(lane < GDN_HEADS, g, _sigmoid(ab))


INPROJ_T = 1024
INPROJ_SUB = 128


def _inproj_kernel(x_ref, nw_ref, w_ref, cw_ref, alog_ref, dtb_ref, wout_ref, wup_ref, wdown_ref,
                   oa_ref, og_ref, oz_ref, ogb_ref, conv_out_ref, wout_b_ref, wup_b_ref, wdown_b_ref,
                   carry_scr, *, tiles_per_seq):
    tm = x_ref.shape[0]
    wout_b_ref[...] = wout_ref[...].astype(BF16)
    wup_b_ref[...] = wup_ref[...].astype(BF16)
    wdown_b_ref[...] = wdown_ref[...].astype(BF16)

    @pl.when(lax.rem(pl.program_id(0), tiles_per_seq) == 0)
    def _():
        carry_scr[...] = jnp.zeros_like(carry_scr)

    carry = carry_scr[...]
    for r0 in range(0, tm, INPROJ_SUB):
        rows = slice(r0, r0 + INPROJ_SUB)
        h = _rms(x_ref[rows, :], nw_ref[...]).astype(BF16)
        pg = jnp.dot(h, w_ref[:, OFF_GDN:OFF_Z], preferred_element_type=F32)
        oa_ref[rows, :] = jnp.dot(h, w_ref[:, 0:OFF_GDN], preferred_element_type=F32)
        oz_ref[rows, :] = jnp.dot(h, w_ref[:, OFF_Z:OFF_AB], preferred_element_type=F32)
        ab = jnp.dot(h, w_ref[:, OFF_AB:IN_COLS], preferred_element_type=F32)
        ogb_ref[rows, :] = _gates(ab, alog_ref[...], dtb_ref[...])
        og_ref[rows, :] = _causal_conv(pg, carry, cw_ref[...])
        carry = pg[INPROJ_SUB - SUBLANES:INPROJ_SUB]
    carry_scr[...] = carry
    conv_out_ref[...] = carry


def _inproj(x, nw, w, cw, alog, dtb, wout, wup, wdown, tm, seq):
    m = x.shape[0]
    steps = m // tm
    tiles_per_seq = seq // tm
    row = lambda i: (i, 0)
    fixed = lambda i: (0, 0)
    slab = lambda a: pl.BlockSpec((a.shape[0] // steps, a.shape[1]), row)
    return pl.pallas_call(
        functools.partial(_inproj_kernel, tiles_per_seq=tiles_per_seq),
        grid=(steps,),
        in_specs=[pl.BlockSpec((tm, D_MODEL), row),
                  pl.BlockSpec((1, D_MODEL), fixed),
                  pl.BlockSpec((D_MODEL, IN_COLS), fixed),
                  pl.BlockSpec((CONV_W, GDN_CONV_DIM), fixed),
                  pl.BlockSpec((1, LANES), fixed),
                  pl.BlockSpec((1, LANES), fixed),
                  slab(wout), slab(wup), slab(wdown)],
        out_specs=[pl.BlockSpec((tm, ATTN_COLS), row),
                   pl.BlockSpec((tm, GDN_CONV_DIM), row),
                   pl.BlockSpec((tm, GDN_VAL_WIDTH), row),
                   pl.BlockSpec((tm, AB_COLS), row),
                   pl.BlockSpec((None, SUBLANES, GDN_CONV_DIM), lambda i: (i // tiles_per_seq, 0, 0)),
                   slab(wout), slab(wup), slab(wdown)],
        out_shape=[jax.ShapeDtypeStruct((m, ATTN_COLS), F32),
                   jax.ShapeDtypeStruct((m, GDN_CONV_DIM), F32),
                   jax.ShapeDtypeStruct((m, GDN_VAL_WIDTH), F32),
                   jax.ShapeDtypeStruct((m, AB_COLS), F32),
                   jax.ShapeDtypeStruct((m // seq, SUBLANES, GDN_CONV_DIM), F32),
                   jax.ShapeDtypeStruct(wout.shape, BF16),
                   jax.ShapeDtypeStruct(wup.shape, BF16),
                   jax.ShapeDtypeStruct(wdown.shape, BF16)],
        scratch_shapes=[pltpu.VMEM((SUBLANES, GDN_CONV_DIM), F32)],
        compiler_params=pltpu.CompilerParams(dimension_semantics=("arbitrary",),
                                             vmem_limit_bytes=VMEM_LIMIT),
        name="inproj",
    )(x, nw, w, cw, alog, dtb, wout, wup, wdown)


FF_UP_CHUNK = 256
FF_DOWN_CHUNK = 256
FF_STAGES = D_FF // FF_UP_CHUNK + D_MODEL // FF_DOWN_CHUNK


def _out_ffn_stages(x_ref, ao_ref, go_ref, z_ref, gnw_ref, wout_ref, npost_ref, npre_ref, wup_ref,
                    wdown_ref, nffn_ref, y_ref):
    gnw = gnw_ref[...]
    gy = []
    for hd in range(GDN_HEADS):
        lanes = slice(hd * LANES, (hd + 1) * LANES)
        z = z_ref[:, lanes]
        gy.append((_rms(go_ref[:, lanes], gnw) * (z * _sigmoid(z))).astype(BF16))
    m = (jnp.dot(ao_ref[...], wout_ref[0:ATTN_WIDTH, :], preferred_element_type=F32)
         + jnp.dot(jnp.concatenate(gy, axis=1), wout_ref[ATTN_WIDTH:, :], preferred_element_type=F32))
    x1 = x_ref[...] + _rms(m, npost_ref[...])
    h = _rms(x1, npre_ref[...]).astype(BF16)
    yield
    hidden = []
    for j in range(0, D_FF, FF_UP_CHUNK):
        u = jnp.maximum(jnp.dot(h, wup_ref[:, j:j + FF_UP_CHUNK], preferred_element_type=F32), 0.0)
        hidden.append((u * u).astype(BF16))
        yield
    hidden = jnp.concatenate(hidden, axis=1)
    f = []
    for j in range(0, D_MODEL, FF_DOWN_CHUNK):
        f.append(jnp.dot(hidden, wdown_ref[:, j:j + FF_DOWN_CHUNK], preferred_element_type=F32))
        yield
    y_ref[...] = x1 + _rms(jnp.concatenate(f, axis=1), nffn_ref[...])


def _dup_half(x, lo):
    xr = pltpu.roll(x, HEAD_DIM, axis=1)
    return jnp.where(lo, x, xr), jnp.where(lo, xr, x)


def _attn_prompt_stages(sink_ref, cur_ref, prev_ref, seq_row0, o_ref):
    tq = cur_ref.shape[0]
    kall = jnp.concatenate([prev_ref[:, 0:KV_WIDTH], cur_ref[:, ATTN_WIDTH:ATTN_WIDTH + KV_WIDTH]], axis=0)
    vall = jnp.concatenate([prev_ref[:, KV_WIDTH:], cur_ref[:, ATTN_WIDTH + KV_WIDTH:]], axis=0)
    lo = lax.broadcasted_iota(jnp.int32, (1, LANES), 1) < HEAD_DIM
    k2 = _dup_half(kall, lo)
    v2 = _dup_half(vall, lo)
    qi = lax.broadcasted_iota(jnp.int32, (WINDOW, 2 * WINDOW), 0)
    si = lax.broadcasted_iota(jnp.int32, (WINDOW, 2 * WINDOW), 1)
    dist = WINDOW + qi - si
    band = (dist >= 0) & (dist <= WINDOW)
    distf = dist.astype(F32)
    for r in range(tq // WINDOW):
        rows = slice(r * WINDOW, (r + 1) * WINDOW)
        valid = band
        if r == 0:
            valid = band & ((seq_row0 - WINDOW + si) >= 0)
        for j in range(KV_HEADS):
            kw = k2[j][r * WINDOW:(r + 2) * WINDOW]
            vw = v2[j][r * WINDOW:(r + 2) * WINDOW]
            t0 = cur_ref[rows, (2 * j) * LANES:(2 * j + 1) * LANES]
            t1 = cur_ref[rows, (2 * j + 1) * LANES:(2 * j + 2) * LANES]
            lhs = jnp.concatenate([jnp.where(lo, t0, 0.0), jnp.where(lo, 0.0, t0),
                                   jnp.where(lo, t1, 0.0), jnp.where(lo, 0.0, t1)], axis=0)
            s = _dot_nt(lhs, kw) * (HEAD_DIM ** -0.5)
            probs = []
            for g in range(GQA_GROUP):
                h = j * GQA_GROUP + g
                sg = s[g * WINDOW:(g + 1) * WINDOW] - _alibi_slope(h) * distf
                sg = jnp.where(valid, sg, -jnp.inf)
                sink = sink_ref[0, h]
                mx = jnp.maximum(jnp.max(sg, axis=-1, keepdims=True), sink)
                p = jnp.exp(sg - mx)
                den = jnp.sum(p, axis=-1, keepdims=True) + jnp.exp(sink - mx)
                probs.append(p / den)
            yield
            o = _dot(jnp.concatenate(probs, axis=0), vw)
            o01 = jnp.where(lo, o[0:WINDOW], o[WINDOW:2 * WINDOW])
            o23 = jnp.where(lo, o[2 * WINDOW:3 * WINDOW], o[3 * WINDOW:4 * WINDOW])
            o_ref[rows, (2 * j) * LANES:(2 * j + 1) * LANES] = o01.astype(o_ref.dtype)
            o_ref[rows, (2 * j + 1) * LANES:(2 * j + 2) * LANES] = o23.astype(o_ref.dtype)
            yield


def _gdn_prompt_stages(conv_ref, gb_ref, state, o_ref):
    t = conv_ref.shape[0]
    beta = gb_ref[...]
    g = beta
    rin = lax.broadcasted_iota(jnp.int32, (t, 1), 0) & (CHUNK - 1)
    step = 1
    while step < CHUNK:
        g = g + jnp.where(rin >= step, pltpu.roll(g, step, axis=0), 0.0)
        step *= 2
    gt = g.T

    row = lax.broadcasted_iota(jnp.int32, (CHUNK, CHUNK), 0)
    col = lax.broadcasted_iota(jnp.int32, (CHUNK, CHUNK), 1)
    lower_incl = row >= col
    lower_strict = row > col

    heads = range(GDN_HEADS)
    units = [(c, h) for c in range(t // CHUNK) for h in heads]
    pre = []
    for c, h in units:
        rows = slice(c * CHUNK, (c + 1) * CHUNK)
        qc = _l2n(_silu(conv_ref[rows, h * LANES:(h + 1) * LANES])) * (GDN_DK ** -0.5)
        kc = _l2n(_silu(conv_ref[rows, GDN_KEY_WIDTH + h * LANES:GDN_KEY_WIDTH + (h + 1) * LANES]))
        vc = _silu(conv_ref[rows, 2 * GDN_KEY_WIDTH + h * LANES:2 * GDN_KEY_WIDTH + (h + 1) * LANES])
        gc = g[rows, h:h + 1]
        bc = beta[rows, GDN_HEADS + h:GDN_HEADS + h + 1]
        gr = gt[h:h + 1, c * CHUNK:(c + 1) * CHUNK]
        decay = jnp.where(lower_incl, jnp.exp(jnp.where(lower_incl, gc - gr, 0.0)), 0.0)
        eg = jnp.exp(gc)
        glast = gc[CHUNK - 1:CHUNK]
        rhs = jnp.concatenate([vc * bc, kc * (bc * eg)], axis=1)
        k_dec_t = (kc * jnp.exp(glast - gc)).T
        kq = _dot_nt(jnp.concatenate([kc, qc], axis=0), kc)
        a = jnp.where(lower_strict, bc * kq[0:CHUNK] * decay, 0.0)
        qk = jnp.where(lower_incl, kq[CHUNK:] * decay, 0.0)
        pre.append(dict(a=a, rhs=rhs, lhs_u=jnp.concatenate([qk, k_dec_t], axis=0), q_dec=qc * eg,
                        ld=jnp.exp(glast)))
        if h == GDN_HEADS - 1:
            yield

    mats = [p["a"] for p in pre]
    first = ((row >> 1) == (col >> 1)) & ((row & 1) == 1) & ((col & 1) == 0)
    xo = [jnp.where(first, -a, 0.0) for a in mats]
    size = 2
    while size < CHUNK:
        shift = size.bit_length()
        m = ((row >> shift) == (col >> shift)) & ((row & size) != 0) & ((col & size) == 0)
        aoff = [jnp.where(m, a, 0.0) for a in mats]
        y = [ao + _dot(ao, x) for ao, x in zip(aoff, xo)]
        yield
        xo = [x - yy - _dot(x, yy) for x, yy in zip(xo, y)]
        size *= 2
        yield
    trs = [p["rhs"] + _dot(x, p["rhs"]) for p, x in zip(pre, xo)]
    yield

    for c in range(t // CHUNK):
        rows = slice(c * CHUNK, (c + 1) * CHUNK)
        idx = [c * GDN_HEADS + h for h in heads]
        ws = [_dot(jnp.concatenate([trs[u][:, GDN_DV:], pre[u]["q_dec"]], axis=0), state[h])
              for h, u in zip(heads, idx)]
        yield
        ou = [_dot(pre[u]["lhs_u"], trs[u][:, 0:GDN_DV] - ws[h][0:CHUNK]) for h, u in zip(heads, idx)]
        for h, u in zip(heads, idx):
            o_ref[rows, h * LANES:(h + 1) * LANES] = ws[h][CHUNK:] + ou[h][0:CHUNK]
            state[h] = state[h] * pre[u]["ld"] + ou[h][CHUNK:]
        yield


def _run_interleaved(primary, n_primary, secondary, n_secondary):
    done = object()
    issued = 0
    for k in range(n_primary):
        next(primary, done)
        target = ((k + 1) * n_secondary + n_primary - 1) // n_primary
        while issued < target:
            next(secondary, done)
            issued += 1
    for _ in primary:
        pass
    for _ in secondary:
        pass


def _roundrobin(*gens):
    done = object()
    live = list(gens)
    while live:
        for gen in list(live):
            if next(gen, done) is done:
                live.remove(gen)
            else:
                yield


MIX_T = 512


def _mix_ffn_kernel(sink_ref, pa_ref, prev_ref, conv_ref, gb_ref, x_ref, z_ref, gnw_ref, wout_ref,
                    npost_ref, npre_ref, wup_ref, wdown_ref, nffn_ref,
                    y_ref, s_out_ref, s_scr, ao_scr, go_scr, *, n_tiles, tiles_per_seq):
    i = pl.program_id(0)

    @pl.when(i == 0)
    def _():
        s_scr[...] = jnp.zeros_like(s_scr)
        ao_scr[...] = jnp.zeros_like(ao_scr)
        go_scr[...] = jnp.zeros_like(go_scr)

    front_tile = jnp.minimum(i, n_tiles - 1)
    tile_in_seq = lax.rem(front_tile, tiles_per_seq)
    seq_start = tile_in_seq == 0
    front_valid = i < n_tiles

    ffn = _out_ffn_stages(x_ref, ao_scr, go_scr, z_ref, gnw_ref, wout_ref, npost_ref, npre_ref,
                          wup_ref, wdown_ref, nffn_ref, y_ref)
    next(ffn)

    old = [s_scr[h] for h in range(GDN_HEADS)]
    state = [jnp.where(seq_start, 0.0, s) for s in old]
    front = _roundrobin(_gdn_prompt_stages(conv_ref, gb_ref, state, go_scr),
                        _attn_prompt_stages(sink_ref, pa_ref, prev_ref, tile_in_seq * MIX_T, ao_scr))
    mix_stages = (MIX_T // CHUNK) * 3 + 2 * (CHUNK.bit_length() - 2) + 1 + 2 * (MIX_T // WINDOW) * KV_HEADS
    _run_interleaved(ffn, FF_STAGES, front, mix_stages)

    for h in range(GDN_HEADS):
        s_new = jnp.where(front_valid, state[h], old[h])
        s_scr[h] = s_new
        s_out_ref[h] = s_new


def _mix_ffn(sinks, pa, act, gb, x, z, gnw, wout, npost, npre, wup, wdown, nffn, seq):
    m = x.shape[0]
    n_tiles = m // MIX_T
    tiles_per_seq = seq // MIX_T
    blocks_per_tile = MIX_T // WINDOW
    front = lambda i: (jnp.minimum(i, n_tiles - 1), 0)
    back = lambda i: (jnp.maximum(i - 1, 0), 0)
    fixed = lambda i: (0, 0)
    resident = dict(pipeline_mode=pl.Buffered(1))
    kern = functools.partial(_mix_ffn_kernel, n_tiles=n_tiles, tiles_per_seq=tiles_per_seq)
    return pl.pallas_call(
        kern,
        grid=(n_tiles + 1,),
        in_specs=[pl.BlockSpec(memory_space=pltpu.SMEM),
                  pl.BlockSpec((MIX_T, ATTN_COLS), front),
                  pl.BlockSpec((WINDOW, 2 * KV_WIDTH),
                               lambda i: (jnp.maximum(jnp.minimum(i, n_tiles - 1) * blocks_per_tile - 1, 0),
                                          ATTN_WIDTH // (2 * KV_WIDTH))),
                  pl.BlockSpec((MIX_T, GDN_CONV_DIM), front),
                  pl.BlockSpec((MIX_T, AB_COLS), front),
                  pl.BlockSpec((MIX_T, D_MODEL), back),
                  pl.BlockSpec((MIX_T, GDN_VAL_WIDTH), back),
                  pl.BlockSpec((1, GDN_DV), fixed),
                  pl.BlockSpec((D_MODEL, D_MODEL), fixed, **resident),
                  pl.BlockSpec((1, D_MODEL), fixed),
                  pl.BlockSpec((1, D_MODEL), fixed),
                  pl.BlockSpec((D_MODEL, D_FF), fixed, **resident),
                  pl.BlockSpec((D_FF, D_MODEL), fixed, **resident),
                  pl.BlockSpec((1, D_MODEL), fixed)],
        out_specs=[pl.BlockSpec((MIX_T, D_MODEL), back),
                   pl.BlockSpec((None, GDN_HEADS, GDN_DK, GDN_DV),
                                lambda i: (jnp.minimum(i, n_tiles - 1) // tiles_per_seq, 0, 0, 0))],
        out_shape=[jax.ShapeDtypeStruct((m, D_MODEL), F32),
                   jax.ShapeDtypeStruct((m // seq, GDN_HEADS, GDN_DK, GDN_DV), F32)],
        scratch_shapes=[pltpu.VMEM((GDN_HEADS, GDN_DK, GDN_DV), F32),
                        pltpu.VMEM((MIX_T, ATTN_WIDTH), BF16),
                        pltpu.VMEM((MIX_T, GDN_VAL_WIDTH), F32)],
        compiler_params=pltpu.CompilerParams(dimension_semantics=("arbitrary",),
                                             vmem_limit_bytes=VMEM_LIMIT_MIX),
        name="mix_ffn",
    )(sinks, pa, pa, act, gb, x, z, gnw, wout, npost, npre, wup, wdown, nffn)


def _out_ffn_kernel(*refs):
    for _ in _out_ffn_stages(*refs):
        pass


def _out_ffn(x, ao, go, z, gnw, wout, npost, npre, wup, wdown, nffn, tm):
    m = x.shape[0]
    row = lambda i: (i, 0)
    fixed = lambda i: (0, 0)
    return pl.pallas_call(
        _out_ffn_kernel,
        grid=(m // tm,),
        in_specs=[pl.BlockSpec((tm, D_MODEL), row),
                  pl.BlockSpec((tm, ATTN_WIDTH), row),
                  pl.BlockSpec((tm, GDN_VAL_WIDTH), row),
                  pl.BlockSpec((tm, GDN_VAL_WIDTH), row),
                  pl.BlockSpec((1, GDN_DV), fixed),
                  pl.BlockSpec((D_MODEL, D_MODEL), fixed, pipeline_mode=pl.Buffered(1)),
                  pl.BlockSpec((1, D_MODEL), fixed),
                  pl.BlockSpec((1, D_MODEL), fixed),
                  pl.BlockSpec((D_MODEL, D_FF), fixed, pipeline_mode=pl.Buffered(1)),
                  pl.BlockSpec((D_FF, D_MODEL), fixed, pipeline_mode=pl.Buffered(1)),
                  pl.BlockSpec((1, D_MODEL), fixed)],
        out_specs=pl.BlockSpec((tm, D_MODEL), row),
        out_shape=jax.ShapeDtypeStruct((m, D_MODEL), F32),
        compiler_params=pltpu.CompilerParams(dimension_semantics=("arbitrary",),
                                             vmem_limit_bytes=VMEM_LIMIT),
        name="out_ffn",
    )(x, ao, go, z, gnw, wout, npost, npre, wup, wdown, nffn)


def _sample_prep_kernel(x_ref, nw_ref, w_ref, cs_ref, cw_ref, alog_ref, dtb_ref,
                        qexp_ref, knew_ref, vnew_ref, qg_ref, kt_ref, vg_ref, gate_ref, z_ref,
                        cs_out_ref):
    h = _rms(x_ref[...], nw_ref[...]).astype(BF16)
    pa = jnp.dot(h, w_ref[:, 0:OFF_GDN], preferred_element_type=F32)
    pg = jnp.dot(h, w_ref[:, OFF_GDN:OFF_Z], preferred_element_type=F32)
    z_ref[...] = jnp.dot(h, w_ref[:, OFF_Z:OFF_AB], preferred_element_type=F32)
    ab = jnp.dot(h, w_ref[:, OFF_AB:IN_COLS], preferred_element_type=F32)

    lo = lax.broadcasted_iota(jnp.int32, (1, LANES), 1) < HEAD_DIM
    for t in range(ATTN_WIDTH // LANES):
        tile = pa[:, t * LANES:(t + 1) * LANES]
        rolled = pltpu.roll(tile, HEAD_DIM, axis=1)
        if t < ATTN_WIDTH // LANES // KV_HEADS:
            even, odd = jnp.where(lo, tile, 0.0), jnp.where(lo, rolled, 0.0)
        else:
            even, odd = jnp.where(lo, 0.0, rolled), jnp.where(lo, 0.0, tile)
        qexp_ref[2 * t] = even
        qexp_ref[2 * t + 1] = odd
    knew_ref[...] = pa[:, ATTN_WIDTH:ATTN_WIDTH + KV_WIDTH]
    vnew_ref[...] = pa[:, ATTN_WIDTH + KV_WIDTH:ATTN_COLS]

    cw = cw_ref[...]
    y = None
    for r in range(CONV_W - 1):
        term = cs_ref[:, r * GDN_CONV_DIM:(r + 1) * GDN_CONV_DIM] * cw[r:r + 1]
        y = term if y is None else y + term
    y = y + pg * cw[CONV_W - 1:CONV_W]
    act = y * _sigmoid(y)
    cs_out_ref[:, 0:(CONV_W - 2) * GDN_CONV_DIM] = cs_ref[:, GDN_CONV_DIM:(CONV_W - 1) * GDN_CONV_DIM]
    cs_out_ref[:, (CONV_W - 2) * GDN_CONV_DIM:] = pg

    for hd in range(GDN_HEADS):
        qn = _l2n(act[:, hd * LANES:(hd + 1) * LANES]) * (GDN_DK ** -0.5)
        kn = _l2n(act[:, GDN_KEY_WIDTH + hd * LANES:GDN_KEY_WIDTH + (hd + 1) * LANES])
        qg_ref[:, hd * LANES:(hd + 1) * LANES] = qn
        kt_ref[hd] = kn.T
    vg_ref[...] = act[:, 2 * GDN_KEY_WIDTH:]
    gb = _gates(ab, alog_ref[...], dtb_ref[...])
    lane = lax.broadcasted_iota(jnp.int32, (1, LANES), 1)
    gate_ref[...] = jnp.where(lane < GDN_HEADS, jnp.exp(gb), gb)


def _sample_prep(x, nw, w, cs, cw, alog, dtb):
    n = x.shape[0]
    return pl.pallas_call(
        _sample_prep_kernel,
        out_shape=[jax.ShapeDtypeStruct((ATTN_HEADS, n, LANES), F32),
                   jax.ShapeDtypeStruct((n, KV_WIDTH), F32),
                   jax.ShapeDtypeStruct((n, KV_WIDTH), F32),
                   jax.ShapeDtypeStruct((n, GDN_KEY_WIDTH), F32),
                   jax.ShapeDtypeStruct((GDN_HEADS, GDN_DK, n), F32),
                   jax.ShapeDtypeStruct((n, GDN_VAL_WIDTH), F32),
                   jax.ShapeDtypeStruct((n, LANES), F32),
                   jax.ShapeDtypeStruct((n, GDN_VAL_WIDTH), F32),
                   jax.ShapeDtypeStruct((n, (CONV_W - 1) * GDN_CONV_DIM), F32)],
        compiler_params=pltpu.CompilerParams(vmem_limit_bytes=VMEM_LIMIT),
        name="sample_prep",
    )(x, nw, w, cs, cw, alog, dtb)


ATTN_BB = 32


def _attn_sample_kernel(sink_ref, qexp_ref, knew_ref, vnew_ref, kbuf_ref, vbuf_ref,
                        o_ref, kout_ref, vout_ref, q_scr, o_scr):
    for h in range(ATTN_HEADS):
        q_scr[pl.ds(h, ATTN_BB, stride=ATTN_HEADS), :] = qexp_ref[h]
    hrow = lax.broadcasted_iota(jnp.int32, (ATTN_HEADS, 1), 0)
    sink = jnp.zeros((ATTN_HEADS, 1), F32)
    slope = jnp.zeros((ATTN_HEADS, 1), F32)
    for h in range(ATTN_HEADS):
        sink = jnp.where(hrow == h, sink_ref[0, h], sink)
        slope = jnp.where(hrow == h, _alibi_slope(h), slope)
    pos = lax.broadcasted_iota(jnp.int32, (1, WINDOW), 1)
    bias = slope * (WINDOW - pos).astype(F32)
    rowid = lax.broadcasted_iota(jnp.int32, (WINDOW, 1), 0)
    last = rowid == WINDOW - 1
    toks = range(ATTN_BB)
    qe = [q_scr[b * ATTN_HEADS:(b + 1) * ATTN_HEADS, :] for b in toks]
    s_old = [_dot_nt(qe[b], kbuf_ref[b]) * (HEAD_DIM ** -0.5) - bias for b in toks]
    probs, p_new = [], []
    for b in toks:
        s_new = jnp.sum(qe[b] * knew_ref[b:b + 1, :], axis=-1, keepdims=True) * (HEAD_DIM ** -0.5)
        mx = jnp.maximum(jnp.maximum(jnp.max(s_old[b], axis=-1, keepdims=True), s_new), sink)
        p_old = jnp.exp(s_old[b] - mx)
        pn = jnp.exp(s_new - mx)
        den = jnp.sum(p_old, axis=-1, keepdims=True) + pn + jnp.exp(sink - mx)
        probs.append(p_old / den)
        p_new.append(pn / den)
    for b in toks:
        o_scr[b * ATTN_HEADS:(b + 1) * ATTN_HEADS, :] = (_dot(probs[b], vbuf_ref[b])
                                                         + p_new[b] * vnew_ref[b:b + 1, :])
    for b in toks:
        kout_ref[b] = jnp.where(last, knew_ref[b:b + 1, :], pltpu.roll(kbuf_ref[b], WINDOW - 1, axis=0))
        vout_ref[b] = jnp.where(last, vnew_ref[b:b + 1, :], pltpu.roll(vbuf_ref[b], WINDOW - 1, axis=0))
    lo = lax.broadcasted_iota(jnp.int32, (1, LANES), 1) < HEAD_DIM
    for t in range(ATTN_WIDTH // LANES):
        even = o_scr[pl.ds(2 * t, ATTN_BB, stride=ATTN_HEADS), :]
        odd = o_scr[pl.ds(2 * t + 1, ATTN_BB, stride=ATTN_HEADS), :]
        if t < ATTN_WIDTH // LANES // KV_HEADS:
            tile = jnp.where(lo, even, pltpu.roll(odd, HEAD_DIM, axis=1))
        else:
            tile = jnp.where(lo, pltpu.roll(even, HEAD_DIM, axis=1), odd)
        o_ref[:, t * LANES:(t + 1) * LANES] = tile.astype(o_ref.dtype)


def _attn_sample(sinks, qexp, knew, vnew, kbuf, vbuf):
    n = knew.shape[0]
    tok2 = lambda i: (i, 0)
    tok3 = lambda i: (i, 0, 0)
    return pl.pallas_call(
        _attn_sample_kernel,
        grid=(n // ATTN_BB,),
        in_specs=[pl.BlockSpec(memory_space=pltpu.SMEM),
                  pl.BlockSpec((ATTN_HEADS, ATTN_BB, LANES), lambda i: (0, i, 0)),
                  pl.BlockSpec((ATTN_BB, KV_WIDTH), tok2),
                  pl.BlockSpec((ATTN_BB, KV_WIDTH), tok2),
                  pl.BlockSpec((ATTN_BB, WINDOW, KV_WIDTH), tok3),
                  pl.BlockSpec((ATTN_BB, WINDOW, KV_WIDTH), tok3)],
        out_specs=[pl.BlockSpec((ATTN_BB, ATTN_WIDTH), tok2),
                   pl.BlockSpec((ATTN_BB, WINDOW, KV_WIDTH), tok3),
                   pl.BlockSpec((ATTN_BB, WINDOW, KV_WIDTH), tok3)],
        out_shape=[jax.ShapeDtypeStruct((n, ATTN_WIDTH), BF16),
                   jax.ShapeDtypeStruct((n, WINDOW, KV_WIDTH), F32),
                   jax.ShapeDtypeStruct((n, WINDOW, KV_WIDTH), F32)],
        scratch_shapes=[pltpu.VMEM((ATTN_BB * ATTN_HEADS, LANES), F32),
                        pltpu.VMEM((ATTN_BB * ATTN_HEADS, LANES), F32)],
        compiler_params=pltpu.CompilerParams(dimension_semantics=("arbitrary",),
                                             vmem_limit_bytes=VMEM_LIMIT),
        name="attn_sample",
    )(sinks, qexp, knew, vnew, kbuf, vbuf)


GDN_BB = 16


def _gdn_sample_kernel(gate_ref, q_ref, kt_ref, vg_ref, s_ref, o_ref, s_out_ref):
    i = pl.program_id(0)
    n = kt_ref.shape[-1]
    shift = (n - i * GDN_BB) % n
    units = [(j, h) for h in range(GDN_HEADS) for j in range(GDN_BB)]
    kt = [pltpu.roll(kt_ref[h], shift, axis=1) for h in range(GDN_HEADS)]
    kb = [jnp.broadcast_to(kt[h][:, j:j + 1], (GDN_DK, GDN_DV)) for j, h in units]
    s = [s_ref[j, h] * gate_ref[(i * GDN_BB + j) * SUBLANES + h] for j, h in units]
    kv = [jnp.sum(su * k, axis=0, keepdims=True) for su, k in zip(s, kb)]
    u = [(vg_ref[j:j + 1, h * LANES:(h + 1) * LANES] - kvu)
         * gate_ref[(i * GDN_BB + j) * SUBLANES + GDN_HEADS + h] for (j, h), kvu in zip(units, kv)]
    s = [su + k * uu for su, k, uu in zip(s, kb, u)]
    for (j, h), su in zip(units, s):
        s_out_ref[j, h] = su
    for (j, h), su in zip(units, s):
        lanes = slice(h * LANES, (h + 1) * LANES)
        o_ref[j:j + 1, lanes] = _dot(q_ref[:, lanes], su)[j:j + 1, :]


def _gdn_sample(gate, q, kt, vg, s0):
    n = vg.shape[0]
    return pl.pallas_call(
        _gdn_sample_kernel,
        grid=(n // GDN_BB,),
        in_specs=[pl.BlockSpec(memory_space=pltpu.SMEM),
                  pl.BlockSpec((GDN_BB, GDN_KEY_WIDTH), lambda i: (i, 0)),
                  pl.BlockSpec((GDN_HEADS, GDN_DK, n), lambda i: (0, 0, 0)),
                  pl.BlockSpec((GDN_BB, GDN_VAL_WIDTH), lambda i: (i, 0)),
                  pl.BlockSpec((GDN_BB, GDN_HEADS, GDN_DK, GDN_DV), lambda i: (i, 0, 0, 0))],
        out_specs=[pl.BlockSpec((GDN_BB, GDN_VAL_WIDTH), lambda i: (i, 0)),
                   pl.BlockSpec((GDN_BB, GDN_HEADS, GDN_DK, GDN_DV), lambda i: (i, 0, 0, 0))],
        out_shape=[jax.ShapeDtypeStruct((n, GDN_VAL_WIDTH), F32),
                   jax.ShapeDtypeStruct((n, GDN_HEADS, GDN_DK, GDN_DV), F32)],
        compiler_params=pltpu.CompilerParams(dimension_semantics=("arbitrary",),
                                             vmem_limit_bytes=VMEM_LIMIT),
        name="gdn_sample",
    )(gate, q, kt, vg, s0)


def _pad_lanes(v):
    return jnp.pad(v.astype(F32), (0, LANES - v.shape[0])).reshape(1, LANES)


def kernel(x_prompt, x_sample, state_conv, cache_win_k, cache_win_v, state_gdn, norm_mix_pre, w_in,
           attn_sinks, conv_w, gdn_a_log, gdn_dt_bias, gdn_norm, w_out, norm_mix_post, norm_ffn_pre,
           w_up, w_down, norm_ffn_post):
    assert w_in.shape[0] == 1, "single-layer trunk"
    bsz, seq, _ = x_prompt.shape
    nsamp = x_sample.shape[0]

    w_in_p = jnp.pad(w_in[0], ((0, 0), (0, IN_COLS - w_in.shape[2]))).astype(BF16)
    n_pre = norm_mix_pre[0].reshape(1, D_MODEL)
    n_post = norm_mix_post[0].reshape(1, D_MODEL)
    n_fpre = norm_ffn_pre[0].reshape(1, D_MODEL)
    n_fpost = norm_ffn_post[0].reshape(1, D_MODEL)
    sinks = attn_sinks[0].reshape(1, ATTN_HEADS)
    cw = conv_w[0]
    alog = _pad_lanes(gdn_a_log[0])
    dtb = _pad_lanes(gdn_dt_bias[0])
    gnw = gdn_norm[0].reshape(1, GDN_DV)

    xp = x_prompt.reshape(bsz * seq, D_MODEL)
    pa, act, pz, gb, p_conv, w_out_b, w_up_b, w_down_b = _inproj(
        xp, n_pre, w_in_p, cw, alog, dtb, w_out[0], w_up[0], w_down[0], INPROJ_T, seq)
    yp, p_s = _mix_ffn(sinks, pa, act, gb, xp, pz, gnw, w_out_b, n_post, n_fpre, w_up_b, w_down_b,
                       n_fpost, seq)
    pa = pa.reshape(bsz, seq, ATTN_COLS)

    xs = x_sample.reshape(nsamp, D_MODEL)
    cs = state_conv[0].reshape(nsamp, (CONV_W - 1) * GDN_CONV_DIM)
    qexp, knew, vnew, qg, kt, vg, gate, zs, cs_new = _sample_prep(xs, n_pre, w_in_p, cs, cw, alog, dtb)
    kbuf = cache_win_k[0].reshape(nsamp, WINDOW, KV_WIDTH)
    vbuf = cache_win_v[0].reshape(nsamp, WINDOW, KV_WIDTH)
    ao_s, k_new_cache, v_new_cache = _attn_sample(sinks, qexp, knew, vnew, kbuf, vbuf)
    go_s, s_new = _gdn_sample(gate[:, :SUBLANES].reshape(nsamp * SUBLANES), qg, kt, vg, state_gdn[0])
    ys = _out_ffn(xs, ao_s, go_s, zs, gnw, w_out_b, n_post, n_fpre, w_up_b, w_down_b, n_fpost, nsamp)

    kv_shape = (1, -1, WINDOW, KV_HEADS, HEAD_DIM)
    return (yp.reshape(bsz, seq, D_MODEL),
            ys.reshape(nsamp, 1, D_MODEL),
            p_conv[:, SUBLANES - (CONV_W - 1):, :][None],
            pa[:, seq - WINDOW:, ATTN_WIDTH:ATTN_WIDTH + KV_WIDTH].reshape(kv_shape),
            pa[:, seq - WINDOW:, ATTN_WIDTH + KV_WIDTH:].reshape(kv_shape),
            p_s[None],
            cs_new.reshape(1, nsamp, CONV_W - 1, GDN_CONV_DIM),
            k_new_cache.reshape(kv_shape),
            v_new_cache.reshape(kv_shape),
            s_new[None])
```

```python
import functools

import jax
import jax.numpy as jnp
from jax import lax
from jax.experimental import pallas as pl
from jax.experimental.pallas import tpu as pltpu

F32 = jnp.float32
BF16 = jnp.bfloat16

D_MODEL = 1024
ATTN_HEADS = 8
KV_HEADS = 2
HEAD_DIM = 64
GQA_GROUP = ATTN_HEADS // KV_HEADS
ATTN_WIDTH = ATTN_HEADS * HEAD_DIM
KV_WIDTH = KV_HEADS * HEAD_DIM
WINDOW = 128
GDN_HEADS = 4
GDN_DK = 128
GDN_DV = 128
GDN_KEY_WIDTH = GDN_HEADS * GDN_DK
GDN_VAL_WIDTH = GDN_HEADS * GDN_DV
GDN_CONV_DIM = 2 * GDN_KEY_WIDTH + GDN_VAL_WIDTH
CONV_W = 4
D_FF = 4 * D_MODEL
EPS = 1e-6

LANES = 128
SUBLANES = 8
ATTN_COLS = ATTN_WIDTH + 2 * KV_WIDTH
AB_COLS = LANES
IN_COLS = ATTN_COLS + GDN_CONV_DIM + GDN_VAL_WIDTH + AB_COLS
OFF_GDN = ATTN_COLS
OFF_Z = OFF_GDN + GDN_CONV_DIM
OFF_AB = OFF_Z + GDN_VAL_WIDTH
CHUNK = 128
V7X_VMEM_BYTES = 64 * 1024 * 1024
VMEM_LIMIT = V7X_VMEM_BYTES - 8 * 1024 * 1024
VMEM_LIMIT_MIX = V7X_VMEM_BYTES - 4 * 1024 * 1024


def _rms(x, w):
    return x * lax.rsqrt(jnp.mean(x * x, axis=-1, keepdims=True) + EPS) * w


def _sigmoid(x):
    return 1.0 / (1.0 + jnp.exp(-x))


def _softplus(x):
    return jnp.maximum(x, 0.0) + jnp.log(1.0 + jnp.exp(-jnp.abs(x)))


def _dot(a, b):
    return jnp.dot(a.astype(BF16), b.astype(BF16), preferred_element_type=F32)


def _dot_nt(a, b):
    return lax.dot_general(a.astype(BF16), b.astype(BF16), (((1,), (1,)), ((), ())),
                           preferred_element_type=F32)


def _alibi_slope(h):
    return 2.0 ** (-8.0 * (h + 1) / ATTN_HEADS)


def _causal_conv(x, carry, w):
    t, c = x.shape
    x3 = x.reshape(t // SUBLANES, SUBLANES, c)
    row8 = lax.broadcasted_iota(jnp.int32, (1, SUBLANES, 1), 1)
    y = None
    for shift in range(CONV_W - 1, -1, -1):
        if shift == 0:
            xs = x3
        else:
            xr = pltpu.roll(x3, shift, axis=1)
            cr = pltpu.roll(carry, shift, axis=0)
            prev = jnp.concatenate([cr[None], xr[:-1]], axis=0)
            xs = jnp.where(row8 < shift, prev, xr)
        term = xs * w[CONV_W - 1 - shift:CONV_W - shift][None]
        y = term if y is None else y + term
    return y.reshape(t, c)


def _silu(y):
    return y * _sigmoid(y)


def _l2n(x):
    return x * lax.rsqrt(jnp.sum(x * x, axis=-1, keepdims=True) + EPS)


def _gates(ab, alog, dtb):
    g = -jnp.exp(alog) * _softplus(ab + dtb)
    lane = lax.broadcasted_iota(jnp.int32, (1, LANES), 1)
    return jnp.where(lane < GDN_HEADS, g, _sigmoid(ab))


INPROJ_T = 1024
INPROJ_SUB = 128


def _inproj_kernel(x_ref, nw_ref, w_ref, cw_ref, alog_ref, dtb_ref, wout_ref, wup_ref, wdown_ref,
                   oa_ref, og_ref, oz_ref, ogb_ref, conv_out_ref, wout_b_ref, wup_b_ref, wdown_b_ref,
                   carry_scr, *, tiles_per_seq):
    tm = x_ref.shape[0]
    wout_b_ref[...] = wout_ref[...].astype(BF16)
    wup_b_ref[...] = wup_ref[...].astype(BF16)
    wdown_b_ref[...] = wdown_ref[...].astype(BF16)

    @pl.when(lax.rem(pl.program_id(0), tiles_per_seq) == 0)
    def _():
        carry_scr[...] = jnp.zeros_like(carry_scr)

    carry = carry_scr[...]
    for r0 in range(0, tm, INPROJ_SUB):
        rows = slice(r0, r0 + INPROJ_SUB)
        h = _rms(x_ref[rows, :], nw_ref[...]).astype(BF16)
        pg = jnp.dot(h, w_ref[:, OFF_GDN:OFF_Z], preferred_element_type=F32)
        oa_ref[rows, :] = jnp.dot(h, w_ref[:, 0:OFF_GDN], preferred_element_type=F32)
        oz_ref[rows, :] = jnp.dot(h, w_ref[:, OFF_Z:OFF_AB], preferred_element_type=F32)
        ab = jnp.dot(h, w_ref[:, OFF_AB:IN_COLS], preferred_element_type=F32)
        ogb_ref[rows, :] = _gates(ab, alog_ref[...], dtb_ref[...])
        og_ref[rows, :] = _causal_conv(pg, carry, cw_ref[...])
        carry = pg[INPROJ_SUB - SUBLANES:INPROJ_SUB]
    carry_scr[...] = carry
    conv_out_ref[...] = carry


def _inproj(x, nw, w, cw, alog, dtb, wout, wup, wdown, tm, seq):
    m = x.shape[0]
    steps = m // tm
    tiles_per_seq = seq // tm
    row = lambda i: (i, 0)
    fixed = lambda i: (0, 0)
    slab = lambda a: pl.BlockSpec((a.shape[0] // steps, a.shape[1]), row)
    return pl.pallas_call(
        functools.partial(_inproj_kernel, tiles_per_seq=tiles_per_seq),
        grid=(steps,),
        in_specs=[pl.BlockSpec((tm, D_MODEL), row),
                  pl.BlockSpec((1, D_MODEL), fixed),
                  pl.BlockSpec((D_MODEL, IN_COLS), fixed),
                  pl.BlockSpec((CONV_W, GDN_CONV_DIM), fixed),
                  pl.BlockSpec((1, LANES), fixed),
                  pl.BlockSpec((1, LANES), fixed),
                  slab(wout), slab(wup), slab(wdown)],
        out_specs=[pl.BlockSpec((tm, ATTN_COLS), row),
                   pl.BlockSpec((tm, GDN_CONV_DIM), row),
                   pl.BlockSpec((tm, GDN_VAL_WIDTH), row),
                   pl.BlockSpec((tm, AB_COLS), row),
                   pl.BlockSpec((None, SUBLANES, GDN_CONV_DIM), lambda i: (i // tiles_per_seq, 0, 0)),
                   slab(wout), slab(wup), slab(wdown)],
        out_shape=[jax.ShapeDtypeStruct((m, ATTN_COLS), F32),
                   jax.ShapeDtypeStruct((m, GDN_CONV_DIM), F32),
                   jax.ShapeDtypeStruct((m, GDN_VAL_WIDTH), F32),
                   jax.ShapeDtypeStruct((m, AB_COLS), F32),
                   jax.ShapeDtypeStruct((m // seq, SUBLANES, GDN_CONV_DIM), F32),
                   jax.ShapeDtypeStruct(wout.shape, BF16),
                   jax.ShapeDtypeStruct(wup.shape, BF16),
                   jax.ShapeDtypeStruct(wdown.shape, BF16)],
        scratch_shapes=[pltpu.VMEM((SUBLANES, GDN_CONV_DIM), F32)],
        compiler_params=pltpu.CompilerParams(dimension_semantics=("arbitrary",),
                                             vmem_limit_bytes=VMEM_LIMIT),
        name="inproj",
    )(x, nw, w, cw, alog, dtb, wout, wup, wdown)


FF_UP_CHUNK = 256
FF_DOWN_CHUNK = 256
FF_STAGES = D_FF // FF_UP_CHUNK + D_MODEL // FF_DOWN_CHUNK


def _out_ffn_stages(x_ref, ao_ref, go_ref, z_ref, gnw_ref, wout_ref, npost_ref, npre_ref, wup_ref,
                    wdown_ref, nffn_ref, y_ref):
    gnw = gnw_ref[...]
    gy = []
    for hd in range(GDN_HEADS):
        lanes = slice(hd * LANES, (hd + 1) * LANES)
        z = z_ref[:, lanes]
        gy.append((_rms(go_ref[:, lanes], gnw) * (z * _sigmoid(z))).astype(BF16))
    m = (jnp.dot(ao_ref[...], wout_ref[0:ATTN_WIDTH, :], preferred_element_type=F32)
         + jnp.dot(jnp.concatenate(gy, axis=1), wout_ref[ATTN_WIDTH:, :], preferred_element_type=F32))
    x1 = x_ref[...] + _rms(m, npost_ref[...])
    h = _rms(x1, npre_ref[...]).astype(BF16)
    yield
    hidden = []
    for j in range(0, D_FF, FF_UP_CHUNK):
        u = jnp.maximum(jnp.dot(h, wup_ref[:, j:j + FF_UP_CHUNK], preferred_element_type=F32), 0.0)
        hidden.append((u * u).astype(BF16))
        yield
    hidden = jnp.concatenate(hidden, axis=1)
    f = []
    for j in range(0, D_MODEL, FF_DOWN_CHUNK):
        f.append(jnp.dot(hidden, wdown_ref[:, j:j + FF_DOWN_CHUNK], preferred_element_type=F32))
        yield
    y_ref[...] = x1 + _rms(jnp.concatenate(f, axis=1), nffn_ref[...])


def _dup_half(x, lo):
    xr = pltpu.roll(x, HEAD_DIM, axis=1)
    return jnp.where(lo, x, xr), jnp.where(lo, xr, x)


def _attn_prompt_stages(sink_ref, cur_ref, prev_ref, seq_row0, o_ref):
    tq = cur_ref.shape[0]
    kall = jnp.concatenate([prev_ref[:, 0:KV_WIDTH], cur_ref[:, ATTN_WIDTH:ATTN_WIDTH + KV_WIDTH]], axis=0)
    vall = jnp.concatenate([prev_ref[:, KV_WIDTH:], cur_ref[:, ATTN_WIDTH + KV_WIDTH:]], axis=0)
    lo = lax.broadcasted_iota(jnp.int32, (1, LANES), 1) < HEAD_DIM
    k2 = _dup_half(kall, lo)
    v2 = _dup_half(vall, lo)
    qi = lax.broadcasted_iota(jnp.int32, (WINDOW, 2 * WINDOW), 0)
    si = lax.broadcasted_iota(jnp.int32, (WINDOW, 2 * WINDOW), 1)
    dist = WINDOW + qi - si
    band = (dist >= 0) & (dist <= WINDOW)
    distf = dist.astype(F32)
    for r in range(tq // WINDOW):
        rows = slice(r * WINDOW, (r + 1) * WINDOW)
        valid = band
        if r == 0:
            valid = band & ((seq_row0 - WINDOW + si) >= 0)
        for j in range(KV_HEADS):
            kw = k2[j][r * WINDOW:(r + 2) * WINDOW]
            vw = v2[j][r * WINDOW:(r + 2) * WINDOW]
            t0 = cur_ref[rows, (2 * j) * LANES:(2 * j + 1) * LANES]
            t1 = cur_ref[rows, (2 * j + 1) * LANES:(2 * j + 2) * LANES]
            lhs = jnp.concatenate([jnp.where(lo, t0, 0.0), jnp.where(lo, 0.0, t0),
                                   jnp.where(lo, t1, 0.0), jnp.where(lo, 0.0, t1)], axis=0)
            s = _dot_nt(lhs, kw) * (HEAD_DIM ** -0.5)
            probs = []
            for g in range(GQA_GROUP):
                h = j * GQA_GROUP + g
                sg = s[g * WINDOW:(g + 1) * WINDOW] - _alibi_slope(h) * distf
                sg = jnp.where(valid, sg, -jnp.inf)
                sink = sink_ref[0, h]
                mx = jnp.maximum(jnp.max(sg, axis=-1, keepdims=True), sink)
                p = jnp.exp(sg - mx)
                den = jnp.sum(p, axis=-1, keepdims=True) + jnp.exp(sink - mx)
                probs.append(p / den)
            yield
            o = _dot(jnp.concatenate(probs, axis=0), vw)
            o01 = jnp.where(lo, o[0:WINDOW], o[WINDOW:2 * WINDOW])
            o23 = jnp.where(lo, o[2 * WINDOW:3 * WINDOW], o[3 * WINDOW:4 * WINDOW])
            o_ref[rows, (2 * j) * LANES:(2 * j + 1) * LANES] = o01.astype(o_ref.dtype)
            o_ref[rows, (2 * j + 1) * LANES:(2 * j + 2) * LANES] = o23.astype(o_ref.dtype)
            yield


def _gdn_prompt_stages(conv_ref, gb_ref, state, o_ref):
    t = conv_ref.shape[0]
    beta = gb_ref[...]
    g = beta
    rin = lax.broadcasted_iota(jnp.int32, (t, 1), 0) & (CHUNK - 1)
    step = 1
    while step < CHUNK:
        g = g + jnp.where(rin >= step, pltpu.roll(g, step, axis=0), 0.0)
        step *= 2
    gt = g.T

    row = lax.broadcasted_iota(jnp.int32, (CHUNK, CHUNK), 0)
    col = lax.broadcasted_iota(jnp.int32, (CHUNK, CHUNK), 1)
    lower_incl = row >= col
    lower_strict = row > col

    heads = range(GDN_HEADS)
    units = [(c, h) for c in range(t // CHUNK) for h in heads]
    pre = []
    for c, h in units:
        rows = slice(c * CHUNK, (c + 1) * CHUNK)
        qc = _l2n(_silu(conv_ref[rows, h * LANES:(h + 1) * LANES])) * (GDN_DK ** -0.5)
        kc = _l2n(_silu(conv_ref[rows, GDN_KEY_WIDTH + h * LANES:GDN_KEY_WIDTH + (h + 1) * LANES]))
        vc = _silu(conv_ref[rows, 2 * GDN_KEY_WIDTH + h * LANES:2 * GDN_KEY_WIDTH + (h + 1) * LANES])
        gc = g[rows, h:h + 1]
        bc = beta[rows, GDN_HEADS + h:GDN_HEADS + h + 1]
        gr = gt[h:h + 1, c * CHUNK:(c + 1) * CHUNK]
        decay = jnp.where(lower_incl, jnp.exp(jnp.where(lower_incl, gc - gr, 0.0)), 0.0)
        eg = jnp.exp(gc)
        glast = gc[CHUNK - 1:CHUNK]
        rhs = jnp.concatenate([vc * bc, kc * (bc * eg)], axis=1)
        k_dec_t = (kc * jnp.exp(glast - gc)).T
        kq = _dot_nt(jnp.concatenate([kc, qc], axis=0), kc)
        a = jnp.where(lower_strict, bc * kq[0:CHUNK] * decay, 0.0)
        qk = jnp.where(lower_incl, kq[CHUNK:] * decay, 0.0)
        pre.append(dict(a=a, rhs=rhs, lhs_u=jnp.concatenate([qk, k_dec_t], axis=0), q_dec=qc * eg,
                        ld=jnp.exp(glast)))
        if h == GDN_HEADS - 1:
            yield

    mats = [p["a"] for p in pre]
    first = ((row >> 1) == (col >> 1)) & ((row & 1) == 1) & ((col & 1) == 0)
    xo = [jnp.where(first, -a, 0.0) for a in mats]
    size = 2
    while size < CHUNK:
        shift = size.bit_length()
        m = ((row >> shift) == (col >> shift)) & ((row & size) != 0) & ((col & size) == 0)
        aoff = [jnp.where(m, a, 0.0) for a in mats]
        y = [ao + _dot(ao, x) for ao, x in zip(aoff, xo)]
        yield
        xo = [x - yy - _dot(x, yy) for x, yy in zip(xo, y)]
        size *= 2
        yield
    trs = [p["rhs"] + _dot(x, p["rhs"]) for p, x in zip(pre, xo)]
    yield

    for c in range(t // CHUNK):
        rows = slice(c * CHUNK, (c + 1) * CHUNK)
        idx = [c * GDN_HEADS + h for h in heads]
        ws = [_dot(jnp.concatenate([trs[u][:, GDN_DV:], pre[u]["q_dec"]], axis=0), state[h])
              for h, u in zip(heads, idx)]
        yield
        ou = [_dot(pre[u]["lhs_u"], trs[u][:, 0:GDN_DV] - ws[h][0:CHUNK]) for h, u in zip(heads, idx)]
        for h, u in zip(heads, idx):
            o_ref[rows, h * LANES:(h + 1) * LANES] = ws[h][CHUNK:] + ou[h][0:CHUNK]
            state[h] = state[h] * pre[u]["ld"] + ou[h][CHUNK:]
        yield


def _run_interleaved(primary, n_primary, secondary, n_secondary):
    done = object()
    issued = 0
    for k in range(n_primary):
        next(primary, done)
        target = ((k + 1) * n_secondary + n_primary - 1) // n_primary
        while issued < target:
            next(secondary, done)
            issued += 1
    for _ in primary:
        pass
    for _ in secondary:
        pass


def _roundrobin(*gens):
    done = object()
    live = list(gens)
    while live:
        for gen in list(live):
            if next(gen, done) is done:
                live.remove(gen)
            else:
                yield


MIX_T = 512


def _mix_ffn_kernel(sink_ref, pa_ref, prev_ref, conv_ref, gb_ref, x_ref, z_ref, gnw_ref, wout_ref,
                    npost_ref, npre_ref, wup_ref, wdown_ref, nffn_ref,
                    y_ref, s_out_ref, s_scr, ao_scr, go_scr, *, n_tiles, tiles_per_seq):
    i = pl.program_id(0)

    @pl.when(i == 0)
    def _():
        s_scr[...] = jnp.zeros_like(s_scr)
        ao_scr[...] = jnp.zeros_like(ao_scr)
        go_scr[...] = jnp.zeros_like(go_scr)

    front_tile = jnp.minimum(i, n_tiles - 1)
    tile_in_seq = lax.rem(front_tile, tiles_per_seq)
    seq_start = tile_in_seq == 0
    front_valid = i < n_tiles

    ffn = _out_ffn_stages(x_ref, ao_scr, go_scr, z_ref, gnw_ref, wout_ref, npost_ref, npre_ref,
                          wup_ref, wdown_ref, nffn_ref, y_ref)
    next(ffn)

    old = [s_scr[h] for h in range(GDN_HEADS)]
    state = [jnp.where(seq_start, 0.0, s) for s in old]
    front = _roundrobin(_gdn_prompt_stages(conv_ref, gb_ref, state, go_scr),
                        _attn_prompt_stages(sink_ref, pa_ref, prev_ref, tile_in_seq * MIX_T, ao_scr))
    mix_stages = (MIX_T // CHUNK) * 3 + 2 * (CHUNK.bit_length() - 2) + 1 + 2 * (MIX_T // WINDOW) * KV_HEADS
    _run_interleaved(ffn, FF_STAGES, front, mix_stages)

    for h in range(GDN_HEADS):
        s_new = jnp.where(front_valid, state[h], old[h])
        s_scr[h] = s_new
        s_out_ref[h] = s_new


def _mix_ffn(sinks, pa, act, gb, x, z, gnw, wout, npost, npre, wup, wdown, nffn, seq):
    m = x.shape[0]
    n_tiles = m // MIX_T
    tiles_per_seq = seq // MIX_T
    blocks_per_tile = MIX_T // WINDOW
    front = lambda i: (jnp.minimum(i, n_tiles - 1), 0)
    back = lambda i: (jnp.maximum(i - 1, 0), 0)
    fixed = lambda i: (0, 0)
    resident = dict(pipeline_mode=pl.Buffered(1))
    kern = functools.partial(_mix_ffn_kernel, n_tiles=n_tiles, tiles_per_seq=tiles_per_seq)
    return pl.pallas_call(
        kern,
        grid=(n_tiles + 1,),
        in_specs=[pl.BlockSpec(memory_space=pltpu.SMEM),
                  pl.BlockSpec((MIX_T, ATTN_COLS), front),
                  pl.BlockSpec((WINDOW, 2 * KV_WIDTH),
                               lambda i: (jnp.maximum(jnp.minimum(i, n_tiles - 1) * blocks_per_tile - 1, 0),
                                          ATTN_WIDTH // (2 * KV_WIDTH))),
                  pl.BlockSpec((MIX_T, GDN_CONV_DIM), front),
                  pl.BlockSpec((MIX_T, AB_COLS), front),
                  pl.BlockSpec((MIX_T, D_MODEL), back),
                  pl.BlockSpec((MIX_T, GDN_VAL_WIDTH), back),
                  pl.BlockSpec((1, GDN_DV), fixed),
                  pl.BlockSpec((D_MODEL, D_MODEL), fixed, **resident),
                  pl.BlockSpec((1, D_MODEL), fixed),
                  pl.BlockSpec((1, D_MODEL), fixed),
                  pl.BlockSpec((D_MODEL, D_FF), fixed, **resident),
                  pl.BlockSpec((D_FF, D_MODEL), fixed, **resident),
                  pl.BlockSpec((1, D_MODEL), fixed)],
        out_specs=[pl.BlockSpec((MIX_T, D_MODEL), back),
                   pl.BlockSpec((None, GDN_HEADS, GDN_DK, GDN_DV),
                                lambda i: (jnp.minimum(i, n_tiles - 1) // tiles_per_seq, 0, 0, 0))],
        out_shape=[jax.ShapeDtypeStruct((m, D_MODEL), F32),
                   jax.ShapeDtypeStruct((m // seq, GDN_HEADS, GDN_DK, GDN_DV), F32)],
        scratch_shapes=[pltpu.VMEM((GDN_HEADS, GDN_DK, GDN_DV), F32),
                        pltpu.VMEM((MIX_T, ATTN_WIDTH), BF16),
                        pltpu.VMEM((MIX_T, GDN_VAL_WIDTH), F32)],
        compiler_params=pltpu.CompilerParams(dimension_semantics=("arbitrary",),
                                             vmem_limit_bytes=VMEM_LIMIT_MIX),
        name="mix_ffn",
    )(sinks, pa, pa, act, gb, x, z, gnw, wout, npost, npre, wup, wdown, nffn)


def _out_ffn_kernel(*refs):
    for _ in _out_ffn_stages(*refs):
        pass


def _out_ffn(x, ao, go, z, gnw, wout, npost, npre, wup, wdown, nffn, tm):
    m = x.shape[0]
    row = lambda i: (i, 0)
    fixed = lambda i: (0, 0)
    return pl.pallas_call(
        _out_ffn_kernel,
        grid=(m // tm,),
        in_specs=[pl.BlockSpec((tm, D_MODEL), row),
                  pl.BlockSpec((tm, ATTN_WIDTH), row),
                  pl.BlockSpec((tm, GDN_VAL_WIDTH), row),
                  pl.BlockSpec((tm, GDN_VAL_WIDTH), row),
                  pl.BlockSpec((1, GDN_DV), fixed),
                  pl.BlockSpec((D_MODEL, D_MODEL), fixed, pipeline_mode=pl.Buffered(1)),
                  pl.BlockSpec((1, D_MODEL), fixed),
                  pl.BlockSpec((1, D_MODEL), fixed),
                  pl.BlockSpec((D_MODEL, D_FF), fixed, pipeline_mode=pl.Buffered(1)),
                  pl.BlockSpec((D_FF, D_MODEL), fixed, pipeline_mode=pl.Buffered(1)),
                  pl.BlockSpec((1, D_MODEL), fixed)],
        out_specs=pl.BlockSpec((tm, D_MODEL), row),
        out_shape=jax.ShapeDtypeStruct((m, D_MODEL), F32),
        compiler_params=pltpu.CompilerParams(dimension_semantics=("arbitrary",),
                                             vmem_limit_bytes=VMEM_LIMIT),
        name="out_ffn",
    )(x, ao, go, z, gnw, wout, npost, npre, wup, wdown, nffn)


def _sample_prep_kernel(x_ref, nw_ref, w_ref, cs_ref, cw_ref, alog_ref, dtb_ref,
                        qexp_ref, knew_ref, vnew_ref, qg_ref, kt_ref, vg_ref, gate_ref, z_ref,
                        cs_out_ref):
    h = _rms(x_ref[...], nw_ref[...]).astype(BF16)
    pa = jnp.dot(h, w_ref[:, 0:OFF_GDN], preferred_element_type=F32)
    pg = jnp.dot(h, w_ref[:, OFF_GDN:OFF_Z], preferred_element_type=F32)
    z_ref[...] = jnp.dot(h, w_ref[:, OFF_Z:OFF_AB], preferred_element_type=F32)
    ab = jnp.dot(h, w_ref[:, OFF_AB:IN_COLS], preferred_element_type=F32)

    lo = lax.broadcasted_iota(jnp.int32, (1, LANES), 1) < HEAD_DIM
    for t in range(ATTN_WIDTH // LANES):
        tile = pa[:, t * LANES:(t + 1) * LANES]
        rolled = pltpu.roll(tile, HEAD_DIM, axis=1)
        if t < ATTN_WIDTH // LANES // KV_HEADS:
            even, odd = jnp.where(lo, tile, 0.0), jnp.where(lo, rolled, 0.0)
        else:
            even, odd = jnp.where(lo, 0.0, rolled), jnp.where(lo, 0.0, tile)
        qexp_ref[2 * t] = even
        qexp_ref[2 * t + 1] = odd
    knew_ref[...] = pa[:, ATTN_WIDTH:ATTN_WIDTH + KV_WIDTH]
    vnew_ref[...] = pa[:, ATTN_WIDTH + KV_WIDTH:ATTN_COLS]

    cw = cw_ref[...]
    y = None
    for r in range(CONV_W - 1):
        term = cs_ref[:, r, :] * cw[r:r + 1]
        y = term if y is None else y + term
    y = y + pg * cw[CONV_W - 1:CONV_W]
    act = y * _sigmoid(y)
    for r in range(CONV_W - 2):
        cs_out_ref[:, r, :] = cs_ref[:, r + 1, :]
    cs_out_ref[:, CONV_W - 2, :] = pg

    for hd in range(GDN_HEADS):
        qn = _l2n(act[:, hd * LANES:(hd + 1) * LANES]) * (GDN_DK ** -0.5)
        kn = _l2n(act[:, GDN_KEY_WIDTH + hd * LANES:GDN_KEY_WIDTH + (hd + 1) * LANES])
        qg_ref[:, hd * LANES:(hd + 1) * LANES] = qn
        kt_ref[hd] = kn.T
    vg_ref[...] = act[:, 2 * GDN_KEY_WIDTH:]
    gb = _gates(ab, alog_ref[...], dtb_ref[...])
    lane = lax.broadcasted_iota(jnp.int32, (1, LANES), 1)
    gate_ref[...] = jnp.where(lane < GDN_HEADS, jnp.exp(gb), gb)


def _sample_prep(x, nw, w, cs, cw, alog, dtb):
    n = x.shape[0]
    return pl.pallas_call(
        _sample_prep_kernel,
        out_shape=[jax.ShapeDtypeStruct((ATTN_HEADS, n, LANES), F32),
                   jax.ShapeDtypeStruct((n, KV_WIDTH), F32),
                   jax.ShapeDtypeStruct((n, KV_WIDTH), F32),
                   jax.ShapeDtypeStruct((n, GDN_KEY_WIDTH), F32),
                   jax.ShapeDtypeStruct((GDN_HEADS, GDN_DK, n), F32),
                   jax.ShapeDtypeStruct((n, GDN_VAL_WIDTH), F32),
                   jax.ShapeDtypeStruct((n, LANES), F32),
                   jax.ShapeDtypeStruct((n, GDN_VAL_WIDTH), F32),
                   jax.ShapeDtypeStruct((n, CONV_W - 1, GDN_CONV_DIM), F32)],
        compiler_params=pltpu.CompilerParams(vmem_limit_bytes=VMEM_LIMIT),
        name="sample_prep",
    )(x, nw, w, cs, cw, alog, dtb)


ATTN_BB = 32


def _attn_sample_kernel(sink_ref, qexp_ref, knew_ref, vnew_ref, kbuf_ref, vbuf_ref,
                        o_ref, kout_ref, vout_ref, q_scr, o_scr):
    for h in range(ATTN_HEADS):
        q_scr[pl.ds(h, ATTN_BB, stride=ATTN_HEADS), :] = qexp_ref[h]
    hrow = lax.broadcasted_iota(jnp.int32, (ATTN_HEADS, 1), 0)
    sink = jnp.zeros((ATTN_HEADS, 1), F32)
    slope = jnp.zeros((ATTN_HEADS, 1), F32)
    for h in range(ATTN_HEADS):
        sink = jnp.where(hrow == h, sink_ref[0, h], sink)
        slope = jnp.where(hrow == h, _alibi_slope(h), slope)
    pos = lax.broadcasted_iota(jnp.int32, (1, WINDOW), 1)
    bias = slope * (WINDOW - pos).astype(F32)
    rowid = lax.broadcasted_iota(jnp.int32, (WINDOW, 1), 0)
    last = rowid == WINDOW - 1
    toks = range(ATTN_BB)
    qe = [q_scr[b * ATTN_HEADS:(b + 1) * ATTN_HEADS, :] for b in toks]
    s_old = [_dot_nt(qe[b], kbuf_ref[b]) * (HEAD_DIM ** -0.5) - bias for b in toks]
    probs, p_new = [], []
    for b in toks:
        s_new = jnp.sum(qe[b] * knew_ref[b:b + 1, :], axis=-1, keepdims=True) * (HEAD_DIM ** -0.5)
        mx = jnp.maximum(jnp.maximum(jnp.max(s_old[b], axis=-1, keepdims=True), s_new), sink)
        p_old = jnp.exp(s_old[b] - mx)
        pn = jnp.exp(s_new - mx)
        den = jnp.sum(p_old, axis=-1, keepdims=True) + pn + jnp.exp(sink - mx)
        probs.append(p_old / den)
        p_new.append(pn / den)
    for b in toks:
        o_scr[b * ATTN_HEADS:(b + 1) * ATTN_HEADS, :] = (_dot(probs[b], vbuf_ref[b])
                                                         + p_new[b] * vnew_ref[b:b + 1, :])
    for b in toks:
        kout_ref[b] = jnp.where(last, knew_ref[b:b + 1, :], pltpu.roll(kbuf_ref[b], WINDOW - 1, axis=0))
        vout_ref[b] = jnp.where(last, vnew_ref[b:b + 1, :], pltpu.roll(vbuf_ref[b], WINDOW - 1, axis=0))
    lo = lax.broadcasted_iota(jnp.int32, (1, LANES), 1) < HEAD_DIM
    for t in range(ATTN_WIDTH // LANES):
        even = o_scr[pl.ds(2 * t, ATTN_BB, stride=ATTN_HEADS), :]
        odd = o_scr[pl.ds(2 * t + 1, ATTN_BB, stride=ATTN_HEADS), :]
        if t < ATTN_WIDTH // LANES // KV_HEADS:
            tile = jnp.where(lo, even, pltpu.roll(odd, HEAD_DIM, axis=1))
        else:
            tile = jnp.where(lo, pltpu.roll(even, HEAD_DIM, axis=1), odd)
        o_ref[:, t * LANES:(t + 1) * LANES] = tile.astype(o_ref.dtype)


def _attn_sample(sinks, qexp, knew, vnew, kbuf, vbuf):
    n = knew.shape[0]
    tok2 = lambda i: (i, 0)
    tok3 = lambda i: (i, 0, 0)
    return pl.pallas_call(
        _attn_sample_kernel,
        grid=(n // ATTN_BB,),
        in_specs=[pl.BlockSpec(memory_space=pltpu.SMEM),
                  pl.BlockSpec((ATTN_HEADS, ATTN_BB, LANES), lambda i: (0, i, 0)),
                  pl.BlockSpec((ATTN_BB, KV_WIDTH), tok2),
                  pl.BlockSpec((ATTN_BB, KV_WIDTH), tok2),
                  pl.BlockSpec((ATTN_BB, WINDOW, KV_WIDTH), tok3),
                  pl.BlockSpec((ATTN_BB, WINDOW, KV_WIDTH), tok3)],
        out_specs=[pl.BlockSpec((ATTN_BB, ATTN_WIDTH), tok2),
                   pl.BlockSpec((ATTN_BB, WINDOW, KV_WIDTH), tok3),
                   pl.BlockSpec((ATTN_BB, WINDOW, KV_WIDTH), tok3)],
        out_shape=[jax.ShapeDtypeStruct((n, ATTN_WIDTH), BF16),
                   jax.ShapeDtypeStruct((n, WINDOW, KV_WIDTH), F32),
                   jax.ShapeDtypeStruct((n, WINDOW, KV_WIDTH), F32)],
        scratch_shapes=[pltpu.VMEM((ATTN_BB * ATTN_HEADS, LANES), F32),
                        pltpu.VMEM((ATTN_BB * ATTN_HEADS, LANES), F32)],
        compiler_params=pltpu.CompilerParams(dimension_semantics=("arbitrary",),
                                             vmem_limit_bytes=VMEM_LIMIT),
        name="attn_sample",
    )(sinks, qexp, knew, vnew, kbuf, vbuf)


GDN_BB = 16


def _gdn_sample_kernel(gate_ref, q_ref, kt_ref, vg_ref, s_ref, o_ref, s_out_ref):
    i = pl.program_id(0)
    n = kt_ref.shape[-1]
    shift = (n - i * GDN_BB) % n
    units = [(j, h) for h in range(GDN_HEADS) for j in range(GDN_BB)]
    kt = [pltpu.roll(kt_ref[h], shift, axis=1) for h in range(GDN_HEADS)]
    kb = [jnp.broadcast_to(kt[h][:, j:j + 1], (GDN_DK, GDN_DV)) for j, h in units]
    s = [s_ref[j, h] * gate_ref[(i * GDN_BB + j) * SUBLANES + h] for j, h in units]
    kv = [jnp.sum(su * k, axis=0, keepdims=True) for su, k in zip(s, kb)]
    u = [(vg_ref[j:j + 1, h * LANES:(h + 1) * LANES] - kvu)
         * gate_ref[(i * GDN_BB + j) * SUBLANES + GDN_HEADS + h] for (j, h), kvu in zip(units, kv)]
    s = [su + k * uu for su, k, uu in zip(s, kb, u)]
    for (j, h), su in zip(units, s):
        s_out_ref[j, h] = su
    for (j, h), su in zip(units, s):
        lanes = slice(h * LANES, (h + 1) * LANES)
        o_ref[j:j + 1, lanes] = _dot(q_ref[:, lanes], su)[j:j + 1, :]


def _gdn_sample(gate, q, kt, vg, s0):
    n = vg.shape[0]
    return pl.pallas_call(
        _gdn_sample_kernel,
        grid=(n // GDN_BB,),
        in_specs=[pl.BlockSpec(memory_space=pltpu.SMEM),
                  pl.BlockSpec((GDN_BB, GDN_KEY_WIDTH), lambda i: (i, 0)),
                  pl.BlockSpec((GDN_HEADS, GDN_DK, n), lambda i: (0, 0, 0)),
                  pl.BlockSpec((GDN_BB, GDN_VAL_WIDTH), lambda i: (i, 0)),
                  pl.BlockSpec((GDN_BB, GDN_HEADS, GDN_DK, GDN_DV), lambda i: (i, 0, 0, 0))],
        out_specs=[pl.BlockSpec((GDN_BB, GDN_VAL_WIDTH), lambda i: (i, 0)),
                   pl.BlockSpec((GDN_BB, GDN_HEADS, GDN_DK, GDN_DV), lambda i: (i, 0, 0, 0))],
        out_shape=[jax.ShapeDtypeStruct((n, GDN_VAL_WIDTH), F32),
                   jax.ShapeDtypeStruct((n, GDN_HEADS, GDN_DK, GDN_DV), F32)],
        compiler_params=pltpu.CompilerParams(dimension_semantics=("arbitrary",),
                                             vmem_limit_bytes=VMEM_LIMIT),
        name="gdn_sample",
    )(gate, q, kt, vg, s0)


def _pad_lanes(v):
    return jnp.pad(v.astype(F32), (0, LANES - v.shape[0])).reshape(1, LANES)


def kernel(x_prompt, x_sample, state_conv, cache_win_k, cache_win_v, state_gdn, norm_mix_pre, w_in,
           attn_sinks, conv_w, gdn_a_log, gdn_dt_bias, gdn_norm, w_out, norm_mix_post, norm_ffn_pre,
           w_up, w_down, norm_ffn_post):
    assert w_in.shape[0] == 1, "single-layer trunk"
    bsz, seq, _ = x_prompt.shape
    nsamp = x_sample.shape[0]

    w_in_p = jnp.pad(w_in[0], ((0, 0), (0, IN_COLS - w_in.shape[2]))).astype(BF16)
    n_pre = norm_mix_pre[0].reshape(1, D_MODEL)
    n_post = norm_mix_post[0].reshape(1, D_MODEL)
    n_fpre = norm_ffn_pre[0].reshape(1, D_MODEL)
    n_fpost = norm_ffn_post[0].reshape(1, D_MODEL)
    sinks = attn_sinks[0].reshape(1, ATTN_HEADS)
    cw = conv_w[0]
    alog = _pad_lanes(gdn_a_log[0])
    dtb = _pad_lanes(gdn_dt_bias[0])
    gnw = gdn_norm[0].reshape(1, GDN_DV)

    xp = x_prompt.reshape(bsz * seq, D_MODEL)
    pa, act, pz, gb, p_conv, w_out_b, w_up_b, w_down_b = _inproj(
        xp, n_pre, w_in_p, cw, alog, dtb, w_out[0], w_up[0], w_down[0], INPROJ_T, seq)
    yp, p_s = _mix_ffn(sinks, pa, act, gb, xp, pz, gnw, w_out_b, n_post, n_fpre, w_up_b, w_down_b,
                       n_fpost, seq)
    pa = pa.reshape(bsz, seq, ATTN_COLS)

    xs = x_sample.reshape(nsamp, D_MODEL)
    cs = state_conv[0]
    qexp, knew, vnew, qg, kt, vg, gate, zs, cs_new = _sample_prep(xs, n_pre, w_in_p, cs, cw, alog, dtb)
    kbuf = cache_win_k[0].reshape(nsamp, WINDOW, KV_WIDTH)
    vbuf = cache_win_v[0].reshape(nsamp, WINDOW, KV_WIDTH)
    ao_s, k_new_cache, v_new_cache = _attn_sample(sinks, qexp, knew, vnew, kbuf, vbuf)
    go_s, s_new = _gdn_sample(gate[:, :SUBLANES].reshape(nsamp * SUBLANES), qg, kt, vg, state_gdn[0])
    ys = _out_ffn(xs, ao_s, go_s, zs, gnw, w_out_b, n_post, n_fpre, w_up_b, w_down_b, n_fpost, nsamp)

    kv_shape = (1, -1, WINDOW, KV_HEADS, HEAD_DIM)
    return (yp.reshape(bsz, seq, D_MODEL),
            ys.reshape(nsamp, 1, D_MODEL),
            p_conv[:, SUBLANES - (CONV_W - 1):, :][None],
            pa[:, seq - WINDOW:, ATTN_WIDTH:ATTN_WIDTH + KV_WIDTH].reshape(kv_shape),
            pa[:, seq - WINDOW:, ATTN_WIDTH + KV_WIDTH:].reshape(kv_shape),
            p_s[None],
            cs_new[None],
            k_new_cache.reshape(kv_shape),
            v_new_cache.reshape(kv_shape),
            s_new[None])
```

```python
import functools

import jax
import jax.numpy as jnp
from jax import lax
from jax.experimental import pallas as pl
from jax.experimental.pallas import tpu as pltpu

F32 = jnp.float32
BF16 = jnp.bfloat16

D_MODEL = 1024
ATTN_HEADS = 8
KV_HEADS = 2
HEAD_DIM = 64
GQA_GROUP = ATTN_HEADS // KV_HEADS
ATTN_WIDTH = ATTN_HEADS * HEAD_DIM
KV_WIDTH = KV_HEADS * HEAD_DIM
WINDOW = 128
GDN_HEADS = 4
GDN_DK = 128
GDN_DV = 128
GDN_KEY_WIDTH = GDN_HEADS * GDN_DK
GDN_VAL_WIDTH = GDN_HEADS * GDN_DV
GDN_CONV_DIM = 2 * GDN_KEY_WIDTH + GDN_VAL_WIDTH
CONV_W = 4
D_FF = 4 * D_MODEL
EPS = 1e-6

LANES = 128
SUBLANES = 8
ATTN_COLS = ATTN_WIDTH + 2 * KV_WIDTH
AB_COLS = LANES
IN_COLS = ATTN_COLS + GDN_CONV_DIM + GDN_VAL_WIDTH + AB_COLS
OFF_GDN = ATTN_COLS
OFF_Z = OFF_GDN + GDN_CONV_DIM
OFF_AB = OFF_Z + GDN_VAL_WIDTH
CHUNK = 128
V7X_VMEM_BYTES = 64 * 1024 * 1024
VMEM_LIMIT = V7X_VMEM_BYTES - 8 * 1024 * 1024
VMEM_LIMIT_MIX = V7X_VMEM_BYTES - 4 * 1024 * 1024


def _rms(x, w):
    return x * lax.rsqrt(jnp.mean(x * x, axis=-1, keepdims=True) + EPS) * w


def _sigmoid(x):
    return 1.0 / (1.0 + jnp.exp(-x))


def _softplus(x):
    return jnp.maximum(x, 0.0) + jnp.log(1.0 + jnp.exp(-jnp.abs(x)))


def _dot(a, b):
    return jnp.dot(a.astype(BF16), b.astype(BF16), preferred_element_type=F32)


def _dot_nt(a, b):
    return lax.dot_general(a.astype(BF16), b.astype(BF16), (((1,), (1,)), ((), ())),
                           preferred_element_type=F32)


def _alibi_slope(h):
    return 2.0 ** (-8.0 * (h + 1) / ATTN_HEADS)


def _causal_conv(x, carry, w):
    t, c = x.shape
    x3 = x.reshape(t // SUBLANES, SUBLANES, c)
    row8 = lax.broadcasted_iota(jnp.int32, (1, SUBLANES, 1), 1)
    y = None
    for shift in range(CONV_W - 1, -1, -1):
        if shift == 0:
            xs = x3
        else:
            xr = pltpu.roll(x3, shift, axis=1)
            cr = pltpu.roll(carry, shift, axis=0)
            prev = jnp.concatenate([cr[None], xr[:-1]], axis=0)
            xs = jnp.where(row8 < shift, prev, xr)
        term = xs * w[CONV_W - 1 - shift:CONV_W - shift][None]
        y = term if y is None else y + term
    return y.reshape(t, c)


def _silu(y):
    return y * _sigmoid(y)


def _l2n(x):
    return x * lax.rsqrt(jnp.sum(x * x, axis=-1, keepdims=True) + EPS)


def _gates(ab, alog, dtb):
    g = -jnp.exp(alog) * _softplus(ab + dtb)
    lane = lax.broadcasted_iota(jnp.int32, (1, LANES), 1)
    return jnp.where(lane < GDN_HEADS, g, _sigmoid(ab))


INPROJ_T = 1024
INPROJ_SUB = 128


def _inproj_kernel(x_ref, nw_ref, w_ref, cw_ref, alog_ref, dtb_ref, wout_ref, wup_ref, wdown_ref,
                   oa_ref, og_ref, oz_ref, ogb_ref, conv_out_ref, wout_b_ref, wup_b_ref, wdown_b_ref,
                   carry_scr, *, tiles_per_seq):
    tm = x_ref.shape[0]
    wout_b_ref[...] = wout_ref[...].astype(BF16)
    wup_b_ref[...] = wup_ref[...].astype(BF16)
    wdown_b_ref[...] = wdown_ref[...].astype(BF16)

    @pl.when(lax.rem(pl.program_id(0), tiles_per_seq) == 0)
    def _():
        carry_scr[...] = jnp.zeros_like(carry_scr)

    carry = carry_scr[...]
    for r0 in range(0, tm, INPROJ_SUB):
        rows = slice(r0, r0 + INPROJ_SUB)
        h = _rms(x_ref[rows, :], nw_ref[...]).astype(BF16)
        pg = jnp.dot(h, w_ref[:, OFF_GDN:OFF_Z], preferred_element_type=F32)
        oa_ref[rows, :] = jnp.dot(h, w_ref[:, 0:OFF_GDN], preferred_element_type=F32)
        oz_ref[rows, :] = jnp.dot(h, w_ref[:, OFF_Z:OFF_AB], preferred_element_type=F32)
        ab = jnp.dot(h, w_ref[:, OFF_AB:IN_COLS], preferred_element_type=F32)
        ogb_ref[rows, :] = _gates(ab, alog_ref[...], dtb_ref[...])
        og_ref[rows, :] = _causal_conv(pg, carry, cw_ref[...])
        carry = pg[INPROJ_SUB - SUBLANES:INPROJ_SUB]
    carry_scr[...] = carry
    conv_out_ref[...] = carry


def _inproj(x, nw, w, cw, alog, dtb, wout, wup, wdown, tm, seq):
    m = x.shape[0]
    steps = m // tm
    tiles_per_seq = seq // tm
    row = lambda i: (i, 0)
    fixed = lambda i: (0, 0)
    slab = lambda a: pl.BlockSpec((a.shape[0] // steps, a.shape[1]), row)
    return pl.pallas_call(
        functools.partial(_inproj_kernel, tiles_per_seq=tiles_per_seq),
        grid=(steps,),
        in_specs=[pl.BlockSpec((tm, D_MODEL), row),
                  pl.BlockSpec((1, D_MODEL), fixed),
                  pl.BlockSpec((D_MODEL, IN_COLS), fixed),
                  pl.BlockSpec((CONV_W, GDN_CONV_DIM), fixed),
                  pl.BlockSpec((1, LANES), fixed),
                  pl.BlockSpec((1, LANES), fixed),
                  slab(wout), slab(wup), slab(wdown)],
        out_specs=[pl.BlockSpec((tm, ATTN_COLS), row),
                   pl.BlockSpec((tm, GDN_CONV_DIM), row),
                   pl.BlockSpec((tm, GDN_VAL_WIDTH), row),
                   pl.BlockSpec((tm, AB_COLS), row),
                   pl.BlockSpec((None, SUBLANES, GDN_CONV_DIM), lambda i: (i // tiles_per_seq, 0, 0)),
                   slab(wout), slab(wup), slab(wdown)],
        out_shape=[jax.ShapeDtypeStruct((m, ATTN_COLS), F32),
                   jax.ShapeDtypeStruct((m, GDN_CONV_DIM), F32),
                   jax.ShapeDtypeStruct((m, GDN_VAL_WIDTH), F32),
                   jax.ShapeDtypeStruct((m, AB_COLS), F32),
                   jax.ShapeDtypeStruct((m // seq, SUBLANES, GDN_CONV_DIM), F32),
                   jax.ShapeDtypeStruct(wout.shape, BF16),
                   jax.ShapeDtypeStruct(wup.shape, BF16),
                   jax.ShapeDtypeStruct(wdown.shape, BF16)],
        scratch_shapes=[pltpu.VMEM((SUBLANES, GDN_CONV_DIM), F32)],
        compiler_params=pltpu.CompilerParams(dimension_semantics=("arbitrary",),
                                             vmem_limit_bytes=VMEM_LIMIT),
        name="inproj",
    )(x, nw, w, cw, alog, dtb, wout, wup, wdown)


FF_UP_CHUNK = 256
FF_DOWN_CHUNK = 256
FF_STAGES = D_FF // FF_UP_CHUNK + D_MODEL // FF_DOWN_CHUNK


def _out_ffn_stages(x_ref, ao_ref, go_ref, z_ref, gnw_ref, wout_ref, npost_ref, npre_ref, wup_ref,
                    wdown_ref, nffn_ref, y_ref):
    gnw = gnw_ref[...]
    gy = []
    for hd in range(GDN_HEADS):
        lanes = slice(hd * LANES, (hd + 1) * LANES)
        z = z_ref[:, lanes]
        gy.append((_rms(go_ref[:, lanes], gnw) * (z * _sigmoid(z))).astype(BF16))
    m = (jnp.dot(ao_ref[...], wout_ref[0:ATTN_WIDTH, :], preferred_element_type=F32)
         + jnp.dot(jnp.concatenate(gy, axis=1), wout_ref[ATTN_WIDTH:, :], preferred_element_type=F32))
    x1 = x_ref[...] + _rms(m, npost_ref[...])
    h = _rms(x1, npre_ref[...]).astype(BF16)
    yield
    hidden = []
    for j in range(0, D_FF, FF_UP_CHUNK):
        u = jnp.maximum(jnp.dot(h, wup_ref[:, j:j + FF_UP_CHUNK], preferred_element_type=F32), 0.0)
        hidden.append((u * u).astype(BF16))
        yield
    hidden = jnp.concatenate(hidden, axis=1)
    f = []
    for j in range(0, D_MODEL, FF_DOWN_CHUNK):
        f.append(jnp.dot(hidden, wdown_ref[:, j:j + FF_DOWN_CHUNK], preferred_element_type=F32))
        yield
    y_ref[...] = x1 + _rms(jnp.concatenate(f, axis=1), nffn_ref[...])


def _dup_half(x, lo):
    xr = pltpu.roll(x, HEAD_DIM, axis=1)
    return jnp.where(lo, x, xr), jnp.where(lo, xr, x)


def _attn_prompt_stages(sink_ref, cur_ref, prev_ref, seq_row0, o_ref):
    tq = cur_ref.shape[0]
    kall = jnp.concatenate([prev_ref[:, 0:KV_WIDTH], cur_ref[:, ATTN_WIDTH:ATTN_WIDTH + KV_WIDTH]], axis=0)
    vall = jnp.concatenate([prev_ref[:, KV_WIDTH:], cur_ref[:, ATTN_WIDTH + KV_WIDTH:]], axis=0)
    lo = lax.broadcasted_iota(jnp.int32, (1, LANES), 1) < HEAD_DIM
    k2 = _dup_half(kall, lo)
    v2 = _dup_half(vall, lo)
    qi = lax.broadcasted_iota(jnp.int32, (WINDOW, 2 * WINDOW), 0)
    si = lax.broadcasted_iota(jnp.int32, (WINDOW, 2 * WINDOW), 1)
    dist = WINDOW + qi - si
    band = (dist >= 0) & (dist <= WINDOW)
    distf = dist.astype(F32)
    for r in range(tq // WINDOW):
        rows = slice(r * WINDOW, (r + 1) * WINDOW)
        valid = band
        if r == 0:
            valid = band & ((seq_row0 - WINDOW + si) >= 0)
        for j in range(KV_HEADS):
            kw = k2[j][r * WINDOW:(r + 2) * WINDOW]
            vw = v2[j][r * WINDOW:(r + 2) * WINDOW]
            t0 = cur_ref[rows, (2 * j) * LANES:(2 * j + 1) * LANES]
            t1 = cur_ref[rows, (2 * j + 1) * LANES:(2 * j + 2) * LANES]
            lhs = jnp.concatenate([jnp.where(lo, t0, 0.0), jnp.where(lo, 0.0, t0),
                                   jnp.where(lo, t1, 0.0), jnp.where(lo, 0.0, t1)], axis=0)
            s = _dot_nt(lhs, kw) * (HEAD_DIM ** -0.5)
            probs = []
            for g in range(GQA_GROUP):
                h = j * GQA_GROUP + g
                sg = s[g * WINDOW:(g + 1) * WINDOW] - _alibi_slope(h) * distf
                sg = jnp.where(valid, sg, -jnp.inf)
                sink = sink_ref[0, h]
                mx = jnp.maximum(jnp.max(sg, axis=-1, keepdims=True), sink)
                p = jnp.exp(sg - mx)
                den = jnp.sum(p, axis=-1, keepdims=True) + jnp.exp(sink - mx)
                probs.append(p / den)
            yield
            o = _dot(jnp.concatenate(probs, axis=0), vw)
            o01 = jnp.where(lo, o[0:WINDOW], o[WINDOW:2 * WINDOW])
            o23 = jnp.where(lo, o[2 * WINDOW:3 * WINDOW], o[3 * WINDOW:4 * WINDOW])
            o_ref[rows, (2 * j) * LANES:(2 * j + 1) * LANES] = o01.astype(o_ref.dtype)
            o_ref[rows, (2 * j + 1) * LANES:(2 * j + 2) * LANES] = o23.astype(o_ref.dtype)
            yield


def _gdn_prompt_stages(conv_ref, gb_ref, state, o_ref):
    t = conv_ref.shape[0]
    beta = gb_ref[...]
    g = beta
    rin = lax.broadcasted_iota(jnp.int32, (t, 1), 0) & (CHUNK - 1)
    step = 1
    while step < CHUNK:
        g = g + jnp.where(rin >= step, pltpu.roll(g, step, axis=0), 0.0)
        step *= 2
    gt = g.T

    row = lax.broadcasted_iota(jnp.int32, (CHUNK, CHUNK), 0)
    col = lax.broadcasted_iota(jnp.int32, (CHUNK, CHUNK), 1)
    lower_incl = row >= col
    lower_strict = row > col

    heads = range(GDN_HEADS)
    units = [(c, h) for c in range(t // CHUNK) for h in heads]
    pre = []
    for c, h in units:
        rows = slice(c * CHUNK, (c + 1) * CHUNK)
        qc = _l2n(_silu(conv_ref[rows, h * LANES:(h + 1) * LANES])) * (GDN_DK ** -0.5)
        kc = _l2n(_silu(conv_ref[rows, GDN_KEY_WIDTH + h * LANES:GDN_KEY_WIDTH + (h + 1) * LANES]))
        vc = _silu(conv_ref[rows, 2 * GDN_KEY_WIDTH + h * LANES:2 * GDN_KEY_WIDTH + (h + 1) * LANES])
        gc = g[rows, h:h + 1]
        bc = beta[rows, GDN_HEADS + h:GDN_HEADS + h + 1]
        gr = gt[h:h + 1, c * CHUNK:(c + 1) * CHUNK]
        decay = jnp.where(lower_incl, jnp.exp(jnp.where(lower_incl, gc - gr, 0.0)), 0.0)
        eg = jnp.exp(gc)
        glast = gc[CHUNK - 1:CHUNK]
        rhs = jnp.concatenate([vc * bc, kc * (bc * eg)], axis=1)
        k_dec_t = (kc * jnp.exp(glast - gc)).T
        kq = _dot_nt(jnp.concatenate([kc, qc], axis=0), kc)
        a = jnp.where(lower_strict, bc * kq[0:CHUNK] * decay, 0.0)
        qk = jnp.where(lower_incl, kq[CHUNK:] * decay, 0.0)
        pre.append(dict(a=a, rhs=rhs, lhs_u=jnp.concatenate([qk, k_dec_t], axis=0), q_dec=qc * eg,
                        ld=jnp.exp(glast)))
        if h == GDN_HEADS - 1:
            yield

    mats = [p["a"] for p in pre]
    first = ((row >> 1) == (col >> 1)) & ((row & 1) == 1) & ((col & 1) == 0)
    xo = [jnp.where(first, -a, 0.0) for a in mats]
    size = 2
    while size < CHUNK:
        shift = size.bit_length()
        m = ((row >> shift) == (col >> shift)) & ((row & size) != 0) & ((col & size) == 0)
        aoff = [jnp.where(m, a, 0.0) for a in mats]
        y = [ao + _dot(ao, x) for ao, x in zip(aoff, xo)]
        yield
        xo = [x - yy - _dot(x, yy) for x, yy in zip(xo, y)]
        size *= 2
        yield
    trs = [p["rhs"] + _dot(x, p["rhs"]) for p, x in zip(pre, xo)]
    yield

    for c in range(t // CHUNK):
        rows = slice(c * CHUNK, (c + 1) * CHUNK)
        idx = [c * GDN_HEADS + h for h in heads]
        ws = [_dot(jnp.concatenate([trs[u][:, GDN_DV:], pre[u]["q_dec"]], axis=0), state[h])
              for h, u in zip(heads, idx)]
        yield
        ou = [_dot(pre[u]["lhs_u"], trs[u][:, 0:GDN_DV] - ws[h][0:CHUNK]) for h, u in zip(heads, idx)]
        for h, u in zip(heads, idx):
            o_ref[rows, h * LANES:(h + 1) * LANES] = ws[h][CHUNK:] + ou[h][0:CHUNK]
            state[h] = state[h] * pre[u]["ld"] + ou[h][CHUNK:]
        yield


def _run_interleaved(primary, n_primary, secondary, n_secondary):
    done = object()
    issued = 0
    for k in range(n_primary):
        next(primary, done)
        target = ((k + 1) * n_secondary + n_primary - 1) // n_primary
        while issued < target:
            next(secondary, done)
            issued += 1
    for _ in primary:
        pass
    for _ in secondary:
        pass


def _roundrobin(*gens):
    done = object()
    live = list(gens)
    while live:
        for gen in list(live):
            if next(gen, done) is done:
                live.remove(gen)
            else:
                yield


MIX_T = 512


def _mix_ffn_kernel(sink_ref, pa_ref, prev_ref, conv_ref, gb_ref, x_ref, z_ref, gnw_ref, wout_ref,
                    npost_ref, npre_ref, wup_ref, wdown_ref, nffn_ref,
                    y_ref, s_out_ref, s_scr, ao_scr, go_scr, *, n_tiles, tiles_per_seq):
    i = pl.program_id(0)

    @pl.when(i == 0)
    def _():
        s_scr[...] = jnp.zeros_like(s_scr)
        ao_scr[...] = jnp.zeros_like(ao_scr)
        go_scr[...] = jnp.zeros_like(go_scr)

    front_tile = jnp.minimum(i, n_tiles - 1)
    tile_in_seq = lax.rem(front_tile, tiles_per_seq)
    seq_start = tile_in_seq == 0
    front_valid = i < n_tiles

    ffn = _out_ffn_stages(x_ref, ao_scr, go_scr, z_ref, gnw_ref, wout_ref, npost_ref, npre_ref,
                          wup_ref, wdown_ref, nffn_ref, y_ref)
    next(ffn)

    old = [s_scr[h] for h in range(GDN_HEADS)]
    state = [jnp.where(seq_start, 0.0, s) for s in old]
    front = _roundrobin(_gdn_prompt_stages(conv_ref, gb_ref, state, go_scr),
                        _attn_prompt_stages(sink_ref, pa_ref, prev_ref, tile_in_seq * MIX_T, ao_scr))
    mix_stages = (MIX_T // CHUNK) * 3 + 2 * (CHUNK.bit_length() - 2) + 1 + 2 * (MIX_T // WINDOW) * KV_HEADS
    _run_interleaved(ffn, FF_STAGES, front, mix_stages)

    for h in range(GDN_HEADS):
        s_new = jnp.where(front_valid, state[h], old[h])
        s_scr[h] = s_new
        s_out_ref[h] = s_new


def _mix_ffn(sinks, pa, act, gb, x, z, gnw, wout, npost, npre, wup, wdown, nffn, seq):
    m = x.shape[0]
    n_tiles = m // MIX_T
    tiles_per_seq = seq // MIX_T
    blocks_per_tile = MIX_T // WINDOW
    front = lambda i: (jnp.minimum(i, n_tiles - 1), 0)
    back = lambda i: (jnp.maximum(i - 1, 0), 0)
    fixed = lambda i: (0, 0)
    resident = dict(pipeline_mode=pl.Buffered(1))
    kern = functools.partial(_mix_ffn_kernel, n_tiles=n_tiles, tiles_per_seq=tiles_per_seq)
    return pl.pallas_call(
        kern,
        grid=(n_tiles + 1,),
        in_specs=[pl.BlockSpec(memory_space=pltpu.SMEM),
                  pl.BlockSpec((MIX_T, ATTN_COLS), front),
                  pl.BlockSpec((WINDOW, 2 * KV_WIDTH),
                               lambda i: (jnp.maximum(jnp.minimum(i, n_tiles - 1) * blocks_per_tile - 1, 0),
                                          ATTN_WIDTH // (2 * KV_WIDTH))),
                  pl.BlockSpec((MIX_T, GDN_CONV_DIM), front),
                  pl.BlockSpec((MIX_T, AB_COLS), front),
                  pl.BlockSpec((MIX_T, D_MODEL), back),
                  pl.BlockSpec((MIX_T, GDN_VAL_WIDTH), back),
                  pl.BlockSpec((1, GDN_DV), fixed),
                  pl.BlockSpec((D_MODEL, D_MODEL), fixed, **resident),
                  pl.BlockSpec((1, D_MODEL), fixed),
                  pl.BlockSpec((1, D_MODEL), fixed),
                  pl.BlockSpec((D_MODEL, D_FF), fixed, **resident),
                  pl.BlockSpec((D_FF, D_MODEL), fixed, **resident),
                  pl.BlockSpec((1, D_MODEL), fixed)],
        out_specs=[pl.BlockSpec((MIX_T, D_MODEL), back),
                   pl.BlockSpec((None, GDN_HEADS, GDN_DK, GDN_DV),
                                lambda i: (jnp.minimum(i, n_tiles - 1) // tiles_per_seq, 0, 0, 0))],
        out_shape=[jax.ShapeDtypeStruct((m, D_MODEL), F32),
                   jax.ShapeDtypeStruct((m // seq, GDN_HEADS, GDN_DK, GDN_DV), F32)],
        scratch_shapes=[pltpu.VMEM((GDN_HEADS, GDN_DK, GDN_DV), F32),
                        pltpu.VMEM((MIX_T, ATTN_WIDTH), BF16),
                        pltpu.VMEM((MIX_T, GDN_VAL_WIDTH), F32)],
        compiler_params=pltpu.CompilerParams(dimension_semantics=("arbitrary",),
                                             vmem_limit_bytes=VMEM_LIMIT_MIX),
        name="mix_ffn",
    )(sinks, pa, pa, act, gb, x, z, gnw, wout, npost, npre, wup, wdown, nffn)


def _out_ffn_kernel(*refs):
    for _ in _out_ffn_stages(*refs):
        pass


def _out_ffn(x, ao, go, z, gnw, wout, npost, npre, wup, wdown, nffn, tm):
    m = x.shape[0]
    row = lambda i: (i, 0)
    fixed = lambda i: (0, 0)
    return pl.pallas_call(
        _out_ffn_kernel,
        grid=(m // tm,),
        in_specs=[pl.BlockSpec((tm, D_MODEL), row),
                  pl.BlockSpec((tm, ATTN_WIDTH), row),
                  pl.BlockSpec((tm, GDN_VAL_WIDTH), row),
                  pl.BlockSpec((tm, GDN_VAL_WIDTH), row),
                  pl.BlockSpec((1, GDN_DV), fixed),
                  pl.BlockSpec((D_MODEL, D_MODEL), fixed, pipeline_mode=pl.Buffered(1)),
                  pl.BlockSpec((1, D_MODEL), fixed),
                  pl.BlockSpec((1, D_MODEL), fixed),
                  pl.BlockSpec((D_MODEL, D_FF), fixed, pipeline_mode=pl.Buffered(1)),
                  pl.BlockSpec((D_FF, D_MODEL), fixed, pipeline_mode=pl.Buffered(1)),
                  pl.BlockSpec((1, D_MODEL), fixed)],
        out_specs=pl.BlockSpec((tm, D_MODEL), row),
        out_shape=jax.ShapeDtypeStruct((m, D_MODEL), F32),
        compiler_params=pltpu.CompilerParams(dimension_semantics=("arbitrary",),
                                             vmem_limit_bytes=VMEM_LIMIT),
        name="out_ffn",
    )(x, ao, go, z, gnw, wout, npost, npre, wup, wdown, nffn)


def _sample_prep_kernel(x_ref, nw_ref, w_ref, cs_ref, cw_ref, alog_ref, dtb_ref,
                        qexp_ref, knew_ref, vnew_ref, qg_ref, kt_ref, vg_ref, gate_ref, z_ref,
                        cs_out_ref):
    h = _rms(x_ref[...], nw_ref[...]).astype(BF16)
    pa = jnp.dot(h, w_ref[:, 0:OFF_GDN], preferred_element_type=F32)
    pg = jnp.dot(h, w_ref[:, OFF_GDN:OFF_Z], preferred_element_type=F32)
    z_ref[...] = jnp.dot(h, w_ref[:, OFF_Z:OFF_AB], preferred_element_type=F32)
    ab = jnp.dot(h, w_ref[:, OFF_AB:IN_COLS], preferred_element_type=F32)

    lo = lax.broadcasted_iota(jnp.int32, (1, LANES), 1) < HEAD_DIM
    for t in range(ATTN_WIDTH // LANES):
        tile = pa[:, t * LANES:(t + 1) * LANES]
        rolled = pltpu.roll(tile, HEAD_DIM, axis=1)
        if t < ATTN_WIDTH // LANES // KV_HEADS:
            even, odd = jnp.where(lo, tile, 0.0), jnp.where(lo, rolled, 0.0)
        else:
            even, odd = jnp.where(lo, 0.0, rolled), jnp.where(lo, 0.0, tile)
        qexp_ref[2 * t] = even
        qexp_ref[2 * t + 1] = odd
    knew_ref[...] = pa[:, ATTN_WIDTH:ATTN_WIDTH + KV_WIDTH]
    vnew_ref[...] = pa[:, ATTN_WIDTH + KV_WIDTH:ATTN_COLS]

    cw = cw_ref[...]
    y = None
    for r in range(CONV_W - 1):
        term = cs_ref[:, r * GDN_CONV_DIM:(r + 1) * GDN_CONV_DIM] * cw[r:r + 1]
        y = term if y is None else y + term
    y = y + pg * cw[CONV_W - 1:CONV_W]
    act = y * _sigmoid(y)
    cs_out_ref[:, 0:(CONV_W - 2) * GDN_CONV_DIM] = cs_ref[:, GDN_CONV_DIM:(CONV_W - 1) * GDN_CONV_DIM]
    cs_out_ref[:, (CONV_W - 2) * GDN_CONV_DIM:] = pg

    for hd in range(GDN_HEADS):
        qn = _l2n(act[:, hd * LANES:(hd + 1) * LANES]) * (GDN_DK ** -0.5)
        kn = _l2n(act[:, GDN_KEY_WIDTH + hd * LANES:GDN_KEY_WIDTH + (hd + 1) * LANES])
        qg_ref[:, hd * LANES:(hd + 1) * LANES] = qn
        kt_ref[hd] = kn.T
    vg_ref[...] = act[:, 2 * GDN_KEY_WIDTH:]
    gb = _gates(ab, alog_ref[...], dtb_ref[...])
    lane = lax.broadcasted_iota(jnp.int32, (1, LANES), 1)
    gate_ref[...] = jnp.where(lane < GDN_HEADS, jnp.exp(gb), gb)


def _sample_prep(x, nw, w, cs, cw, alog, dtb):
    n = x.shape[0]
    return pl.pallas_call(
        _sample_prep_kernel,
        out_shape=[jax.ShapeDtypeStruct((ATTN_HEADS, n, LANES), F32),
                   jax.ShapeDtypeStruct((n, KV_WIDTH), F32),
                   jax.ShapeDtypeStruct((n, KV_WIDTH), F32),
                   jax.ShapeDtypeStruct((n, GDN_KEY_WIDTH), F32),
                   jax.ShapeDtypeStruct((GDN_HEADS, GDN_DK, n), F32),
                   jax.ShapeDtypeStruct((n, GDN_VAL_WIDTH), F32),
                   jax.ShapeDtypeStruct((n, LANES), F32),
                   jax.ShapeDtypeStruct((n, GDN_VAL_WIDTH), F32),
                   jax.ShapeDtypeStruct((n, (CONV_W - 1) * GDN_CONV_DIM), F32)],
        compiler_params=pltpu.CompilerParams(vmem_limit_bytes=VMEM_LIMIT),
        name="sample_prep",
    )(x, nw, w, cs, cw, alog, dtb)


ATTN_BB = 16


def _attn_sample_kernel(sink_ref, qexp_ref, knew_ref, vnew_ref, kbuf_ref, vbuf_ref,
                        o_ref, kout_ref, vout_ref, q_scr, o_scr):
    for h in range(ATTN_HEADS):
        q_scr[pl.ds(h, ATTN_BB, stride=ATTN_HEADS), :] = qexp_ref[h]
    hrow = lax.broadcasted_iota(jnp.int32, (ATTN_HEADS, 1), 0)
    sink = jnp.zeros((ATTN_HEADS, 1), F32)
    slope = jnp.zeros((ATTN_HEADS, 1), F32)
    for h in range(ATTN_HEADS):
        sink = jnp.where(hrow == h, sink_ref[0, h], sink)
        slope = jnp.where(hrow == h, _alibi_slope(h), slope)
    pos = lax.broadcasted_iota(jnp.int32, (1, WINDOW), 1)
    bias = slope * (WINDOW - pos).astype(F32)
    rowid = lax.broadcasted_iota(jnp.int32, (WINDOW, 1), 0)
    last = rowid == WINDOW - 1
    toks = range(ATTN_BB)
    qe = [q_scr[b * ATTN_HEADS:(b + 1) * ATTN_HEADS, :] for b in toks]
    s_old = [_dot_nt(qe[b], kbuf_ref[b]) * (HEAD_DIM ** -0.5) - bias for b in toks]
    probs, p_new = [], []
    for b in toks:
        s_new = jnp.sum(qe[b] * knew_ref[b:b + 1, :], axis=-1, keepdims=True) * (HEAD_DIM ** -0.5)
        mx = jnp.maximum(jnp.maximum(jnp.max(s_old[b], axis=-1, keepdims=True), s_new), sink)
        p_old = jnp.exp(s_old[b] - mx)
        pn = jnp.exp(s_new - mx)
        den = jnp.sum(p_old, axis=-1, keepdims=True) + pn + jnp.exp(sink - mx)
        probs.append(p_old / den)
        p_new.append(pn / den)
    for b in toks:
        o_scr[b * ATTN_HEADS:(b + 1) * ATTN_HEADS, :] = (_dot(probs[b], vbuf_ref[b])
                                                         + p_new[b] * vnew_ref[b:b + 1, :])
    for b in toks:
        kout_ref[b] = jnp.where(last, knew_ref[b:b + 1, :], pltpu.roll(kbuf_ref[b], WINDOW - 1, axis=0))
        vout_ref[b] = jnp.where(last, vnew_ref[b:b + 1, :], pltpu.roll(vbuf_ref[b], WINDOW - 1, axis=0))
    lo = lax.broadcasted_iota(jnp.int32, (1, LANES), 1) < HEAD_DIM
    for t in range(ATTN_WIDTH // LANES):
        even = o_scr[pl.ds(2 * t, ATTN_BB, stride=ATTN_HEADS), :]
        odd = o_scr[pl.ds(2 * t + 1, ATTN_BB, stride=ATTN_HEADS), :]
        if t < ATTN_WIDTH // LANES // KV_HEADS:
            tile = jnp.where(lo, even, pltpu.roll(odd, HEAD_DIM, axis=1))
        else:
            tile = jnp.where(lo, pltpu.roll(even, HEAD_DIM, axis=1), odd)
        o_ref[:, t * LANES:(t + 1) * LANES] = tile.astype(o_ref.dtype)


def _attn_sample(sinks, qexp, knew, vnew, kbuf, vbuf):
    n = knew.shape[0]
    tok2 = lambda i: (i, 0)
    tok3 = lambda i: (i, 0, 0)
    return pl.pallas_call(
        _attn_sample_kernel,
        grid=(n // ATTN_BB,),
        in_specs=[pl.BlockSpec(memory_space=pltpu.SMEM),
                  pl.BlockSpec((ATTN_HEADS, ATTN_BB, LANES), lambda i: (0, i, 0)),
                  pl.BlockSpec((ATTN_BB, KV_WIDTH), tok2),
                  pl.BlockSpec((ATTN_BB, KV_WIDTH), tok2),
                  pl.BlockSpec((ATTN_BB, WINDOW, KV_WIDTH), tok3),
                  pl.BlockSpec((ATTN_BB, WINDOW, KV_WIDTH), tok3)],
        out_specs=[pl.BlockSpec((ATTN_BB, ATTN_WIDTH), tok2),
                   pl.BlockSpec((ATTN_BB, WINDOW, KV_WIDTH), tok3),
                   pl.BlockSpec((ATTN_BB, WINDOW, KV_WIDTH), tok3)],
        out_shape=[jax.ShapeDtypeStruct((n, ATTN_WIDTH), BF16),
                   jax.ShapeDtypeStruct((n, WINDOW, KV_WIDTH), F32),
                   jax.ShapeDtypeStruct((n, WINDOW, KV_WIDTH), F32)],
        scratch_shapes=[pltpu.VMEM((ATTN_BB * ATTN_HEADS, LANES), F32),
                        pltpu.VMEM((ATTN_BB * ATTN_HEADS, LANES), F32)],
        compiler_params=pltpu.CompilerParams(dimension_semantics=("arbitrary",),
                                             vmem_limit_bytes=VMEM_LIMIT),
        name="attn_sample",
    )(sinks, qexp, knew, vnew, kbuf, vbuf)


GDN_BB = 16


def _gdn_sample_kernel(gate_ref, q_ref, kt_ref, vg_ref, s_ref, o_ref, s_out_ref):
    i = pl.program_id(0)
    n = kt_ref.shape[-1]
    shift = (n - i * GDN_BB) % n
    units = [(j, h) for h in range(GDN_HEADS) for j in range(GDN_BB)]
    kt = [pltpu.roll(kt_ref[h], shift, axis=1) for h in range(GDN_HEADS)]
    kb = [jnp.broadcast_to(kt[h][:, j:j + 1], (GDN_DK, GDN_DV)) for j, h in units]
    s = [s_ref[j, h] * gate_ref[(i * GDN_BB + j) * SUBLANES + h] for j, h in units]
    kv = [jnp.sum(su * k, axis=0, keepdims=True) for su, k in zip(s, kb)]
    u = [(vg_ref[j:j + 1, h * LANES:(h + 1) * LANES] - kvu)
         * gate_ref[(i * GDN_BB + j) * SUBLANES + GDN_HEADS + h] for (j, h), kvu in zip(units, kv)]
    s = [su + k * uu for su, k, uu in zip(s, kb, u)]
    for (j, h), su in zip(units, s):
        s_out_ref[j, h] = su
    for (j, h), su in zip(units, s):
        lanes = slice(h * LANES, (h + 1) * LANES)
        o_ref[j:j + 1, lanes] = _dot(q_ref[:, lanes], su)[j:j + 1, :]


def _gdn_sample(gate, q, kt, vg, s0):
    n = vg.shape[0]
    return pl.pallas_call(
        _gdn_sample_kernel,
        grid=(n // GDN_BB,),
        in_specs=[pl.BlockSpec(memory_space=pltpu.SMEM),
                  pl.BlockSpec((GDN_BB, GDN_KEY_WIDTH), lambda i: (i, 0)),
                  pl.BlockSpec((GDN_HEADS, GDN_DK, n), lambda i: (0, 0, 0)),
                  pl.BlockSpec((GDN_BB, GDN_VAL_WIDTH), lambda i: (i, 0)),
                  pl.BlockSpec((GDN_BB, GDN_HEADS, GDN_DK, GDN_DV), lambda i: (i, 0, 0, 0))],
        out_specs=[pl.BlockSpec((GDN_BB, GDN_VAL_WIDTH), lambda i: (i, 0)),
                   pl.BlockSpec((GDN_BB, GDN_HEADS, GDN_DK, GDN_DV), lambda i: (i, 0, 0, 0))],
        out_shape=[jax.ShapeDtypeStruct((n, GDN_VAL_WIDTH), F32),
                   jax.ShapeDtypeStruct((n, GDN_HEADS, GDN_DK, GDN_DV), F32)],
        compiler_params=pltpu.CompilerParams(dimension_semantics=("arbitrary",),
                                             vmem_limit_bytes=VMEM_LIMIT),
        name="gdn_sample",
    )(gate, q, kt, vg, s0)


def _sample_mixers_kernel(sink_ref, gate_ref, qexp_ref, knew_ref, vnew_ref, kbuf_ref, vbuf_ref,
                          q_ref, kt_ref, vg_ref, s_ref,
                          ao_ref, kout_ref, vout_ref, go_ref, s_out_ref, q_scr, o_scr):
    _attn_sample_kernel(sink_ref, qexp_ref, knew_ref, vnew_ref, kbuf_ref, vbuf_ref,
                        ao_ref, kout_ref, vout_ref, q_scr, o_scr)
    _gdn_sample_kernel(gate_ref, q_ref, kt_ref, vg_ref, s_ref, go_ref, s_out_ref)


def _sample_mixers(sinks, gate, qexp, knew, vnew, kbuf, vbuf, q, kt, vg, s0):
    assert ATTN_BB == GDN_BB
    n = knew.shape[0]
    bb = ATTN_BB
    tok2 = lambda i: (i, 0)
    tok3 = lambda i: (i, 0, 0)
    tok4 = lambda i: (i, 0, 0, 0)
    smem = pl.BlockSpec(memory_space=pltpu.SMEM)
    cache = pl.BlockSpec((bb, WINDOW, KV_WIDTH), tok3)
    state = pl.BlockSpec((bb, GDN_HEADS, GDN_DK, GDN_DV), tok4)
    return pl.pallas_call(
        _sample_mixers_kernel,
        grid=(n // bb,),
        in_specs=[smem, smem,
                  pl.BlockSpec((ATTN_HEADS, bb, LANES), lambda i: (0, i, 0)),
                  pl.BlockSpec((bb, KV_WIDTH), tok2),
                  pl.BlockSpec((bb, KV_WIDTH), tok2),
                  cache, cache,
                  pl.BlockSpec((bb, GDN_KEY_WIDTH), tok2),
                  pl.BlockSpec((GDN_HEADS, GDN_DK, n), lambda i: (0, 0, 0)),
                  pl.BlockSpec((bb, GDN_VAL_WIDTH), tok2),
                  state],
        out_specs=[pl.BlockSpec((bb, ATTN_WIDTH), tok2), cache, cache,
                   pl.BlockSpec((bb, GDN_VAL_WIDTH), tok2), state],
        out_shape=[jax.ShapeDtypeStruct((n, ATTN_WIDTH), BF16),
                   jax.ShapeDtypeStruct((n, WINDOW, KV_WIDTH), F32),
                   jax.ShapeDtypeStruct((n, WINDOW, KV_WIDTH), F32),
                   jax.ShapeDtypeStruct((n, GDN_VAL_WIDTH), F32),
                   jax.ShapeDtypeStruct((n, GDN_HEADS, GDN_DK, GDN_DV), F32)],
        scratch_shapes=[pltpu.VMEM((bb * ATTN_HEADS, LANES), F32),
                        pltpu.VMEM((bb * ATTN_HEADS, LANES), F32)],
        compiler_params=pltpu.CompilerParams(dimension_semantics=("arbitrary",),
                                             vmem_limit_bytes=VMEM_LIMIT),
        name="sample_mixers",
    )(sinks, gate, qexp, knew, vnew, kbuf, vbuf, q, kt, vg, s0)


def _pad_lanes(v):
    return jnp.pad(v.astype(F32), (0, LANES - v.shape[0])).reshape(1, LANES)


def kernel(x_prompt, x_sample, state_conv, cache_win_k, cache_win_v, state_gdn, norm_mix_pre, w_in,
           attn_sinks, conv_w, gdn_a_log, gdn_dt_bias, gdn_norm, w_out, norm_mix_post, norm_ffn_pre,
           w_up, w_down, norm_ffn_post):
    assert w_in.shape[0] == 1, "single-layer trunk"
    bsz, seq, _ = x_prompt.shape
    nsamp = x_sample.shape[0]

    w_in_p = jnp.pad(w_in[0], ((0, 0), (0, IN_COLS - w_in.shape[2]))).astype(BF16)
    n_pre = norm_mix_pre[0].reshape(1, D_MODEL)
    n_post = norm_mix_post[0].reshape(1, D_MODEL)
    n_fpre = norm_ffn_pre[0].reshape(1, D_MODEL)
    n_fpost = norm_ffn_post[0].reshape(1, D_MODEL)
    sinks = attn_sinks[0].reshape(1, ATTN_HEADS)
    cw = conv_w[0]
    alog = _pad_lanes(gdn_a_log[0])
    dtb = _pad_lanes(gdn_dt_bias[0])
    gnw = gdn_norm[0].reshape(1, GDN_DV)

    xp = x_prompt.reshape(bsz * seq, D_MODEL)
    pa, act, pz, gb, p_conv, w_out_b, w_up_b, w_down_b = _inproj(
        xp, n_pre, w_in_p, cw, alog, dtb, w_out[0], w_up[0], w_down[0], INPROJ_T, seq)
    yp, p_s = _mix_ffn(sinks, pa, act, gb, xp, pz, gnw, w_out_b, n_post, n_fpre, w_up_b, w_down_b,
                       n_fpost, seq)
    pa = pa.reshape(bsz, seq, ATTN_COLS)

    xs = x_sample.reshape(nsamp, D_MODEL)
    cs = state_conv[0].reshape(nsamp, (CONV_W - 1) * GDN_CONV_DIM)
    qexp, knew, vnew, qg, kt, vg, gate, zs, cs_new = _sample_prep(xs, n_pre, w_in_p, cs, cw, alog, dtb)
    kbuf = cache_win_k[0].reshape(nsamp, WINDOW, KV_WIDTH)
    vbuf = cache_win_v[0].reshape(nsamp, WINDOW, KV_WIDTH)
    ao_s, k_new_cache, v_new_cache, go_s, s_new = _sample_mixers(
        sinks, gate[:, :SUBLANES].reshape(nsamp * SUBLANES), qexp, knew, vnew, kbuf, vbuf, qg, kt, vg,
        state_gdn[0])
    ys = _out_ffn(xs, ao_s, go_s, zs, gnw, w_out_b, n_post, n_fpre, w_up_b, w_down_b, n_fpost, nsamp)

    kv_shape = (1, -1, WINDOW, KV_HEADS, HEAD_DIM)
    return (yp.reshape(bsz, seq, D_MODEL),
            ys.reshape(nsamp, 1, D_MODEL),
            p_conv[:, SUBLANES - (CONV_W - 1):, :][None],
            pa[:, seq - WINDOW:, ATTN_WIDTH:ATTN_WIDTH + KV_WIDTH].reshape(kv_shape),
            pa[:, seq - WINDOW:, ATTN_WIDTH + KV_WIDTH:].reshape(kv_shape),
            p_s[None],
            cs_new.reshape(1, nsamp, CONV_W - 1, GDN_CONV_DIM),
            k_new_cache.reshape(kv_shape),
            v_new_cache.reshape(kv_shape),
            s_new[None])
```

```python
import functools

import jax
import jax.numpy as jnp
from jax import lax
from jax.experimental import pallas as pl
from jax.experimental.pallas import tpu as pltpu

F32 = jnp.float32
BF16 = jnp.bfloat16

D_MODEL = 1024
ATTN_HEADS = 8
KV_HEADS = 2
HEAD_DIM = 64
GQA_GROUP = ATTN_HEADS // KV_HEADS
ATTN_WIDTH = ATTN_HEADS * HEAD_DIM
KV_WIDTH = KV_HEADS * HEAD_DIM
WINDOW = 128
GDN_HEADS = 4
GDN_DK = 128
GDN_DV = 128
GDN_KEY_WIDTH = GDN_HEADS * GDN_DK
GDN_VAL_WIDTH = GDN_HEADS * GDN_DV
GDN_CONV_DIM = 2 * GDN_KEY_WIDTH + GDN_VAL_WIDTH
CONV_W = 4
D_FF = 4 * D_MODEL
EPS = 1e-6

LANES = 128
SUBLANES = 8
ATTN_COLS = ATTN_WIDTH + 2 * KV_WIDTH
AB_COLS = LANES
IN_COLS = ATTN_COLS + GDN_CONV_DIM + GDN_VAL_WIDTH + AB_COLS
OFF_GDN = ATTN_COLS
OFF_Z = OFF_GDN + GDN_CONV_DIM
OFF_AB = OFF_Z + GDN_VAL_WIDTH
CHUNK = 128
V7X_VMEM_BYTES = 64 * 1024 * 1024
VMEM_LIMIT = V7X_VMEM_BYTES - 8 * 1024 * 1024
VMEM_LIMIT_MIX = V7X_VMEM_BYTES - 4 * 1024 * 1024


def _rms(x, w):
    return x * lax.rsqrt(jnp.mean(x * x, axis=-1, keepdims=True) + EPS) * w


def _sigmoid(x):
    return 1.0 / (1.0 + jnp.exp(-x))


def _softplus(x):
    return jnp.maximum(x, 0.0) + jnp.log(1.0 + jnp.exp(-jnp.abs(x)))


def _dot(a, b):
    return jnp.dot(a.astype(BF16), b.astype(BF16), preferred_element_type=F32)


def _dot_nt(a, b):
    return lax.dot_general(a.astype(BF16), b.astype(BF16), (((1,), (1,)), ((), ())),
                           preferred_element_type=F32)


def _alibi_slope(h):
    return 2.0 ** (-8.0 * (h + 1) / ATTN_HEADS)


def _causal_conv(x, carry, w):
    t, c = x.shape
    x3 = x.reshape(t // SUBLANES, SUBLANES, c)
    row8 = lax.broadcasted_iota(jnp.int32, (1, SUBLANES, 1), 1)
    y = None
    for shift in range(CONV_W - 1, -1, -1):
        if shift == 0:
            xs = x3
        else:
            xr = pltpu.roll(x3, shift, axis=1)
            cr = pltpu.roll(carry, shift, axis=0)
            prev = jnp.concatenate([cr[None], xr[:-1]], axis=0)
            xs = jnp.where(row8 < shift, prev, xr)
        term = xs * w[CONV_W - 1 - shift:CONV_W - shift][None]
        y = term if y is None else y + term
    return y.reshape(t, c)


def _silu(y):
    return y * _sigmoid(y)


def _l2n(x):
    return x * lax.rsqrt(jnp.sum(x * x, axis=-1, keepdims=True) + EPS)


def _gates(ab, alog, dtb):
    g = -jnp.exp(alog) * _softplus(ab + dtb)
    lane = lax.broadcasted_iota(jnp.int32, (1, LANES), 1)
    return jnp.where(lane < GDN_HEADS, g, _sigmoid(ab))


INPROJ_T = 1024
INPROJ_SUB = 128


def _inproj_kernel(x_ref, nw_ref, w_ref, cw_ref, alog_ref, dtb_ref, wout_ref, wup_ref, wdown_ref,
                   oa_ref, og_ref, oz_ref, ogb_ref, conv_out_ref, wout_b_ref, wup_b_ref, wdown_b_ref,
                   carry_scr, *, tiles_per_seq):
    tm = x_ref.shape[0]
    wout_b_ref[...] = wout_ref[...].astype(BF16)
    wup_b_ref[...] = wup_ref[...].astype(BF16)
    wdown_b_ref[...] = wdown_ref[...].astype(BF16)

    @pl.when(lax.rem(pl.program_id(0), tiles_per_seq) == 0)
    def _():
        carry_scr[...] = jnp.zeros_like(carry_scr)

    carry = carry_scr[...]
    for r0 in range(0, tm, INPROJ_SUB):
        rows = slice(r0, r0 + INPROJ_SUB)
        h = _rms(x_ref[rows, :], nw_ref[...]).astype(BF16)
        pg = jnp.dot(h, w_ref[:, OFF_GDN:OFF_Z], preferred_element_type=F32)
        oa_ref[rows, :] = jnp.dot(h, w_ref[:, 0:OFF_GDN], preferred_element_type=F32)
        oz_ref[rows, :] = jnp.dot(h, w_ref[:, OFF_Z:OFF_AB], preferred_element_type=F32)
        ab = jnp.dot(h, w_ref[:, OFF_AB:IN_COLS], preferred_element_type=F32)
        ogb_ref[rows, :] = _gates(ab, alog_ref[...], dtb_ref[...])
        og_ref[rows, :] = _causal_conv(pg, carry, cw_ref[...])
        carry = pg[INPROJ_SUB - SUBLANES:INPROJ_SUB]
    carry_scr[...] = carry
    conv_out_ref[...] = carry


def _inproj(x, nw, w, cw, alog, dtb, wout, wup, wdown, tm, seq):
    m = x.shape[0]
    steps = m // tm
    tiles_per_seq = seq // tm
    row = lambda i: (i, 0)
    fixed = lambda i: (0, 0)
    slab = lambda a: pl.BlockSpec((a.shape[0] // steps, a.shape[1]), row)
    return pl.pallas_call(
        functools.partial(_inproj_kernel, tiles_per_seq=tiles_per_seq),
        grid=(steps,),
        in_specs=[pl.BlockSpec((tm, D_MODEL), row),
                  pl.BlockSpec((1, D_MODEL), fixed),
                  pl.BlockSpec((D_MODEL, IN_COLS), fixed),
                  pl.BlockSpec((CONV_W, GDN_CONV_DIM), fixed),
                  pl.BlockSpec((1, LANES), fixed),
                  pl.BlockSpec((1, LANES), fixed),
                  slab(wout), slab(wup), slab(wdown)],
        out_specs=[pl.BlockSpec((tm, ATTN_COLS), row),
                   pl.BlockSpec((tm, GDN_CONV_DIM), row),
                   pl.BlockSpec((tm, GDN_VAL_WIDTH), row),
                   pl.BlockSpec((tm, AB_COLS), row),
                   pl.BlockSpec((None, SUBLANES, GDN_CONV_DIM), lambda i: (i // tiles_per_seq, 0, 0)),
                   slab(wout), slab(wup), slab(wdown)],
        out_shape=[jax.ShapeDtypeStruct((m, ATTN_COLS), F32),
                   jax.ShapeDtypeStruct((m, GDN_CONV_DIM), F32),
                   jax.ShapeDtypeStruct((m, GDN_VAL_WIDTH), F32),
                   jax.ShapeDtypeStruct((m, AB_COLS), F32),
                   jax.ShapeDtypeStruct((m // seq, SUBLANES, GDN_CONV_DIM), F32),
                   jax.ShapeDtypeStruct(wout.shape, BF16),
                   jax.ShapeDtypeStruct(wup.shape, BF16),
                   jax.ShapeDtypeStruct(wdown.shape, BF16)],
        scratch_shapes=[pltpu.VMEM((SUBLANES, GDN_CONV_DIM), F32)],
        compiler_params=pltpu.CompilerParams(dimension_semantics=("arbitrary",),
                                             vmem_limit_bytes=VMEM_LIMIT),
        name="inproj",
    )(x, nw, w, cw, alog, dtb, wout, wup, wdown)


FF_UP_CHUNK = 256
FF_DOWN_CHUNK = 256
FF_STAGES = D_FF // FF_UP_CHUNK + D_MODEL // FF_DOWN_CHUNK


def _out_proj(x_ref, ao_ref, go_ref, z_ref, gnw_ref, wout_ref, npost_ref, npre_ref):
    gnw = gnw_ref[...]
    gy = []
    for hd in range(GDN_HEADS):
        lanes = slice(hd * LANES, (hd + 1) * LANES)
        z = z_ref[:, lanes]
        gy.append((_rms(go_ref[:, lanes], gnw) * (z * _sigmoid(z))).astype(BF16))
    m = (jnp.dot(ao_ref[...], wout_ref[0:ATTN_WIDTH, :], preferred_element_type=F32)
         + jnp.dot(jnp.concatenate(gy, axis=1), wout_ref[ATTN_WIDTH:, :], preferred_element_type=F32))
    x1 = x_ref[...] + _rms(m, npost_ref[...])
    return x1, _rms(x1, npre_ref[...]).astype(BF16)


def _out_ffn_stages(x_ref, ao_ref, go_ref, z_ref, gnw_ref, wout_ref, npost_ref, npre_ref, wup_ref,
                    wdown_ref, nffn_ref, y_ref):
    x1, h = _out_proj(x_ref, ao_ref, go_ref, z_ref, gnw_ref, wout_ref, npost_ref, npre_ref)
    yield
    hidden = []
    for j in range(0, D_FF, FF_UP_CHUNK):
        u = jnp.maximum(jnp.dot(h, wup_ref[:, j:j + FF_UP_CHUNK], preferred_element_type=F32), 0.0)
        hidden.append((u * u).astype(BF16))
        yield
    hidden = jnp.concatenate(hidden, axis=1)
    f = []
    for j in range(0, D_MODEL, FF_DOWN_CHUNK):
        f.append(jnp.dot(hidden, wdown_ref[:, j:j + FF_DOWN_CHUNK], preferred_element_type=F32))
        yield
    y_ref[...] = x1 + _rms(jnp.concatenate(f, axis=1), nffn_ref[...])


def _dup_half(x, lo):
    xr = pltpu.roll(x, HEAD_DIM, axis=1)
    return jnp.where(lo, x, xr), jnp.where(lo, xr, x)


def _attn_prompt_stages(sink_ref, cur_ref, prev_ref, seq_row0, o_ref):
    tq = cur_ref.shape[0]
    kall = jnp.concatenate([prev_ref[:, 0:KV_WIDTH], cur_ref[:, ATTN_WIDTH:ATTN_WIDTH + KV_WIDTH]], axis=0)
    vall = jnp.concatenate([prev_ref[:, KV_WIDTH:], cur_ref[:, ATTN_WIDTH + KV_WIDTH:]], axis=0)
    lo = lax.broadcasted_iota(jnp.int32, (1, LANES), 1) < HEAD_DIM
    k2 = _dup_half(kall, lo)
    v2 = _dup_half(vall, lo)
    qi = lax.broadcasted_iota(jnp.int32, (WINDOW, 2 * WINDOW), 0)
    si = lax.broadcasted_iota(jnp.int32, (WINDOW, 2 * WINDOW), 1)
    dist = WINDOW + qi - si
    band = (dist >= 0) & (dist <= WINDOW)
    distf = dist.astype(F32)
    for r in range(tq // WINDOW):
        rows = slice(r * WINDOW, (r + 1) * WINDOW)
        valid = band
        if r == 0:
            valid = band & ((seq_row0 - WINDOW + si) >= 0)
        for j in range(KV_HEADS):
            kw = k2[j][r * WINDOW:(r + 2) * WINDOW]
            vw = v2[j][r * WINDOW:(r + 2) * WINDOW]
            t0 = cur_ref[rows, (2 * j) * LANES:(2 * j + 1) * LANES]
            t1 = cur_ref[rows, (2 * j + 1) * LANES:(2 * j + 2) * LANES]
            lhs = jnp.concatenate([jnp.where(lo, t0, 0.0), jnp.where(lo, 0.0, t0),
                                   jnp.where(lo, t1, 0.0), jnp.where(lo, 0.0, t1)], axis=0)
            s = _dot_nt(lhs, kw) * (HEAD_DIM ** -0.5)
            probs = []
            for g in range(GQA_GROUP):
                h = j * GQA_GROUP + g
                sg = s[g * WINDOW:(g + 1) * WINDOW] - _alibi_slope(h) * distf
                sg = jnp.where(valid, sg, -jnp.inf)
                sink = sink_ref[0, h]
                mx = jnp.maximum(jnp.max(sg, axis=-1, keepdims=True), sink)
                p = jnp.exp(sg - mx)
                den = jnp.sum(p, axis=-1, keepdims=True) + jnp.exp(sink - mx)
                probs.append(p / den)
            yield
            o = _dot(jnp.concatenate(probs, axis=0), vw)
            o01 = jnp.where(lo, o[0:WINDOW], o[WINDOW:2 * WINDOW])
            o23 = jnp.where(lo, o[2 * WINDOW:3 * WINDOW], o[3 * WINDOW:4 * WINDOW])
            o_ref[rows, (2 * j) * LANES:(2 * j + 1) * LANES] = o01.astype(o_ref.dtype)
            o_ref[rows, (2 * j + 1) * LANES:(2 * j + 2) * LANES] = o23.astype(o_ref.dtype)
            yield


def _gdn_prompt_stages(conv_ref, gb_ref, state, o_ref):
    t = conv_ref.shape[0]
    beta = gb_ref[...]
    g = beta
    rin = lax.broadcasted_iota(jnp.int32, (t, 1), 0) & (CHUNK - 1)
    step = 1
    while step < CHUNK:
        g = g + jnp.where(rin >= step, pltpu.roll(g, step, axis=0), 0.0)
        step *= 2
    gt = g.T

    row = lax.broadcasted_iota(jnp.int32, (CHUNK, CHUNK), 0)
    col = lax.broadcasted_iota(jnp.int32, (CHUNK, CHUNK), 1)
    lower_incl = row >= col
    lower_strict = row > col

    heads = range(GDN_HEADS)
    units = [(c, h) for c in range(t // CHUNK) for h in heads]
    pre = []
    for c, h in units:
        rows = slice(c * CHUNK, (c + 1) * CHUNK)
        qc = _l2n(_silu(conv_ref[rows, h * LANES:(h + 1) * LANES])) * (GDN_DK ** -0.5)
        kc = _l2n(_silu(conv_ref[rows, GDN_KEY_WIDTH + h * LANES:GDN_KEY_WIDTH + (h + 1) * LANES]))
        vc = _silu(conv_ref[rows, 2 * GDN_KEY_WIDTH + h * LANES:2 * GDN_KEY_WIDTH + (h + 1) * LANES])
        gc = g[rows, h:h + 1]
        bc = beta[rows, GDN_HEADS + h:GDN_HEADS + h + 1]
        gr = gt[h:h + 1, c * CHUNK:(c + 1) * CHUNK]
        decay = jnp.where(lower_incl, jnp.exp(jnp.where(lower_incl, gc - gr, 0.0)), 0.0)
        eg = jnp.exp(gc)
        glast = gc[CHUNK - 1:CHUNK]
        rhs = jnp.concatenate([vc * bc, kc * (bc * eg)], axis=1)
        k_dec_t = (kc * jnp.exp(glast - gc)).T
        kq = _dot_nt(jnp.concatenate([kc, qc], axis=0), kc)
        a = jnp.where(lower_strict, bc * kq[0:CHUNK] * decay, 0.0)
        qk = jnp.where(lower_incl, kq[CHUNK:] * decay, 0.0)
        pre.append(dict(a=a, rhs=rhs, lhs_u=jnp.concatenate([qk, k_dec_t], axis=0), q_dec=qc * eg,
                        ld=jnp.exp(glast)))
        if h == GDN_HEADS - 1:
            yield

    mats = [p["a"] for p in pre]
    first = ((row >> 1) == (col >> 1)) & ((row & 1) == 1) & ((col & 1) == 0)
    xo = [jnp.where(first, -a, 0.0) for a in mats]
    size = 2
    while size < CHUNK:
        shift = size.bit_length()
        m = ((row >> shift) == (col >> shift)) & ((row & size) != 0) & ((col & size) == 0)
        aoff = [jnp.where(m, a, 0.0) for a in mats]
        y = [ao + _dot(ao, x) for ao, x in zip(aoff, xo)]
        yield
        xo = [x - yy - _dot(x, yy) for x, yy in zip(xo, y)]
        size *= 2
        yield
    trs = [p["rhs"] + _dot(x, p["rhs"]) for p, x in zip(pre, xo)]
    yield

    for c in range(t // CHUNK):
        rows = slice(c * CHUNK, (c + 1) * CHUNK)
        idx = [c * GDN_HEADS + h for h in heads]
        ws = [_dot(jnp.concatenate([trs[u][:, GDN_DV:], pre[u]["q_dec"]], axis=0), state[h])
              for h, u in zip(heads, idx)]
        yield
        ou = [_dot(pre[u]["lhs_u"], trs[u][:, 0:GDN_DV] - ws[h][0:CHUNK]) for h, u in zip(heads, idx)]
        for h, u in zip(heads, idx):
            o_ref[rows, h * LANES:(h + 1) * LANES] = ws[h][CHUNK:] + ou[h][0:CHUNK]
            state[h] = state[h] * pre[u]["ld"] + ou[h][CHUNK:]
        yield


def _run_interleaved(primary, n_primary, secondary, n_secondary):
    done = object()
    issued = 0
    for k in range(n_primary):
        next(primary, done)
        target = ((k + 1) * n_secondary + n_primary - 1) // n_primary
        while issued < target:
            next(secondary, done)
            issued += 1
    for _ in primary:
        pass
    for _ in secondary:
        pass


def _roundrobin(*gens):
    done = object()
    live = list(gens)
    while live:
        for gen in list(live):
            if next(gen, done) is done:
                live.remove(gen)
            else:
                yield


MIX_T = 512


def _mix_ffn_kernel(sink_ref, pa_ref, prev_ref, conv_ref, gb_ref, x_ref, z_ref, gnw_ref, wout_ref,
                    npost_ref, npre_ref, wup_ref, wdown_ref, nffn_ref,
                    y_ref, s_out_ref, s_scr, ao_scr, go_scr, *, n_tiles, tiles_per_seq):
    i = pl.program_id(0)

    @pl.when(i == 0)
    def _():
        s_scr[...] = jnp.zeros_like(s_scr)
        ao_scr[...] = jnp.zeros_like(ao_scr)
        go_scr[...] = jnp.zeros_like(go_scr)

    front_tile = jnp.minimum(i, n_tiles - 1)
    tile_in_seq = lax.rem(front_tile, tiles_per_seq)
    seq_start = tile_in_seq == 0
    front_valid = i < n_tiles

    ffn = _out_ffn_stages(x_ref, ao_scr, go_scr, z_ref, gnw_ref, wout_ref, npost_ref, npre_ref,
                          wup_ref, wdown_ref, nffn_ref, y_ref)
    next(ffn)

    old = [s_scr[h] for h in range(GDN_HEADS)]
    state = [jnp.where(seq_start, 0.0, s) for s in old]
    front = _roundrobin(_gdn_prompt_stages(conv_ref, gb_ref, state, go_scr),
                        _attn_prompt_stages(sink_ref, pa_ref, prev_ref, tile_in_seq * MIX_T, ao_scr))
    mix_stages = (MIX_T // CHUNK) * 3 + 2 * (CHUNK.bit_length() - 2) + 1 + 2 * (MIX_T // WINDOW) * KV_HEADS
    _run_interleaved(ffn, FF_STAGES, front, mix_stages)

    for h in range(GDN_HEADS):
        s_new = jnp.where(front_valid, state[h], old[h])
        s_scr[h] = s_new
        s_out_ref[h] = s_new


def _mix_ffn(sinks, pa, act, gb, x, z, gnw, wout, npost, npre, wup, wdown, nffn, seq):
    m = x.shape[0]
    n_tiles = m // MIX_T
    tiles_per_seq = seq // MIX_T
    blocks_per_tile = MIX_T // WINDOW
    front = lambda i: (jnp.minimum(i, n_tiles - 1), 0)
    back = lambda i: (jnp.maximum(i - 1, 0), 0)
    fixed = lambda i: (0, 0)
    resident = dict(pipeline_mode=pl.Buffered(1))
    kern = functools.partial(_mix_ffn_kernel, n_tiles=n_tiles, tiles_per_seq=tiles_per_seq)
    return pl.pallas_call(
        kern,
        grid=(n_tiles + 1,),
        in_specs=[pl.BlockSpec(memory_space=pltpu.SMEM),
                  pl.BlockSpec((MIX_T, ATTN_COLS), front),
                  pl.BlockSpec((WINDOW, 2 * KV_WIDTH),
                               lambda i: (jnp.maximum(jnp.minimum(i, n_tiles - 1) * blocks_per_tile - 1, 0),
                                          ATTN_WIDTH // (2 * KV_WIDTH))),
                  pl.BlockSpec((MIX_T, GDN_CONV_DIM), front),
                  pl.BlockSpec((MIX_T, AB_COLS), front),
                  pl.BlockSpec((MIX_T, D_MODEL), back),
                  pl.BlockSpec((MIX_T, GDN_VAL_WIDTH), back),
                  pl.BlockSpec((1, GDN_DV), fixed),
                  pl.BlockSpec((D_MODEL, D_MODEL), fixed, **resident),
                  pl.BlockSpec((1, D_MODEL), fixed),
                  pl.BlockSpec((1, D_MODEL), fixed),
                  pl.BlockSpec((D_MODEL, D_FF), fixed, **resident),
                  pl.BlockSpec((D_FF, D_MODEL), fixed, **resident),
                  pl.BlockSpec((1, D_MODEL), fixed)],
        out_specs=[pl.BlockSpec((MIX_T, D_MODEL), back),
                   pl.BlockSpec((None, GDN_HEADS, GDN_DK, GDN_DV),
                                lambda i: (jnp.minimum(i, n_tiles - 1) // tiles_per_seq, 0, 0, 0))],
        out_shape=[jax.ShapeDtypeStruct((m, D_MODEL), F32),
                   jax.ShapeDtypeStruct((m // seq, GDN_HEADS, GDN_DK, GDN_DV), F32)],
        scratch_shapes=[pltpu.VMEM((GDN_HEADS, GDN_DK, GDN_DV), F32),
                        pltpu.VMEM((MIX_T, ATTN_WIDTH), BF16),
                        pltpu.VMEM((MIX_T, GDN_VAL_WIDTH), F32)],
        compiler_params=pltpu.CompilerParams(dimension_semantics=("arbitrary",),
                                             vmem_limit_bytes=VMEM_LIMIT_MIX),
        name="mix_ffn",
    )(sinks, pa, pa, act, gb, x, z, gnw, wout, npost, npre, wup, wdown, nffn)


SAMPLE_FF_CHUNK = 512


def _out_ffn_kernel(x_ref, ao_ref, go_ref, z_ref, gnw_ref, wout_ref, npost_ref, npre_ref, wup_ref,
                    wdown_ref, nffn_ref, y_ref, x1_scr, h_scr, acc_scr):
    j = pl.program_id(0)

    @pl.when(j == 0)
    def _():
        x1, h = _out_proj(x_ref, ao_ref, go_ref, z_ref, gnw_ref, wout_ref, npost_ref, npre_ref)
        x1_scr[...] = x1
        h_scr[...] = h
        acc_scr[...] = jnp.zeros_like(acc_scr)

    u = jnp.maximum(jnp.dot(h_scr[...], wup_ref[...], preferred_element_type=F32), 0.0)
    acc_scr[...] += jnp.dot((u * u).astype(BF16), wdown_ref[...], preferred_element_type=F32)

    @pl.when(j == pl.num_programs(0) - 1)
    def _():
        y_ref[...] = x1_scr[...] + _rms(acc_scr[...], nffn_ref[...])


def _out_ffn(x, ao, go, z, gnw, wout, npost, npre, wup, wdown, nffn):
    m = x.shape[0]
    fixed = lambda j: (0, 0)
    return pl.pallas_call(
        _out_ffn_kernel,
        grid=(D_FF // SAMPLE_FF_CHUNK,),
        in_specs=[pl.BlockSpec((m, D_MODEL), fixed),
                  pl.BlockSpec((m, ATTN_WIDTH), fixed),
                  pl.BlockSpec((m, GDN_VAL_WIDTH), fixed),
                  pl.BlockSpec((m, GDN_VAL_WIDTH), fixed),
                  pl.BlockSpec((1, GDN_DV), fixed),
                  pl.BlockSpec((D_MODEL, D_MODEL), fixed),
                  pl.BlockSpec((1, D_MODEL), fixed),
                  pl.BlockSpec((1, D_MODEL), fixed),
                  pl.BlockSpec((D_MODEL, SAMPLE_FF_CHUNK), lambda j: (0, j)),
                  pl.BlockSpec((SAMPLE_FF_CHUNK, D_MODEL), lambda j: (j, 0)),
                  pl.BlockSpec((1, D_MODEL), fixed)],
        out_specs=pl.BlockSpec((m, D_MODEL), fixed),
        out_shape=jax.ShapeDtypeStruct((m, D_MODEL), F32),
        scratch_shapes=[pltpu.VMEM((m, D_MODEL), F32), pltpu.VMEM((m, D_MODEL), BF16),
                        pltpu.VMEM((m, D_MODEL), F32)],
        compiler_params=pltpu.CompilerParams(dimension_semantics=("arbitrary",),
                                             vmem_limit_bytes=VMEM_LIMIT),
        name="out_ffn",
    )(x, ao, go, z, gnw, wout, npost, npre, wup, wdown, nffn)


def _sample_prep_kernel(x_ref, nw_ref, w_ref, cs_ref, cw_ref, alog_ref, dtb_ref,
                        qexp_ref, knew_ref, vnew_ref, qg_ref, kt_ref, vg_ref, gate_ref, z_ref,
                        cs_out_ref):
    h = _rms(x_ref[...], nw_ref[...]).astype(BF16)
    pa = jnp.dot(h, w_ref[:, 0:OFF_GDN], preferred_element_type=F32)
    pg = jnp.dot(h, w_ref[:, OFF_GDN:OFF_Z], preferred_element_type=F32)
    z_ref[...] = jnp.dot(h, w_ref[:, OFF_Z:OFF_AB], preferred_element_type=F32)
    ab = jnp.dot(h, w_ref[:, OFF_AB:IN_COLS], preferred_element_type=F32)

    lo = lax.broadcasted_iota(jnp.int32, (1, LANES), 1) < HEAD_DIM
    for t in range(ATTN_WIDTH // LANES):
        tile = pa[:, t * LANES:(t + 1) * LANES]
        rolled = pltpu.roll(tile, HEAD_DIM, axis=1)
        if t < ATTN_WIDTH // LANES // KV_HEADS:
            even, odd = jnp.where(lo, tile, 0.0), jnp.where(lo, rolled, 0.0)
        else:
            even, odd = jnp.where(lo, 0.0, rolled), jnp.where(lo, 0.0, tile)
        qexp_ref[2 * t] = even
        qexp_ref[2 * t + 1] = odd
    knew_ref[...] = pa[:, ATTN_WIDTH:ATTN_WIDTH + KV_WIDTH]
    vnew_ref[...] = pa[:, ATTN_WIDTH + KV_WIDTH:ATTN_COLS]

    cw = cw_ref[...]
    y = None
    for r in range(CONV_W - 1):
        term = cs_ref[:, r * GDN_CONV_DIM:(r + 1) * GDN_CONV_DIM] * cw[r:r + 1]
        y = term if y is None else y + term
    y = y + pg * cw[CONV_W - 1:CONV_W]
    act = y * _sigmoid(y)
    cs_out_ref[:, 0:(CONV_W - 2) * GDN_CONV_DIM] = cs_ref[:, GDN_CONV_DIM:(CONV_W - 1) * GDN_CONV_DIM]
    cs_out_ref[:, (CONV_W - 2) * GDN_CONV_DIM:] = pg

    for hd in range(GDN_HEADS):
        qn = _l2n(act[:, hd * LANES:(hd + 1) * LANES]) * (GDN_DK ** -0.5)
        kn = _l2n(act[:, GDN_KEY_WIDTH + hd * LANES:GDN_KEY_WIDTH + (hd + 1) * LANES])
        qg_ref[:, hd * LANES:(hd + 1) * LANES] = qn
        kt_ref[hd] = kn.T
    vg_ref[...] = act[:, 2 * GDN_KEY_WIDTH:]
    gb = _gates(ab, alog_ref[...], dtb_ref[...])
    lane = lax.broadcasted_iota(jnp.int32, (1, LANES), 1)
    gate_ref[...] = jnp.where(lane < GDN_HEADS, jnp.exp(gb), gb)


def _sample_prep(x, nw, w, cs, cw, alog, dtb):
    n = x.shape[0]
    return pl.pallas_call(
        _sample_prep_kernel,
        out_shape=[jax.ShapeDtypeStruct((ATTN_HEADS, n, LANES), F32),
                   jax.ShapeDtypeStruct((n, KV_WIDTH), F32),
                   jax.ShapeDtypeStruct((n, KV_WIDTH), F32),
                   jax.ShapeDtypeStruct((n, GDN_KEY_WIDTH), F32),
                   jax.ShapeDtypeStruct((GDN_HEADS, GDN_DK, n), F32),
                   jax.ShapeDtypeStruct((n, GDN_VAL_WIDTH), F32),
                   jax.ShapeDtypeStruct((n, LANES), F32),
                   jax.ShapeDtypeStruct((n, GDN_VAL_WIDTH), F32),
                   jax.ShapeDtypeStruct((n, (CONV_W - 1) * GDN_CONV_DIM), F32)],
        compiler_params=pltpu.CompilerParams(vmem_limit_bytes=VMEM_LIMIT),
        name="sample_prep",
    )(x, nw, w, cs, cw, alog, dtb)


ATTN_BB = 16


def _attn_sample_kernel(sink_ref, qexp_ref, knew_ref, vnew_ref, kbuf_ref, vbuf_ref,
                        o_ref, kout_ref, vout_ref, q_scr, o_scr):
    for h in range(ATTN_HEADS):
        q_scr[pl.ds(h, ATTN_BB, stride=ATTN_HEADS), :] = qexp_ref[h]
    hrow = lax.broadcasted_iota(jnp.int32, (ATTN_HEADS, 1), 0)
    sink = jnp.zeros((ATTN_HEADS, 1), F32)
    slope = jnp.zeros((ATTN_HEADS, 1), F32)
    for h in range(ATTN_HEADS):
        sink = jnp.where(hrow == h, sink_ref[0, h], sink)
        slope = jnp.where(hrow == h, _alibi_slope(h), slope)
    pos = lax.broadcasted_iota(jnp.int32, (1, WINDOW), 1)
    bias = slope * (WINDOW - pos).astype(F32)
    rowid = lax.broadcasted_iota(jnp.int32, (WINDOW, 1), 0)
    last = rowid == WINDOW - 1
    toks = range(ATTN_BB)
    qe = [q_scr[b * ATTN_HEADS:(b + 1) * ATTN_HEADS, :] for b in toks]
    s_old = [_dot_nt(qe[b], kbuf_ref[b]) * (HEAD_DIM ** -0.5) - bias for b in toks]
    probs, p_new = [], []
    for b in toks:
        s_new = jnp.sum(qe[b] * knew_ref[b:b + 1, :], axis=-1, keepdims=True) * (HEAD_DIM ** -0.5)
        mx = jnp.maximum(jnp.maximum(jnp.max(s_old[b], axis=-1, keepdims=True), s_new), sink)
        p_old = jnp.exp(s_old[b] - mx)
        pn = jnp.exp(s_new - mx)
        den = jnp.sum(p_old, axis=-1, keepdims=True) + pn + jnp.exp(sink - mx)
        probs.append(p_old / den)
        p_new.append(pn / den)
    for b in toks:
        o_scr[b * ATTN_HEADS:(b + 1) * ATTN_HEADS, :] = (_dot(probs[b], vbuf_ref[b])
                                                         + p_new[b] * vnew_ref[b:b + 1, :])
    for b in toks:
        kout_ref[b] = jnp.where(last, knew_ref[b:b + 1, :], pltpu.roll(kbuf_ref[b], WINDOW - 1, axis=0))
        vout_ref[b] = jnp.where(last, vnew_ref[b:b + 1, :], pltpu.roll(vbuf_ref[b], WINDOW - 1, axis=0))
    lo = lax.broadcasted_iota(jnp.int32, (1, LANES), 1) < HEAD_DIM
    for t in range(ATTN_WIDTH // LANES):
        even = o_scr[pl.ds(2 * t, ATTN_BB, stride=ATTN_HEADS), :]
        odd = o_scr[pl.ds(2 * t + 1, ATTN_BB, stride=ATTN_HEADS), :]
        if t < ATTN_WIDTH // LANES // KV_HEADS:
            tile = jnp.where(lo, even, pltpu.roll(odd, HEAD_DIM, axis=1))
        else:
            tile = jnp.where(lo, pltpu.roll(even, HEAD_DIM, axis=1), odd)
        o_ref[:, t * LANES:(t + 1) * LANES] = tile.astype(o_ref.dtype)


def _attn_sample(sinks, qexp, knew, vnew, kbuf, vbuf):
    n = knew.shape[0]
    tok2 = lambda i: (i, 0)
    tok3 = lambda i: (i, 0, 0)
    return pl.pallas_call(
        _attn_sample_kernel,
        grid=(n // ATTN_BB,),
        in_specs=[pl.BlockSpec(memory_space=pltpu.SMEM),
                  pl.BlockSpec((ATTN_HEADS, ATTN_BB, LANES), lambda i: (0, i, 0)),
                  pl.BlockSpec((ATTN_BB, KV_WIDTH), tok2),
                  pl.BlockSpec((ATTN_BB, KV_WIDTH), tok2),
                  pl.BlockSpec((ATTN_BB, WINDOW, KV_WIDTH), tok3),
                  pl.BlockSpec((ATTN_BB, WINDOW, KV_WIDTH), tok3)],
        out_specs=[pl.BlockSpec((ATTN_BB, ATTN_WIDTH), tok2),
                   pl.BlockSpec((ATTN_BB, WINDOW, KV_WIDTH), tok3),
                   pl.BlockSpec((ATTN_BB, WINDOW, KV_WIDTH), tok3)],
        out_shape=[jax.ShapeDtypeStruct((n, ATTN_WIDTH), BF16),
                   jax.ShapeDtypeStruct((n, WINDOW, KV_WIDTH), F32),
                   jax.ShapeDtypeStruct((n, WINDOW, KV_WIDTH), F32)],
        scratch_shapes=[pltpu.VMEM((ATTN_BB * ATTN_HEADS, LANES), F32),
                        pltpu.VMEM((ATTN_BB * ATTN_HEADS, LANES), F32)],
        compiler_params=pltpu.CompilerParams(dimension_semantics=("arbitrary",),
                                             vmem_limit_bytes=VMEM_LIMIT),
        name="attn_sample",
    )(sinks, qexp, knew, vnew, kbuf, vbuf)


GDN_BB = 16


def _gdn_sample_kernel(gate_ref, q_ref, kt_ref, vg_ref, s_ref, o_ref, s_out_ref):
    i = pl.program_id(0)
    n = kt_ref.shape[-1]
    shift = (n - i * GDN_BB) % n
    units = [(j, h) for h in range(GDN_HEADS) for j in range(GDN_BB)]
    kt = [pltpu.roll(kt_ref[h], shift, axis=1) for h in range(GDN_HEADS)]
    kb = [jnp.broadcast_to(kt[h][:, j:j + 1], (GDN_DK, GDN_DV)) for j, h in units]
    s = [s_ref[j, h] * gate_ref[(i * GDN_BB + j) * SUBLANES + h] for j, h in units]
    kv = [jnp.sum(su * k, axis=0, keepdims=True) for su, k in zip(s, kb)]
    u = [(vg_ref[j:j + 1, h * LANES:(h + 1) * LANES] - kvu)
         * gate_ref[(i * GDN_BB + j) * SUBLANES + GDN_HEADS + h] for (j, h), kvu in zip(units, kv)]
    s = [su + k * uu for su, k, uu in zip(s, kb, u)]
    for (j, h), su in zip(units, s):
        s_out_ref[j, h] = su
    for (j, h), su in zip(units, s):
        lanes = slice(h * LANES, (h + 1) * LANES)
        o_ref[j:j + 1, lanes] = _dot(q_ref[:, lanes], su)[j:j + 1, :]


def _gdn_sample(gate, q, kt, vg, s0):
    n = vg.shape[0]
    return pl.pallas_call(
        _gdn_sample_kernel,
        grid=(n // GDN_BB,),
        in_specs=[pl.BlockSpec(memory_space=pltpu.SMEM),
                  pl.BlockSpec((GDN_BB, GDN_KEY_WIDTH), lambda i: (i, 0)),
                  pl.BlockSpec((GDN_HEADS, GDN_DK, n), lambda i: (0, 0, 0)),
                  pl.BlockSpec((GDN_BB, GDN_VAL_WIDTH), lambda i: (i, 0)),
                  pl.BlockSpec((GDN_BB, GDN_HEADS, GDN_DK, GDN_DV), lambda i: (i, 0, 0, 0))],
        out_specs=[pl.BlockSpec((GDN_BB, GDN_VAL_WIDTH), lambda i: (i, 0)),
                   pl.BlockSpec((GDN_BB, GDN_HEADS, GDN_DK, GDN_DV), lambda i: (i, 0, 0, 0))],
        out_shape=[jax.ShapeDtypeStruct((n, GDN_VAL_WIDTH), F32),
                   jax.ShapeDtypeStruct((n, GDN_HEADS, GDN_DK, GDN_DV), F32)],
        compiler_params=pltpu.CompilerParams(dimension_semantics=("arbitrary",),
                                             vmem_limit_bytes=VMEM_LIMIT),
        name="gdn_sample",
    )(gate, q, kt, vg, s0)


def _sample_mixers_kernel(sink_ref, gate_ref, qexp_ref, knew_ref, vnew_ref, kbuf_ref, vbuf_ref,
                          q_ref, kt_ref, vg_ref, s_ref,
                          ao_ref, kout_ref, vout_ref, go_ref, s_out_ref, q_scr, o_scr):
    _attn_sample_kernel(sink_ref, qexp_ref, knew_ref, vnew_ref, kbuf_ref, vbuf_ref,
                        ao_ref, kout_ref, vout_ref, q_scr, o_scr)
    _gdn_sample_kernel(gate_ref, q_ref, kt_ref, vg_ref, s_ref, go_ref, s_out_ref)


def _sample_mixers(sinks, gate, qexp, knew, vnew, kbuf, vbuf, q, kt, vg, s0):
    assert ATTN_BB == GDN_BB
    n = knew.shape[0]
    bb = ATTN_BB
    tok2 = lambda i: (i, 0)
    tok3 = lambda i: (i, 0, 0)
    tok4 = lambda i: (i, 0, 0, 0)
    smem = pl.BlockSpec(memory_space=pltpu.SMEM)
    cache = pl.BlockSpec((bb, WINDOW, KV_WIDTH), tok3)
    state = pl.BlockSpec((bb, GDN_HEADS, GDN_DK, GDN_DV), tok4)
    return pl.pallas_call(
        _sample_mixers_kernel,
        grid=(n // bb,),
        in_specs=[smem, smem,
                  pl.BlockSpec((ATTN_HEADS, bb, LANES), lambda i: (0, i, 0)),
                  pl.BlockSpec((bb, KV_WIDTH), tok2),
                  pl.BlockSpec((bb, KV_WIDTH), tok2),
                  cache, cache,
                  pl.BlockSpec((bb, GDN_KEY_WIDTH), tok2),
                  pl.BlockSpec((GDN_HEADS, GDN_DK, n), lambda i: (0, 0, 0)),
                  pl.BlockSpec((bb, GDN_VAL_WIDTH), tok2),
                  state],
        out_specs=[pl.BlockSpec((bb, ATTN_WIDTH), tok2), cache, cache,
                   pl.BlockSpec((bb, GDN_VAL_WIDTH), tok2), state],
        out_shape=[jax.ShapeDtypeStruct((n, ATTN_WIDTH), BF16),
                   jax.ShapeDtypeStruct((n, WINDOW, KV_WIDTH), F32),
                   jax.ShapeDtypeStruct((n, WINDOW, KV_WIDTH), F32),
                   jax.ShapeDtypeStruct((n, GDN_VAL_WIDTH), F32),
                   jax.ShapeDtypeStruct((n, GDN_HEADS, GDN_DK, GDN_DV), F32)],
        scratch_shapes=[pltpu.VMEM((bb * ATTN_HEADS, LANES), F32),
                        pltpu.VMEM((bb * ATTN_HEADS, LANES), F32)],
        compiler_params=pltpu.CompilerParams(dimension_semantics=("arbitrary",),
                                             vmem_limit_bytes=VMEM_LIMIT),
        name="sample_mixers",
    )(sinks, gate, qexp, knew, vnew, kbuf, vbuf, q, kt, vg, s0)


def _pad_lanes(v):
    return jnp.pad(v.astype(F32), (0, LANES - v.shape[0])).reshape(1, LANES)


def kernel(x_prompt, x_sample, state_conv, cache_win_k, cache_win_v, state_gdn, norm_mix_pre, w_in,
           attn_sinks, conv_w, gdn_a_log, gdn_dt_bias, gdn_norm, w_out, norm_mix_post, norm_ffn_pre,
           w_up, w_down, norm_ffn_post):
    assert w_in.shape[0] == 1, "single-layer trunk"
    bsz, seq, _ = x_prompt.shape
    nsamp = x_sample.shape[0]

    w_in_p = jnp.pad(w_in[0], ((0, 0), (0, IN_COLS - w_in.shape[2]))).astype(BF16)
    n_pre = norm_mix_pre[0].reshape(1, D_MODEL)
    n_post = norm_mix_post[0].reshape(1, D_MODEL)
    n_fpre = norm_ffn_pre[0].reshape(1, D_MODEL)
    n_fpost = norm_ffn_post[0].reshape(1, D_MODEL)
    sinks = attn_sinks[0].reshape(1, ATTN_HEADS)
    cw = conv_w[0]
    alog = _pad_lanes(gdn_a_log[0])
    dtb = _pad_lanes(gdn_dt_bias[0])
    gnw = gdn_norm[0].reshape(1, GDN_DV)

    xp = x_prompt.reshape(bsz * seq, D_MODEL)
    pa, act, pz, gb, p_conv, w_out_b, w_up_b, w_down_b = _inproj(
        xp, n_pre, w_in_p, cw, alog, dtb, w_out[0], w_up[0], w_down[0], INPROJ_T, seq)
    yp, p_s = _mix_ffn(sinks, pa, act, gb, xp, pz, gnw, w_out_b, n_post, n_fpre, w_up_b, w_down_b,
                       n_fpost, seq)
    pa = pa.reshape(bsz, seq, ATTN_COLS)

    xs = x_sample.reshape(nsamp, D_MODEL)
    cs = state_conv[0].reshape(nsamp, (CONV_W - 1) * GDN_CONV_DIM)
    qexp, knew, vnew, qg, kt, vg, gate, zs, cs_new = _sample_prep(xs, n_pre, w_in_p, cs, cw, alog, dtb)
    kbuf = cache_win_k[0].reshape(nsamp, WINDOW, KV_WIDTH)
    vbuf = cache_win_v[0].reshape(nsamp, WINDOW, KV_WIDTH)
    ao_s, k_new_cache, v_new_cache, go_s, s_new = _sample_mixers(
        sinks, gate[:, :SUBLANES].reshape(nsamp * SUBLANES), qexp, knew, vnew, kbuf, vbuf, qg, kt, vg,
        state_gdn[0])
    ys = _out_ffn(xs, ao_s, go_s, zs, gnw, w_out_b, n_post, n_fpre, w_up_b, w_down_b, n_fpost)

    kv_shape = (1, -1, WINDOW, KV_HEADS, HEAD_DIM)
    return (yp.reshape(bsz, seq, D_MODEL),
            ys.reshape(nsamp, 1, D_MODEL),
            p_conv[:, SUBLANES - (CONV_W - 1):, :][None],
            pa[:, seq - WINDOW:, ATTN_WIDTH:ATTN_WIDTH + KV_WIDTH].reshape(kv_shape),
            pa[:, seq - WINDOW:, ATTN_WIDTH + KV_WIDTH:].reshape(kv_shape),
            p_s[None],
            cs_new.reshape(1, nsamp, CONV_W - 1, GDN_CONV_DIM),
            k_new_cache.reshape(kv_shape),
            v_new_cache.reshape(kv_shape),
            s_new[None])
```

```python
import functools

import jax
import jax.numpy as jnp
from jax import lax
from jax.experimental import pallas as pl
from jax.experimental.pallas import tpu as pltpu

F32 = jnp.float32
BF16 = jnp.bfloat16

D_MODEL = 1024
ATTN_HEADS = 8
KV_HEADS = 2
HEAD_DIM = 64
GQA_GROUP = ATTN_HEADS // KV_HEADS
ATTN_WIDTH = ATTN_HEADS * HEAD_DIM
KV_WIDTH = KV_HEADS * HEAD_DIM
WINDOW = 128
GDN_HEADS = 4
GDN_DK = 128
GDN_DV = 128
GDN_KEY_WIDTH = GDN_HEADS * GDN_DK
GDN_VAL_WIDTH = GDN_HEADS * GDN_DV
GDN_CONV_DIM = 2 * GDN_KEY_WIDTH + GDN_VAL_WIDTH
CONV_W = 4
D_FF = 4 * D_MODEL
EPS = 1e-6

LANES = 128
SUBLANES = 8
ATTN_COLS = ATTN_WIDTH + 2 * KV_WIDTH
AB_COLS = LANES
IN_COLS = ATTN_COLS + GDN_CONV_DIM + GDN_VAL_WIDTH + AB_COLS
OFF_GDN = ATTN_COLS
OFF_Z = OFF_GDN + GDN_CONV_DIM
OFF_AB = OFF_Z + GDN_VAL_WIDTH
CHUNK = 128
V7X_VMEM_BYTES = 64 * 1024 * 1024
VMEM_LIMIT = V7X_VMEM_BYTES - 8 * 1024 * 1024
VMEM_LIMIT_MIX = V7X_VMEM_BYTES - 4 * 1024 * 1024


def _rms(x, w):
    return x * lax.rsqrt(jnp.mean(x * x, axis=-1, keepdims=True) + EPS) * w


def _sigmoid(x):
    return 1.0 / (1.0 + jnp.exp(-x))


def _softplus(x):
    return jnp.maximum(x, 0.0) + jnp.log(1.0 + jnp.exp(-jnp.abs(x)))


def _dot(a, b):
    return jnp.dot(a.astype(BF16), b.astype(BF16), preferred_element_type=F32)


def _dot_nt(a, b):
    return lax.dot_general(a.astype(BF16), b.astype(BF16), (((1,), (1,)), ((), ())),
                           preferred_element_type=F32)


def _alibi_slope(h):
    return 2.0 ** (-8.0 * (h + 1) / ATTN_HEADS)


def _causal_conv(x, carry, w):
    t, c = x.shape
    x3 = x.reshape(t // SUBLANES, SUBLANES, c)
    row8 = lax.broadcasted_iota(jnp.int32, (1, SUBLANES, 1), 1)
    y = None
    for shift in range(CONV_W - 1, -1, -1):
        if shift == 0:
            xs = x3
        else:
            xr = pltpu.roll(x3, shift, axis=1)
            cr = pltpu.roll(carry, shift, axis=0)
            prev = jnp.concatenate([cr[None], xr[:-1]], axis=0)
            xs = jnp.where(row8 < shift, prev, xr)
        term = xs * w[CONV_W - 1 - shift:CONV_W - shift][None]
        y = term if y is None else y + term
    return y.reshape(t, c)


def _silu(y):
    return y * _sigmoid(y)


def _l2n(x):
    return x * lax.rsqrt(jnp.sum(x * x, axis=-1, keepdims=True) + EPS)


def _gates(ab, alog, dtb):
    g = -jnp.exp(alog) * _softplus(ab + dtb)
    lane = lax.broadcasted_iota(jnp.int32, (1, LANES), 1)
    return jnp.where(lane < GDN_HEADS, g, _sigmoid(ab))


INPROJ_T = 1024
INPROJ_SUB = 128


def _inproj_kernel(x_ref, nw_ref, w_ref, cw_ref, alog_ref, dtb_ref, wout_ref, wup_ref, wdown_ref,
                   oa_ref, og_ref, oz_ref, ogb_ref, conv_out_ref, wout_b_ref, wup_b_ref, wdown_b_ref,
                   carry_scr, *, tiles_per_seq):
    tm = x_ref.shape[0]
    wout_b_ref[...] = wout_ref[...].astype(BF16)
    wup_b_ref[...] = wup_ref[...].astype(BF16)
    wdown_b_ref[...] = wdown_ref[...].astype(BF16)

    @pl.when(lax.rem(pl.program_id(0), tiles_per_seq) == 0)
    def _():
        carry_scr[...] = jnp.zeros_like(carry_scr)

    carry = carry_scr[...]
    for r0 in range(0, tm, INPROJ_SUB):
        rows = slice(r0, r0 + INPROJ_SUB)
        h = _rms(x_ref[rows, :], nw_ref[...]).astype(BF16)
        pg = jnp.dot(h, w_ref[:, OFF_GDN:OFF_Z], preferred_element_type=F32)
        oa_ref[rows, :] = jnp.dot(h, w_ref[:, 0:OFF_GDN], preferred_element_type=F32)
        oz_ref[rows, :] = jnp.dot(h, w_ref[:, OFF_Z:OFF_AB], preferred_element_type=F32)
        ab = jnp.dot(h, w_ref[:, OFF_AB:IN_COLS], preferred_element_type=F32)
        ogb_ref[rows, :] = _gates(ab, alog_ref[...], dtb_ref[...])
        og_ref[rows, :] = _causal_conv(pg, carry, cw_ref[...])
        carry = pg[INPROJ_SUB - SUBLANES:INPROJ_SUB]
    carry_scr[...] = carry
    conv_out_ref[...] = carry


def _inproj(x, nw, w, cw, alog, dtb, wout, wup, wdown, tm, seq):
    m = x.shape[0]
    steps = m // tm
    tiles_per_seq = seq // tm
    row = lambda i: (i, 0)
    fixed = lambda i: (0, 0)
    slab = lambda a: pl.BlockSpec((a.shape[0] // steps, a.shape[1]), row)
    return pl.pallas_call(
        functools.partial(_inproj_kernel, tiles_per_seq=tiles_per_seq),
        grid=(steps,),
        in_specs=[pl.BlockSpec((tm, D_MODEL), row),
                  pl.BlockSpec((1, D_MODEL), fixed),
                  pl.BlockSpec((D_MODEL, IN_COLS), fixed),
                  pl.BlockSpec((CONV_W, GDN_CONV_DIM), fixed),
                  pl.BlockSpec((1, LANES), fixed),
                  pl.BlockSpec((1, LANES), fixed),
                  slab(wout), slab(wup), slab(wdown)],
        out_specs=[pl.BlockSpec((tm, ATTN_COLS), row),
                   pl.BlockSpec((tm, GDN_CONV_DIM), row),
                   pl.BlockSpec((tm, GDN_VAL_WIDTH), row),
                   pl.BlockSpec((tm, AB_COLS), row),
                   pl.BlockSpec((None, SUBLANES, GDN_CONV_DIM), lambda i: (i // tiles_per_seq, 0, 0)),
                   slab(wout), slab(wup), slab(wdown)],
        out_shape=[jax.ShapeDtypeStruct((m, ATTN_COLS), F32),
                   jax.ShapeDtypeStruct((m, GDN_CONV_DIM), F32),
                   jax.ShapeDtypeStruct((m, GDN_VAL_WIDTH), F32),
                   jax.ShapeDtypeStruct((m, AB_COLS), F32),
                   jax.ShapeDtypeStruct((m // seq, SUBLANES, GDN_CONV_DIM), F32),
                   jax.ShapeDtypeStruct(wout.shape, BF16),
                   jax.ShapeDtypeStruct(wup.shape, BF16),
                   jax.ShapeDtypeStruct(wdown.shape, BF16)],
        scratch_shapes=[pltpu.VMEM((SUBLANES, GDN_CONV_DIM), F32)],
        compiler_params=pltpu.CompilerParams(dimension_semantics=("arbitrary",),
                                             vmem_limit_bytes=VMEM_LIMIT),
        name="inproj",
    )(x, nw, w, cw, alog, dtb, wout, wup, wdown)


FF_UP_CHUNK = 256
FF_DOWN_CHUNK = 256
FF_STAGES = D_FF // FF_UP_CHUNK + D_MODEL // FF_DOWN_CHUNK


def _out_ffn_stages(x_ref, ao_ref, go_ref, z_ref, gnw_ref, wout_ref, npost_ref, npre_ref, wup_ref,
                    wdown_ref, nffn_ref, y_ref):
    gnw = gnw_ref[...]
    gy = []
    for hd in range(GDN_HEADS):
        lanes = slice(hd * LANES, (hd + 1) * LANES)
        z = z_ref[:, lanes]
        gy.append((_rms(go_ref[:, lanes], gnw) * (z * _sigmoid(z))).astype(BF16))
    m = (jnp.dot(ao_ref[...], wout_ref[0:ATTN_WIDTH, :], preferred_element_type=F32)
         + jnp.dot(jnp.concatenate(gy, axis=1), wout_ref[ATTN_WIDTH:, :], preferred_element_type=F32))
    x1 = x_ref[...] + _rms(m, npost_ref[...])
    h = _rms(x1, npre_ref[...]).astype(BF16)
    yield
    hidden = []
    for j in range(0, D_FF, FF_UP_CHUNK):
        u = jnp.maximum(jnp.dot(h, wup_ref[:, j:j + FF_UP_CHUNK], preferred_element_type=F32), 0.0)
        hidden.append((u * u).astype(BF16))
        yield
    hidden = jnp.concatenate(hidden, axis=1)
    f = []
    for j in range(0, D_MODEL, FF_DOWN_CHUNK):
        f.append(jnp.dot(hidden, wdown_ref[:, j:j + FF_DOWN_CHUNK], preferred_element_type=F32))
        yield
    y_ref[...] = x1 + _rms(jnp.concatenate(f, axis=1), nffn_ref[...])


def _dup_half(x, lo):
    xr = pltpu.roll(x, HEAD_DIM, axis=1)
    return jnp.where(lo, x, xr), jnp.where(lo, xr, x)


def _attn_prompt_stages(sink_ref, cur_ref, prev_ref, seq_row0, o_ref):
    tq = cur_ref.shape[0]
    kall = jnp.concatenate([prev_ref[:, 0:KV_WIDTH], cur_ref[:, ATTN_WIDTH:ATTN_WIDTH + KV_WIDTH]], axis=0)
    vall = jnp.concatenate([prev_ref[:, KV_WIDTH:], cur_ref[:, ATTN_WIDTH + KV_WIDTH:]], axis=0)
    lo = lax.broadcasted_iota(jnp.int32, (1, LANES), 1) < HEAD_DIM
    k2 = _dup_half(kall, lo)
    v2 = _dup_half(vall, lo)
    qi = lax.broadcasted_iota(jnp.int32, (WINDOW, 2 * WINDOW), 0)
    si = lax.broadcasted_iota(jnp.int32, (WINDOW, 2 * WINDOW), 1)
    dist = WINDOW + qi - si
    band = (dist >= 0) & (dist <= WINDOW)
    distf = dist.astype(F32)
    for r in range(tq // WINDOW):
        rows = slice(r * WINDOW, (r + 1) * WINDOW)
        valid = band
        if r == 0:
            valid = band & ((seq_row0 - WINDOW + si) >= 0)
        for j in range(KV_HEADS):
            kw = k2[j][r * WINDOW:(r + 2) * WINDOW]
            vw = v2[j][r * WINDOW:(r + 2) * WINDOW]
            t0 = cur_ref[rows, (2 * j) * LANES:(2 * j + 1) * LANES]
            t1 = cur_ref[rows, (2 * j + 1) * LANES:(2 * j + 2) * LANES]
            lhs = jnp.concatenate([jnp.where(lo, t0, 0.0), jnp.where(lo, 0.0, t0),
                                   jnp.where(lo, t1, 0.0), jnp.where(lo, 0.0, t1)], axis=0)
            s = _dot_nt(lhs, kw) * (HEAD_DIM ** -0.5)
            probs = []
            for g in range(GQA_GROUP):
                h = j * GQA_GROUP + g
                sg = s[g * WINDOW:(g + 1) * WINDOW] - _alibi_slope(h) * distf
                sg = jnp.where(valid, sg, -jnp.inf)
                sink = sink_ref[0, h]
                mx = jnp.maximum(jnp.max(sg, axis=-1, keepdims=True), sink)
                p = jnp.exp(sg - mx)
                den = jnp.sum(p, axis=-1, keepdims=True) + jnp.exp(sink - mx)
                probs.append(p / den)
            yield
            o = _dot(jnp.concatenate(probs, axis=0), vw)
            o01 = jnp.where(lo, o[0:WINDOW], o[WINDOW:2 * WINDOW])
            o23 = jnp.where(lo, o[2 * WINDOW:3 * WINDOW], o[3 * WINDOW:4 * WINDOW])
            o_ref[rows, (2 * j) * LANES:(2 * j + 1) * LANES] = o01.astype(o_ref.dtype)
            o_ref[rows, (2 * j + 1) * LANES:(2 * j + 2) * LANES] = o23.astype(o_ref.dtype)
            yield


def _gdn_prompt_stages(conv_ref, gb_ref, state, o_ref):
    t = conv_ref.shape[0]
    beta = gb_ref[...]
    g = beta
    rin = lax.broadcasted_iota(jnp.int32, (t, 1), 0) & (CHUNK - 1)
    step = 1
    while step < CHUNK:
        g = g + jnp.where(rin >= step, pltpu.roll(g, step, axis=0), 0.0)
        step *= 2
    gt = g.T

    row = lax.broadcasted_iota(jnp.int32, (CHUNK, CHUNK), 0)
    col = lax.broadcasted_iota(jnp.int32, (CHUNK, CHUNK), 1)
    lower_incl = row >= col
    lower_strict = row > col

    heads = range(GDN_HEADS)
    units = [(c, h) for c in range(t // CHUNK) for h in heads]
    pre = []
    for c, h in units:
        rows = slice(c * CHUNK, (c + 1) * CHUNK)
        qc = _l2n(_silu(conv_ref[rows, h * LANES:(h + 1) * LANES])) * (GDN_DK ** -0.5)
        kc = _l2n(_silu(conv_ref[rows, GDN_KEY_WIDTH + h * LANES:GDN_KEY_WIDTH + (h + 1) * LANES]))
        vc = _silu(conv_ref[rows, 2 * GDN_KEY_WIDTH + h * LANES:2 * GDN_KEY_WIDTH + (h + 1) * LANES])
        gc = g[rows, h:h + 1]
        bc = beta[rows, GDN_HEADS + h:GDN_HEADS + h + 1]
        gr = gt[h:h + 1, c * CHUNK:(c + 1) * CHUNK]
        decay = jnp.where(lower_incl, jnp.exp(jnp.where(lower_incl, gc - gr, 0.0)), 0.0)
        eg = jnp.exp(gc)
        glast = gc[CHUNK - 1:CHUNK]
        rhs = jnp.concatenate([vc * bc, kc * (bc * eg)], axis=1)
        k_dec_t = (kc * jnp.exp(glast - gc)).T
        kq = _dot_nt(jnp.concatenate([kc, qc], axis=0), kc)
        a = jnp.where(lower_strict, bc * kq[0:CHUNK] * decay, 0.0)
        qk = jnp.where(lower_incl, kq[CHUNK:] * decay, 0.0)
        pre.append(dict(a=a, rhs=rhs, lhs_u=jnp.concatenate([qk, k_dec_t], axis=0), q_dec=qc * eg,
                        ld=jnp.exp(glast)))
        if h == GDN_HEADS - 1:
            yield

    mats = [p["a"] for p in pre]
    first = ((row >> 1) == (col >> 1)) & ((row & 1) == 1) & ((col & 1) == 0)
    xo = [jnp.where(first, -a, 0.0) for a in mats]
    size = 2
    while size < CHUNK:
        shift = size.bit_length()
        m = ((row >> shift) == (col >> shift)) & ((row & size) != 0) & ((col & size) == 0)
        aoff = [jnp.where(m, a, 0.0) for a in mats]
        y = [ao + _dot(ao, x) for ao, x in zip(aoff, xo)]
        yield
        xo = [x - yy - _dot(x, yy) for x, yy in zip(xo, y)]
        size *= 2
        yield
    trs = [p["rhs"] + _dot(x, p["rhs"]) for p, x in zip(pre, xo)]
    yield

    for c in range(t // CHUNK):
        rows = slice(c * CHUNK, (c + 1) * CHUNK)
        idx = [c * GDN_HEADS + h for h in heads]
        ws = [_dot(jnp.concatenate([trs[u][:, GDN_DV:], pre[u]["q_dec"]], axis=0), state[h])
              for h, u in zip(heads, idx)]
        yield
        ou = [_dot(pre[u]["lhs_u"], trs[u][:, 0:GDN_DV] - ws[h][0:CHUNK]) for h, u in zip(heads, idx)]
        for h, u in zip(heads, idx):
            o_ref[rows, h * LANES:(h + 1) * LANES] = ws[h][CHUNK:] + ou[h][0:CHUNK]
            state[h] = state[h] * pre[u]["ld"] + ou[h][CHUNK:]
        yield


def _run_interleaved(primary, n_primary, secondary, n_secondary):
    done = object()
    issued = 0
    for k in range(n_primary):
        next(primary, done)
        target = ((k + 1) * n_secondary + n_primary - 1) // n_primary
        while issued < target:
            next(secondary, done)
            issued += 1
    for _ in primary:
        pass
    for _ in secondary:
        pass


def _roundrobin(*gens):
    done = object()
    live = list(gens)
    while live:
        for gen in list(live):
            if next(gen, done) is done:
                live.remove(gen)
            else:
                yield


MIX_T = 512


def _mix_ffn_kernel(sink_ref, pa_ref, prev_ref, conv_ref, gb_ref, x_ref, z_ref, gnw_ref, wout_ref,
                    npost_ref, npre_ref, wup_ref, wdown_ref, nffn_ref,
                    y_ref, s_out_ref, s_scr, ao_scr, go_scr, *, n_tiles, tiles_per_seq):
    i = pl.program_id(0)

    @pl.when(i == 0)
    def _():
        s_scr[...] = jnp.zeros_like(s_scr)
        ao_scr[...] = jnp.zeros_like(ao_scr)
        go_scr[...] = jnp.zeros_like(go_scr)

    front_tile = jnp.minimum(i, n_tiles - 1)
    tile_in_seq = lax.rem(front_tile, tiles_per_seq)
    seq_start = tile_in_seq == 0
    front_valid = i < n_tiles

    ffn = _out_ffn_stages(x_ref, ao_scr, go_scr, z_ref, gnw_ref, wout_ref, npost_ref, npre_ref,
                          wup_ref, wdown_ref, nffn_ref, y_ref)
    next(ffn)

    old = [s_scr[h] for h in range(GDN_HEADS)]
    state = [jnp.where(seq_start, 0.0, s) for s in old]
    front = _roundrobin(_gdn_prompt_stages(conv_ref, gb_ref, state, go_scr),
                        _attn_prompt_stages(sink_ref, pa_ref, prev_ref, tile_in_seq * MIX_T, ao_scr))
    mix_stages = (MIX_T // CHUNK) * 3 + 2 * (CHUNK.bit_length() - 2) + 1 + 2 * (MIX_T // WINDOW) * KV_HEADS
    _run_interleaved(ffn, FF_STAGES, front, mix_stages)

    for h in range(GDN_HEADS):
        s_new = jnp.where(front_valid, state[h], old[h])
        s_scr[h] = s_new
        s_out_ref[h] = s_new


def _mix_ffn(sinks, pa, act, gb, x, z, gnw, wout, npost, npre, wup, wdown, nffn, seq):
    m = x.shape[0]
    n_tiles = m // MIX_T
    tiles_per_seq = seq // MIX_T
    blocks_per_tile = MIX_T // WINDOW
    front = lambda i: (jnp.minimum(i, n_tiles - 1), 0)
    back = lambda i: (jnp.maximum(i - 1, 0), 0)
    fixed = lambda i: (0, 0)
    resident = dict(pipeline_mode=pl.Buffered(1))
    kern = functools.partial(_mix_ffn_kernel, n_tiles=n_tiles, tiles_per_seq=tiles_per_seq)
    return pl.pallas_call(
        kern,
        grid=(n_tiles + 1,),
        in_specs=[pl.BlockSpec(memory_space=pltpu.SMEM),
                  pl.BlockSpec((MIX_T, ATTN_COLS), front),
                  pl.BlockSpec((WINDOW, 2 * KV_WIDTH),
                               lambda i: (jnp.maximum(jnp.minimum(i, n_tiles - 1) * blocks_per_tile - 1, 0),
                                          ATTN_WIDTH // (2 * KV_WIDTH))),
                  pl.BlockSpec((MIX_T, GDN_CONV_DIM), front),
                  pl.BlockSpec((MIX_T, AB_COLS), front),
                  pl.BlockSpec((MIX_T, D_MODEL), back),
                  pl.BlockSpec((MIX_T, GDN_VAL_WIDTH), back),
                  pl.BlockSpec((1, GDN_DV), fixed),
                  pl.BlockSpec((D_MODEL, D_MODEL), fixed, **resident),
                  pl.BlockSpec((1, D_MODEL), fixed),
                  pl.BlockSpec((1, D_MODEL), fixed),
                  pl.BlockSpec((D_MODEL, D_FF), fixed, **resident),
                  pl.BlockSpec((D_FF, D_MODEL), fixed, **resident),
                  pl.BlockSpec((1, D_MODEL), fixed)],
        out_specs=[pl.BlockSpec((MIX_T, D_MODEL), back),
                   pl.BlockSpec((None, GDN_HEADS, GDN_DK, GDN_DV),
                                lambda i: (jnp.minimum(i, n_tiles - 1) // tiles_per_seq, 0, 0, 0))],
        out_shape=[jax.ShapeDtypeStruct((m, D_MODEL), F32),
                   jax.ShapeDtypeStruct((m // seq, GDN_HEADS, GDN_DK, GDN_DV), F32)],
        scratch_shapes=[pltpu.VMEM((GDN_HEADS, GDN_DK, GDN_DV), F32),
                        pltpu.VMEM((MIX_T, ATTN_WIDTH), BF16),
                        pltpu.VMEM((MIX_T, GDN_VAL_WIDTH), F32)],
        compiler_params=pltpu.CompilerParams(dimension_semantics=("arbitrary",),
                                             vmem_limit_bytes=VMEM_LIMIT_MIX),
        name="mix_ffn",
    )(sinks, pa, pa, act, gb, x, z, gnw, wout, npost, npre, wup, wdown, nffn)


def _out_ffn_kernel(*refs):
    for _ in _out_ffn_stages(*refs):
        pass


def _out_ffn(x, ao, go, z, gnw, wout, npost, npre, wup, wdown, nffn, tm):
    m = x.shape[0]
    row = lambda i: (i, 0)
    fixed = lambda i: (0, 0)
    return pl.pallas_call(
        _out_ffn_kernel,
        grid=(m // tm,),
        in_specs=[pl.BlockSpec((tm, D_MODEL), row),
                  pl.BlockSpec((tm, ATTN_WIDTH), row),
                  pl.BlockSpec((tm, GDN_VAL_WIDTH), row),
                  pl.BlockSpec((tm, GDN_VAL_WIDTH), row),
                  pl.BlockSpec((1, GDN_DV), fixed),
                  pl.BlockSpec((D_MODEL, D_MODEL), fixed, pipeline_mode=pl.Buffered(1)),
                  pl.BlockSpec((1, D_MODEL), fixed),
                  pl.BlockSpec((1, D_MODEL), fixed),
                  pl.BlockSpec((D_MODEL, D_FF), fixed, pipeline_mode=pl.Buffered(1)),
                  pl.BlockSpec((D_FF, D_MODEL), fixed, pipeline_mode=pl.Buffered(1)),
                  pl.BlockSpec((1, D_MODEL), fixed)],
        out_specs=pl.BlockSpec((tm, D_MODEL), row),
        out_shape=jax.ShapeDtypeStruct((m, D_MODEL), F32),
        compiler_params=pltpu.CompilerParams(dimension_semantics=("arbitrary",),
                                             vmem_limit_bytes=VMEM_LIMIT),
        name="out_ffn",
    )(x, ao, go, z, gnw, wout, npost, npre, wup, wdown, nffn)


def _sample_prep_kernel(x_ref, nw_ref, w_ref, cs_ref, cw_ref, alog_ref, dtb_ref,
                        qexp_ref, knew_ref, vnew_ref, qg_ref, kt_ref, vg_ref, gate_ref, z_ref,
                        cs_out_ref):
    h = _rms(x_ref[...], nw_ref[...]).astype(BF16)
    pa = jnp.dot(h, w_ref[:, 0:OFF_GDN], preferred_element_type=F32)
    pg = jnp.dot(h, w_ref[:, OFF_GDN:OFF_Z], preferred_element_type=F32)
    z_ref[...] = jnp.dot(h, w_ref[:, OFF_Z:OFF_AB], preferred_element_type=F32)
    ab = jnp.dot(h, w_ref[:, OFF_AB:IN_COLS], preferred_element_type=F32)

    lo = lax.broadcasted_iota(jnp.int32, (1, LANES), 1) < HEAD_DIM
    for t in range(ATTN_WIDTH // LANES):
        tile = pa[:, t * LANES:(t + 1) * LANES]
        rolled = pltpu.roll(tile, HEAD_DIM, axis=1)
        if t < ATTN_WIDTH // LANES // KV_HEADS:
            even, odd = jnp.where(lo, tile, 0.0), jnp.where(lo, rolled, 0.0)
        else:
            even, odd = jnp.where(lo, 0.0, rolled), jnp.where(lo, 0.0, tile)
        qexp_ref[2 * t] = even
        qexp_ref[2 * t + 1] = odd
    knew_ref[...] = pa[:, ATTN_WIDTH:ATTN_WIDTH + KV_WIDTH]
    vnew_ref[...] = pa[:, ATTN_WIDTH + KV_WIDTH:ATTN_COLS]

    cw = cw_ref[...]
    y = None
    for r in range(CONV_W - 1):
        term = cs_ref[:, r * GDN_CONV_DIM:(r + 1) * GDN_CONV_DIM] * cw[r:r + 1]
        y = term if y is None else y + term
    y = y + pg * cw[CONV_W - 1:CONV_W]
    act = y * _sigmoid(y)
    cs_out_ref[:, 0:(CONV_W - 2) * GDN_CONV_DIM] = cs_ref[:, GDN_CONV_DIM:(CONV_W - 1) * GDN_CONV_DIM]
    cs_out_ref[:, (CONV_W - 2) * GDN_CONV_DIM:] = pg

    for hd in range(GDN_HEADS):
        qn = _l2n(act[:, hd * LANES:(hd + 1) * LANES]) * (GDN_DK ** -0.5)
        kn = _l2n(act[:, GDN_KEY_WIDTH + hd * LANES:GDN_KEY_WIDTH + (hd + 1) * LANES])
        qg_ref[:, hd * LANES:(hd + 1) * LANES] = qn
        kt_ref[hd] = kn.T
    vg_ref[...] = act[:, 2 * GDN_KEY_WIDTH:]
    gb = _gates(ab, alog_ref[...], dtb_ref[...])
    lane = lax.broadcasted_iota(jnp.int32, (1, LANES), 1)
    gate_ref[...] = jnp.where(lane < GDN_HEADS, jnp.exp(gb), gb)


def _sample_prep(x, nw, w, cs, cw, alog, dtb):
    n = x.shape[0]
    return pl.pallas_call(
        _sample_prep_kernel,
        out_shape=[jax.ShapeDtypeStruct((ATTN_HEADS, n, LANES), F32),
                   jax.ShapeDtypeStruct((n, KV_WIDTH), F32),
                   jax.ShapeDtypeStruct((n, KV_WIDTH), F32),
                   jax.ShapeDtypeStruct((n, GDN_KEY_WIDTH), F32),
                   jax.ShapeDtypeStruct((GDN_HEADS, GDN_DK, n), F32),
                   jax.ShapeDtypeStruct((n, GDN_VAL_WIDTH), F32),
                   jax.ShapeDtypeStruct((n, LANES), F32),
                   jax.ShapeDtypeStruct((n, GDN_VAL_WIDTH), F32),
                   jax.ShapeDtypeStruct((n, (CONV_W - 1) * GDN_CONV_DIM), F32)],
        compiler_params=pltpu.CompilerParams(vmem_limit_bytes=VMEM_LIMIT),
        name="sample_prep",
    )(x, nw, w, cs, cw, alog, dtb)


ATTN_BB = 16


def _attn_sample_kernel(sink_ref, qexp_ref, knew_ref, vnew_ref, kbuf_ref, vbuf_ref,
                        o_ref, kout_ref, vout_ref, q_scr, o_scr):
    for h in range(ATTN_HEADS):
        q_scr[pl.ds(h, ATTN_BB, stride=ATTN_HEADS), :] = qexp_ref[h]
    hrow = lax.broadcasted_iota(jnp.int32, (ATTN_HEADS, 1), 0)
    sink = jnp.zeros((ATTN_HEADS, 1), F32)
    slope = jnp.zeros((ATTN_HEADS, 1), F32)
    for h in range(ATTN_HEADS):
        sink = jnp.where(hrow == h, sink_ref[0, h], sink)
        slope = jnp.where(hrow == h, _alibi_slope(h), slope)
    pos = lax.broadcasted_iota(jnp.int32, (1, WINDOW), 1)
    bias = slope * (WINDOW - pos).astype(F32)
    rowid = lax.broadcasted_iota(jnp.int32, (WINDOW, 1), 0)
    last = rowid == WINDOW - 1
    toks = range(ATTN_BB)
    qe = [q_scr[b * ATTN_HEADS:(b + 1) * ATTN_HEADS, :] for b in toks]
    s_old = [_dot_nt(qe[b], kbuf_ref[b]) * (HEAD_DIM ** -0.5) - bias for b in toks]
    probs, p_new = [], []
    for b in toks:
        s_new = jnp.sum(qe[b] * knew_ref[b:b + 1, :], axis=-1, keepdims=True) * (HEAD_DIM ** -0.5)
        mx = jnp.maximum(jnp.maximum(jnp.max(s_old[b], axis=-1, keepdims=True), s_new), sink)
        p_old = jnp.exp(s_old[b] - mx)
        pn = jnp.exp(s_new - mx)
        probs.append(p_old)
        p_new.append((pn, 1.0 / (jnp.sum(p_old, axis=-1, keepdims=True) + pn + jnp.exp(sink - mx))))
    for b in toks:
        pn, inv = p_new[b]
        o_scr[b * ATTN_HEADS:(b + 1) * ATTN_HEADS, :] = (_dot(probs[b], vbuf_ref[b])
                                                         + pn * vnew_ref[b:b + 1, :]) * inv
    for b in toks:
        kout_ref[b] = jnp.where(last, knew_ref[b:b + 1, :], pltpu.roll(kbuf_ref[b], WINDOW - 1, axis=0))
        vout_ref[b] = jnp.where(last, vnew_ref[b:b + 1, :], pltpu.roll(vbuf_ref[b], WINDOW - 1, axis=0))
    lo = lax.broadcasted_iota(jnp.int32, (1, LANES), 1) < HEAD_DIM
    for t in range(ATTN_WIDTH // LANES):
        even = o_scr[pl.ds(2 * t, ATTN_BB, stride=ATTN_HEADS), :]
        odd = o_scr[pl.ds(2 * t + 1, ATTN_BB, stride=ATTN_HEADS), :]
        if t < ATTN_WIDTH // LANES // KV_HEADS:
            tile = jnp.where(lo, even, pltpu.roll(odd, HEAD_DIM, axis=1))
        else:
            tile = jnp.where(lo, pltpu.roll(even, HEAD_DIM, axis=1), odd)
        o_ref[:, t * LANES:(t + 1) * LANES] = tile.astype(o_ref.dtype)


def _attn_sample(sinks, qexp, knew, vnew, kbuf, vbuf):
    n = knew.shape[0]
    tok2 = lambda i: (i, 0)
    tok3 = lambda i: (i, 0, 0)
    return pl.pallas_call(
        _attn_sample_kernel,
        grid=(n // ATTN_BB,),
        in_specs=[pl.BlockSpec(memory_space=pltpu.SMEM),
                  pl.BlockSpec((ATTN_HEADS, ATTN_BB, LANES), lambda i: (0, i, 0)),
                  pl.BlockSpec((ATTN_BB, KV_WIDTH), tok2),
                  pl.BlockSpec((ATTN_BB, KV_WIDTH), tok2),
                  pl.BlockSpec((ATTN_BB, WINDOW, KV_WIDTH), tok3),
                  pl.BlockSpec((ATTN_BB, WINDOW, KV_WIDTH), tok3)],
        out_specs=[pl.BlockSpec((ATTN_BB, ATTN_WIDTH), tok2),
                   pl.BlockSpec((ATTN_BB, WINDOW, KV_WIDTH), tok3),
                   pl.BlockSpec((ATTN_BB, WINDOW, KV_WIDTH), tok3)],
        out_shape=[jax.ShapeDtypeStruct((n, ATTN_WIDTH), BF16),
                   jax.ShapeDtypeStruct((n, WINDOW, KV_WIDTH), F32),
                   jax.ShapeDtypeStruct((n, WINDOW, KV_WIDTH), F32)],
        scratch_shapes=[pltpu.VMEM((ATTN_BB * ATTN_HEADS, LANES), F32),
                        pltpu.VMEM((ATTN_BB * ATTN_HEADS, LANES), F32)],
        compiler_params=pltpu.CompilerParams(dimension_semantics=("arbitrary",),
                                             vmem_limit_bytes=VMEM_LIMIT),
        name="attn_sample",
    )(sinks, qexp, knew, vnew, kbuf, vbuf)


GDN_BB = 16


def _gdn_sample_kernel(gate_ref, q_ref, kt_ref, vg_ref, s_ref, o_ref, s_out_ref):
    i = pl.program_id(0)
    n = kt_ref.shape[-1]
    shift = (n - i * GDN_BB) % n
    units = [(j, h) for h in range(GDN_HEADS) for j in range(GDN_BB)]
    kt = [pltpu.roll(kt_ref[h], shift, axis=1) for h in range(GDN_HEADS)]
    kb = [jnp.broadcast_to(kt[h][:, j:j + 1], (GDN_DK, GDN_DV)) for j, h in units]
    s = [s_ref[j, h] * gate_ref[(i * GDN_BB + j) * SUBLANES + h] for j, h in units]
    kv = [jnp.sum(su * k, axis=0, keepdims=True) for su, k in zip(s, kb)]
    u = [(vg_ref[j:j + 1, h * LANES:(h + 1) * LANES] - kvu)
         * gate_ref[(i * GDN_BB + j) * SUBLANES + GDN_HEADS + h] for (j, h), kvu in zip(units, kv)]
    s = [su + k * uu for su, k, uu in zip(s, kb, u)]
    for (j, h), su in zip(units, s):
        s_out_ref[j, h] = su
    for (j, h), su in zip(units, s):
        lanes = slice(h * LANES, (h + 1) * LANES)
        o_ref[j:j + 1, lanes] = _dot(q_ref[:, lanes], su)[j:j + 1, :]


def _gdn_sample(gate, q, kt, vg, s0):
    n = vg.shape[0]
    return pl.pallas_call(
        _gdn_sample_kernel,
        grid=(n // GDN_BB,),
        in_specs=[pl.BlockSpec(memory_space=pltpu.SMEM),
                  pl.BlockSpec((GDN_BB, GDN_KEY_WIDTH), lambda i: (i, 0)),
                  pl.BlockSpec((GDN_HEADS, GDN_DK, n), lambda i: (0, 0, 0)),
                  pl.BlockSpec((GDN_BB, GDN_VAL_WIDTH), lambda i: (i, 0)),
                  pl.BlockSpec((GDN_BB, GDN_HEADS, GDN_DK, GDN_DV), lambda i: (i, 0, 0, 0))],
        out_specs=[pl.BlockSpec((GDN_BB, GDN_VAL_WIDTH), lambda i: (i, 0)),
                   pl.BlockSpec((GDN_BB, GDN_HEADS, GDN_DK, GDN_DV), lambda i: (i, 0, 0, 0))],
        out_shape=[jax.ShapeDtypeStruct((n, GDN_VAL_WIDTH), F32),
                   jax.ShapeDtypeStruct((n, GDN_HEADS, GDN_DK, GDN_DV), F32)],
        compiler_params=pltpu.CompilerParams(dimension_semantics=("arbitrary",),
                                             vmem_limit_bytes=VMEM_LIMIT),
        name="gdn_sample",
    )(gate, q, kt, vg, s0)


def _sample_mixers_kernel(sink_ref, gate_ref, qexp_ref, knew_ref, vnew_ref, kbuf_ref, vbuf_ref,
                          q_ref, kt_ref, vg_ref, s_ref,
                          ao_ref, kout_ref, vout_ref, go_ref, s_out_ref, q_scr, o_scr):
    _attn_sample_kernel(sink_ref, qexp_ref, knew_ref, vnew_ref, kbuf_ref, vbuf_ref,
                        ao_ref, kout_ref, vout_ref, q_scr, o_scr)
    _gdn_sample_kernel(gate_ref, q_ref, kt_ref, vg_ref, s_ref, go_ref, s_out_ref)


def _sample_mixers(sinks, gate, qexp, knew, vnew, kbuf, vbuf, q, kt, vg, s0):
    assert ATTN_BB == GDN_BB
    n = knew.shape[0]
    bb = ATTN_BB
    tok2 = lambda i: (i, 0)
    tok3 = lambda i: (i, 0, 0)
    tok4 = lambda i: (i, 0, 0, 0)
    smem = pl.BlockSpec(memory_space=pltpu.SMEM)
    cache = pl.BlockSpec((bb, WINDOW, KV_WIDTH), tok3)
    state = pl.BlockSpec((bb, GDN_HEADS, GDN_DK, GDN_DV), tok4)
    return pl.pallas_call(
        _sample_mixers_kernel,
        grid=(n // bb,),
        in_specs=[smem, smem,
                  pl.BlockSpec((ATTN_HEADS, bb, LANES), lambda i: (0, i, 0)),
                  pl.BlockSpec((bb, KV_WIDTH), tok2),
                  pl.BlockSpec((bb, KV_WIDTH), tok2),
                  cache, cache,
                  pl.BlockSpec((bb, GDN_KEY_WIDTH), tok2),
                  pl.BlockSpec((GDN_HEADS, GDN_DK, n), lambda i: (0, 0, 0)),
                  pl.BlockSpec((bb, GDN_VAL_WIDTH), tok2),
                  state],
        out_specs=[pl.BlockSpec((bb, ATTN_WIDTH), tok2), cache, cache,
                   pl.BlockSpec((bb, GDN_VAL_WIDTH), tok2), state],
        out_shape=[jax.ShapeDtypeStruct((n, ATTN_WIDTH), BF16),
                   jax.ShapeDtypeStruct((n, WINDOW, KV_WIDTH), F32),
                   jax.ShapeDtypeStruct((n, WINDOW, KV_WIDTH), F32),
                   jax.ShapeDtypeStruct((n, GDN_VAL_WIDTH), F32),
                   jax.ShapeDtypeStruct((n, GDN_HEADS, GDN_DK, GDN_DV), F32)],
        scratch_shapes=[pltpu.VMEM((bb * ATTN_HEADS, LANES), F32),
                        pltpu.VMEM((bb * ATTN_HEADS, LANES), F32)],
        compiler_params=pltpu.CompilerParams(dimension_semantics=("arbitrary",),
                                             vmem_limit_bytes=VMEM_LIMIT),
        name="sample_mixers",
    )(sinks, gate, qexp, knew, vnew, kbuf, vbuf, q, kt, vg, s0)


def _pad_lanes(v):
    return jnp.pad(v.astype(F32), (0, LANES - v.shape[0])).reshape(1, LANES)


def kernel(x_prompt, x_sample, state_conv, cache_win_k, cache_win_v, state_gdn, norm_mix_pre, w_in,
           attn_sinks, conv_w, gdn_a_log, gdn_dt_bias, gdn_norm, w_out, norm_mix_post, norm_ffn_pre,
           w_up, w_down, norm_ffn_post):
    assert w_in.shape[0] == 1, "single-layer trunk"
    bsz, seq, _ = x_prompt.shape
    nsamp = x_sample.shape[0]

    w_in_p = jnp.pad(w_in[0], ((0, 0), (0, IN_COLS - w_in.shape[2]))).astype(BF16)
    n_pre = norm_mix_pre[0].reshape(1, D_MODEL)
    n_post = norm_mix_post[0].reshape(1, D_MODEL)
    n_fpre = norm_ffn_pre[0].reshape(1, D_MODEL)
    n_fpost = norm_ffn_post[0].reshape(1, D_MODEL)
    sinks = attn_sinks[0].reshape(1, ATTN_HEADS)
    cw = conv_w[0]
    alog = _pad_lanes(gdn_a_log[0])
    dtb = _pad_lanes(gdn_dt_bias[0])
    gnw = gdn_norm[0].reshape(1, GDN_DV)

    xp = x_prompt.reshape(bsz * seq, D_MODEL)
    pa, act, pz, gb, p_conv, w_out_b, w_up_b, w_down_b = _inproj(
        xp, n_pre, w_in_p, cw, alog, dtb, w_out[0], w_up[0], w_down[0], INPROJ_T, seq)
    yp, p_s = _mix_ffn(sinks, pa, act, gb, xp, pz, gnw, w_out_b, n_post, n_fpre, w_up_b, w_down_b,
                       n_fpost, seq)
    pa = pa.reshape(bsz, seq, ATTN_COLS)

    xs = x_sample.reshape(nsamp, D_MODEL)
    cs = state_conv[0].reshape(nsamp, (CONV_W - 1) * GDN_CONV_DIM)
    qexp, knew, vnew, qg, kt, vg, gate, zs, cs_new = _sample_prep(xs, n_pre, w_in_p, cs, cw, alog, dtb)
    kbuf = cache_win_k[0].reshape(nsamp, WINDOW, KV_WIDTH)
    vbuf = cache_win_v[0].reshape(nsamp, WINDOW, KV_WIDTH)
    ao_s, k_new_cache, v_new_cache, go_s, s_new = _sample_mixers(
        sinks, gate[:, :SUBLANES].reshape(nsamp * SUBLANES), qexp, knew, vnew, kbuf, vbuf, qg, kt, vg,
        state_gdn[0])
    ys = _out_ffn(xs, ao_s, go_s, zs, gnw, w_out_b, n_post, n_fpre, w_up_b, w_down_b, n_fpost, nsamp)

    kv_shape = (1, -1, WINDOW, KV_HEADS, HEAD_DIM)
    return (yp.reshape(bsz, seq, D_MODEL),
            ys.reshape(nsamp, 1, D_MODEL),
            p_conv[:, SUBLANES - (CONV_W - 1):, :][None],
            pa[:, seq - WINDOW:, ATTN_WIDTH:ATTN_WIDTH + KV_WIDTH].reshape(kv_shape),
            pa[:, seq - WINDOW:, ATTN_WIDTH + KV_WIDTH:].reshape(kv_shape),
            p_s[None],
            cs_new.reshape(1, nsamp, CONV_W - 1, GDN_CONV_DIM),
            k_new_cache.reshape(kv_shape),
            v_new_cache.reshape(kv_shape),
            s_new[None])
```
